```python
import jax, jax.numpy as jnp
from jax import lax
import numpy as np

D_MODEL = 1024
BATCH = 16
SEQ = 256
DEPTH = 4
DEC_BATCH = 8
DEC_SEQ = 1024
PAST_LEN = 512

GRID_W = 64
D_A = D_MODEL
CONV_A_WIDTH = 3
D_B = D_MODEL
CONV_B_WIDTH = 31
HEAD_DIM = 64
N_Q_HEADS = D_MODEL // HEAD_DIM
N_KV_HEADS = N_Q_HEADS // 4
KV_GROUP = N_Q_HEADS // N_KV_HEADS
D_Q = N_Q_HEADS * HEAD_DIM
D_KV = N_KV_HEADS * HEAD_DIM
Q_BLOCK = 128
ROPE_THETA = 10000.0
N_EXPERTS = 16
N_GROUPS = 4
EXPERTS_PER_GROUP = N_EXPERTS // N_GROUPS
TOPK_GROUPS = 1
TOP_K = 2
D_EXPERT = D_MODEL // 2
MOE_BLOCK = 128
N_BRANCHES = 3
SECTION_WIDTHS = (D_A, D_A, D_A, D_B, D_B, D_Q, D_KV, D_KV, N_BRANCHES * D_MODEL)
SPLIT_POINTS = tuple(int(s) for s in np.cumsum(SECTION_WIDTHS)[:-1])
IN_COLS = sum(SECTION_WIDTHS)
N_MOD = 6
DEEPNORM_ALPHA = (2 * DEPTH) ** 0.25
DEEPNORM_BETA = (8 * DEPTH) ** -0.25
LN_EPS = 1e-5
RMS_EPS = 1e-6

kernel_name = 'hybrid_diffusion_prefix_trunk_step'


def layer_norm(x, g, b):
    xf = x.astype(jnp.float32)
    xc = xf - xf.mean(-1, keepdims=True)
    var = (xc * xc).mean(-1, keepdims=True)
    return (xc * lax.rsqrt(var + LN_EPS) * g.astype(jnp.float32) + b.astype(jnp.float32)).astype(x.dtype)


def rms_norm(x, g):
    xf = x.astype(jnp.float32)
    return (xf * lax.rsqrt((xf * xf).mean(-1, keepdims=True) + RMS_EPS) * g.astype(jnp.float32)).astype(x.dtype)


def depthwise_conv(x, w):
    k_width, ch = w.shape
    return lax.conv_general_dilated(
        x, w[:, None, :].astype(x.dtype), window_strides=(1,),
        padding=[(k_width // 2, k_width // 2)],
        dimension_numbers=('NWC', 'WIO', 'NWC'), feature_group_count=ch)


def grid_positions(n_tokens):
    rows = n_tokens // GRID_W
    row = jnp.repeat(jnp.arange(rows, dtype=jnp.int32), GRID_W)
    col = jnp.tile(jnp.arange(GRID_W, dtype=jnp.int32), rows)
    return row, col


def rope_1d(x, pos):
    half = x.shape[-1] // 2
    freqs = jnp.power(ROPE_THETA, -jnp.arange(half, dtype=jnp.float32) / half)
    ang = pos.astype(jnp.float32)[:, None] * freqs[None, :]
    cos = jnp.cos(ang)[:, None, :]
    sin = jnp.sin(ang)[:, None, :]
    xf = x.astype(jnp.float32)
    x1, x2 = xf[..., :half], xf[..., half:]
    return jnp.concatenate([x1 * cos - x2 * sin, x2 * cos + x1 * sin], axis=-1).astype(x.dtype)


def rope_2d(x, row, col):
    half = HEAD_DIM // 2
    return jnp.concatenate([rope_1d(x[..., :half], row), rope_1d(x[..., half:], col)], axis=-1)


def block_attention(q, k, v):
    bsz, sq = q.shape[0], q.shape[1]
    n_blk = sq // Q_BLOCK
    qb = q.reshape(bsz, n_blk, Q_BLOCK, N_KV_HEADS, KV_GROUP, HEAD_DIM).transpose(1, 0, 2, 3, 4, 5)
    scale = HEAD_DIM ** -0.5

    def one_block(q_blk):
        s = jnp.einsum('bqkgd,bskd->bkgqs', q_blk, k).astype(jnp.float32) * scale
        p = jax.nn.softmax(s, axis=-1).astype(v.dtype)
        return jnp.einsum('bkgqs,bskd->bqkgd', p, v)

    o = lax.map(one_block, qb)
    return o.transpose(1, 0, 2, 3, 4, 5).reshape(bsz, sq, D_Q)


def parallel_mixer(h, lp, ctx_k, ctx_v, pos):
    bsz, s, _ = h.shape
    proj = jnp.einsum('bsd,dn->bsn', h, lp['w_in'])
    a_b, a_c, a_x, b_u, b_g, q, k, v, gates = jnp.split(proj, SPLIT_POINTS, axis=-1)
    y_a = jnp.einsum('bsc,cd->bsd', a_b * depthwise_conv(a_c * a_x, lp['conv_a_w']), lp['w_a_out'])
    u = depthwise_conv(b_u * jax.nn.sigmoid(b_g), lp['conv_b_w'])
    u = u + lp['conv_b_b'].astype(u.dtype)
    y_b = jnp.einsum('bsc,cd->bsd', jax.nn.silu(layer_norm(u, lp['norm_b_g'], lp['norm_b_b'])), lp['w_b_out'])
    q = rms_norm(q.reshape(bsz, s, N_Q_HEADS, HEAD_DIM), lp['q_norm_g'])
    k = rms_norm(k.reshape(bsz, s, N_KV_HEADS, HEAD_DIM), lp['k_norm_g'])
    v = v.reshape(bsz, s, N_KV_HEADS, HEAD_DIM)
    if pos is None:
        o = block_attention(q, k, v)
    else:
        row, col = pos
        keys = jnp.concatenate([ctx_k.astype(k.dtype), rope_2d(k, row, col)], axis=1)
        vals = jnp.concatenate([ctx_v.astype(v.dtype), v], axis=1)
        o = block_attention(rope_2d(q, row, col), keys, vals)
    y_c = jnp.einsum('bsc,cd->bsd', o, lp['w_c_out'])
    g_a, g_b, g_c = jnp.split(jax.nn.sigmoid(gates), N_BRANCHES, axis=-1)
    out = jnp.einsum('bsd,de->bse', g_a * y_a + g_b * y_b + g_c * y_c, lp['w_o'])
    return out, k, v


def routed_moe(x2, w_router, router_bias, w_gate_up, w_down):
    t_count, d = x2.shape
    scores = jax.nn.sigmoid(jnp.einsum('td,de->te', x2, w_router).astype(jnp.float32))
    sel = scores + router_bias.astype(jnp.float32)
    group_score = lax.top_k(sel.reshape(t_count, N_GROUPS, EXPERTS_PER_GROUP), 2)[0].sum(-1)
    _, g_idx = lax.top_k(group_score, TOPK_GROUPS)
    group_mask = (jnp.arange(N_GROUPS)[None, None, :] == g_idx[:, :, None]).any(axis=1)
    expert_mask = jnp.repeat(group_mask, EXPERTS_PER_GROUP, axis=1)
    _, e_idx = lax.top_k(jnp.where(expert_mask, sel, -jnp.inf), TOP_K)
    w = jnp.take_along_axis(scores, e_idx, axis=1)
    w = w / w.sum(-1, keepdims=True)
    n_assign = t_count * TOP_K
    flat_e = e_idx.reshape(n_assign)
    order = jnp.argsort(flat_e)
    e_sorted = flat_e[order]
    tok_sorted = (order // TOP_K).astype(jnp.int32)
    counts = jnp.zeros((N_EXPERTS,), jnp.int32).at[flat_e].add(1)
    padded = (counts + MOE_BLOCK - 1) // MOE_BLOCK * MOE_BLOCK
    pad_end = jnp.cumsum(padded)
    pad_start = pad_end - padded
    start = jnp.cumsum(counts) - counts
    dest = pad_start[e_sorted] + (jnp.arange(n_assign, dtype=jnp.int32) - start[e_sorted])
    n_blocks = (n_assign + N_EXPERTS * (MOE_BLOCK - 1) + MOE_BLOCK - 1) // MOE_BLOCK
    slot_tok = jnp.full((n_blocks * MOE_BLOCK,), t_count, jnp.int32).at[dest].set(tok_sorted)
    block_exp = jnp.minimum(
        jnp.searchsorted(pad_end, jnp.arange(n_blocks, dtype=jnp.int32) * MOE_BLOCK, side='right'),
        N_EXPERTS - 1)
    x_pad = jnp.concatenate([x2, jnp.zeros((1, d), x2.dtype)], axis=0)

    def run_block(args):
        toks, e = args
        gu = x_pad[toks] @ w_gate_up[e]
        gate, up = jnp.split(gu, 2, axis=-1)
        return (jax.nn.silu(gate) * up) @ w_down[e]

    y_slots = lax.map(run_block, (slot_tok.reshape(n_blocks, MOE_BLOCK), block_exp))
    y_slots = y_slots.reshape(n_blocks * MOE_BLOCK, d)
    y_assign = jnp.zeros((n_assign, d), y_slots.dtype).at[order].set(y_slots[dest])
    return (y_assign.reshape(t_count, TOP_K, d) * w[..., None].astype(y_slots.dtype)).sum(axis=1)


def trunk_layer(x, cond, lp, w_router, router_bias, ctx_k, ctx_v, pos):
    bsz, s, d = x.shape
    mod = jnp.einsum('bd,dn->bn', jax.nn.silu(cond), lp['w_ada']) + lp['b_ada']
    sh1, sc1, g1, sh2, sc2, g2 = jnp.split(mod[:, None, :], N_MOD, axis=-1)
    h = x * (1.0 + sc1) + sh1
    m, k, v = parallel_mixer(h, lp, ctx_k, ctx_v, pos)
    x = layer_norm(DEEPNORM_ALPHA * x + g1 * m, lp['ln_g'][0], lp['ln_b'][0])
    h = x * (1.0 + sc2) + sh2
    f = routed_moe(h.reshape(bsz * s, d), w_router, router_bias, lp['w_gate_up'], lp['w_down']).reshape(bsz, s, d)
    x = layer_norm(DEEPNORM_ALPHA * x + g2 * f, lp['ln_g'][1], lp['ln_b'][1])
    return x, k, v


def setup_inputs(seed: int = 0) -> dict:
    key = jax.random.key(seed)
    ks = jax.random.split(key, 26)

    def nrm(k, shape, scale):
        return jax.random.normal(k, shape, jnp.float32) * scale

    return {
        'x_prompt': nrm(ks[0], (BATCH, SEQ, D_MODEL), 1.0),
        'x_sample': nrm(ks[1], (DEC_BATCH, DEC_SEQ, D_MODEL), 1.0),
        'c': nrm(ks[2], (DEC_BATCH, D_MODEL), 1.0),
        'c_ctx': nrm(ks[3], (D_MODEL,), 1.0),
        'cache_k': nrm(ks[4], (DEC_BATCH, DEPTH, PAST_LEN, N_KV_HEADS, HEAD_DIM), 1.0),
        'cache_v': nrm(ks[5], (DEC_BATCH, DEPTH, PAST_LEN, N_KV_HEADS, HEAD_DIM), 1.0),
        'w_in': nrm(ks[6], (DEPTH, D_MODEL, IN_COLS), D_MODEL ** -0.5),
        'conv_a_w': nrm(ks[7], (DEPTH, CONV_A_WIDTH, D_A), CONV_A_WIDTH ** -0.5),
        'w_a_out': nrm(ks[8], (DEPTH, D_A, D_MODEL), D_A ** -0.5),
        'conv_b_w': nrm(ks[9], (DEPTH, CONV_B_WIDTH, D_B), CONV_B_WIDTH ** -0.5),
        'conv_b_b': nrm(ks[10], (DEPTH, D_B), 0.02),
        'norm_b_g': 1.0 + nrm(ks[11], (DEPTH, D_B), 0.02),
        'norm_b_b': nrm(ks[12], (DEPTH, D_B), 0.02),
        'w_b_out': nrm(ks[13], (DEPTH, D_B, D_MODEL), D_B ** -0.5),
        'q_norm_g': 1.0 + nrm(ks[14], (DEPTH, HEAD_DIM), 0.02),
        'k_norm_g': 1.0 + nrm(ks[15], (DEPTH, HEAD_DIM), 0.02),
        'w_c_out': nrm(ks[16], (DEPTH, D_Q, D_MODEL), D_Q ** -0.5),
        'w_o': nrm(ks[17], (DEPTH, D_MODEL, D_MODEL), DEEPNORM_BETA * D_MODEL ** -0.5),
        'w_ada': nrm(ks[18], (DEPTH, D_MODEL, N_MOD * D_MODEL), D_MODEL ** -0.5),
        'b_ada': nrm(ks[19], (DEPTH, N_MOD * D_MODEL), 0.02),
        'ln_g': 1.0 + nrm(ks[20], (DEPTH, 2, D_MODEL), 0.02),
        'ln_b': nrm(ks[21], (DEPTH, 2, D_MODEL), 0.02),
        'w_router': nrm(ks[22], (D_MODEL, N_EXPERTS), D_MODEL ** -0.5),
        'router_bias': nrm(ks[23], (N_EXPERTS,), 0.01),
        'w_gate_up': nrm(ks[24], (DEPTH, N_EXPERTS, D_MODEL, 2 * D_EXPERT), D_MODEL ** -0.5),
        'w_down': nrm(ks[25], (DEPTH, N_EXPERTS, D_EXPERT, D_MODEL), DEEPNORM_BETA * D_EXPERT ** -0.5),
    }


def reference(x_prompt, x_sample, c, c_ctx, cache_k, cache_v, w_in, conv_a_w, w_a_out, conv_b_w,
              conv_b_b, norm_b_g, norm_b_b, w_b_out, q_norm_g, k_norm_g, w_c_out, w_o, w_ada, b_ada,
              ln_g, ln_b, w_router, router_bias, w_gate_up, w_down):
    lat_pos = grid_positions(x_sample.shape[1])
    ctx_cond = c_ctx[None, :]
    y_prompt = x_prompt
    y_sample = x_sample
    new_k_layers = []
    new_v_layers = []
    for l in range(DEPTH):
        lp = {
            'w_in': w_in[l], 'conv_a_w': conv_a_w[l], 'w_a_out': w_a_out[l],
            'conv_b_w': conv_b_w[l], 'conv_b_b': conv_b_b[l], 'norm_b_g': norm_b_g[l],
            'norm_b_b': norm_b_b[l], 'w_b_out': w_b_out[l], 'q_norm_g': q_norm_g[l],
            'k_norm_g': k_norm_g[l], 'w_c_out': w_c_out[l], 'w_o': w_o[l], 'w_ada': w_ada[l],
            'b_ada': b_ada[l], 'ln_g': ln_g[l], 'ln_b': ln_b[l], 'w_gate_up': w_gate_up[l],
            'w_down': w_down[l],
        }
        y_prompt, k_l, v_l = trunk_layer(y_prompt, ctx_cond, lp, w_router, router_bias, None, None, None)
        new_k_layers.append(k_l)
        new_v_layers.append(v_l)
        y_sample, _, _ = trunk_layer(y_sample, c, lp, w_router, router_bias,
                                     cache_k[:, l], cache_v[:, l], lat_pos)
    new_k = jnp.stack(new_k_layers, axis=1)
    new_v = jnp.stack(new_v_layers, axis=1)
    return (y_prompt, y_sample, new_k, new_v)
```

```python
import functools

import numpy as np
import jax
import jax.numpy as jnp
from jax import lax
from jax.experimental import pallas as pl
from jax.experimental.pallas import tpu as pltpu

F32 = jnp.float32
BF16 = jnp.bfloat16
I32 = jnp.int32

D = 1024
DEPTH = 4
N_CTX_SEQ = 16
CTX_LEN = 256
N_LAT_SEQ = 8
LAT_LEN = 1024
PAST = 512
T_CTX = N_CTX_SEQ * CTX_LEN
T_LAT = N_LAT_SEQ * LAT_LEN
T = T_CTX + T_LAT
GRID_W = 64
HEAD = 64
N_Q = 16
N_KV = 4
GROUP = 4
D_KV = N_KV * HEAD
N_EXP = 16
N_GRP = 4
EXP_PER_GRP = 4
D_EXP = 512
IN_COLS = 9728
N_MOD = 6
ALPHA = (2 * DEPTH) ** 0.25
LN_EPS = 1e-5
RMS_EPS = 1e-6
ROPE_THETA = 10000.0

LANE = 128
TM = 512
N_TILES = T // TM
CTX_TILES = T_CTX // TM
TILES_PER_LAT = LAT_LEN // TM
TMP = 1024
SEG = 256
N_SEG = T // SEG
CTX_SEGS = T_CTX // SEG
SEGS_PER_LAT = LAT_LEN // SEG
HALO_A = 8
HALO_B = 16
FFN_BLK = 256
N_FFN_BLOCKS = (2 * T) // FFN_BLK + N_EXP
N_SLOTS = N_FFN_BLOCKS * FFN_BLK
VMEM_LIMIT = 56 * 1024 * 1024

COL_AB, COL_AC, COL_AX, COL_BU, COL_BG, COL_Q, COL_KV, COL_GATES = 0, 2, 4, 6, 8, 10, 12, 13


def _params(sem):
    return pltpu.CompilerParams(dimension_semantics=sem, vmem_limit_bytes=VMEM_LIMIT)


def _cond_row(m, tiles_ctx, tiles_per_lat):
    return jnp.where(m < tiles_ctx, 0, 1 + (m - tiles_ctx) // tiles_per_lat)


def _layer_norm(x, g, b):
    mu = jnp.mean(x, axis=-1, keepdims=True)
    xc = x - mu
    var = jnp.mean(xc * xc, axis=-1, keepdims=True)
    return xc * lax.rsqrt(var + LN_EPS) * g + b


def _sigmoid(x):
    return 1.0 / (1.0 + jnp.exp(-x))


def _mod_kernel(cond_ref, w_ref, b_ref, o_ref):
    cnd = cond_ref[...]
    s = (cnd * _sigmoid(cnd)).astype(BF16)
    o_ref[...] = jnp.dot(s, w_ref[...].astype(BF16), preferred_element_type=F32) + b_ref[...]


def _modulation(cond16, w_ada, b_ada):
    n_col = N_MOD * D
    tn = 1024
    return pl.pallas_call(
        _mod_kernel,
        out_shape=jax.ShapeDtypeStruct((DEPTH, 16, n_col), F32),
        grid=(DEPTH, n_col // tn),
        in_specs=[
            pl.BlockSpec((16, D), lambda l, n: (0, 0)),
            pl.BlockSpec((None, D, tn), lambda l, n: (l, 0, n)),
            pl.BlockSpec((None, 1, tn), lambda l, n: (l, 0, n)),
        ],
        out_specs=pl.BlockSpec((None, 16, tn), lambda l, n: (l, 0, n)),
        compiler_params=_params(("parallel", "parallel")),
        name="adaln_mod",
    )(cond16, w_ada, b_ada.reshape(DEPTH, 1, n_col))


def _modulate_kernel(x_ref, mod_ref, h_ref):
    h_ref[...] = (x_ref[...] * (1.0 + mod_ref[1:2, :]) + mod_ref[0:1, :]).astype(BF16)


def _modulate(x, mod_l):
    return pl.pallas_call(
        _modulate_kernel,
        out_shape=jax.ShapeDtypeStruct((T, D), BF16),
        grid=(N_TILES,),
        in_specs=[
            pl.BlockSpec((TM, D), lambda m: (m, 0)),
            pl.BlockSpec((None, N_MOD, D), lambda m: (_cond_row(m, CTX_TILES, TILES_PER_LAT), 0, 0)),
        ],
        out_specs=pl.BlockSpec((TM, D), lambda m: (m, 0)),
        compiler_params=_params(("parallel",)),
        name="modulate",
    )(x, mod_l)


def _proj_a_kernel(h_ref, wb_ref, wc_ref, wx_ref, ab_ref, acx_ref):
    h = h_ref[...]
    ab_ref[...] = jnp.dot(h, wb_ref[...], preferred_element_type=F32).astype(BF16)
    acx_ref[...] = (jnp.dot(h, wc_ref[...], preferred_element_type=F32)
                    * jnp.dot(h, wx_ref[...], preferred_element_type=F32))


def _proj_b_kernel(h_ref, wu_ref, wg_ref, glu_ref):
    h = h_ref[...]
    glu_ref[...] = (jnp.dot(h, wu_ref[...], preferred_element_type=F32)
                    * _sigmoid(jnp.dot(h, wg_ref[...], preferred_element_type=F32)))


def _proj_gate_kernel(h_ref, w_ref, g_ref):
    g_ref[...] = _sigmoid(jnp.dot(h_ref[...], w_ref[...], preferred_element_type=F32)).astype(BF16)


def _w_spec(l, col0, tn):
    return pl.BlockSpec((None, D, tn), lambda c, m: (l, 0, col0 + c))


def _proj_a(h, w_in, l):
    tn = 512
    out_spec = pl.BlockSpec((TMP, tn), lambda c, m: (m, c))
    return pl.pallas_call(
        _proj_a_kernel,
        out_shape=(jax.ShapeDtypeStruct((T, D), BF16), jax.ShapeDtypeStruct((T, D), F32)),
        grid=(D // tn, T // TMP),
        in_specs=[pl.BlockSpec((TMP, D), lambda c, m: (m, 0)),
                  _w_spec(l, COL_AB, tn), _w_spec(l, COL_AC, tn), _w_spec(l, COL_AX, tn)],
        out_specs=(out_spec, out_spec),
        compiler_params=_params(("parallel", "parallel")),
        name="proj_a",
    )(h, w_in, w_in, w_in)


def _proj_b(h, w_in, l):
    tn = 512
    return pl.pallas_call(
        _proj_b_kernel,
        out_shape=jax.ShapeDtypeStruct((T, D), F32),
        grid=(D // tn, T // TMP),
        in_specs=[pl.BlockSpec((TMP, D), lambda c, m: (m, 0)),
                  _w_spec(l, COL_BU, tn), _w_spec(l, COL_BG, tn)],
        out_specs=pl.BlockSpec((TMP, tn), lambda c, m: (m, c)),
        compiler_params=_params(("parallel", "parallel")),
        name="proj_b",
    )(h, w_in, w_in)


def _proj_gates(h, w_in, l):
    tn = 512
    return pl.pallas_call(
        _proj_gate_kernel,
        out_shape=jax.ShapeDtypeStruct((T, 3 * D), BF16),
        grid=(3 * D // tn, T // TMP),
        in_specs=[pl.BlockSpec((TMP, D), lambda c, m: (m, 0)), _w_spec(l, COL_GATES, tn)],
        out_specs=pl.BlockSpec((TMP, tn), lambda c, m: (m, c)),
        compiler_params=_params(("parallel", "parallel")),
        name="proj_gates",
    )(h, w_in)


def _head_mean_square(x, ones_bd):
    out = []
    for c in range(x.shape[1] // LANE):
        sq = x[:, c * LANE:(c + 1) * LANE]
        sq = sq * sq
        hi = sq.astype(BF16)
        lo = (sq - hi.astype(F32)).astype(BF16)
        out.append(jnp.dot(hi, ones_bd, preferred_element_type=F32)
                   + jnp.dot(lo, ones_bd, preferred_element_type=F32))
    return jnp.concatenate(out, axis=1) * (1.0 / HEAD)


def _rope(x, cos, sin, first_half):
    out = []
    for c in range(x.shape[1] // LANE):
        xc = x[:, c * LANE:(c + 1) * LANE]
        partner = jnp.where(first_half, pltpu.roll(xc, LANE - 16, axis=1), pltpu.roll(xc, 16, axis=1))
        out.append(xc * cos + partner * sin)
    return jnp.concatenate(out, axis=1)


def _qkv_kernel(h_ref, wq_ref, wkv_ref, gq_ref, gk_ref, cos_ref, sin_ref, ones_ref,
                q_ref, katt_ref, vatt_ref, kn_ref, vf_ref):
    h = h_ref[...]
    ones_bd = ones_ref[...]
    cos = cos_ref[...]
    sin = sin_ref[...]
    lane = lax.broadcasted_iota(I32, (TM, LANE), 1)
    first_half = (lane & 16) == 0
    q = jnp.dot(h, wq_ref[...], preferred_element_type=F32)
    qn = q * lax.rsqrt(_head_mean_square(q, ones_bd) + RMS_EPS) * gq_ref[...]
    q_ref[...] = (_rope(qn, cos, sin, first_half) * (HEAD ** -0.5)).astype(BF16)
    kv = jnp.dot(h, wkv_ref[...], preferred_element_type=F32)
    k = kv[:, :D_KV]
    v = kv[:, D_KV:]
    kn = k * lax.rsqrt(_head_mean_square(k, ones_bd) + RMS_EPS) * gk_ref[...]
    kn_ref[...] = kn
    katt_ref[...] = _rope(kn, cos, sin, first_half).astype(BF16)
    vf_ref[...] = v
    vatt_ref[...] = v.astype(BF16)


def _proj_qkv(h, w_in, gq, gk, cos_tab, sin_tab, ones_bd, l):
    def tab_idx(m):
        return jnp.where(m < CTX_TILES, TILES_PER_LAT, (m - CTX_TILES) % TILES_PER_LAT)

    row = lambda m: (m, 0)
    return pl.pallas_call(
        _qkv_kernel,
        out_shape=(jax.ShapeDtypeStruct((T, D), BF16),
                   jax.ShapeDtypeStruct((T, D_KV), BF16),
                   jax.ShapeDtypeStruct((T, D_KV), BF16),
                   jax.ShapeDtypeStruct((T, D_KV), F32),
                   jax.ShapeDtypeStruct((T, D_KV), F32)),
        grid=(N_TILES,),
        in_specs=[
            pl.BlockSpec((TM, D), row),
            pl.BlockSpec((None, D, D), lambda m: (l, 0, COL_Q // 2)),
            pl.BlockSpec((None, D, 2 * D_KV), lambda m: (l, 0, COL_KV)),
            pl.BlockSpec((1, D), lambda m: (0, 0)),
            pl.BlockSpec((1, D_KV), lambda m: (0, 0)),
            pl.BlockSpec((TM, LANE), lambda m: (tab_idx(m), 0)),
            pl.BlockSpec((TM, LANE), lambda m: (tab_idx(m), 0)),
            pl.BlockSpec((LANE, LANE), lambda m: (0, 0)),
        ],
        out_specs=(pl.BlockSpec((TM, D), row), pl.BlockSpec((TM, D_KV), row),
                   pl.BlockSpec((TM, D_KV), row), pl.BlockSpec((TM, D_KV), row),
                   pl.BlockSpec((TM, D_KV), row)),
        compiler_params=_params(("parallel",)),
        name="proj_qkv",
    )(h, w_in, w_in, gq, gk, cos_tab, sin_tab, ones_bd)


def _rope_tables():
    lane = np.arange(LANE)
    j = lane % 16
    freqs = jnp.power(ROPE_THETA, -jnp.arange(16, dtype=F32) / 16)[j]
    pos = jnp.arange(LAT_LEN, dtype=I32)
    row = (pos // GRID_W).astype(F32)
    col = (pos % GRID_W).astype(F32)
    use_row = jnp.asarray((lane % HEAD) < HEAD // 2)
    p = jnp.where(use_row[None, :], row[:, None], col[:, None])
    ang = p * freqs[None, :]
    sign = jnp.asarray(np.where((lane & 16) == 0, -1.0, 1.0), F32)
    cos = jnp.concatenate([jnp.cos(ang), jnp.ones((TM, LANE), F32)], axis=0)
    sin = jnp.concatenate([jnp.sin(ang) * sign[None, :], jnp.zeros((TM, LANE), F32)], axis=0)
    return cos, sin


def _conv_kernel(acx_ref, acx_l_ref, acx_r_ref, ab_ref, glu_ref, glu_l_ref, glu_r_ref,
                 wa_ref, wb_ref, bb_ref, ng_ref, nb_ref, ya_ref, yb_ref, pad_a, pad_b, u_ref):
    s = pl.program_id(0)
    lat = s >= CTX_SEGS
    pos = (s - CTX_SEGS) % SEGS_PER_LAT
    has_left = jnp.logical_and(lat, pos != 0)
    has_right = jnp.logical_and(lat, pos != SEGS_PER_LAT - 1)

    pad_a[0:HALO_A, :] = jnp.where(has_left, acx_l_ref[...], 0.0)
    pad_a[HALO_A:HALO_A + SEG, :] = acx_ref[...]
    pad_a[HALO_A + SEG:, :] = jnp.where(has_right, acx_r_ref[...], 0.0)
    pad_b[0:HALO_B, :] = jnp.where(has_left, glu_l_ref[...], 0.0)
    pad_b[HALO_B:HALO_B + SEG, :] = glu_ref[...]
    pad_b[HALO_B + SEG:, :] = jnp.where(has_right, glu_r_ref[...], 0.0)

    conv_a = (wa_ref[0:1, :] * pad_a[HALO_A - 1:HALO_A - 1 + SEG, :]
              + wa_ref[1:2, :] * pad_a[HALO_A:HALO_A + SEG, :]
              + wa_ref[2:3, :] * pad_a[HALO_A + 1:HALO_A + 1 + SEG, :])
    ya_ref[...] = (ab_ref[...].astype(F32) * conv_a).astype(BF16)

    rows = 64
    kb = wb_ref.shape[0]

    def lane_chunk(c, carry):
        lanes = pl.ds(pl.multiple_of(c * LANE, LANE), LANE)
        for r in range(SEG // rows):
            acc = jnp.zeros((rows, LANE), F32)
            for k in range(kb):
                start = HALO_B + r * rows + k - kb // 2
                acc = acc + wb_ref[k:k + 1, lanes] * pad_b[start:start + rows, lanes]
            u_ref[r * rows:(r + 1) * rows, lanes] = acc
        return carry

    lax.fori_loop(0, D // LANE, lane_chunk, 0)
    u = _layer_norm(u_ref[...] + bb_ref[...], ng_ref[...], nb_ref[...])
    yb_ref[...] = (u * _sigmoid(u)).astype(BF16)


def _convs(acx, ab, glu, conv_a_w, conv_b_w, conv_b_b, norm_g, norm_b):
    seg = lambda s: (s, 0)
    const = lambda s: (0, 0)
    ra, rb = SEG // HALO_A, SEG // HALO_B
    left_a = lambda s: (jnp.maximum(s * ra - 1, 0), 0)
    right_a = lambda s: (jnp.minimum((s + 1) * ra, T // HALO_A - 1), 0)
    left_b = lambda s: (jnp.maximum(s * rb - 1, 0), 0)
    right_b = lambda s: (jnp.minimum((s + 1) * rb, T // HALO_B - 1), 0)
    return pl.pallas_call(
        _conv_kernel,
        out_shape=(jax.ShapeDtypeStruct((T, D), BF16), jax.ShapeDtypeStruct((T, D), BF16)),
        grid=(N_SEG,),
        in_specs=[
            pl.BlockSpec((SEG, D), seg), pl.BlockSpec((HALO_A, D), left_a), pl.BlockSpec((HALO_A, D), right_a),
            pl.BlockSpec((SEG, D), seg),
            pl.BlockSpec((SEG, D), seg), pl.BlockSpec((HALO_B, D), left_b), pl.BlockSpec((HALO_B, D), right_b),
            pl.BlockSpec(conv_a_w.shape, const), pl.BlockSpec(conv_b_w.shape, const),
            pl.BlockSpec((1, D), const), pl.BlockSpec((1, D), const), pl.BlockSpec((1, D), const),
        ],
        out_specs=(pl.BlockSpec((SEG, D), seg), pl.BlockSpec((SEG, D), seg)),
        scratch_shapes=[pltpu.VMEM((SEG + 2 * HALO_A, D), F32),
                        pltpu.VMEM((SEG + 2 * HALO_B, D), F32),
                        pltpu.VMEM((SEG, D), F32)],
        compiler_params=_params(("parallel",)),
        name="convs",
    )(acx, acx, acx, ab, glu, glu, glu, conv_a_w, conv_b_w, conv_b_b, norm_g, norm_b)


def _attend(q_ref, key_refs, val_refs, o_ref):
    nt = (((1,), (1,)), ((), ()))
    keys = [r[...].astype(BF16) for r in key_refs]
    vals = [r[...].astype(BF16) for r in val_refs]
    for g in range(N_KV):
        kg = [k[:, g * HEAD:(g + 1) * HEAD] for k in keys]
        vg = [v[:, g * HEAD:(g + 1) * HEAD] for v in vals]
        for hh in range(GROUP):
            hd = g * GROUP + hh
            qh = q_ref[:, hd * HEAD:(hd + 1) * HEAD]
            s = [lax.dot_general(qh, k, nt, preferred_element_type=F32) for k in kg]
            mx = functools.reduce(jnp.maximum, [jnp.max(x, axis=-1, keepdims=True) for x in s])
            p = [jnp.exp(x - mx) for x in s]
            den = functools.reduce(jnp.add, [jnp.sum(x, axis=-1, keepdims=True) for x in p])
            acc = functools.reduce(jnp.add, [jnp.dot(x.astype(BF16), v, preferred_element_type=F32)
                                             for x, v in zip(p, vg)])
            o_ref[:, hd * HEAD:(hd + 1) * HEAD] = (acc / den).astype(BF16)


def _attn_ctx_kernel(q_ref, k_ref, v_ref, o_ref):
    _attend(q_ref, [k_ref], [v_ref], o_ref)


def _attn_lat_kernel(q_ref, k_ref, v_ref, ck_ref, cv_ref, o_in_ref, o_ref):
    del o_in_ref
    _attend(q_ref, [ck_ref, k_ref], [cv_ref, v_ref], o_ref)


def _attention(q, katt, vatt, cache_k, cache_v, l):
    o = pl.pallas_call(
        _attn_ctx_kernel,
        out_shape=jax.ShapeDtypeStruct((T, D), BF16),
        grid=(N_CTX_SEQ,),
        in_specs=[pl.BlockSpec((CTX_LEN, D), lambda b: (b, 0)),
                  pl.BlockSpec((CTX_LEN, D_KV), lambda b: (b, 0)),
                  pl.BlockSpec((CTX_LEN, D_KV), lambda b: (b, 0))],
        out_specs=pl.BlockSpec((CTX_LEN, D), lambda b: (b, 0)),
        compiler_params=_params(("parallel",)),
        name="attn_ctx",
    )(q, katt, vatt)
    seg0 = CTX_SEGS
    lat0 = T_CTX // LAT_LEN
    return pl.pallas_call(
        _attn_lat_kernel,
        out_shape=jax.ShapeDtypeStruct((T, D), BF16),
        grid=(N_LAT_SEQ, SEGS_PER_LAT),
        in_specs=[pl.BlockSpec((SEG, D), lambda b, i: (seg0 + b * SEGS_PER_LAT + i, 0)),
                  pl.BlockSpec((LAT_LEN, D_KV), lambda b, i: (lat0 + b, 0)),
                  pl.BlockSpec((LAT_LEN, D_KV), lambda b, i: (lat0 + b, 0)),
                  pl.BlockSpec((None, None, PAST, D_KV), lambda b, i: (b, l, 0, 0)),
                  pl.BlockSpec((None, None, PAST, D_KV), lambda b, i: (b, l, 0, 0)),
                  pl.BlockSpec(memory_space=pl.ANY)],
        out_specs=pl.BlockSpec((SEG, D), lambda b, i: (seg0 + b * SEGS_PER_LAT + i, 0)),
        input_output_aliases={5: 0},
        compiler_params=_params(("parallel", "parallel")),
        name="attn_lat",
    )(q, katt, vatt, cache_k, cache_v, o)


def _post_kernel(ya_ref, yb_ref, o_ref, g_ref, x_ref, mod_ref, wa_ref, wb_ref, wc_ref, wo_ref,
                 lng_ref, lnb_ref, wr_hi_ref, wr_lo_ref, rb_ref,
                 x1_ref, h2_ref, route_ref, rw_ref, cnt_ref, carry_ref):
    m = pl.program_id(0)

    @pl.when(m == 0)
    def _():
        carry_ref[...] = jnp.zeros_like(carry_ref)

    ya = jnp.dot(ya_ref[...], wa_ref[...], preferred_element_type=F32)
    yb = jnp.dot(yb_ref[...], wb_ref[...], preferred_element_type=F32)
    yc = jnp.dot(o_ref[...], wc_ref[...], preferred_element_type=F32)
    merged = (g_ref[:, 0:D].astype(F32) * ya + g_ref[:, D:2 * D].astype(F32) * yb
              + g_ref[:, 2 * D:3 * D].astype(F32) * yc)
    mix = jnp.dot(merged.astype(BF16), wo_ref[...], preferred_element_type=F32)
    x1 = _layer_norm(ALPHA * x_ref[...] + mod_ref[2:3, :] * mix, lng_ref[...], lnb_ref[...])
    x1_ref[...] = x1
    h2 = x1 * (1.0 + mod_ref[4:5, :]) + mod_ref[3:4, :]
    h2_ref[...] = h2

    hi = h2.astype(BF16)
    lo = (h2 - hi.astype(F32)).astype(BF16)
    wr_hi = wr_hi_ref[...]
    logits = (jnp.dot(hi, wr_hi, preferred_element_type=F32)
              + jnp.dot(lo, wr_hi, preferred_element_type=F32)
              + jnp.dot(hi, wr_lo_ref[...], preferred_element_type=F32))
    scores = _sigmoid(logits.T[0:N_EXP, :])
    sel = scores + rb_ref[...]

    gscore = []
    for g in range(N_GRP):
        r = [sel[g * EXP_PER_GRP + j:g * EXP_PER_GRP + j + 1, :] for j in range(EXP_PER_GRP)]
        pairs = [r[a] + r[b] for a in range(EXP_PER_GRP) for b in range(a + 1, EXP_PER_GRP)]
        gscore.append(functools.reduce(jnp.maximum, pairs))
    best = functools.reduce(jnp.maximum, gscore)
    gsel = jnp.full(best.shape, N_GRP - 1, I32)
    for g in range(N_GRP - 2, -1, -1):
        gsel = jnp.where(gscore[g] == best, g, gsel)

    eidx = lax.broadcasted_iota(I32, (N_EXP, TM), 0)
    neg = jnp.float32(-jnp.inf)
    cand = jnp.where((eidx // EXP_PER_GRP) == gsel, sel, neg)
    top1 = jnp.max(cand, axis=0, keepdims=True)
    idx1 = jnp.min(jnp.where(cand == top1, eidx, N_EXP), axis=0, keepdims=True)
    cand2 = jnp.where(eidx == idx1, neg, cand)
    top2 = jnp.max(cand2, axis=0, keepdims=True)
    idx2 = jnp.min(jnp.where(cand2 == top2, eidx, N_EXP), axis=0, keepdims=True)
    is1 = eidx == idx1
    is2 = eidx == idx2
    w1 = jnp.sum(jnp.where(is1, scores, 0.0), axis=0, keepdims=True)
    w2 = jnp.sum(jnp.where(is2, scores, 0.0), axis=0, keepdims=True)
    wsum = w1 + w2
    w1 = w1 / wsum
    w2 = w2 / wsum

    onehot = jnp.where(jnp.logical_or(is1, is2), 1.0, 0.0)
    r_i = lax.broadcasted_iota(I32, (TM, TM), 0)
    c_i = lax.broadcasted_iota(I32, (TM, TM), 1)
    upper = jnp.where(r_i < c_i, 1.0, 0.0).astype(BF16)
    prefix = jnp.dot(onehot.astype(BF16), upper, preferred_element_type=F32) + carry_ref[:, 0:1]
    rank1 = jnp.sum(jnp.where(is1, prefix, 0.0), axis=0, keepdims=True)
    rank2 = jnp.sum(jnp.where(is2, prefix, 0.0), axis=0, keepdims=True)
    carry = carry_ref[...] + jnp.sum(onehot, axis=1, keepdims=True)
    carry_ref[...] = carry
    cnt_ref[...] = carry.astype(I32)

    route_ref[0:1, :] = idx1
    route_ref[1:2, :] = idx2
    route_ref[2:3, :] = rank1.astype(I32)
    route_ref[3:4, :] = rank2.astype(I32)
    route_ref[4:8, :] = jnp.zeros((4, TM), I32)
    wrow = lax.broadcasted_iota(I32, (LANE, TM), 0)
    wcols = jnp.where(wrow == 0, w1, jnp.where(wrow == 1, w2, 0.0))
    rw_ref[...] = wcols.T


def _post(ya_pre, yb_pre, o, gates, x, mod_l, wa, wb, wc, wo, ln_g, ln_b, wr_hi, wr_lo, rbias, l):
    row = lambda m: (m, 0)
    const = lambda m: (0, 0)
    wspec = pl.BlockSpec((None, D, D), lambda m: (l, 0, 0))
    return pl.pallas_call(
        _post_kernel,
        out_shape=(jax.ShapeDtypeStruct((T, D), F32),
                   jax.ShapeDtypeStruct((T, D), F32),
                   jax.ShapeDtypeStruct((N_TILES, 8, TM), I32),
                   jax.ShapeDtypeStruct((T, LANE), F32),
                   jax.ShapeDtypeStruct((N_EXP, LANE), I32)),
        grid=(N_TILES,),
        in_specs=[pl.BlockSpec((TM, D), row), pl.BlockSpec((TM, D), row), pl.BlockSpec((TM, D), row),
                  pl.BlockSpec((TM, 3 * D), row), pl.BlockSpec((TM, D), row),
                  pl.BlockSpec((None, N_MOD, D), lambda m: (_cond_row(m, CTX_TILES, TILES_PER_LAT), 0, 0)),
                  wspec, wspec, wspec, wspec,
                  pl.BlockSpec((1, D), const), pl.BlockSpec((1, D), const),
                  pl.BlockSpec((D, LANE), const), pl.BlockSpec((D, LANE), const),
                  pl.BlockSpec((N_EXP, 1), const)],
        out_specs=(pl.BlockSpec((TM, D), row), pl.BlockSpec((TM, D), row),
                   pl.BlockSpec((None, 8, TM), lambda m: (m, 0, 0)),
                   pl.BlockSpec((TM, LANE), row),
                   pl.BlockSpec((N_EXP, LANE), const)),
        scratch_shapes=[pltpu.VMEM((N_EXP, LANE), F32)],
        compiler_params=_params(("arbitrary",)),
        name="merge_ln_router",
    )(ya_pre, yb_pre, o, gates, x, mod_l, wa, wb, wc, wo, ln_g, ln_b, wr_hi, wr_lo, rbias)


def _row_copy(src_hbm, dst_hbm, src_row, dst_row, sem):
    return pltpu.make_async_copy(src_hbm.at[pl.ds(src_row, 1), :], dst_hbm.at[pl.ds(dst_row, 1), :], sem)


def _dispatch_kernel(route_ref, start_ref, last_ref, h2_hbm, zeros_hbm, xs_hbm, sem, zsem):
    m = pl.program_id(0)
    base = m * TM

    @pl.when(m == 0)
    def _():
        def pad_copy(e):
            row = pl.multiple_of(last_ref[e], FFN_BLK)
            return pltpu.make_async_copy(zeros_hbm, xs_hbm.at[pl.ds(row, FFN_BLK), :], zsem)

        def zero_one(e, carry):
            pad_copy(e).start()
            pad_copy(e).wait()
            return carry

        lax.fori_loop(0, N_EXP, zero_one, 0)

    def issue(t, carry):
        for k in range(2):
            slot = start_ref[route_ref[k, t]] + route_ref[2 + k, t]
            _row_copy(h2_hbm, xs_hbm, base + t, slot, sem).start()
        return carry

    lax.fori_loop(0, TM, issue, 0)

    def drain(t, carry):
        _row_copy(h2_hbm, xs_hbm, 0, 0, sem).wait()
        _row_copy(h2_hbm, xs_hbm, 0, 0, sem).wait()
        return carry

    lax.fori_loop(0, TM, drain, 0)


def _dispatch(route, seg_start, last_blk, h2, zeros_blk):
    return pl.pallas_call(
        _dispatch_kernel,
        out_shape=jax.ShapeDtypeStruct((N_SLOTS, D), F32),
        grid=(N_TILES,),
        in_specs=[pl.BlockSpec((None, 8, TM), lambda m: (m, 0, 0), memory_space=pltpu.SMEM),
                  pl.BlockSpec(memory_space=pltpu.SMEM),
                  pl.BlockSpec(memory_space=pltpu.SMEM),
                  pl.BlockSpec(memory_space=pl.ANY),
                  pl.BlockSpec(memory_space=pl.ANY)],
        out_specs=pl.BlockSpec(memory_space=pl.ANY),
        scratch_shapes=[pltpu.SemaphoreType.DMA, pltpu.SemaphoreType.DMA],
        compiler_params=pltpu.CompilerParams(dimension_semantics=("arbitrary",)),
        name="moe_dispatch",
    )(route, seg_start, last_blk, h2, zeros_blk)


def _ffn_kernel(blk_row_ref, blk_exp_ref, n_act_ref, x_ref, wgu_ref, wd_ref, y_ref):
    del blk_row_ref, blk_exp_ref

    @pl.when(pl.program_id(0) < n_act_ref[0])
    def _():
        gu = jnp.dot(x_ref[...].astype(BF16), wgu_ref[...], preferred_element_type=F32)
        gate = gu[:, :D_EXP]
        up = gu[:, D_EXP:]
        act = (gate * _sigmoid(gate) * up).astype(BF16)
        y_ref[...] = jnp.dot(act, wd_ref[...], preferred_element_type=F32)


def _expert_ffn(blk_row, blk_exp, n_act, xs, wgu, wd, l):
    grid_spec = pltpu.PrefetchScalarGridSpec(
        num_scalar_prefetch=3,
        grid=(N_FFN_BLOCKS,),
        in_specs=[pl.BlockSpec((FFN_BLK, D), lambda i, br, be, na: (br[i], 0)),
                  pl.BlockSpec((None, None, D, 2 * D_EXP), lambda i, br, be, na: (l, be[i], 0, 0)),
                  pl.BlockSpec((None, None, D_EXP, D), lambda i, br, be, na: (l, be[i], 0, 0))],
        out_specs=pl.BlockSpec((FFN_BLK, D), lambda i, br, be, na: (br[i], 0)),
    )
    return pl.pallas_call(
        _ffn_kernel,
        out_shape=jax.ShapeDtypeStruct((N_SLOTS, D), F32),
        grid_spec=grid_spec,
        compiler_params=_params(("arbitrary",)),
        name="expert_ffn",
    )(blk_row, blk_exp, n_act, xs, wgu, wd)


def _block_plan(counts):
    nblk = (counts + FFN_BLK - 1) // FFN_BLK
    end = jnp.cumsum(nblk)
    start = end - nblk
    n_act = end[-1]
    i = jnp.arange(N_FFN_BLOCKS, dtype=I32)
    i_eff = jnp.minimum(i, n_act - 1)
    e = jnp.minimum(jnp.searchsorted(end, i_eff, side='right'), N_EXP - 1).astype(I32)
    last_blk = jnp.minimum(start + jnp.maximum(nblk - 1, 0), N_FFN_BLOCKS - 1) * FFN_BLK
    return ((start * FFN_BLK).astype(I32), last_blk.astype(I32), i_eff.astype(I32), e,
            n_act.reshape(1).astype(I32))


def _combine_kernel(route_ref, start_ref, y_hbm, rw_ref, x1_ref, mod_ref, modn_ref, lng_ref, lnb_ref,
                    x2_ref, hn_ref, buf, sem):
    def issue(t, carry):
        for k in range(2):
            slot = start_ref[route_ref[k, t]] + route_ref[2 + k, t]
            _gather_copy(y_hbm, buf, slot, k, t, sem).start()
        return carry

    def drain(t, carry):
        _gather_copy(y_hbm, buf, 0, 0, 0, sem).wait()
        _gather_copy(y_hbm, buf, 0, 1, 0, sem).wait()
        return carry

    lax.fori_loop(0, TM, issue, 0)
    lax.fori_loop(0, TM, drain, 0)
    f = rw_ref[:, 0:1] * buf[0] + rw_ref[:, 1:2] * buf[1]
    x2 = _layer_norm(ALPHA * x1_ref[...] + mod_ref[5:6, :] * f, lng_ref[...], lnb_ref[...])
    x2_ref[...] = x2
    hn_ref[...] = (x2 * (1.0 + modn_ref[1:2, :]) + modn_ref[0:1, :]).astype(BF16)


def _gather_copy(y_hbm, buf, src_row, k, t, sem):
    return pltpu.make_async_copy(y_hbm.at[pl.ds(src_row, 1), :], buf.at[k, pl.ds(t, 1), :], sem)


def _combine(route, seg_start, y_slots, rw, x1, mod_l, mod_next, ln_g, ln_b):
    row = lambda m: (m, 0)
    const = lambda m: (0, 0)
    mod_spec = pl.BlockSpec((None, N_MOD, D), lambda m: (_cond_row(m, CTX_TILES, TILES_PER_LAT), 0, 0))
    return pl.pallas_call(
        _combine_kernel,
        out_shape=(jax.ShapeDtypeStruct((T, D), F32), jax.ShapeDtypeStruct((T, D), BF16)),
        grid=(N_TILES,),
        in_specs=[pl.BlockSpec((None, 8, TM), lambda m: (m, 0, 0), memory_space=pltpu.SMEM),
                  pl.BlockSpec(memory_space=pltpu.SMEM),
                  pl.BlockSpec(memory_space=pl.ANY),
                  pl.BlockSpec((TM, LANE), row), pl.BlockSpec((TM, D), row),
                  mod_spec, mod_spec,
                  pl.BlockSpec((1, D), const), pl.BlockSpec((1, D), const)],
        out_specs=(pl.BlockSpec((TM, D), row), pl.BlockSpec((TM, D), row)),
        scratch_shapes=[pltpu.VMEM((2, TM, D), F32), pltpu.SemaphoreType.DMA],
        compiler_params=_params(("arbitrary",)),
        name="moe_combine_ln",
    )(route, seg_start, y_slots, rw, x1, mod_l, mod_next, ln_g, ln_b)


def kernel(x_prompt, x_sample, c, c_ctx, cache_k, cache_v, w_in, conv_a_w, w_a_out, conv_b_w, conv_b_b,
           norm_b_g, norm_b_b, w_b_out, q_norm_g, k_norm_g, w_c_out, w_o, w_ada, b_ada, ln_g, ln_b,
           w_router, router_bias, w_gate_up, w_down):
    x = jnp.concatenate([x_prompt.reshape(T_CTX, D), x_sample.reshape(T_LAT, D)], axis=0)
    cond16 = jnp.concatenate([c_ctx[None, :], c, jnp.zeros((16 - 1 - N_LAT_SEQ, D), F32)], axis=0)
    mod = _modulation(cond16, w_ada, b_ada).reshape(DEPTH, 16, N_MOD, D)

    w_in_b = w_in.astype(BF16)
    wa_b, wb_b, wc_b, wo_b = (w.astype(BF16) for w in (w_a_out, w_b_out, w_c_out, w_o))
    wgu_b = w_gate_up.astype(BF16)
    wd_b = w_down.astype(BF16)
    wr = jnp.pad(w_router, ((0, 0), (0, LANE - N_EXP)))
    wr_hi = wr.astype(BF16)
    wr_lo = (wr - wr_hi.astype(F32)).astype(BF16)
    rbias = router_bias.reshape(N_EXP, 1)

    cos_tab, sin_tab = _rope_tables()
    half = np.arange(LANE) // HEAD
    ones_bd = jnp.asarray(half[:, None] == half[None, :], BF16)
    zeros_blk = jnp.zeros((FFN_BLK, D), F32)
    ck = cache_k.reshape(N_LAT_SEQ, DEPTH, PAST, D_KV)
    cv = cache_v.reshape(N_LAT_SEQ, DEPTH, PAST, D_KV)

    h = _modulate(x, mod[0])
    new_k, new_v = [], []
    for l in range(DEPTH):
        ab, acx = _proj_a(h, w_in_b, l)
        glu = _proj_b(h, w_in_b, l)
        gates = _proj_gates(h, w_in_b, l)
        q, katt, vatt, kn, vf = _proj_qkv(h, w_in_b, jnp.tile(q_norm_g[l], N_Q)[None, :],
                                          jnp.tile(k_norm_g[l], N_KV)[None, :], cos_tab, sin_tab, ones_bd, l)
        new_k.append(kn[:T_CTX].reshape(N_CTX_SEQ, CTX_LEN, N_KV, HEAD))
        new_v.append(vf[:T_CTX].reshape(N_CTX_SEQ, CTX_LEN, N_KV, HEAD))
        ya_pre, yb_pre = _convs(acx, ab, glu, conv_a_w[l], conv_b_w[l], conv_b_b[l][None, :],
                                norm_b_g[l][None, :], norm_b_b[l][None, :])
        o = _attention(q, katt, vatt, ck, cv, l)
        x1, h2, route, rw, counts = _post(ya_pre, yb_pre, o, gates, x, mod[l], wa_b, wb_b, wc_b, wo_b,
                                          ln_g[l, 0][None, :], ln_b[l, 0][None, :], wr_hi, wr_lo, rbias, l)
        cnt = counts[:, 0]
        seg_start, last_blk, blk_row, blk_exp, n_act = _block_plan(cnt)
        xs = _dispatch(route, seg_start, last_blk, h2, zeros_blk)
        y_slots = _expert_ffn(blk_row, blk_exp, n_act, xs, wgu_b, wd_b, l)
        x, h = _combine(route, seg_start, y_slots, rw, x1, mod[l], mod[min(l + 1, DEPTH - 1)],
                        ln_g[l, 1][None, :], ln_b[l, 1][None, :])
    y_prompt = x[:T_CTX].reshape(N_CTX_SEQ, CTX_LEN, D)
    y_sample = x[T_CTX:].reshape(N_LAT_SEQ, LAT_LEN, D)
    return y_prompt, y_sample, jnp.stack(new_k, axis=1), jnp.stack(new_v, axis=1)
```

```python
import functools

import numpy as np
import jax
import jax.numpy as jnp
from jax import lax
from jax.experimental import pallas as pl
from jax.experimental.pallas import tpu as pltpu

F32 = jnp.float32
BF16 = jnp.bfloat16
I32 = jnp.int32

D = 1024
DEPTH = 4
N_CTX_SEQ = 16
CTX_LEN = 256
N_LAT_SEQ = 8
LAT_LEN = 1024
PAST = 512
T_CTX = N_CTX_SEQ * CTX_LEN
T_LAT = N_LAT_SEQ * LAT_LEN
T = T_CTX + T_LAT
GRID_W = 64
HEAD = 64
N_Q = 16
N_KV = 4
GROUP = 4
D_KV = N_KV * HEAD
N_EXP = 16
N_GRP = 4
EXP_PER_GRP = 4
D_EXP = 512
IN_COLS = 9728
N_MOD = 6
ALPHA = (2 * DEPTH) ** 0.25
LN_EPS = 1e-5
RMS_EPS = 1e-6
ROPE_THETA = 10000.0

LANE = 128
TM = 512
N_TILES = T // TM
CTX_TILES = T_CTX // TM
TILES_PER_LAT = LAT_LEN // TM
TMP = 1024
SEG = 256
N_SEG = T // SEG
CTX_SEGS = T_CTX // SEG
SEGS_PER_LAT = LAT_LEN // SEG
HALO_A = 8
HALO_B = 16
SH_ROWS = SEG + 2 * HALO_B - 8
FFN_BLK = 256
N_FFN_BLOCKS = (2 * T) // FFN_BLK + N_EXP
N_SLOTS = N_FFN_BLOCKS * FFN_BLK
VMEM_LIMIT = 56 * 1024 * 1024

COL_AB, COL_AC, COL_AX, COL_BU, COL_BG, COL_Q, COL_KV, COL_GATES = 0, 2, 4, 6, 8, 10, 12, 13


def _params(sem):
    return pltpu.CompilerParams(dimension_semantics=sem, vmem_limit_bytes=VMEM_LIMIT)


def _cond_row(m, tiles_ctx, tiles_per_lat):
    return jnp.where(m < tiles_ctx, 0, 1 + (m - tiles_ctx) // tiles_per_lat)


def _layer_norm(x, g, b):
    mu = jnp.mean(x, axis=-1, keepdims=True)
    xc = x - mu
    var = jnp.mean(xc * xc, axis=-1, keepdims=True)
    return xc * lax.rsqrt(var + LN_EPS) * g + b


def _sigmoid(x):
    return 1.0 / (1.0 + jnp.exp(-x))


def _mod_kernel(cond_ref, w_ref, b_ref, o_ref):
    cnd = cond_ref[...]
    s = (cnd * _sigmoid(cnd)).astype(BF16)
    o_ref[...] = jnp.dot(s, w_ref[...].astype(BF16), preferred_element_type=F32) + b_ref[...]


def _modulation(cond16, w_ada, b_ada):
    n_col = N_MOD * D
    tn = 1024
    return pl.pallas_call(
        _mod_kernel,
        out_shape=jax.ShapeDtypeStruct((DEPTH, 16, n_col), F32),
        grid=(DEPTH, n_col // tn),
        in_specs=[
            pl.BlockSpec((16, D), lambda l, n: (0, 0)),
            pl.BlockSpec((None, D, tn), lambda l, n: (l, 0, n)),
            pl.BlockSpec((None, 1, tn), lambda l, n: (l, 0, n)),
        ],
        out_specs=pl.BlockSpec((None, 16, tn), lambda l, n: (l, 0, n)),
        compiler_params=_params(("parallel", "parallel")),
        name="adaln_mod",
    )(cond16, w_ada, b_ada.reshape(DEPTH, 1, n_col))


def _modulate_kernel(x_ref, mod_ref, h_ref):
    h_ref[...] = (x_ref[...] * (1.0 + mod_ref[1:2, :]) + mod_ref[0:1, :]).astype(BF16)


def _modulate(x, mod_l):
    return pl.pallas_call(
        _modulate_kernel,
        out_shape=jax.ShapeDtypeStruct((T, D), BF16),
        grid=(N_TILES,),
        in_specs=[
            pl.BlockSpec((TM, D), lambda m: (m, 0)),
            pl.BlockSpec((None, N_MOD, D), lambda m: (_cond_row(m, CTX_TILES, TILES_PER_LAT), 0, 0)),
        ],
        out_specs=pl.BlockSpec((TM, D), lambda m: (m, 0)),
        compiler_params=_params(("parallel",)),
        name="modulate",
    )(x, mod_l)


def _proj_a_kernel(h_ref, wb_ref, wc_ref, wx_ref, ab_ref, acx_ref):
    h = h_ref[...]
    ab_ref[...] = jnp.dot(h, wb_ref[...], preferred_element_type=F32).astype(BF16)
    acx_ref[...] = (jnp.dot(h, wc_ref[...], preferred_element_type=F32)
                    * jnp.dot(h, wx_ref[...], preferred_element_type=F32))


def _proj_b_kernel(h_ref, wu_ref, wg_ref, glu_ref):
    h = h_ref[...]
    glu_ref[...] = (jnp.dot(h, wu_ref[...], preferred_element_type=F32)
                    * _sigmoid(jnp.dot(h, wg_ref[...], preferred_element_type=F32)))


def _proj_gate_kernel(h_ref, w_ref, g_ref):
    g_ref[...] = _sigmoid(jnp.dot(h_ref[...], w_ref[...], preferred_element_type=F32)).astype(BF16)


def _w_spec(l, col0, tn):
    return pl.BlockSpec((None, D, tn), lambda c, m: (l, 0, col0 + c))


def _proj_a(h, w_in, l):
    tn = 512
    out_spec = pl.BlockSpec((TMP, tn), lambda c, m: (m, c))
    return pl.pallas_call(
        _proj_a_kernel,
        out_shape=(jax.ShapeDtypeStruct((T, D), BF16), jax.ShapeDtypeStruct((T, D), F32)),
        grid=(D // tn, T // TMP),
        in_specs=[pl.BlockSpec((TMP, D), lambda c, m: (m, 0)),
                  _w_spec(l, COL_AB, tn), _w_spec(l, COL_AC, tn), _w_spec(l, COL_AX, tn)],
        out_specs=(out_spec, out_spec),
        compiler_params=_params(("parallel", "parallel")),
        name="proj_a",
    )(h, w_in, w_in, w_in)


def _proj_b(h, w_in, l):
    tn = 1024
    return pl.pallas_call(
        _proj_b_kernel,
        out_shape=jax.ShapeDtypeStruct((T, D), F32),
        grid=(D // tn, T // TMP),
        in_specs=[pl.BlockSpec((TMP, D), lambda c, m: (m, 0)),
                  _w_spec(l, COL_BU * 512 // tn, tn), _w_spec(l, COL_BG * 512 // tn, tn)],
        out_specs=pl.BlockSpec((TMP, tn), lambda c, m: (m, c)),
        compiler_params=_params(("parallel", "parallel")),
        name="proj_b",
    )(h, w_in, w_in)


def _proj_gates(h, w_gates, l):
    tn = 1024
    return pl.pallas_call(
        _proj_gate_kernel,
        out_shape=jax.ShapeDtypeStruct((T, 3 * D), BF16),
        grid=(3 * D // tn, T // TMP),
        in_specs=[pl.BlockSpec((TMP, D), lambda c, m: (m, 0)),
                  pl.BlockSpec((None, D, tn), lambda c, m: (l, 0, c))],
        out_specs=pl.BlockSpec((TMP, tn), lambda c, m: (m, c)),
        compiler_params=_params(("parallel", "parallel")),
        name="proj_gates",
    )(h, w_gates)


def _head_mean_square(x, ones_bd):
    out = []
    for c in range(x.shape[1] // LANE):
        sq = x[:, c * LANE:(c + 1) * LANE]
        sq = sq * sq
        hi = sq.astype(BF16)
        lo = (sq - hi.astype(F32)).astype(BF16)
        out.append(jnp.dot(hi, ones_bd, preferred_element_type=F32)
                   + jnp.dot(lo, ones_bd, preferred_element_type=F32))
    return jnp.concatenate(out, axis=1) * (1.0 / HEAD)


def _rope(x, cos, sin, first_half):
    out = []
    for c in range(x.shape[1] // LANE):
        xc = x[:, c * LANE:(c + 1) * LANE]
        partner = jnp.where(first_half, pltpu.roll(xc, LANE - 16, axis=1), pltpu.roll(xc, 16, axis=1))
        out.append(xc * cos + partner * sin)
    return jnp.concatenate(out, axis=1)


def _qkv_kernel(h_ref, wq_ref, wkv_ref, gq_ref, gk_ref, cos_ref, sin_ref, ones_ref,
                q_ref, katt_ref, vatt_ref, kn_ref, vf_ref):
    h = h_ref[...]
    ones_bd = ones_ref[...]
    cos = cos_ref[...]
    sin = sin_ref[...]
    lane = lax.broadcasted_iota(I32, (TM, LANE), 1)
    first_half = (lane & 16) == 0
    q = jnp.dot(h, wq_ref[...], preferred_element_type=F32)
    qn = q * lax.rsqrt(_head_mean_square(q, ones_bd) + RMS_EPS) * gq_ref[...]
    q_ref[...] = (_rope(qn, cos, sin, first_half) * (HEAD ** -0.5)).astype(BF16)
    kv = jnp.dot(h, wkv_ref[...], preferred_element_type=F32)
    k = kv[:, :D_KV]
    v = kv[:, D_KV:]
    kn = k * lax.rsqrt(_head_mean_square(k, ones_bd) + RMS_EPS) * gk_ref[...]
    kn_ref[...] = kn
    katt_ref[...] = _rope(kn, cos, sin, first_half).astype(BF16)
    vf_ref[...] = v
    vatt_ref[...] = v.astype(BF16)


def _proj_qkv(h, w_in, gq, gk, cos_tab, sin_tab, ones_bd, l):
    def tab_idx(m):
        return jnp.where(m < CTX_TILES, TILES_PER_LAT, (m - CTX_TILES) % TILES_PER_LAT)

    row = lambda m: (m, 0)
    return pl.pallas_call(
        _qkv_kernel,
        out_shape=(jax.ShapeDtypeStruct((T, D), BF16),
                   jax.ShapeDtypeStruct((T, D_KV), BF16),
                   jax.ShapeDtypeStruct((T, D_KV), BF16),
                   jax.ShapeDtypeStruct((T, D_KV), F32),
                   jax.ShapeDtypeStruct((T, D_KV), F32)),
        grid=(N_TILES,),
        in_specs=[
            pl.BlockSpec((TM, D), row),
            pl.BlockSpec((None, D, D), lambda m: (l, 0, COL_Q // 2)),
            pl.BlockSpec((None, D, 2 * D_KV), lambda m: (l, 0, COL_KV)),
            pl.BlockSpec((1, D), lambda m: (0, 0)),
            pl.BlockSpec((1, D_KV), lambda m: (0, 0)),
            pl.BlockSpec((TM, LANE), lambda m: (tab_idx(m), 0)),
            pl.BlockSpec((TM, LANE), lambda m: (tab_idx(m), 0)),
            pl.BlockSpec((LANE, LANE), lambda m: (0, 0)),
        ],
        out_specs=(pl.BlockSpec((TM, D), row), pl.BlockSpec((TM, D_KV), row),
                   pl.BlockSpec((TM, D_KV), row), pl.BlockSpec((TM, D_KV), row),
                   pl.BlockSpec((TM, D_KV), row)),
        compiler_params=_params(("parallel",)),
        name="proj_qkv",
    )(h, w_in, w_in, gq, gk, cos_tab, sin_tab, ones_bd)


def _rope_tables():
    lane = np.arange(LANE)
    j = lane % 16
    freqs = jnp.power(ROPE_THETA, -jnp.arange(16, dtype=F32) / 16)[j]
    pos = jnp.arange(LAT_LEN, dtype=I32)
    row = (pos // GRID_W).astype(F32)
    col = (pos % GRID_W).astype(F32)
    use_row = jnp.asarray((lane % HEAD) < HEAD // 2)
    p = jnp.where(use_row[None, :], row[:, None], col[:, None])
    ang = p * freqs[None, :]
    sign = jnp.asarray(np.where((lane & 16) == 0, -1.0, 1.0), F32)
    cos = jnp.concatenate([jnp.cos(ang), jnp.ones((TM, LANE), F32)], axis=0)
    sin = jnp.concatenate([jnp.sin(ang) * sign[None, :], jnp.zeros((TM, LANE), F32)], axis=0)
    return cos, sin


def _conv_kernel(acx_ref, acx_l_ref, acx_r_ref, ab_ref, glu_ref, glu_l_ref, glu_r_ref,
                 wa_ref, wb_ref, bb_ref, ng_ref, nb_ref, ya_ref, yb_ref, pad_a, pad_b, u_ref, sh_ref):
    s = pl.program_id(0)
    lat = s >= CTX_SEGS
    pos = (s - CTX_SEGS) % SEGS_PER_LAT
    has_left = jnp.logical_and(lat, pos != 0)
    has_right = jnp.logical_and(lat, pos != SEGS_PER_LAT - 1)

    pad_a[0:HALO_A, :] = jnp.where(has_left, acx_l_ref[...], 0.0)
    pad_a[HALO_A:HALO_A + SEG, :] = acx_ref[...]
    pad_a[HALO_A + SEG:, :] = jnp.where(has_right, acx_r_ref[...], 0.0)
    pad_b[0:HALO_B, :] = jnp.where(has_left, glu_l_ref[...], 0.0)
    pad_b[HALO_B:HALO_B + SEG, :] = glu_ref[...]
    pad_b[HALO_B + SEG:, :] = jnp.where(has_right, glu_r_ref[...], 0.0)

    conv_a = (wa_ref[0:1, :] * pad_a[HALO_A - 1:HALO_A - 1 + SEG, :]
              + wa_ref[1:2, :] * pad_a[HALO_A:HALO_A + SEG, :]
              + wa_ref[2:3, :] * pad_a[HALO_A + 1:HALO_A + 1 + SEG, :])
    ya_ref[...] = (ab_ref[...].astype(F32) * conv_a).astype(BF16)

    rows = 64
    kb = wb_ref.shape[0]

    def lane_chunk(c, carry):
        lanes = pl.ds(pl.multiple_of(c * LANE, LANE), LANE)
        for b in range(1, 8):
            sh_ref[b, :, :] = pad_b[b:b + SH_ROWS, lanes]
        for r in range(SEG // rows):
            acc = jnp.zeros((rows, LANE), F32)
            for k in range(kb):
                off = HALO_B + k - kb // 2
                row0 = r * rows + 8 * (off // 8)
                if off % 8 == 0:
                    src = pad_b[row0:row0 + rows, lanes]
                else:
                    src = sh_ref[off % 8, row0:row0 + rows, :]
                acc = acc + wb_ref[k:k + 1, lanes] * src
            u_ref[r * rows:(r + 1) * rows, lanes] = acc
        return carry

    lax.fori_loop(0, D // LANE, lane_chunk, 0)
    u = _layer_norm(u_ref[...] + bb_ref[...], ng_ref[...], nb_ref[...])
    yb_ref[...] = (u * _sigmoid(u)).astype(BF16)


def _convs(acx, ab, glu, conv_a_w, conv_b_w, conv_b_b, norm_g, norm_b):
    seg = lambda s: (s, 0)
    const = lambda s: (0, 0)
    ra, rb = SEG // HALO_A, SEG // HALO_B
    left_a = lambda s: (jnp.maximum(s * ra - 1, 0), 0)
    right_a = lambda s: (jnp.minimum((s + 1) * ra, T // HALO_A - 1), 0)
    left_b = lambda s: (jnp.maximum(s * rb - 1, 0), 0)
    right_b = lambda s: (jnp.minimum((s + 1) * rb, T // HALO_B - 1), 0)
    return pl.pallas_call(
        _conv_kernel,
        out_shape=(jax.ShapeDtypeStruct((T, D), BF16), jax.ShapeDtypeStruct((T, D), BF16)),
        grid=(N_SEG,),
        in_specs=[
            pl.BlockSpec((SEG, D), seg), pl.BlockSpec((HALO_A, D), left_a), pl.BlockSpec((HALO_A, D), right_a),
            pl.BlockSpec((SEG, D), seg),
            pl.BlockSpec((SEG, D), seg), pl.BlockSpec((HALO_B, D), left_b), pl.BlockSpec((HALO_B, D), right_b),
            pl.BlockSpec(conv_a_w.shape, const), pl.BlockSpec(conv_b_w.shape, const),
            pl.BlockSpec((1, D), const), pl.BlockSpec((1, D), const), pl.BlockSpec((1, D), const),
        ],
        out_specs=(pl.BlockSpec((SEG, D), seg), pl.BlockSpec((SEG, D), seg)),
        scratch_shapes=[pltpu.VMEM((SEG + 2 * HALO_A, D), F32),
                        pltpu.VMEM((SEG + 2 * HALO_B, D), F32),
                        pltpu.VMEM((SEG, D), F32),
                        pltpu.VMEM((8, SH_ROWS, LANE), F32)],
        compiler_params=_params(("parallel",)),
        name="convs",
    )(acx, acx, acx, ab, glu, glu, glu, conv_a_w, conv_b_w, conv_b_b, norm_g, norm_b)


def _attend(q_ref, key_refs, val_refs, o_ref):
    nt = (((1,), (1,)), ((), ()))
    keys = [r[...].astype(BF16) for r in key_refs]
    vals = [r[...].astype(BF16) for r in val_refs]
    for g in range(N_KV):
        kg = [k[:, g * HEAD:(g + 1) * HEAD] for k in keys]
        vg = [v[:, g * HEAD:(g + 1) * HEAD] for v in vals]
        for hh in range(GROUP):
            hd = g * GROUP + hh
            qh = q_ref[:, hd * HEAD:(hd + 1) * HEAD]
            s = [lax.dot_general(qh, k, nt, preferred_element_type=F32) for k in kg]
            mx = functools.reduce(jnp.maximum, [jnp.max(x, axis=-1, keepdims=True) for x in s])
            p = [jnp.exp(x - mx) for x in s]
            den = functools.reduce(jnp.add, [jnp.sum(x, axis=-1, keepdims=True) for x in p])
            acc = functools.reduce(jnp.add, [jnp.dot(x.astype(BF16), v, preferred_element_type=F32)
                                             for x, v in zip(p, vg)])
            o_ref[:, hd * HEAD:(hd + 1) * HEAD] = (acc / den).astype(BF16)


def _attn_ctx_kernel(q_ref, k_ref, v_ref, o_ref):
    _attend(q_ref, [k_ref], [v_ref], o_ref)


def _attn_lat_kernel(q_ref, k_ref, v_ref, ck_ref, cv_ref, o_in_ref, o_ref):
    del o_in_ref
    _attend(q_ref, [ck_ref, k_ref], [cv_ref, v_ref], o_ref)


def _attention(q, katt, vatt, cache_k, cache_v, l):
    o = pl.pallas_call(
        _attn_ctx_kernel,
        out_shape=jax.ShapeDtypeStruct((T, D), BF16),
        grid=(N_CTX_SEQ,),
        in_specs=[pl.BlockSpec((CTX_LEN, D), lambda b: (b, 0)),
                  pl.BlockSpec((CTX_LEN, D_KV), lambda b: (b, 0)),
                  pl.BlockSpec((CTX_LEN, D_KV), lambda b: (b, 0))],
        out_specs=pl.BlockSpec((CTX_LEN, D), lambda b: (b, 0)),
        compiler_params=_params(("parallel",)),
        name="attn_ctx",
    )(q, katt, vatt)
    seg0 = CTX_SEGS
    lat0 = T_CTX // LAT_LEN
    return pl.pallas_call(
        _attn_lat_kernel,
        out_shape=jax.ShapeDtypeStruct((T, D), BF16),
        grid=(N_LAT_SEQ, SEGS_PER_LAT),
        in_specs=[pl.BlockSpec((SEG, D), lambda b, i: (seg0 + b * SEGS_PER_LAT + i, 0)),
                  pl.BlockSpec((LAT_LEN, D_KV), lambda b, i: (lat0 + b, 0)),
                  pl.BlockSpec((LAT_LEN, D_KV), lambda b, i: (lat0 + b, 0)),
                  pl.BlockSpec((None, None, PAST, D_KV), lambda b, i: (b, l, 0, 0)),
                  pl.BlockSpec((None, None, PAST, D_KV), lambda b, i: (b, l, 0, 0)),
                  pl.BlockSpec(memory_space=pl.ANY)],
        out_specs=pl.BlockSpec((SEG, D), lambda b, i: (seg0 + b * SEGS_PER_LAT + i, 0)),
        input_output_aliases={5: 0},
        compiler_params=_params(("parallel", "parallel")),
        name="attn_lat",
    )(q, katt, vatt, cache_k, cache_v, o)


def _post_kernel(ya_ref, yb_ref, o_ref, g_ref, x_ref, mod_ref, wa_ref, wb_ref, wc_ref, wo_ref,
                 lng_ref, lnb_ref, wr_hi_ref, wr_lo_ref, rb_ref,
                 x1_ref, h2_ref, route_ref, rw_ref, cnt_ref, carry_ref):
    m = pl.program_id(0)

    @pl.when(m == 0)
    def _():
        carry_ref[...] = jnp.zeros_like(carry_ref)

    ya = jnp.dot(ya_ref[...], wa_ref[...], preferred_element_type=F32)
    yb = jnp.dot(yb_ref[...], wb_ref[...], preferred_element_type=F32)
    yc = jnp.dot(o_ref[...], wc_ref[...], preferred_element_type=F32)
    merged = (g_ref[:, 0:D].astype(F32) * ya + g_ref[:, D:2 * D].astype(F32) * yb
              + g_ref[:, 2 * D:3 * D].astype(F32) * yc)
    mix = jnp.dot(merged.astype(BF16), wo_ref[...], preferred_element_type=F32)
    x1 = _layer_norm(ALPHA * x_ref[...] + mod_ref[2:3, :] * mix, lng_ref[...], lnb_ref[...])
    x1_ref[...] = x1
    h2 = x1 * (1.0 + mod_ref[4:5, :]) + mod_ref[3:4, :]
    h2_ref[...] = h2

    hi = h2.astype(BF16)
    lo = (h2 - hi.astype(F32)).astype(BF16)
    wr_hi = wr_hi_ref[...]
    logits = (jnp.dot(hi, wr_hi, preferred_element_type=F32)
              + jnp.dot(lo, wr_hi, preferred_element_type=F32)
              + jnp.dot(hi, wr_lo_ref[...], preferred_element_type=F32))
    scores = _sigmoid(logits.T[0:N_EXP, :])
    sel = scores + rb_ref[...]

    gscore = []
    for g in range(N_GRP):
        r = [sel[g * EXP_PER_GRP + j:g * EXP_PER_GRP + j + 1, :] for j in range(EXP_PER_GRP)]
        pairs = [r[a] + r[b] for a in range(EXP_PER_GRP) for b in range(a + 1, EXP_PER_GRP)]
        gscore.append(functools.reduce(jnp.maximum, pairs))
    best = functools.reduce(jnp.maximum, gscore)
    gsel = jnp.full(best.shape, N_GRP - 1, I32)
    for g in range(N_GRP - 2, -1, -1):
        gsel = jnp.where(gscore[g] == best, g, gsel)

    eidx = lax.broadcasted_iota(I32, (N_EXP, TM), 0)
    neg = jnp.float32(-jnp.inf)
    cand = jnp.where((eidx // EXP_PER_GRP) == gsel, sel, neg)
    top1 = jnp.max(cand, axis=0, keepdims=True)
    idx1 = jnp.min(jnp.where(cand == top1, eidx, N_EXP), axis=0, keepdims=True)
    cand2 = jnp.where(eidx == idx1, neg, cand)
    top2 = jnp.max(cand2, axis=0, keepdims=True)
    idx2 = jnp.min(jnp.where(cand2 == top2, eidx, N_EXP), axis=0, keepdims=True)
    is1 = eidx == idx1
    is2 = eidx == idx2
    w1 = jnp.sum(jnp.where(is1, scores, 0.0), axis=0, keepdims=True)
    w2 = jnp.sum(jnp.where(is2, scores, 0.0), axis=0, keepdims=True)
    wsum = w1 + w2
    w1 = w1 / wsum
    w2 = w2 / wsum

    onehot = jnp.where(jnp.logical_or(is1, is2), 1.0, 0.0)
    r_i = lax.broadcasted_iota(I32, (TM, TM), 0)
    c_i = lax.broadcasted_iota(I32, (TM, TM), 1)
    upper = jnp.where(r_i < c_i, 1.0, 0.0).astype(BF16)
    prefix = jnp.dot(onehot.astype(BF16), upper, preferred_element_type=F32) + carry_ref[:, 0:1]
    rank1 = jnp.sum(jnp.where(is1, prefix, 0.0), axis=0, keepdims=True)
    rank2 = jnp.sum(jnp.where(is2, prefix, 0.0), axis=0, keepdims=True)
    carry = carry_ref[...] + jnp.sum(onehot, axis=1, keepdims=True)
    carry_ref[...] = carry
    cnt_ref[...] = carry.astype(I32)

    route_ref[0:1, :] = idx1
    route_ref[1:2, :] = idx2
    route_ref[2:3, :] = rank1.astype(I32)
    route_ref[3:4, :] = rank2.astype(I32)
    route_ref[4:8, :] = jnp.zeros((4, TM), I32)
    wrow = lax.broadcasted_iota(I32, (LANE, TM), 0)
    wcols = jnp.where(wrow == 0, w1, jnp.where(wrow == 1, w2, 0.0))
    rw_ref[...] = wcols.T


def _post(ya_pre, yb_pre, o, gates, x, mod_l, wa, wb, wc, wo, ln_g, ln_b, wr_hi, wr_lo, rbias, l):
    row = lambda m: (m, 0)
    const = lambda m: (0, 0)
    wspec = pl.BlockSpec((None, D, D), lambda m: (l, 0, 0))
    return pl.pallas_call(
        _post_kernel,
        out_shape=(jax.ShapeDtypeStruct((T, D), F32),
                   jax.ShapeDtypeStruct((T, D), F32),
                   jax.ShapeDtypeStruct((N_TILES, 8, TM), I32),
                   jax.ShapeDtypeStruct((T, LANE), F32),
                   jax.ShapeDtypeStruct((N_EXP, LANE), I32)),
        grid=(N_TILES,),
        in_specs=[pl.BlockSpec((TM, D), row), pl.BlockSpec((TM, D), row), pl.BlockSpec((TM, D), row),
                  pl.BlockSpec((TM, 3 * D), row), pl.BlockSpec((TM, D), row),
                  pl.BlockSpec((None, N_MOD, D), lambda m: (_cond_row(m, CTX_TILES, TILES_PER_LAT), 0, 0)),
                  wspec, wspec, wspec, wspec,
                  pl.BlockSpec((1, D), const), pl.BlockSpec((1, D), const),
                  pl.BlockSpec((D, LANE), const), pl.BlockSpec((D, LANE), const),
                  pl.BlockSpec((N_EXP, 1), const)],
        out_specs=(pl.BlockSpec((TM, D), row), pl.BlockSpec((TM, D), row),
                   pl.BlockSpec((None, 8, TM), lambda m: (m, 0, 0)),
                   pl.BlockSpec((TM, LANE), row),
                   pl.BlockSpec((N_EXP, LANE), const)),
        scratch_shapes=[pltpu.VMEM((N_EXP, LANE), F32)],
        compiler_params=_params(("arbitrary",)),
        name="merge_ln_router",
    )(ya_pre, yb_pre, o, gates, x, mod_l, wa, wb, wc, wo, ln_g, ln_b, wr_hi, wr_lo, rbias)


def _row_copy(src, dst_hbm, src_row, dst_row, sem):
    return pltpu.make_async_copy(src.at[pl.ds(src_row, 1), :], dst_hbm.at[pl.ds(dst_row, 1), :], sem)


DMA_UNROLL = 4


def _dispatch_kernel(route_ref, start_ref, last_ref, h2_ref, zeros_hbm, xs_hbm, sem, zsem):
    m = pl.program_id(0)

    @pl.when(m == 0)
    def _():
        def pad_copy(e):
            row = pl.multiple_of(last_ref[e], FFN_BLK)
            return pltpu.make_async_copy(zeros_hbm, xs_hbm.at[pl.ds(row, FFN_BLK), :], zsem)

        def zero_one(e, carry):
            pad_copy(e).start()
            pad_copy(e).wait()
            return carry

        lax.fori_loop(0, N_EXP, zero_one, 0)

    def issue(i, carry):
        for u in range(DMA_UNROLL):
            t = i * DMA_UNROLL + u
            for k in range(2):
                slot = start_ref[route_ref[k, t]] + route_ref[2 + k, t]
                _row_copy(h2_ref, xs_hbm, t, slot, sem).start()
        return carry

    lax.fori_loop(0, TM // DMA_UNROLL, issue, 0)

    def drain(i, carry):
        for _ in range(2 * DMA_UNROLL):
            _row_copy(h2_ref, xs_hbm, 0, 0, sem).wait()
        return carry

    lax.fori_loop(0, TM // DMA_UNROLL, drain, 0)


def _dispatch(route, seg_start, last_blk, h2, zeros_blk):
    return pl.pallas_call(
        _dispatch_kernel,
        out_shape=jax.ShapeDtypeStruct((N_SLOTS, D), F32),
        grid=(N_TILES,),
        in_specs=[pl.BlockSpec((None, 8, TM), lambda m: (m, 0, 0), memory_space=pltpu.SMEM),
                  pl.BlockSpec(memory_space=pltpu.SMEM),
                  pl.BlockSpec(memory_space=pltpu.SMEM),
                  pl.BlockSpec((TM, D), lambda m: (m, 0)),
                  pl.BlockSpec(memory_space=pl.ANY)],
        out_specs=pl.BlockSpec(memory_space=pl.ANY),
        scratch_shapes=[pltpu.SemaphoreType.DMA, pltpu.SemaphoreType.DMA],
        compiler_params=pltpu.CompilerParams(dimension_semantics=("arbitrary",)),
        name="moe_dispatch",
    )(route, seg_start, last_blk, h2, zeros_blk)


def _ffn_kernel(blk_row_ref, blk_exp_ref, n_act_ref, x_ref, wgu_ref, wd_ref, y_ref):
    del blk_row_ref, blk_exp_ref

    @pl.when(pl.program_id(0) < n_act_ref[0])
    def _():
        gu = jnp.dot(x_ref[...].astype(BF16), wgu_ref[...], preferred_element_type=F32)
        gate = gu[:, :D_EXP]
        up = gu[:, D_EXP:]
        act = (gate * _sigmoid(gate) * up).astype(BF16)
        y_ref[...] = jnp.dot(act, wd_ref[...], preferred_element_type=F32)


def _expert_ffn(blk_row, blk_exp, n_act, xs, wgu, wd, l):
    grid_spec = pltpu.PrefetchScalarGridSpec(
        num_scalar_prefetch=3,
        grid=(N_FFN_BLOCKS,),
        in_specs=[pl.BlockSpec((FFN_BLK, D), lambda i, br, be, na: (br[i], 0)),
                  pl.BlockSpec((None, None, D, 2 * D_EXP), lambda i, br, be, na: (l, be[i], 0, 0)),
                  pl.BlockSpec((None, None, D_EXP, D), lambda i, br, be, na: (l, be[i], 0, 0))],
        out_specs=pl.BlockSpec((FFN_BLK, D), lambda i, br, be, na: (br[i], 0)),
    )
    return pl.pallas_call(
        _ffn_kernel,
        out_shape=jax.ShapeDtypeStruct((N_SLOTS, D), F32),
        grid_spec=grid_spec,
        compiler_params=_params(("arbitrary",)),
        name="expert_ffn",
    )(blk_row, blk_exp, n_act, xs, wgu, wd)


def _block_plan(counts):
    nblk = (counts + FFN_BLK - 1) // FFN_BLK
    end = jnp.cumsum(nblk)
    start = end - nblk
    n_act = end[-1]
    i = jnp.arange(N_FFN_BLOCKS, dtype=I32)
    i_eff = jnp.minimum(i, n_act - 1)
    e = jnp.minimum(jnp.sum(i_eff[:, None] >= end[None, :], axis=1), N_EXP - 1).astype(I32)
    last_blk = jnp.minimum(start + jnp.maximum(nblk - 1, 0), N_FFN_BLOCKS - 1) * FFN_BLK
    return ((start * FFN_BLK).astype(I32), last_blk.astype(I32), i_eff.astype(I32), e,
            n_act.reshape(1).astype(I32))


def _gather_copy(y_hbm, buf, sems, src_row, half, k, t):
    return pltpu.make_async_copy(y_hbm.at[pl.ds(src_row, 1), :], buf.at[half, k, pl.ds(t, 1), :],
                                 sems.at[half])


def _combine_kernel(route_ref, route_next_ref, start_ref, y_hbm, rw_ref, x1_ref, mod_ref, modn_ref,
                    lng_ref, lnb_ref, x2_ref, hn_ref, buf, sems):
    m = pl.program_id(0)
    half = m % 2

    def gather_tile(r_ref, dst_half):
        def body(i, carry):
            for u in range(DMA_UNROLL):
                t = i * DMA_UNROLL + u
                for k in range(2):
                    row = start_ref[r_ref[k, t]] + r_ref[2 + k, t]
                    _gather_copy(y_hbm, buf, sems, row, dst_half, k, t).start()
            return carry

        lax.fori_loop(0, TM // DMA_UNROLL, body, 0)

    @pl.when(m == 0)
    def _():
        gather_tile(route_ref, 0)

    @pl.when(m + 1 < pl.num_programs(0))
    def _():
        gather_tile(route_next_ref, 1 - half)

    def drain(i, carry):
        for _ in range(2 * DMA_UNROLL):
            _gather_copy(y_hbm, buf, sems, 0, half, 0, 0).wait()
        return carry

    lax.fori_loop(0, TM // DMA_UNROLL, drain, 0)
    f = rw_ref[:, 0:1] * buf[half, 0] + rw_ref[:, 1:2] * buf[half, 1]
    x2 = _layer_norm(ALPHA * x1_ref[...] + mod_ref[5:6, :] * f, lng_ref[...], lnb_ref[...])
    x2_ref[...] = x2
    hn_ref[...] = (x2 * (1.0 + modn_ref[1:2, :]) + modn_ref[0:1, :]).astype(BF16)


def _combine(route, seg_start, y_slots, rw, x1, mod_l, mod_next, ln_g, ln_b):
    row = lambda m: (m, 0)
    const = lambda m: (0, 0)
    mod_spec = pl.BlockSpec((None, N_MOD, D), lambda m: (_cond_row(m, CTX_TILES, TILES_PER_LAT), 0, 0))
    return pl.pallas_call(
        _combine_kernel,
        out_shape=(jax.ShapeDtypeStruct((T, D), F32), jax.ShapeDtypeStruct((T, D), BF16)),
        grid=(N_TILES,),
        in_specs=[pl.BlockSpec((None, 8, TM), lambda m: (m, 0, 0), memory_space=pltpu.SMEM),
                  pl.BlockSpec((None, 8, TM), lambda m: (jnp.minimum(m + 1, N_TILES - 1), 0, 0),
                               memory_space=pltpu.SMEM),
                  pl.BlockSpec(memory_space=pltpu.SMEM),
                  pl.BlockSpec(memory_space=pl.ANY),
                  pl.BlockSpec((TM, LANE), row), pl.BlockSpec((TM, D), row),
                  mod_spec, mod_spec,
                  pl.BlockSpec((1, D), const), pl.BlockSpec((1, D), const)],
        out_specs=(pl.BlockSpec((TM, D), row), pl.BlockSpec((TM, D), row)),
        scratch_shapes=[pltpu.VMEM((2, 2, TM, D), F32), pltpu.SemaphoreType.DMA((2,))],
        compiler_params=_params(("arbitrary",)),
        name="moe_combine_ln",
    )(route, route, seg_start, y_slots, rw, x1, mod_l, mod_next, ln_g, ln_b)


def kernel(x_prompt, x_sample, c, c_ctx, cache_k, cache_v, w_in, conv_a_w, w_a_out, conv_b_w, conv_b_b,
           norm_b_g, norm_b_b, w_b_out, q_norm_g, k_norm_g, w_c_out, w_o, w_ada, b_ada, ln_g, ln_b,
           w_router, router_bias, w_gate_up, w_down):
    x = jnp.concatenate([x_prompt.reshape(T_CTX, D), x_sample.reshape(T_LAT, D)], axis=0)
    cond16 = jnp.concatenate([c_ctx[None, :], c, jnp.zeros((16 - 1 - N_LAT_SEQ, D), F32)], axis=0)
    mod = _modulation(cond16, w_ada, b_ada).reshape(DEPTH, 16, N_MOD, D)

    w_in_b = w_in[:, :, :COL_GATES * 512].astype(BF16)
    w_gates_b = w_in[:, :, COL_GATES * 512:].astype(BF16)
    wa_b, wb_b, wc_b, wo_b = (w.astype(BF16) for w in (w_a_out, w_b_out, w_c_out, w_o))
    wgu_b = w_gate_up.astype(BF16)
    wd_b = w_down.astype(BF16)
    wr = jnp.pad(w_router, ((0, 0), (0, LANE - N_EXP)))
    wr_hi = wr.astype(BF16)
    wr_lo = (wr - wr_hi.astype(F32)).astype(BF16)
    rbias = router_bias.reshape(N_EXP, 1)

    cos_tab, sin_tab = _rope_tables()
    half = np.arange(LANE) // HEAD
    ones_bd = jnp.asarray(half[:, None] == half[None, :], BF16)
    zeros_blk = jnp.zeros((FFN_BLK, D), F32)
    ck = cache_k.reshape(N_LAT_SEQ, DEPTH, PAST, D_KV)
    cv = cache_v.reshape(N_LAT_SEQ, DEPTH, PAST, D_KV)

    h = _modulate(x, mod[0])
    new_k, new_v = [], []
    for l in range(DEPTH):
        ab, acx = _proj_a(h, w_in_b, l)
        glu = _proj_b(h, w_in_b, l)
        gates = _proj_gates(h, w_gates_b, l)
        q, katt, vatt, kn, vf = _proj_qkv(h, w_in_b, jnp.tile(q_norm_g[l], N_Q)[None, :],
                                          jnp.tile(k_norm_g[l], N_KV)[None, :], cos_tab, sin_tab, ones_bd, l)
        new_k.append(kn[:T_CTX].reshape(N_CTX_SEQ, CTX_LEN, N_KV, HEAD))
        new_v.append(vf[:T_CTX].reshape(N_CTX_SEQ, CTX_LEN, N_KV, HEAD))
        ya_pre, yb_pre = _convs(acx, ab, glu, conv_a_w[l], conv_b_w[l], conv_b_b[l][None, :],
                                norm_b_g[l][None, :], norm_b_b[l][None, :])
        o = _attention(q, katt, vatt, ck, cv, l)
        x1, h2, route, rw, counts = _post(ya_pre, yb_pre, o, gates, x, mod[l], wa_b, wb_b, wc_b, wo_b,
                                          ln_g[l, 0][None, :], ln_b[l, 0][None, :], wr_hi, wr_lo, rbias, l)
        cnt = counts[:, 0]
        seg_start, last_blk, blk_row, blk_exp, n_act = _block_plan(cnt)
        xs = _dispatch(route, seg_start, last_blk, h2, zeros_blk)
        y_slots = _expert_ffn(blk_row, blk_exp, n_act, xs, wgu_b, wd_b, l)
        x, h = _combine(route, seg_start, y_slots, rw, x1, mod[l], mod[min(l + 1, DEPTH - 1)],
                        ln_g[l, 1][None, :], ln_b[l, 1][None, :])
    y_prompt = x[:T_CTX].reshape(N_CTX_SEQ, CTX_LEN, D)
    y_sample = x[T_CTX:].reshape(N_LAT_SEQ, LAT_LEN, D)
    return y_prompt, y_sample, jnp.stack(new_k, axis=1), jnp.stack(new_v, axis=1)
```

```python
import functools

import numpy as np
import jax
import jax.numpy as jnp
from jax import lax
from jax.experimental import pallas as pl
from jax.experimental.pallas import tpu as pltpu

F32 = jnp.float32
BF16 = jnp.bfloat16
I32 = jnp.int32

D = 1024
DEPTH = 4
N_CTX_SEQ = 16
CTX_LEN = 256
N_LAT_SEQ = 8
LAT_LEN = 1024
PAST = 512
T_CTX = N_CTX_SEQ * CTX_LEN
T_LAT = N_LAT_SEQ * LAT_LEN
T = T_CTX + T_LAT
GRID_W = 64
HEAD = 64
N_Q = 16
N_KV = 4
GROUP = 4
D_KV = N_KV * HEAD
N_EXP = 16
N_GRP = 4
EXP_PER_GRP = 4
D_EXP = 512
IN_COLS = 9728
N_MOD = 6
ALPHA = (2 * DEPTH) ** 0.25
LN_EPS = 1e-5
RMS_EPS = 1e-6
ROPE_THETA = 10000.0

LANE = 128
TM = 512
N_TILES = T // TM
CTX_TILES = T_CTX // TM
TILES_PER_LAT = LAT_LEN // TM
TMP = 1024
SEG = 256
N_SEG = T // SEG
CTX_SEGS = T_CTX // SEG
SEGS_PER_LAT = LAT_LEN // SEG
HALO_A = 8
HALO_B = 16
SH_ROWS = SEG + 2 * HALO_B - 8
FFN_BLK = 256
N_FFN_BLOCKS = (2 * T) // FFN_BLK + N_EXP
N_SLOTS = N_FFN_BLOCKS * FFN_BLK
VMEM_LIMIT = 56 * 1024 * 1024

COL_AB, COL_AC, COL_AX, COL_BU, COL_BG, COL_Q, COL_KV, COL_GATES = 0, 2, 4, 6, 8, 10, 12, 13


def _params(sem):
    return pltpu.CompilerParams(dimension_semantics=sem, vmem_limit_bytes=VMEM_LIMIT)


def _cond_row(m, tiles_ctx, tiles_per_lat):
    return jnp.where(m < tiles_ctx, 0, 1 + (m - tiles_ctx) // tiles_per_lat)


def _layer_norm(x, g, b):
    mu = jnp.mean(x, axis=-1, keepdims=True)
    xc = x - mu
    var = jnp.mean(xc * xc, axis=-1, keepdims=True)
    return xc * lax.rsqrt(var + LN_EPS) * g + b


def _sigmoid(x):
    return 1.0 / (1.0 + jnp.exp(-x))


ROW_TILE = D // LANE


def _store_rows(ref, x):
    for j in range(ROW_TILE):
        ref[pl.ds(j, x.shape[0], stride=ROW_TILE), :] = x[:, j * LANE:(j + 1) * LANE]


def _load_rows(ref, n_rows):
    return jnp.concatenate([ref[pl.ds(j, n_rows, stride=ROW_TILE), :] for j in range(ROW_TILE)], axis=1)


def _mod_kernel(cond_ref, w_ref, b_ref, o_ref):
    cnd = cond_ref[...]
    s = (cnd * _sigmoid(cnd)).astype(BF16)
    o_ref[...] = jnp.dot(s, w_ref[...].astype(BF16), preferred_element_type=F32) + b_ref[...]


def _modulation(cond16, w_ada, b_ada):
    n_col = N_MOD * D
    tn = 1024
    return pl.pallas_call(
        _mod_kernel,
        out_shape=jax.ShapeDtypeStruct((DEPTH, 16, n_col), F32),
        grid=(DEPTH, n_col // tn),
        in_specs=[
            pl.BlockSpec((16, D), lambda l, n: (0, 0)),
            pl.BlockSpec((None, D, tn), lambda l, n: (l, 0, n)),
            pl.BlockSpec((None, 1, tn), lambda l, n: (l, 0, n)),
        ],
        out_specs=pl.BlockSpec((None, 16, tn), lambda l, n: (l, 0, n)),
        compiler_params=_params(("parallel", "parallel")),
        name="adaln_mod",
    )(cond16, w_ada, b_ada.reshape(DEPTH, 1, n_col))


def _modulate_kernel(x_ref, mod_ref, h_ref):
    h_ref[...] = (x_ref[...] * (1.0 + mod_ref[1:2, :]) + mod_ref[0:1, :]).astype(BF16)


def _modulate(x, mod_l):
    return pl.pallas_call(
        _modulate_kernel,
        out_shape=jax.ShapeDtypeStruct((T, D), BF16),
        grid=(N_TILES,),
        in_specs=[
            pl.BlockSpec((TM, D), lambda m: (m, 0)),
            pl.BlockSpec((None, N_MOD, D), lambda m: (_cond_row(m, CTX_TILES, TILES_PER_LAT), 0, 0)),
        ],
        out_specs=pl.BlockSpec((TM, D), lambda m: (m, 0)),
        compiler_params=_params(("parallel",)),
        name="modulate",
    )(x, mod_l)


def _proj_a_kernel(h_ref, wb_ref, wc_ref, wx_ref, ab_ref, acx_ref):
    h = h_ref[...]
    ab_ref[...] = jnp.dot(h, wb_ref[...], preferred_element_type=F32).astype(BF16)
    acx_ref[...] = (jnp.dot(h, wc_ref[...], preferred_element_type=F32)
                    * jnp.dot(h, wx_ref[...], preferred_element_type=F32))


def _proj_b_kernel(h_ref, wu_ref, wg_ref, glu_ref):
    h = h_ref[...]
    glu_ref[...] = (jnp.dot(h, wu_ref[...], preferred_element_type=F32)
                    * _sigmoid(jnp.dot(h, wg_ref[...], preferred_element_type=F32)))


def _proj_gate_kernel(h_ref, w_ref, g_ref):
    g_ref[...] = _sigmoid(jnp.dot(h_ref[...], w_ref[...], preferred_element_type=F32)).astype(BF16)


def _w_spec(l, col0, tn):
    return pl.BlockSpec((None, D, tn), lambda c, m: (l, 0, col0 + c))


def _proj_a(h, w_in, l):
    tn = 512
    out_spec = pl.BlockSpec((TMP, tn), lambda c, m: (m, c))
    return pl.pallas_call(
        _proj_a_kernel,
        out_shape=(jax.ShapeDtypeStruct((T, D), BF16), jax.ShapeDtypeStruct((T, D), F32)),
        grid=(D // tn, T // TMP),
        in_specs=[pl.BlockSpec((TMP, D), lambda c, m: (m, 0)),
                  _w_spec(l, COL_AB, tn), _w_spec(l, COL_AC, tn), _w_spec(l, COL_AX, tn)],
        out_specs=(out_spec, out_spec),
        compiler_params=_params(("parallel", "parallel")),
        name="proj_a",
    )(h, w_in, w_in, w_in)


def _proj_b(h, w_in, l):
    tn = 1024
    return pl.pallas_call(
        _proj_b_kernel,
        out_shape=jax.ShapeDtypeStruct((T, D), F32),
        grid=(D // tn, T // TMP),
        in_specs=[pl.BlockSpec((TMP, D), lambda c, m: (m, 0)),
                  _w_spec(l, COL_BU * 512 // tn, tn), _w_spec(l, COL_BG * 512 // tn, tn)],
        out_specs=pl.BlockSpec((TMP, tn), lambda c, m: (m, c)),
        compiler_params=_params(("parallel", "parallel")),
        name="proj_b",
    )(h, w_in, w_in)


def _proj_gates(h, w_gates, l):
    tn = 1024
    return pl.pallas_call(
        _proj_gate_kernel,
        out_shape=jax.ShapeDtypeStruct((T, 3 * D), BF16),
        grid=(3 * D // tn, T // TMP),
        in_specs=[pl.BlockSpec((TMP, D), lambda c, m: (m, 0)),
                  pl.BlockSpec((None, D, tn), lambda c, m: (l, 0, c))],
        out_specs=pl.BlockSpec((TMP, tn), lambda c, m: (m, c)),
        compiler_params=_params(("parallel", "parallel")),
        name="proj_gates",
    )(h, w_gates)


def _head_mean_square(x, ones_bd):
    out = []
    for c in range(x.shape[1] // LANE):
        sq = x[:, c * LANE:(c + 1) * LANE]
        sq = sq * sq
        hi = sq.astype(BF16)
        lo = (sq - hi.astype(F32)).astype(BF16)
        out.append(jnp.dot(hi, ones_bd, preferred_element_type=F32)
                   + jnp.dot(lo, ones_bd, preferred_element_type=F32))
    return jnp.concatenate(out, axis=1) * (1.0 / HEAD)


def _rope(x, cos, sin, first_half):
    out = []
    for c in range(x.shape[1] // LANE):
        xc = x[:, c * LANE:(c + 1) * LANE]
        partner = jnp.where(first_half, pltpu.roll(xc, LANE - 16, axis=1), pltpu.roll(xc, 16, axis=1))
        out.append(xc * cos + partner * sin)
    return jnp.concatenate(out, axis=1)


def _qkv_kernel(h_ref, wq_ref, wkv_ref, gq_ref, gk_ref, cos_ref, sin_ref, ones_ref,
                q_ref, katt_ref, vatt_ref, kn_ref, vf_ref):
    h = h_ref[...]
    ones_bd = ones_ref[...]
    cos = cos_ref[...]
    sin = sin_ref[...]
    lane = lax.broadcasted_iota(I32, (TM, LANE), 1)
    first_half = (lane & 16) == 0
    q = jnp.dot(h, wq_ref[...], preferred_element_type=F32)
    qn = q * lax.rsqrt(_head_mean_square(q, ones_bd) + RMS_EPS) * gq_ref[...]
    q_ref[...] = (_rope(qn, cos, sin, first_half) * (HEAD ** -0.5)).astype(BF16)
    kv = jnp.dot(h, wkv_ref[...], preferred_element_type=F32)
    k = kv[:, :D_KV]
    v = kv[:, D_KV:]
    kn = k * lax.rsqrt(_head_mean_square(k, ones_bd) + RMS_EPS) * gk_ref[...]
    kn_ref[...] = kn
    katt_ref[...] = _rope(kn, cos, sin, first_half).astype(BF16)
    vf_ref[...] = v
    vatt_ref[...] = v.astype(BF16)


def _proj_qkv(h, w_in, gq, gk, cos_tab, sin_tab, ones_bd, l):
    def tab_idx(m):
        return jnp.where(m < CTX_TILES, TILES_PER_LAT, (m - CTX_TILES) % TILES_PER_LAT)

    row = lambda m: (m, 0)
    return pl.pallas_call(
        _qkv_kernel,
        out_shape=(jax.ShapeDtypeStruct((T, D), BF16),
                   jax.ShapeDtypeStruct((T, D_KV), BF16),
                   jax.ShapeDtypeStruct((T, D_KV), BF16),
                   jax.ShapeDtypeStruct((T, D_KV), F32),
                   jax.ShapeDtypeStruct((T, D_KV), F32)),
        grid=(N_TILES,),
        in_specs=[
            pl.BlockSpec((TM, D), row),
            pl.BlockSpec((None, D, D), lambda m: (l, 0, COL_Q // 2)),
            pl.BlockSpec((None, D, 2 * D_KV), lambda m: (l, 0, COL_KV)),
            pl.BlockSpec((1, D), lambda m: (0, 0)),
            pl.BlockSpec((1, D_KV), lambda m: (0, 0)),
            pl.BlockSpec((TM, LANE), lambda m: (tab_idx(m), 0)),
            pl.BlockSpec((TM, LANE), lambda m: (tab_idx(m), 0)),
            pl.BlockSpec((LANE, LANE), lambda m: (0, 0)),
        ],
        out_specs=(pl.BlockSpec((TM, D), row), pl.BlockSpec((TM, D_KV), row),
                   pl.BlockSpec((TM, D_KV), row), pl.BlockSpec((TM, D_KV), row),
                   pl.BlockSpec((TM, D_KV), row)),
        compiler_params=_params(("parallel",)),
        name="proj_qkv",
    )(h, w_in, w_in, gq, gk, cos_tab, sin_tab, ones_bd)


def _rope_tables():
    lane = np.arange(LANE)
    j = lane % 16
    freqs = jnp.power(ROPE_THETA, -jnp.arange(16, dtype=F32) / 16)[j]
    pos = jnp.arange(LAT_LEN, dtype=I32)
    row = (pos // GRID_W).astype(F32)
    col = (pos % GRID_W).astype(F32)
    use_row = jnp.asarray((lane % HEAD) < HEAD // 2)
    p = jnp.where(use_row[None, :], row[:, None], col[:, None])
    ang = p * freqs[None, :]
    sign = jnp.asarray(np.where((lane & 16) == 0, -1.0, 1.0), F32)
    cos = jnp.concatenate([jnp.cos(ang), jnp.ones((TM, LANE), F32)], axis=0)
    sin = jnp.concatenate([jnp.sin(ang) * sign[None, :], jnp.zeros((TM, LANE), F32)], axis=0)
    return cos, sin


def _conv_kernel(acx_ref, acx_l_ref, acx_r_ref, ab_ref, glu_ref, glu_l_ref, glu_r_ref,
                 wa_ref, wb_ref, bb_ref, ng_ref, nb_ref, ya_ref, yb_ref, pad_a, pad_b, u_ref, sh_ref):
    s = pl.program_id(0)
    lat = s >= CTX_SEGS
    pos = (s - CTX_SEGS) % SEGS_PER_LAT
    has_left = jnp.logical_and(lat, pos != 0)
    has_right = jnp.logical_and(lat, pos != SEGS_PER_LAT - 1)

    pad_a[0:HALO_A, :] = jnp.where(has_left, acx_l_ref[...], 0.0)
    pad_a[HALO_A:HALO_A + SEG, :] = acx_ref[...]
    pad_a[HALO_A + SEG:, :] = jnp.where(has_right, acx_r_ref[...], 0.0)
    pad_b[0:HALO_B, :] = jnp.where(has_left, glu_l_ref[...], 0.0)
    pad_b[HALO_B:HALO_B + SEG, :] = glu_ref[...]
    pad_b[HALO_B + SEG:, :] = jnp.where(has_right, glu_r_ref[...], 0.0)

    conv_a = (wa_ref[0:1, :] * pad_a[HALO_A - 1:HALO_A - 1 + SEG, :]
              + wa_ref[1:2, :] * pad_a[HALO_A:HALO_A + SEG, :]
              + wa_ref[2:3, :] * pad_a[HALO_A + 1:HALO_A + 1 + SEG, :])
    ya_ref[...] = (ab_ref[...].astype(F32) * conv_a).astype(BF16)

    rows = 64
    kb = wb_ref.shape[0]

    def lane_chunk(c, carry):
        lanes = pl.ds(pl.multiple_of(c * LANE, LANE), LANE)
        for b in range(1, 8):
            sh_ref[b, :, :] = pad_b[b:b + SH_ROWS, lanes]
        for r in range(SEG // rows):
            acc = jnp.zeros((rows, LANE), F32)
            for k in range(kb):
                off = HALO_B + k - kb // 2
                row0 = r * rows + 8 * (off // 8)
                if off % 8 == 0:
                    src = pad_b[row0:row0 + rows, lanes]
                else:
                    src = sh_ref[off % 8, row0:row0 + rows, :]
                acc = acc + wb_ref[k:k + 1, lanes] * src
            u_ref[r * rows:(r + 1) * rows, lanes] = acc
        return carry

    lax.fori_loop(0, D // LANE, lane_chunk, 0)
    u = _layer_norm(u_ref[...] + bb_ref[...], ng_ref[...], nb_ref[...])
    yb_ref[...] = (u * _sigmoid(u)).astype(BF16)


def _convs(acx, ab, glu, conv_a_w, conv_b_w, conv_b_b, norm_g, norm_b):
    seg = lambda s: (s, 0)
    const = lambda s: (0, 0)
    ra, rb = SEG // HALO_A, SEG // HALO_B
    left_a = lambda s: (jnp.maximum(s * ra - 1, 0), 0)
    right_a = lambda s: (jnp.minimum((s + 1) * ra, T // HALO_A - 1), 0)
    left_b = lambda s: (jnp.maximum(s * rb - 1, 0), 0)
    right_b = lambda s: (jnp.minimum((s + 1) * rb, T // HALO_B - 1), 0)
    return pl.pallas_call(
        _conv_kernel,
        out_shape=(jax.ShapeDtypeStruct((T, D), BF16), jax.ShapeDtypeStruct((T, D), BF16)),
        grid=(N_SEG,),
        in_specs=[
            pl.BlockSpec((SEG, D), seg), pl.BlockSpec((HALO_A, D), left_a), pl.BlockSpec((HALO_A, D), right_a),
            pl.BlockSpec((SEG, D), seg),
            pl.BlockSpec((SEG, D), seg), pl.BlockSpec((HALO_B, D), left_b), pl.BlockSpec((HALO_B, D), right_b),
            pl.BlockSpec(conv_a_w.shape, const), pl.BlockSpec(conv_b_w.shape, const),
            pl.BlockSpec((1, D), const), pl.BlockSpec((1, D), const), pl.BlockSpec((1, D), const),
        ],
        out_specs=(pl.BlockSpec((SEG, D), seg), pl.BlockSpec((SEG, D), seg)),
        scratch_shapes=[pltpu.VMEM((SEG + 2 * HALO_A, D), F32),
                        pltpu.VMEM((SEG + 2 * HALO_B, D), F32),
                        pltpu.VMEM((SEG, D), F32),
                        pltpu.VMEM((8, SH_ROWS, LANE), F32)],
        compiler_params=_params(("parallel",)),
        name="convs",
    )(acx, acx, acx, ab, glu, glu, glu, conv_a_w, conv_b_w, conv_b_b, norm_g, norm_b)


def _attend(q_ref, key_refs, val_refs, o_ref):
    nt = (((1,), (1,)), ((), ()))
    keys = [r[...].astype(BF16) for r in key_refs]
    vals = [r[...].astype(BF16) for r in val_refs]
    for g in range(N_KV):
        kg = [k[:, g * HEAD:(g + 1) * HEAD] for k in keys]
        vg = [v[:, g * HEAD:(g + 1) * HEAD] for v in vals]
        for hh in range(GROUP):
            hd = g * GROUP + hh
            qh = q_ref[:, hd * HEAD:(hd + 1) * HEAD]
            s = [lax.dot_general(qh, k, nt, preferred_element_type=F32) for k in kg]
            mx = functools.reduce(jnp.maximum, [jnp.max(x, axis=-1, keepdims=True) for x in s])
            p = [jnp.exp(x - mx) for x in s]
            den = functools.reduce(jnp.add, [jnp.sum(x, axis=-1, keepdims=True) for x in p])
            acc = functools.reduce(jnp.add, [jnp.dot(x.astype(BF16), v, preferred_element_type=F32)
                                             for x, v in zip(p, vg)])
            o_ref[:, hd * HEAD:(hd + 1) * HEAD] = (acc / den).astype(BF16)


def _attn_ctx_kernel(q_ref, k_ref, v_ref, o_ref):
    _attend(q_ref, [k_ref], [v_ref], o_ref)


def _attn_lat_kernel(q_ref, k_ref, v_ref, ck_ref, cv_ref, o_in_ref, o_ref):
    del o_in_ref
    _attend(q_ref, [ck_ref, k_ref], [cv_ref, v_ref], o_ref)


def _attention(q, katt, vatt, cache_k, cache_v, l):
    o = pl.pallas_call(
        _attn_ctx_kernel,
        out_shape=jax.ShapeDtypeStruct((T, D), BF16),
        grid=(N_CTX_SEQ,),
        in_specs=[pl.BlockSpec((CTX_LEN, D), lambda b: (b, 0)),
                  pl.BlockSpec((CTX_LEN, D_KV), lambda b: (b, 0)),
                  pl.BlockSpec((CTX_LEN, D_KV), lambda b: (b, 0))],
        out_specs=pl.BlockSpec((CTX_LEN, D), lambda b: (b, 0)),
        compiler_params=_params(("parallel",)),
        name="attn_ctx",
    )(q, katt, vatt)
    seg0 = CTX_SEGS
    lat0 = T_CTX // LAT_LEN
    return pl.pallas_call(
        _attn_lat_kernel,
        out_shape=jax.ShapeDtypeStruct((T, D), BF16),
        grid=(N_LAT_SEQ, SEGS_PER_LAT),
        in_specs=[pl.BlockSpec((SEG, D), lambda b, i: (seg0 + b * SEGS_PER_LAT + i, 0)),
                  pl.BlockSpec((LAT_LEN, D_KV), lambda b, i: (lat0 + b, 0)),
                  pl.BlockSpec((LAT_LEN, D_KV), lambda b, i: (lat0 + b, 0)),
                  pl.BlockSpec((None, None, PAST, D_KV), lambda b, i: (b, l, 0, 0)),
                  pl.BlockSpec((None, None, PAST, D_KV), lambda b, i: (b, l, 0, 0)),
                  pl.BlockSpec(memory_space=pl.ANY)],
        out_specs=pl.BlockSpec((SEG, D), lambda b, i: (seg0 + b * SEGS_PER_LAT + i, 0)),
        input_output_aliases={5: 0},
        compiler_params=_params(("parallel", "parallel")),
        name="attn_lat",
    )(q, katt, vatt, cache_k, cache_v, o)


def _post_kernel(ya_ref, yb_ref, o_ref, g_ref, x_ref, mod_ref, wa_ref, wb_ref, wc_ref, wo_ref,
                 lng_ref, lnb_ref, wr_hi_ref, wr_lo_ref, rb_ref,
                 x1_ref, h2_ref, route_ref, rw_ref, cnt_ref, carry_ref):
    m = pl.program_id(0)

    @pl.when(m == 0)
    def _():
        carry_ref[...] = jnp.zeros_like(carry_ref)

    ya = jnp.dot(ya_ref[...], wa_ref[...], preferred_element_type=F32)
    yb = jnp.dot(yb_ref[...], wb_ref[...], preferred_element_type=F32)
    yc = jnp.dot(o_ref[...], wc_ref[...], preferred_element_type=F32)
    merged = (g_ref[:, 0:D].astype(F32) * ya + g_ref[:, D:2 * D].astype(F32) * yb
              + g_ref[:, 2 * D:3 * D].astype(F32) * yc)
    mix = jnp.dot(merged.astype(BF16), wo_ref[...], preferred_element_type=F32)
    x1 = _layer_norm(ALPHA * x_ref[...] + mod_ref[2:3, :] * mix, lng_ref[...], lnb_ref[...])
    x1_ref[...] = x1
    h2 = x1 * (1.0 + mod_ref[4:5, :]) + mod_ref[3:4, :]
    _store_rows(h2_ref, h2)

    hi = h2.astype(BF16)
    lo = (h2 - hi.astype(F32)).astype(BF16)
    wr_hi = wr_hi_ref[...]
    logits = (jnp.dot(hi, wr_hi, preferred_element_type=F32)
              + jnp.dot(lo, wr_hi, preferred_element_type=F32)
              + jnp.dot(hi, wr_lo_ref[...], preferred_element_type=F32))
    scores = _sigmoid(logits.T[0:N_EXP, :])
    sel = scores + rb_ref[...]

    gscore = []
    for g in range(N_GRP):
        r = [sel[g * EXP_PER_GRP + j:g * EXP_PER_GRP + j + 1, :] for j in range(EXP_PER_GRP)]
        pairs = [r[a] + r[b] for a in range(EXP_PER_GRP) for b in range(a + 1, EXP_PER_GRP)]
        gscore.append(functools.reduce(jnp.maximum, pairs))
    best = functools.reduce(jnp.maximum, gscore)
    gsel = jnp.full(best.shape, N_GRP - 1, I32)
    for g in range(N_GRP - 2, -1, -1):
        gsel = jnp.where(gscore[g] == best, g, gsel)

    eidx = lax.broadcasted_iota(I32, (N_EXP, TM), 0)
    neg = jnp.float32(-jnp.inf)
    cand = jnp.where((eidx // EXP_PER_GRP) == gsel, sel, neg)
    top1 = jnp.max(cand, axis=0, keepdims=True)
    idx1 = jnp.min(jnp.where(cand == top1, eidx, N_EXP), axis=0, keepdims=True)
    cand2 = jnp.where(eidx == idx1, neg, cand)
    top2 = jnp.max(cand2, axis=0, keepdims=True)
    idx2 = jnp.min(jnp.where(cand2 == top2, eidx, N_EXP), axis=0, keepdims=True)
    is1 = eidx == idx1
    is2 = eidx == idx2
    w1 = jnp.sum(jnp.where(is1, scores, 0.0), axis=0, keepdims=True)
    w2 = jnp.sum(jnp.where(is2, scores, 0.0), axis=0, keepdims=True)
    wsum = w1 + w2
    w1 = w1 / wsum
    w2 = w2 / wsum

    onehot = jnp.where(jnp.logical_or(is1, is2), 1.0, 0.0)
    r_i = lax.broadcasted_iota(I32, (TM, TM), 0)
    c_i = lax.broadcasted_iota(I32, (TM, TM), 1)
    upper = jnp.where(r_i < c_i, 1.0, 0.0).astype(BF16)
    prefix = jnp.dot(onehot.astype(BF16), upper, preferred_element_type=F32) + carry_ref[:, 0:1]
    rank1 = jnp.sum(jnp.where(is1, prefix, 0.0), axis=0, keepdims=True)
    rank2 = jnp.sum(jnp.where(is2, prefix, 0.0), axis=0, keepdims=True)
    carry = carry_ref[...] + jnp.sum(onehot, axis=1, keepdims=True)
    carry_ref[...] = carry
    cnt_ref[...] = carry.astype(I32)

    route_ref[0:1, :] = idx1
    route_ref[1:2, :] = idx2
    route_ref[2:3, :] = rank1.astype(I32)
    route_ref[3:4, :] = rank2.astype(I32)
    route_ref[4:8, :] = jnp.zeros((4, TM), I32)
    wrow = lax.broadcasted_iota(I32, (LANE, TM), 0)
    wcols = jnp.where(wrow == 0, w1, jnp.where(wrow == 1, w2, 0.0))
    rw_ref[...] = wcols.T


def _post(ya_pre, yb_pre, o, gates, x, mod_l, wa, wb, wc, wo, ln_g, ln_b, wr_hi, wr_lo, rbias, l):
    row = lambda m: (m, 0)
    const = lambda m: (0, 0)
    wspec = pl.BlockSpec((None, D, D), lambda m: (l, 0, 0))
    return pl.pallas_call(
        _post_kernel,
        out_shape=(jax.ShapeDtypeStruct((T, D), F32),
                   jax.ShapeDtypeStruct((T * ROW_TILE, LANE), F32),
                   jax.ShapeDtypeStruct((N_TILES, 8, TM), I32),
                   jax.ShapeDtypeStruct((T, LANE), F32),
                   jax.ShapeDtypeStruct((N_EXP, LANE), I32)),
        grid=(N_TILES,),
        in_specs=[pl.BlockSpec((TM, D), row), pl.BlockSpec((TM, D), row), pl.BlockSpec((TM, D), row),
                  pl.BlockSpec((TM, 3 * D), row), pl.BlockSpec((TM, D), row),
                  pl.BlockSpec((None, N_MOD, D), lambda m: (_cond_row(m, CTX_TILES, TILES_PER_LAT), 0, 0)),
                  wspec, wspec, wspec, wspec,
                  pl.BlockSpec((1, D), const), pl.BlockSpec((1, D), const),
                  pl.BlockSpec((D, LANE), const), pl.BlockSpec((D, LANE), const),
                  pl.BlockSpec((N_EXP, 1), const)],
        out_specs=(pl.BlockSpec((TM, D), row), pl.BlockSpec((TM * ROW_TILE, LANE), row),
                   pl.BlockSpec((None, 8, TM), lambda m: (m, 0, 0)),
                   pl.BlockSpec((TM, LANE), row),
                   pl.BlockSpec((N_EXP, LANE), const)),
        scratch_shapes=[pltpu.VMEM((N_EXP, LANE), F32)],
        compiler_params=_params(("arbitrary",)),
        name="merge_ln_router",
    )(ya_pre, yb_pre, o, gates, x, mod_l, wa, wb, wc, wo, ln_g, ln_b, wr_hi, wr_lo, rbias)


def _row_copy(src, dst, src_row, dst_row, sem):
    return pltpu.make_async_copy(src.at[pl.ds(pl.multiple_of(src_row * ROW_TILE, ROW_TILE), ROW_TILE), :],
                                 dst.at[pl.ds(pl.multiple_of(dst_row * ROW_TILE, ROW_TILE), ROW_TILE), :], sem)


DMA_UNROLL = 4


def _dispatch_kernel(slot_ref, last_ref, h2_ref, zeros_hbm, xs_hbm, sem, zsem):
    m = pl.program_id(0)

    @pl.when(m == 0)
    def _():
        def pad_copy(e):
            row = pl.multiple_of(last_ref[e] * ROW_TILE, FFN_BLK * ROW_TILE)
            return pltpu.make_async_copy(zeros_hbm, xs_hbm.at[pl.ds(row, FFN_BLK * ROW_TILE), :], zsem)

        def zero_one(e, carry):
            pad_copy(e).start()
            pad_copy(e).wait()
            return carry

        lax.fori_loop(0, N_EXP, zero_one, 0)

    def issue(i, carry):
        for u in range(DMA_UNROLL):
            t = i * DMA_UNROLL + u
            for k in range(2):
                _row_copy(h2_ref, xs_hbm, t, slot_ref[k, t], sem).start()
        return carry

    lax.fori_loop(0, TM // DMA_UNROLL, issue, 0)

    def drain(i, carry):
        for _ in range(2 * DMA_UNROLL):
            _row_copy(h2_ref, xs_hbm, 0, 0, sem).wait()
        return carry

    lax.fori_loop(0, TM // DMA_UNROLL, drain, 0)


def _dispatch(slots, last_blk, h2, zeros_blk):
    return pl.pallas_call(
        _dispatch_kernel,
        out_shape=jax.ShapeDtypeStruct((N_SLOTS * ROW_TILE, LANE), F32),
        grid=(N_TILES,),
        in_specs=[pl.BlockSpec((None, 2, TM), lambda m: (m, 0, 0), memory_space=pltpu.SMEM),
                  pl.BlockSpec(memory_space=pltpu.SMEM),
                  pl.BlockSpec((TM * ROW_TILE, LANE), lambda m: (m, 0)),
                  pl.BlockSpec(memory_space=pl.ANY)],
        out_specs=pl.BlockSpec(memory_space=pl.ANY),
        scratch_shapes=[pltpu.SemaphoreType.DMA, pltpu.SemaphoreType.DMA],
        compiler_params=pltpu.CompilerParams(dimension_semantics=("arbitrary",)),
        name="moe_dispatch",
    )(slots, last_blk, h2, zeros_blk)


def _ffn_kernel(blk_row_ref, blk_exp_ref, n_act_ref, x_ref, wgu_ref, wd_ref, y_ref):
    del blk_row_ref, blk_exp_ref

    @pl.when(pl.program_id(0) < n_act_ref[0])
    def _():
        gu = jnp.dot(_load_rows(x_ref, FFN_BLK).astype(BF16), wgu_ref[...], preferred_element_type=F32)
        gate = gu[:, :D_EXP]
        up = gu[:, D_EXP:]
        act = (gate * _sigmoid(gate) * up).astype(BF16)
        _store_rows(y_ref, jnp.dot(act, wd_ref[...], preferred_element_type=F32))


def _expert_ffn(blk_row, blk_exp, n_act, xs, wgu, wd, l):
    grid_spec = pltpu.PrefetchScalarGridSpec(
        num_scalar_prefetch=3,
        grid=(N_FFN_BLOCKS,),
        in_specs=[pl.BlockSpec((FFN_BLK * ROW_TILE, LANE), lambda i, br, be, na: (br[i], 0)),
                  pl.BlockSpec((None, None, D, 2 * D_EXP), lambda i, br, be, na: (l, be[i], 0, 0)),
                  pl.BlockSpec((None, None, D_EXP, D), lambda i, br, be, na: (l, be[i], 0, 0))],
        out_specs=pl.BlockSpec((FFN_BLK * ROW_TILE, LANE), lambda i, br, be, na: (br[i], 0)),
    )
    return pl.pallas_call(
        _ffn_kernel,
        out_shape=jax.ShapeDtypeStruct((N_SLOTS * ROW_TILE, LANE), F32),
        grid_spec=grid_spec,
        compiler_params=_params(("arbitrary",)),
        name="expert_ffn",
    )(blk_row, blk_exp, n_act, xs, wgu, wd)


def _block_plan(counts):
    nblk = (counts + FFN_BLK - 1) // FFN_BLK
    end = jnp.cumsum(nblk)
    start = end - nblk
    n_act = end[-1]
    i = jnp.arange(N_FFN_BLOCKS, dtype=I32)
    i_eff = jnp.minimum(i, n_act - 1)
    e = jnp.minimum(jnp.sum(i_eff[:, None] >= end[None, :], axis=1), N_EXP - 1).astype(I32)
    last_blk = jnp.minimum(start + jnp.maximum(nblk - 1, 0), N_FFN_BLOCKS - 1) * FFN_BLK
    return ((start * FFN_BLK).astype(I32), last_blk.astype(I32), i_eff.astype(I32), e,
            n_act.reshape(1).astype(I32))


def _combine_kernel(slot_ref, slot_next_ref, y_hbm, rw_ref, x1_ref, mod_ref, modn_ref,
                    lng_ref, lnb_ref, x2_ref, hn_ref, buf, sems):
    m = pl.program_id(0)
    half = m % 2

    def gather_tile(s_ref, dst_half):
        def body(i, carry):
            for u in range(DMA_UNROLL):
                t = i * DMA_UNROLL + u
                for k in range(2):
                    _row_copy(y_hbm, buf.at[dst_half, k], s_ref[k, t], t, sems.at[dst_half]).start()
            return carry

        lax.fori_loop(0, TM // DMA_UNROLL, body, 0)

    @pl.when(m == 0)
    def _():
        gather_tile(slot_ref, 0)

    @pl.when(m + 1 < pl.num_programs(0))
    def _():
        gather_tile(slot_next_ref, 1 - half)

    def drain(i, carry):
        for _ in range(2 * DMA_UNROLL):
            _row_copy(y_hbm, buf.at[half, 0], 0, 0, sems.at[half]).wait()
        return carry

    lax.fori_loop(0, TM // DMA_UNROLL, drain, 0)
    f = rw_ref[:, 0:1] * _load_rows(buf.at[half, 0], TM) + rw_ref[:, 1:2] * _load_rows(buf.at[half, 1], TM)
    x2 = _layer_norm(ALPHA * x1_ref[...] + mod_ref[5:6, :] * f, lng_ref[...], lnb_ref[...])
    x2_ref[...] = x2
    hn_ref[...] = (x2 * (1.0 + modn_ref[1:2, :]) + modn_ref[0:1, :]).astype(BF16)


def _combine(slots, y_slots, rw, x1, mod_l, mod_next, ln_g, ln_b):
    row = lambda m: (m, 0)
    const = lambda m: (0, 0)
    mod_spec = pl.BlockSpec((None, N_MOD, D), lambda m: (_cond_row(m, CTX_TILES, TILES_PER_LAT), 0, 0))
    return pl.pallas_call(
        _combine_kernel,
        out_shape=(jax.ShapeDtypeStruct((T, D), F32), jax.ShapeDtypeStruct((T, D), BF16)),
        grid=(N_TILES,),
        in_specs=[pl.BlockSpec((None, 2, TM), lambda m: (m, 0, 0), memory_space=pltpu.SMEM),
                  pl.BlockSpec((None, 2, TM), lambda m: (jnp.minimum(m + 1, N_TILES - 1), 0, 0),
                               memory_space=pltpu.SMEM),
                  pl.BlockSpec(memory_space=pl.ANY),
                  pl.BlockSpec((TM, LANE), row), pl.BlockSpec((TM, D), row),
                  mod_spec, mod_spec,
                  pl.BlockSpec((1, D), const), pl.BlockSpec((1, D), const)],
        out_specs=(pl.BlockSpec((TM, D), row), pl.BlockSpec((TM, D), row)),
        scratch_shapes=[pltpu.VMEM((2, 2, TM * ROW_TILE, LANE), F32), pltpu.SemaphoreType.DMA((2,))],
        compiler_params=_params(("arbitrary",)),
        name="moe_combine_ln",
    )(slots, slots, y_slots, rw, x1, mod_l, mod_next, ln_g, ln_b)


def kernel(x_prompt, x_sample, c, c_ctx, cache_k, cache_v, w_in, conv_a_w, w_a_out, conv_b_w, conv_b_b,
           norm_b_g, norm_b_b, w_b_out, q_norm_g, k_norm_g, w_c_out, w_o, w_ada, b_ada, ln_g, ln_b,
           w_router, router_bias, w_gate_up, w_down):
    x = jnp.concatenate([x_prompt.reshape(T_CTX, D), x_sample.reshape(T_LAT, D)], axis=0)
    cond16 = jnp.concatenate([c_ctx[None, :], c, jnp.zeros((16 - 1 - N_LAT_SEQ, D), F32)], axis=0)
    mod = _modulation(cond16, w_ada, b_ada).reshape(DEPTH, 16, N_MOD, D)

    w_in_b = w_in[:, :, :COL_GATES * 512].astype(BF16)
    w_gates_b = w_in[:, :, COL_GATES * 512:].astype(BF16)
    wa_b, wb_b, wc_b, wo_b = (w.astype(BF16) for w in (w_a_out, w_b_out, w_c_out, w_o))
    wgu_b = w_gate_up.astype(BF16)
    wd_b = w_down.astype(BF16)
    wr = jnp.pad(w_router, ((0, 0), (0, LANE - N_EXP)))
    wr_hi = wr.astype(BF16)
    wr_lo = (wr - wr_hi.astype(F32)).astype(BF16)
    rbias = router_bias.reshape(N_EXP, 1)

    cos_tab, sin_tab = _rope_tables()
    half = np.arange(LANE) // HEAD
    ones_bd = jnp.asarray(half[:, None] == half[None, :], BF16)
    zeros_blk = jnp.zeros((FFN_BLK * ROW_TILE, LANE), F32)
    ck = cache_k.reshape(N_LAT_SEQ, DEPTH, PAST, D_KV)
    cv = cache_v.reshape(N_LAT_SEQ, DEPTH, PAST, D_KV)

    h = _modulate(x, mod[0])
    new_k, new_v = [], []
    for l in range(DEPTH):
        ab, acx = _proj_a(h, w_in_b, l)
        glu = _proj_b(h, w_in_b, l)
        gates = _proj_gates(h, w_gates_b, l)
        q, katt, vatt, kn, vf = _proj_qkv(h, w_in_b, jnp.tile(q_norm_g[l], N_Q)[None, :],
                                          jnp.tile(k_norm_g[l], N_KV)[None, :], cos_tab, sin_tab, ones_bd, l)
        new_k.append(kn[:T_CTX].reshape(N_CTX_SEQ, CTX_LEN, N_KV, HEAD))
        new_v.append(vf[:T_CTX].reshape(N_CTX_SEQ, CTX_LEN, N_KV, HEAD))
        ya_pre, yb_pre = _convs(acx, ab, glu, conv_a_w[l], conv_b_w[l], conv_b_b[l][None, :],
                                norm_b_g[l][None, :], norm_b_b[l][None, :])
        o = _attention(q, katt, vatt, ck, cv, l)
        x1, h2, route, rw, counts = _post(ya_pre, yb_pre, o, gates, x, mod[l], wa_b, wb_b, wc_b, wo_b,
                                          ln_g[l, 0][None, :], ln_b[l, 0][None, :], wr_hi, wr_lo, rbias, l)
        cnt = counts[:, 0]
        seg_start, last_blk, blk_row, blk_exp, n_act = _block_plan(cnt)
        slots = jnp.take(seg_start, route[:, 0:2, :]) + route[:, 2:4, :]
        xs = _dispatch(slots, last_blk, h2, zeros_blk)
        y_slots = _expert_ffn(blk_row, blk_exp, n_act, xs, wgu_b, wd_b, l)
        x, h = _combine(slots, y_slots, rw, x1, mod[l], mod[min(l + 1, DEPTH - 1)],
                        ln_g[l, 1][None, :], ln_b[l, 1][None, :])
    y_prompt = x[:T_CTX].reshape(N_CTX_SEQ, CTX_LEN, D)
    y_sample = x[T_CTX:].reshape(N_LAT_SEQ, LAT_LEN, D)
    return y_prompt, y_sample, jnp.stack(new_k, axis=1), jnp.stack(new_v, axis=1)
```

```python
import functools

import numpy as np
import jax
import jax.numpy as jnp
from jax import lax
from jax.experimental import pallas as pl
from jax.experimental.pallas import tpu as pltpu

F32 = jnp.float32
BF16 = jnp.bfloat16
I32 = jnp.int32

D = 1024
DEPTH = 4
N_CTX_SEQ = 16
CTX_LEN = 256
N_LAT_SEQ = 8
LAT_LEN = 1024
PAST = 512
T_CTX = N_CTX_SEQ * CTX_LEN
T_LAT = N_LAT_SEQ * LAT_LEN
T = T_CTX + T_LAT
GRID_W = 64
HEAD = 64
N_Q = 16
N_KV = 4
GROUP = 4
D_KV = N_KV * HEAD
N_EXP = 16
N_GRP = 4
EXP_PER_GRP = 4
D_EXP = 512
IN_COLS = 9728
N_MOD = 6
ALPHA = (2 * DEPTH) ** 0.25
LN_EPS = 1e-5
RMS_EPS = 1e-6
ROPE_THETA = 10000.0

LANE = 128
TM = 512
N_TILES = T // TM
CTX_TILES = T_CTX // TM
TILES_PER_LAT = LAT_LEN // TM
TMP = 1024
SEG = 256
N_SEG = T // SEG
CTX_SEGS = T_CTX // SEG
SEGS_PER_LAT = LAT_LEN // SEG
HALO_A = 8
HALO_B = 16
SH_ROWS = SEG + 2 * HALO_B - 8
FFN_BLK = 256
N_FFN_BLOCKS = (2 * T) // FFN_BLK + N_EXP
N_SLOTS = N_FFN_BLOCKS * FFN_BLK
VMEM_LIMIT = 56 * 1024 * 1024

COL_AB, COL_AC, COL_AX, COL_BU, COL_BG, COL_Q, COL_KV, COL_GATES = 0, 2, 4, 6, 8, 10, 12, 13


def _params(sem):
    return pltpu.CompilerParams(dimension_semantics=sem, vmem_limit_bytes=VMEM_LIMIT)


def _cond_row(m, tiles_ctx, tiles_per_lat):
    return jnp.where(m < tiles_ctx, 0, 1 + (m - tiles_ctx) // tiles_per_lat)


def _layer_norm(x, g, b):
    mu = jnp.mean(x, axis=-1, keepdims=True)
    xc = x - mu
    var = jnp.mean(xc * xc, axis=-1, keepdims=True)
    return xc * lax.rsqrt(var + LN_EPS) * g + b


def _sigmoid(x):
    return 1.0 / (1.0 + jnp.exp(-x))


ROW_TILE = D // LANE


def _store_rows(ref, x):
    for j in range(ROW_TILE):
        ref[pl.ds(j, x.shape[0], stride=ROW_TILE), :] = x[:, j * LANE:(j + 1) * LANE]


def _load_rows(ref, n_rows):
    return jnp.concatenate([ref[pl.ds(j, n_rows, stride=ROW_TILE), :] for j in range(ROW_TILE)], axis=1)


def _mod_kernel(cond_ref, w_ref, b_ref, o_ref):
    cnd = cond_ref[...]
    s = (cnd * _sigmoid(cnd)).astype(BF16)
    o_ref[...] = jnp.dot(s, w_ref[...].astype(BF16), preferred_element_type=F32) + b_ref[...]


def _modulation(cond16, w_ada, b_ada):
    n_col = N_MOD * D
    tn = 1024
    return pl.pallas_call(
        _mod_kernel,
        out_shape=jax.ShapeDtypeStruct((DEPTH, 16, n_col), F32),
        grid=(DEPTH, n_col // tn),
        in_specs=[
            pl.BlockSpec((16, D), lambda l, n: (0, 0)),
            pl.BlockSpec((None, D, tn), lambda l, n: (l, 0, n)),
            pl.BlockSpec((None, 1, tn), lambda l, n: (l, 0, n)),
        ],
        out_specs=pl.BlockSpec((None, 16, tn), lambda l, n: (l, 0, n)),
        compiler_params=_params(("parallel", "parallel")),
        name="adaln_mod",
    )(cond16, w_ada, b_ada.reshape(DEPTH, 1, n_col))


def _modulate_kernel(x_ref, mod_ref, h_ref):
    h_ref[...] = (x_ref[...] * (1.0 + mod_ref[1:2, :]) + mod_ref[0:1, :]).astype(BF16)


def _modulate(x, mod_l):
    return pl.pallas_call(
        _modulate_kernel,
        out_shape=jax.ShapeDtypeStruct((T, D), BF16),
        grid=(N_TILES,),
        in_specs=[
            pl.BlockSpec((TM, D), lambda m: (m, 0)),
            pl.BlockSpec((None, N_MOD, D), lambda m: (_cond_row(m, CTX_TILES, TILES_PER_LAT), 0, 0)),
        ],
        out_specs=pl.BlockSpec((TM, D), lambda m: (m, 0)),
        compiler_params=_params(("parallel",)),
        name="modulate",
    )(x, mod_l)


def _proj_a_kernel(h_ref, wb_ref, wc_ref, wx_ref, ab_ref, acx_ref):
    h = h_ref[...]
    ab_ref[...] = jnp.dot(h, wb_ref[...], preferred_element_type=F32).astype(BF16)
    acx_ref[...] = (jnp.dot(h, wc_ref[...], preferred_element_type=F32)
                    * jnp.dot(h, wx_ref[...], preferred_element_type=F32))


def _proj_b_kernel(h_ref, wu_ref, wg_ref, glu_ref):
    h = h_ref[...]
    glu_ref[...] = (jnp.dot(h, wu_ref[...], preferred_element_type=F32)
                    * _sigmoid(jnp.dot(h, wg_ref[...], preferred_element_type=F32)))


def _proj_gate_kernel(h_ref, w_ref, g_ref):
    g_ref[...] = _sigmoid(jnp.dot(h_ref[...], w_ref[...], preferred_element_type=F32)).astype(BF16)


def _w_spec(l, col0, tn):
    return pl.BlockSpec((None, D, tn), lambda c, m: (l, 0, col0 + c))


def _proj_a(h, w_in, l):
    tn = 512
    out_spec = pl.BlockSpec((TMP, tn), lambda c, m: (m, c))
    return pl.pallas_call(
        _proj_a_kernel,
        out_shape=(jax.ShapeDtypeStruct((T, D), BF16), jax.ShapeDtypeStruct((T, D), F32)),
        grid=(D // tn, T // TMP),
        in_specs=[pl.BlockSpec((TMP, D), lambda c, m: (m, 0)),
                  _w_spec(l, COL_AB, tn), _w_spec(l, COL_AC, tn), _w_spec(l, COL_AX, tn)],
        out_specs=(out_spec, out_spec),
        compiler_params=_params(("parallel", "parallel")),
        name="proj_a",
    )(h, w_in, w_in, w_in)


def _proj_b(h, w_in, l):
    tn = 1024
    return pl.pallas_call(
        _proj_b_kernel,
        out_shape=jax.ShapeDtypeStruct((T, D), F32),
        grid=(D // tn, T // TMP),
        in_specs=[pl.BlockSpec((TMP, D), lambda c, m: (m, 0)),
                  _w_spec(l, COL_BU * 512 // tn, tn), _w_spec(l, COL_BG * 512 // tn, tn)],
        out_specs=pl.BlockSpec((TMP, tn), lambda c, m: (m, c)),
        compiler_params=_params(("parallel", "parallel")),
        name="proj_b",
    )(h, w_in, w_in)


def _proj_gates(h, w_gates, l):
    tn = 1024
    return pl.pallas_call(
        _proj_gate_kernel,
        out_shape=jax.ShapeDtypeStruct((T, 3 * D), BF16),
        grid=(3 * D // tn, T // TMP),
        in_specs=[pl.BlockSpec((TMP, D), lambda c, m: (m, 0)),
                  pl.BlockSpec((None, D, tn), lambda c, m: (l, 0, c))],
        out_specs=pl.BlockSpec((TMP, tn), lambda c, m: (m, c)),
        compiler_params=_params(("parallel", "parallel")),
        name="proj_gates",
    )(h, w_gates)


def _head_mean_square(x, ones_bd):
    out = []
    for c in range(x.shape[1] // LANE):
        sq = x[:, c * LANE:(c + 1) * LANE]
        sq = sq * sq
        hi = sq.astype(BF16)
        lo = (sq - hi.astype(F32)).astype(BF16)
        out.append(jnp.dot(hi, ones_bd, preferred_element_type=F32)
                   + jnp.dot(lo, ones_bd, preferred_element_type=F32))
    return jnp.concatenate(out, axis=1) * (1.0 / HEAD)


def _rope(x, cos, sin, first_half):
    out = []
    for c in range(x.shape[1] // LANE):
        xc = x[:, c * LANE:(c + 1) * LANE]
        partner = jnp.where(first_half, pltpu.roll(xc, LANE - 16, axis=1), pltpu.roll(xc, 16, axis=1))
        out.append(xc * cos + partner * sin)
    return jnp.concatenate(out, axis=1)


def _qkv_kernel(h_ref, wq_ref, wkv_ref, gq_ref, gk_ref, cos_ref, sin_ref, ones_ref,
                q_ref, katt_ref, vatt_ref, kn_ref, vf_ref):
    h = h_ref[...]
    ones_bd = ones_ref[...]
    cos = cos_ref[...]
    sin = sin_ref[...]
    lane = lax.broadcasted_iota(I32, (TM, LANE), 1)
    first_half = (lane & 16) == 0
    q = jnp.dot(h, wq_ref[...], preferred_element_type=F32)
    qn = q * lax.rsqrt(_head_mean_square(q, ones_bd) + RMS_EPS) * gq_ref[...]
    q_ref[...] = (_rope(qn, cos, sin, first_half) * (HEAD ** -0.5)).astype(BF16)
    kv = jnp.dot(h, wkv_ref[...], preferred_element_type=F32)
    k = kv[:, :D_KV]
    v = kv[:, D_KV:]
    kn = k * lax.rsqrt(_head_mean_square(k, ones_bd) + RMS_EPS) * gk_ref[...]
    kn_ref[...] = kn
    katt_ref[...] = _rope(kn, cos, sin, first_half).astype(BF16)
    vf_ref[...] = v
    vatt_ref[...] = v.astype(BF16)


def _proj_qkv(h, w_in, gq, gk, cos_tab, sin_tab, ones_bd, l):
    def tab_idx(m):
        return jnp.where(m < CTX_TILES, TILES_PER_LAT, (m - CTX_TILES) % TILES_PER_LAT)

    row = lambda m: (m, 0)
    return pl.pallas_call(
        _qkv_kernel,
        out_shape=(jax.ShapeDtypeStruct((T, D), BF16),
                   jax.ShapeDtypeStruct((T, D_KV), BF16),
                   jax.ShapeDtypeStruct((T, D_KV), BF16),
                   jax.ShapeDtypeStruct((T, D_KV), F32),
                   jax.ShapeDtypeStruct((T, D_KV), F32)),
        grid=(N_TILES,),
        in_specs=[
            pl.BlockSpec((TM, D), row),
            pl.BlockSpec((None, D, D), lambda m: (l, 0, COL_Q // 2)),
            pl.BlockSpec((None, D, 2 * D_KV), lambda m: (l, 0, COL_KV)),
            pl.BlockSpec((1, D), lambda m: (0, 0)),
            pl.BlockSpec((1, D_KV), lambda m: (0, 0)),
            pl.BlockSpec((TM, LANE), lambda m: (tab_idx(m), 0)),
            pl.BlockSpec((TM, LANE), lambda m: (tab_idx(m), 0)),
            pl.BlockSpec((LANE, LANE), lambda m: (0, 0)),
        ],
        out_specs=(pl.BlockSpec((TM, D), row), pl.BlockSpec((TM, D_KV), row),
                   pl.BlockSpec((TM, D_KV), row), pl.BlockSpec((TM, D_KV), row),
                   pl.BlockSpec((TM, D_KV), row)),
        compiler_params=_params(("parallel",)),
        name="proj_qkv",
    )(h, w_in, w_in, gq, gk, cos_tab, sin_tab, ones_bd)


def _rope_tables():
    lane = np.arange(LANE)
    j = lane % 16
    freqs = jnp.power(ROPE_THETA, -jnp.arange(16, dtype=F32) / 16)[j]
    pos = jnp.arange(LAT_LEN, dtype=I32)
    row = (pos // GRID_W).astype(F32)
    col = (pos % GRID_W).astype(F32)
    use_row = jnp.asarray((lane % HEAD) < HEAD // 2)
    p = jnp.where(use_row[None, :], row[:, None], col[:, None])
    ang = p * freqs[None, :]
    sign = jnp.asarray(np.where((lane & 16) == 0, -1.0, 1.0), F32)
    cos = jnp.concatenate([jnp.cos(ang), jnp.ones((TM, LANE), F32)], axis=0)
    sin = jnp.concatenate([jnp.sin(ang) * sign[None, :], jnp.zeros((TM, LANE), F32)], axis=0)
    return cos, sin


def _conv_kernel(acx_ref, acx_l_ref, acx_r_ref, ab_ref, glu_ref, glu_l_ref, glu_r_ref,
                 wa_ref, wb_ref, bb_ref, ng_ref, nb_ref, ya_ref, yb_ref, pad_a, pad_b, u_ref, sh_ref):
    s = pl.program_id(0)
    lat = s >= CTX_SEGS
    pos = (s - CTX_SEGS) % SEGS_PER_LAT
    has_left = jnp.logical_and(lat, pos != 0)
    has_right = jnp.logical_and(lat, pos != SEGS_PER_LAT - 1)

    pad_a[0:HALO_A, :] = jnp.where(has_left, acx_l_ref[...], 0.0)
    pad_a[HALO_A:HALO_A + SEG, :] = acx_ref[...]
    pad_a[HALO_A + SEG:, :] = jnp.where(has_right, acx_r_ref[...], 0.0)
    pad_b[0:HALO_B, :] = jnp.where(has_left, glu_l_ref[...], 0.0)
    pad_b[HALO_B:HALO_B + SEG, :] = glu_ref[...]
    pad_b[HALO_B + SEG:, :] = jnp.where(has_right, glu_r_ref[...], 0.0)

    conv_a = (wa_ref[0:1, :] * pad_a[HALO_A - 1:HALO_A - 1 + SEG, :]
              + wa_ref[1:2, :] * pad_a[HALO_A:HALO_A + SEG, :]
              + wa_ref[2:3, :] * pad_a[HALO_A + 1:HALO_A + 1 + SEG, :])
    ya_ref[...] = (ab_ref[...].astype(F32) * conv_a).astype(BF16)

    rows = 64
    kb = wb_ref.shape[0]

    def lane_chunk(c, carry):
        lanes = pl.ds(pl.multiple_of(c * LANE, LANE), LANE)
        for b in range(1, 8):
            sh_ref[b, :, :] = pad_b[b:b + SH_ROWS, lanes]
        for r in range(SEG // rows):
            acc = jnp.zeros((rows, LANE), F32)
            for k in range(kb):
                off = HALO_B + k - kb // 2
                row0 = r * rows + 8 * (off // 8)
                if off % 8 == 0:
                    src = pad_b[row0:row0 + rows, lanes]
                else:
                    src = sh_ref[off % 8, row0:row0 + rows, :]
                acc = acc + wb_ref[k:k + 1, lanes] * src
            u_ref[r * rows:(r + 1) * rows, lanes] = acc
        return carry

    lax.fori_loop(0, D // LANE, lane_chunk, 0)
    u = _layer_norm(u_ref[...] + bb_ref[...], ng_ref[...], nb_ref[...])
    yb_ref[...] = (u * _sigmoid(u)).astype(BF16)


def _convs(acx, ab, glu, conv_a_w, conv_b_w, conv_b_b, norm_g, norm_b):
    seg = lambda s: (s, 0)
    const = lambda s: (0, 0)
    ra, rb = SEG // HALO_A, SEG // HALO_B
    left_a = lambda s: (jnp.maximum(s * ra - 1, 0), 0)
    right_a = lambda s: (jnp.minimum((s + 1) * ra, T // HALO_A - 1), 0)
    left_b = lambda s: (jnp.maximum(s * rb - 1, 0), 0)
    right_b = lambda s: (jnp.minimum((s + 1) * rb, T // HALO_B - 1), 0)
    return pl.pallas_call(
        _conv_kernel,
        out_shape=(jax.ShapeDtypeStruct((T, D), BF16), jax.ShapeDtypeStruct((T, D), BF16)),
        grid=(N_SEG,),
        in_specs=[
            pl.BlockSpec((SEG, D), seg), pl.BlockSpec((HALO_A, D), left_a), pl.BlockSpec((HALO_A, D), right_a),
            pl.BlockSpec((SEG, D), seg),
            pl.BlockSpec((SEG, D), seg), pl.BlockSpec((HALO_B, D), left_b), pl.BlockSpec((HALO_B, D), right_b),
            pl.BlockSpec(conv_a_w.shape, const), pl.BlockSpec(conv_b_w.shape, const),
            pl.BlockSpec((1, D), const), pl.BlockSpec((1, D), const), pl.BlockSpec((1, D), const),
        ],
        out_specs=(pl.BlockSpec((SEG, D), seg), pl.BlockSpec((SEG, D), seg)),
        scratch_shapes=[pltpu.VMEM((SEG + 2 * HALO_A, D), F32),
                        pltpu.VMEM((SEG + 2 * HALO_B, D), F32),
                        pltpu.VMEM((SEG, D), F32),
                        pltpu.VMEM((8, SH_ROWS, LANE), F32)],
        compiler_params=_params(("parallel",)),
        name="convs",
    )(acx, acx, acx, ab, glu, glu, glu, conv_a_w, conv_b_w, conv_b_b, norm_g, norm_b)


def _attend(q_ref, key_refs, val_refs, o_ref):
    nt = (((1,), (1,)), ((), ()))
    keys = [r[...].astype(BF16) for r in key_refs]
    vals = [r[...].astype(BF16) for r in val_refs]
    for g in range(N_KV):
        kg = [k[:, g * HEAD:(g + 1) * HEAD] for k in keys]
        vg = [v[:, g * HEAD:(g + 1) * HEAD] for v in vals]
        for hh in range(GROUP):
            hd = g * GROUP + hh
            qh = q_ref[:, hd * HEAD:(hd + 1) * HEAD]
            s = [lax.dot_general(qh, k, nt, preferred_element_type=F32) for k in kg]
            mx = functools.reduce(jnp.maximum, [jnp.max(x, axis=-1, keepdims=True) for x in s])
            p = [jnp.exp(x - mx) for x in s]
            den = functools.reduce(jnp.add, [jnp.sum(x, axis=-1, keepdims=True) for x in p])
            acc = functools.reduce(jnp.add, [jnp.dot(x.astype(BF16), v, preferred_element_type=F32)
                                             for x, v in zip(p, vg)])
            o_ref[:, hd * HEAD:(hd + 1) * HEAD] = (acc / den).astype(BF16)


def _attn_ctx_kernel(q_ref, k_ref, v_ref, o_ref):
    _attend(q_ref, [k_ref], [v_ref], o_ref)


def _attn_lat_kernel(q_ref, k_ref, v_ref, ck_ref, cv_ref, o_in_ref, o_ref):
    del o_in_ref
    _attend(q_ref, [ck_ref, k_ref], [cv_ref, v_ref], o_ref)


def _attention(q, katt, vatt, cache_k, cache_v, l):
    o = pl.pallas_call(
        _attn_ctx_kernel,
        out_shape=jax.ShapeDtypeStruct((T, D), BF16),
        grid=(N_CTX_SEQ,),
        in_specs=[pl.BlockSpec((CTX_LEN, D), lambda b: (b, 0)),
                  pl.BlockSpec((CTX_LEN, D_KV), lambda b: (b, 0)),
                  pl.BlockSpec((CTX_LEN, D_KV), lambda b: (b, 0))],
        out_specs=pl.BlockSpec((CTX_LEN, D), lambda b: (b, 0)),
        compiler_params=_params(("parallel",)),
        name="attn_ctx",
    )(q, katt, vatt)
    seg0 = CTX_SEGS
    lat0 = T_CTX // LAT_LEN
    return pl.pallas_call(
        _attn_lat_kernel,
        out_shape=jax.ShapeDtypeStruct((T, D), BF16),
        grid=(N_LAT_SEQ, SEGS_PER_LAT),
        in_specs=[pl.BlockSpec((SEG, D), lambda b, i: (seg0 + b * SEGS_PER_LAT + i, 0)),
                  pl.BlockSpec((LAT_LEN, D_KV), lambda b, i: (lat0 + b, 0)),
                  pl.BlockSpec((LAT_LEN, D_KV), lambda b, i: (lat0 + b, 0)),
                  pl.BlockSpec((None, None, PAST, D_KV), lambda b, i: (b, l, 0, 0)),
                  pl.BlockSpec((None, None, PAST, D_KV), lambda b, i: (b, l, 0, 0)),
                  pl.BlockSpec(memory_space=pl.ANY)],
        out_specs=pl.BlockSpec((SEG, D), lambda b, i: (seg0 + b * SEGS_PER_LAT + i, 0)),
        input_output_aliases={5: 0},
        compiler_params=_params(("parallel", "parallel")),
        name="attn_lat",
    )(q, katt, vatt, cache_k, cache_v, o)


def _post_kernel(ya_ref, yb_ref, o_ref, g_ref, x_ref, mod_ref, wa_ref, wb_ref, wc_ref, wo_ref,
                 lng_ref, lnb_ref, wr_hi_ref, wr_lo_ref, rb_ref,
                 x1_ref, h2_ref, route_ref, rw_ref, cnt_ref, carry_ref):
    m = pl.program_id(0)

    @pl.when(m == 0)
    def _():
        carry_ref[...] = jnp.zeros_like(carry_ref)

    ya = jnp.dot(ya_ref[...], wa_ref[...], preferred_element_type=F32)
    yb = jnp.dot(yb_ref[...], wb_ref[...], preferred_element_type=F32)
    yc = jnp.dot(o_ref[...], wc_ref[...], preferred_element_type=F32)
    merged = (g_ref[:, 0:D].astype(F32) * ya + g_ref[:, D:2 * D].astype(F32) * yb
              + g_ref[:, 2 * D:3 * D].astype(F32) * yc)
    mix = jnp.dot(merged.astype(BF16), wo_ref[...], preferred_element_type=F32)
    x1 = _layer_norm(ALPHA * x_ref[...] + mod_ref[2:3, :] * mix, lng_ref[...], lnb_ref[...])
    x1_ref[...] = x1
    h2 = x1 * (1.0 + mod_ref[4:5, :]) + mod_ref[3:4, :]
    _store_rows(h2_ref, h2)

    hi = h2.astype(BF16)
    lo = (h2 - hi.astype(F32)).astype(BF16)
    wr_hi = wr_hi_ref[...]
    logits = (jnp.dot(hi, wr_hi, preferred_element_type=F32)
              + jnp.dot(lo, wr_hi, preferred_element_type=F32)
              + jnp.dot(hi, wr_lo_ref[...], preferred_element_type=F32))
    scores = _sigmoid(logits.T[0:N_EXP, :])
    sel = scores + rb_ref[...]

    gscore = []
    for g in range(N_GRP):
        r = [sel[g * EXP_PER_GRP + j:g * EXP_PER_GRP + j + 1, :] for j in range(EXP_PER_GRP)]
        pairs = [r[a] + r[b] for a in range(EXP_PER_GRP) for b in range(a + 1, EXP_PER_GRP)]
        gscore.append(functools.reduce(jnp.maximum, pairs))
    best = functools.reduce(jnp.maximum, gscore)
    gsel = jnp.full(best.shape, N_GRP - 1, I32)
    for g in range(N_GRP - 2, -1, -1):
        gsel = jnp.where(gscore[g] == best, g, gsel)

    eidx = lax.broadcasted_iota(I32, (N_EXP, TM), 0)
    neg = jnp.float32(-jnp.inf)
    cand = jnp.where((eidx // EXP_PER_GRP) == gsel, sel, neg)
    top1 = jnp.max(cand, axis=0, keepdims=True)
    idx1 = jnp.min(jnp.where(cand == top1, eidx, N_EXP), axis=0, keepdims=True)
    cand2 = jnp.where(eidx == idx1, neg, cand)
    top2 = jnp.max(cand2, axis=0, keepdims=True)
    idx2 = jnp.min(jnp.where(cand2 == top2, eidx, N_EXP), axis=0, keepdims=True)
    is1 = eidx == idx1
    is2 = eidx == idx2
    w1 = jnp.sum(jnp.where(is1, scores, 0.0), axis=0, keepdims=True)
    w2 = jnp.sum(jnp.where(is2, scores, 0.0), axis=0, keepdims=True)
    wsum = w1 + w2
    w1 = w1 / wsum
    w2 = w2 / wsum

    onehot = jnp.where(jnp.logical_or(is1, is2), 1.0, 0.0)
    r_i = lax.broadcasted_iota(I32, (TM, TM), 0)
    c_i = lax.broadcasted_iota(I32, (TM, TM), 1)
    upper = jnp.where(r_i < c_i, 1.0, 0.0).astype(BF16)
    prefix = jnp.dot(onehot.astype(BF16), upper, preferred_element_type=F32) + carry_ref[:, 0:1]
    rank1 = jnp.sum(jnp.where(is1, prefix, 0.0), axis=0, keepdims=True)
    rank2 = jnp.sum(jnp.where(is2, prefix, 0.0), axis=0, keepdims=True)
    carry = carry_ref[...] + jnp.sum(onehot, axis=1, keepdims=True)
    carry_ref[...] = carry
    cnt_ref[...] = carry.astype(I32)

    route_ref[0:1, :] = idx1
    route_ref[1:2, :] = idx2
    route_ref[2:3, :] = rank1.astype(I32)
    route_ref[3:4, :] = rank2.astype(I32)
    route_ref[4:8, :] = jnp.zeros((4, TM), I32)
    wrow = lax.broadcasted_iota(I32, (LANE, TM), 0)
    wcols = jnp.where(wrow == 0, w1, jnp.where(wrow == 1, w2, 0.0))
    rw_ref[...] = wcols.T


def _post(ya_pre, yb_pre, o, gates, x, mod_l, wa, wb, wc, wo, ln_g, ln_b, wr_hi, wr_lo, rbias, l):
    row = lambda m: (m, 0)
    const = lambda m: (0, 0)
    wspec = pl.BlockSpec((None, D, D), lambda m: (l, 0, 0))
    return pl.pallas_call(
        _post_kernel,
        out_shape=(jax.ShapeDtypeStruct((T, D), F32),
                   jax.ShapeDtypeStruct((T * ROW_TILE, LANE), F32),
                   jax.ShapeDtypeStruct((N_TILES, 8, TM), I32),
                   jax.ShapeDtypeStruct((T, LANE), F32),
                   jax.ShapeDtypeStruct((N_EXP, LANE), I32)),
        grid=(N_TILES,),
        in_specs=[pl.BlockSpec((TM, D), row), pl.BlockSpec((TM, D), row), pl.BlockSpec((TM, D), row),
                  pl.BlockSpec((TM, 3 * D), row), pl.BlockSpec((TM, D), row),
                  pl.BlockSpec((None, N_MOD, D), lambda m: (_cond_row(m, CTX_TILES, TILES_PER_LAT), 0, 0)),
                  wspec, wspec, wspec, wspec,
                  pl.BlockSpec((1, D), const), pl.BlockSpec((1, D), const),
                  pl.BlockSpec((D, LANE), const), pl.BlockSpec((D, LANE), const),
                  pl.BlockSpec((N_EXP, 1), const)],
        out_specs=(pl.BlockSpec((TM, D), row), pl.BlockSpec((TM * ROW_TILE, LANE), row),
                   pl.BlockSpec((None, 8, TM), lambda m: (m, 0, 0)),
                   pl.BlockSpec((TM, LANE), row),
                   pl.BlockSpec((N_EXP, LANE), const)),
        scratch_shapes=[pltpu.VMEM((N_EXP, LANE), F32)],
        compiler_params=_params(("arbitrary",)),
        name="merge_ln_router",
    )(ya_pre, yb_pre, o, gates, x, mod_l, wa, wb, wc, wo, ln_g, ln_b, wr_hi, wr_lo, rbias)


def _row_copy(src, dst, src_row, dst_row, sem):
    return pltpu.make_async_copy(src.at[pl.ds(pl.multiple_of(src_row * ROW_TILE, ROW_TILE), ROW_TILE), :],
                                 dst.at[pl.ds(pl.multiple_of(dst_row * ROW_TILE, ROW_TILE), ROW_TILE), :], sem)


DMA_UNROLL = 4


def _slot_owner_kernel(slot_ref, owner_ref):
    def clear(i, carry):
        for u in range(8):
            owner_ref[i * 8 + u] = 0
        return carry

    lax.fori_loop(0, N_SLOTS // 8, clear, 0)

    def fill(i, carry):
        for u in range(DMA_UNROLL):
            t = i * DMA_UNROLL + u
            owner_ref[slot_ref[t]] = t
            owner_ref[slot_ref[T + t]] = t
        return carry

    lax.fori_loop(0, T // DMA_UNROLL, fill, 0)


def _slot_owner(slots_flat):
    return pl.pallas_call(
        _slot_owner_kernel,
        out_shape=jax.ShapeDtypeStruct((N_SLOTS,), I32),
        in_specs=[pl.BlockSpec(memory_space=pltpu.SMEM)],
        out_specs=pl.BlockSpec(memory_space=pltpu.SMEM),
        name="moe_slot_owner",
    )(slots_flat)


def _ffn_kernel(blk_row_ref, blk_exp_ref, n_act_ref, owner_ref, h2_hbm, wgu_ref, wd_ref, y_ref, buf, sems):
    del blk_exp_ref
    i = pl.program_id(0)
    n_act = n_act_ref[0]
    half = i % 2

    def gather_block(b, dst_half):
        base = blk_row_ref[b] * FFN_BLK

        def body(j, carry):
            for u in range(2 * DMA_UNROLL):
                r = j * 2 * DMA_UNROLL + u
                _row_copy(h2_hbm, buf.at[dst_half], owner_ref[base + r], r, sems.at[dst_half]).start()
            return carry

        lax.fori_loop(0, FFN_BLK // (2 * DMA_UNROLL), body, 0)

    @pl.when(i == 0)
    def _():
        gather_block(0, 0)

    @pl.when(i + 1 < n_act)
    def _():
        gather_block(i + 1, 1 - half)

    @pl.when(i < n_act)
    def _():
        def drain(j, carry):
            for _ in range(2 * DMA_UNROLL):
                _row_copy(h2_hbm, buf.at[half], 0, 0, sems.at[half]).wait()
            return carry

        lax.fori_loop(0, FFN_BLK // (2 * DMA_UNROLL), drain, 0)
        x = _load_rows(buf.at[half], FFN_BLK).astype(BF16)
        gu = jnp.dot(x, wgu_ref[...], preferred_element_type=F32)
        gate = gu[:, :D_EXP]
        up = gu[:, D_EXP:]
        act = (gate * _sigmoid(gate) * up).astype(BF16)
        _store_rows(y_ref, jnp.dot(act, wd_ref[...], preferred_element_type=F32))


def _expert_ffn(blk_row, blk_exp, n_act, owner, h2, wgu, wd, l):
    grid_spec = pltpu.PrefetchScalarGridSpec(
        num_scalar_prefetch=4,
        grid=(N_FFN_BLOCKS,),
        in_specs=[pl.BlockSpec(memory_space=pl.ANY),
                  pl.BlockSpec((None, None, D, 2 * D_EXP), lambda i, br, be, na, ow: (l, be[i], 0, 0)),
                  pl.BlockSpec((None, None, D_EXP, D), lambda i, br, be, na, ow: (l, be[i], 0, 0))],
        out_specs=pl.BlockSpec((FFN_BLK * ROW_TILE, LANE), lambda i, br, be, na, ow: (br[i], 0)),
        scratch_shapes=[pltpu.VMEM((2, FFN_BLK * ROW_TILE, LANE), F32), pltpu.SemaphoreType.DMA((2,))],
    )
    return pl.pallas_call(
        _ffn_kernel,
        out_shape=jax.ShapeDtypeStruct((N_SLOTS * ROW_TILE, LANE), F32),
        grid_spec=grid_spec,
        compiler_params=_params(("arbitrary",)),
        name="expert_ffn",
    )(blk_row, blk_exp, n_act, owner, h2, wgu, wd)


def _block_plan(counts):
    nblk = (counts + FFN_BLK - 1) // FFN_BLK
    end = jnp.cumsum(nblk)
    start = end - nblk
    n_act = end[-1]
    i = jnp.arange(N_FFN_BLOCKS, dtype=I32)
    i_eff = jnp.minimum(i, n_act - 1)
    e = jnp.minimum(jnp.sum(i_eff[:, None] >= end[None, :], axis=1), N_EXP - 1).astype(I32)
    return (start * FFN_BLK).astype(I32), i_eff.astype(I32), e, n_act.reshape(1).astype(I32)


def _combine_kernel(slot_ref, slot_next_ref, y_hbm, rw_ref, x1_ref, mod_ref, modn_ref,
                    lng_ref, lnb_ref, x2_ref, hn_ref, buf, sems):
    m = pl.program_id(0)
    half = m % 2

    def gather_tile(s_ref, dst_half):
        def body(i, carry):
            for u in range(DMA_UNROLL):
                t = i * DMA_UNROLL + u
                for k in range(2):
                    _row_copy(y_hbm, buf.at[dst_half, k], s_ref[k, t], t, sems.at[dst_half]).start()
            return carry

        lax.fori_loop(0, TM // DMA_UNROLL, body, 0)

    @pl.when(m == 0)
    def _():
        gather_tile(slot_ref, 0)

    @pl.when(m + 1 < pl.num_programs(0))
    def _():
        gather_tile(slot_next_ref, 1 - half)

    def drain(i, carry):
        for _ in range(2 * DMA_UNROLL):
            _row_copy(y_hbm, buf.at[half, 0], 0, 0, sems.at[half]).wait()
        return carry

    lax.fori_loop(0, TM // DMA_UNROLL, drain, 0)
    f = rw_ref[:, 0:1] * _load_rows(buf.at[half, 0], TM) + rw_ref[:, 1:2] * _load_rows(buf.at[half, 1], TM)
    x2 = _layer_norm(ALPHA * x1_ref[...] + mod_ref[5:6, :] * f, lng_ref[...], lnb_ref[...])
    x2_ref[...] = x2
    hn_ref[...] = (x2 * (1.0 + modn_ref[1:2, :]) + modn_ref[0:1, :]).astype(BF16)


def _combine(slots, y_slots, rw, x1, mod_l, mod_next, ln_g, ln_b):
    row = lambda m: (m, 0)
    const = lambda m: (0, 0)
    mod_spec = pl.BlockSpec((None, N_MOD, D), lambda m: (_cond_row(m, CTX_TILES, TILES_PER_LAT), 0, 0))
    return pl.pallas_call(
        _combine_kernel,
        out_shape=(jax.ShapeDtypeStruct((T, D), F32), jax.ShapeDtypeStruct((T, D), BF16)),
        grid=(N_TILES,),
        in_specs=[pl.BlockSpec((None, 2, TM), lambda m: (m, 0, 0), memory_space=pltpu.SMEM),
                  pl.BlockSpec((None, 2, TM), lambda m: (jnp.minimum(m + 1, N_TILES - 1), 0, 0),
                               memory_space=pltpu.SMEM),
                  pl.BlockSpec(memory_space=pl.ANY),
                  pl.BlockSpec((TM, LANE), row), pl.BlockSpec((TM, D), row),
                  mod_spec, mod_spec,
                  pl.BlockSpec((1, D), const), pl.BlockSpec((1, D), const)],
        out_specs=(pl.BlockSpec((TM, D), row), pl.BlockSpec((TM, D), row)),
        scratch_shapes=[pltpu.VMEM((2, 2, TM * ROW_TILE, LANE), F32), pltpu.SemaphoreType.DMA((2,))],
        compiler_params=_params(("arbitrary",)),
        name="moe_combine_ln",
    )(slots, slots, y_slots, rw, x1, mod_l, mod_next, ln_g, ln_b)


def kernel(x_prompt, x_sample, c, c_ctx, cache_k, cache_v, w_in, conv_a_w, w_a_out, conv_b_w, conv_b_b,
           norm_b_g, norm_b_b, w_b_out, q_norm_g, k_norm_g, w_c_out, w_o, w_ada, b_ada, ln_g, ln_b,
           w_router, router_bias, w_gate_up, w_down):
    x = jnp.concatenate([x_prompt.reshape(T_CTX, D), x_sample.reshape(T_LAT, D)], axis=0)
    cond16 = jnp.concatenate([c_ctx[None, :], c, jnp.zeros((16 - 1 - N_LAT_SEQ, D), F32)], axis=0)
    mod = _modulation(cond16, w_ada, b_ada).reshape(DEPTH, 16, N_MOD, D)

    w_in_b = w_in[:, :, :COL_GATES * 512].astype(BF16)
    w_gates_b = w_in[:, :, COL_GATES * 512:].astype(BF16)
    wa_b, wb_b, wc_b, wo_b = (w.astype(BF16) for w in (w_a_out, w_b_out, w_c_out, w_o))
    wgu_b = w_gate_up.astype(BF16)
    wd_b = w_down.astype(BF16)
    wr = jnp.pad(w_router, ((0, 0), (0, LANE - N_EXP)))
    wr_hi = wr.astype(BF16)
    wr_lo = (wr - wr_hi.astype(F32)).astype(BF16)
    rbias = router_bias.reshape(N_EXP, 1)

    cos_tab, sin_tab = _rope_tables()
    half = np.arange(LANE) // HEAD
    ones_bd = jnp.asarray(half[:, None] == half[None, :], BF16)
    ck = cache_k.reshape(N_LAT_SEQ, DEPTH, PAST, D_KV)
    cv = cache_v.reshape(N_LAT_SEQ, DEPTH, PAST, D_KV)

    h = _modulate(x, mod[0])
    new_k, new_v = [], []
    for l in range(DEPTH):
        ab, acx = _proj_a(h, w_in_b, l)
        glu = _proj_b(h, w_in_b, l)
        gates = _proj_gates(h, w_gates_b, l)
        q, katt, vatt, kn, vf = _proj_qkv(h, w_in_b, jnp.tile(q_norm_g[l], N_Q)[None, :],
                                          jnp.tile(k_norm_g[l], N_KV)[None, :], cos_tab, sin_tab, ones_bd, l)
        new_k.append(kn[:T_CTX].reshape(N_CTX_SEQ, CTX_LEN, N_KV, HEAD))
        new_v.append(vf[:T_CTX].reshape(N_CTX_SEQ, CTX_LEN, N_KV, HEAD))
        ya_pre, yb_pre = _convs(acx, ab, glu, conv_a_w[l], conv_b_w[l], conv_b_b[l][None, :],
                                norm_b_g[l][None, :], norm_b_b[l][None, :])
        o = _attention(q, katt, vatt, ck, cv, l)
        x1, h2, route, rw, counts = _post(ya_pre, yb_pre, o, gates, x, mod[l], wa_b, wb_b, wc_b, wo_b,
                                          ln_g[l, 0][None, :], ln_b[l, 0][None, :], wr_hi, wr_lo, rbias, l)
        cnt = counts[:, 0]
        seg_start, blk_row, blk_exp, n_act = _block_plan(cnt)
        is_exp = route[:, 0:2, :, None] == jnp.arange(N_EXP, dtype=I32)
        slots = jnp.sum(jnp.where(is_exp, seg_start, 0), axis=-1) + route[:, 2:4, :]
        owner = _slot_owner(slots.transpose(1, 0, 2).reshape(2 * T))
        y_slots = _expert_ffn(blk_row, blk_exp, n_act, owner, h2, wgu_b, wd_b, l)
        x, h = _combine(slots, y_slots, rw, x1, mod[l], mod[min(l + 1, DEPTH - 1)],
                        ln_g[l, 1][None, :], ln_b[l, 1][None, :])
    y_prompt = x[:T_CTX].reshape(N_CTX_SEQ, CTX_LEN, D)
    y_sample = x[T_CTX:].reshape(N_LAT_SEQ, LAT_LEN, D)
    return y_prompt, y_sample, jnp.stack(new_k, axis=1), jnp.stack(new_v, axis=1)
```

```python
import functools

import numpy as np
import jax
import jax.numpy as jnp
from jax import lax
from jax.experimental import pallas as pl
from jax.experimental.pallas import tpu as pltpu

F32 = jnp.float32
BF16 = jnp.bfloat16
I32 = jnp.int32

D = 1024
DEPTH = 4
N_CTX_SEQ = 16
CTX_LEN = 256
N_LAT_SEQ = 8
LAT_LEN = 1024
PAST = 512
T_CTX = N_CTX_SEQ * CTX_LEN
T_LAT = N_LAT_SEQ * LAT_LEN
T = T_CTX + T_LAT
GRID_W = 64
HEAD = 64
N_Q = 16
N_KV = 4
GROUP = 4
D_KV = N_KV * HEAD
N_EXP = 16
N_GRP = 4
EXP_PER_GRP = 4
D_EXP = 512
IN_COLS = 9728
N_MOD = 6
ALPHA = (2 * DEPTH) ** 0.25
LN_EPS = 1e-5
RMS_EPS = 1e-6
ROPE_THETA = 10000.0

LANE = 128
TM = 512
N_TILES = T // TM
CTX_TILES = T_CTX // TM
TILES_PER_LAT = LAT_LEN // TM
TMP = 1024
SEG = 256
N_SEG = T // SEG
CTX_SEGS = T_CTX // SEG
SEGS_PER_LAT = LAT_LEN // SEG
HALO_A = 8
HALO_B = 16
SH_ROWS = SEG + 2 * HALO_B - 8
FFN_BLK = 512
N_FFN_BLOCKS = (2 * T) // FFN_BLK + N_EXP
N_SLOTS = N_FFN_BLOCKS * FFN_BLK
VMEM_LIMIT = 56 * 1024 * 1024

COL_AB, COL_AC, COL_AX, COL_BU, COL_BG, COL_Q, COL_KV, COL_GATES = 0, 2, 4, 6, 8, 10, 12, 13


def _params(sem):
    return pltpu.CompilerParams(dimension_semantics=sem, vmem_limit_bytes=VMEM_LIMIT)


def _cond_row(m, tiles_ctx, tiles_per_lat):
    return jnp.where(m < tiles_ctx, 0, 1 + (m - tiles_ctx) // tiles_per_lat)


def _layer_norm(x, g, b):
    mu = jnp.mean(x, axis=-1, keepdims=True)
    xc = x - mu
    var = jnp.mean(xc * xc, axis=-1, keepdims=True)
    return xc * lax.rsqrt(var + LN_EPS) * g + b


def _sigmoid(x):
    return 1.0 / (1.0 + jnp.exp(-x))


ROW_TILE = D // LANE


def _store_rows(ref, x):
    for j in range(ROW_TILE):
        ref[pl.ds(j, x.shape[0], stride=ROW_TILE), :] = x[:, j * LANE:(j + 1) * LANE]


def _load_rows(ref, n_rows):
    return jnp.concatenate([ref[pl.ds(j, n_rows, stride=ROW_TILE), :] for j in range(ROW_TILE)], axis=1)


def _mod_kernel(cond_ref, w_ref, b_ref, o_ref):
    cnd = cond_ref[...]
    s = (cnd * _sigmoid(cnd)).astype(BF16)
    o_ref[...] = jnp.dot(s, w_ref[...].astype(BF16), preferred_element_type=F32) + b_ref[...]


def _modulation(cond16, w_ada, b_ada):
    n_col = N_MOD * D
    tn = 1024
    return pl.pallas_call(
        _mod_kernel,
        out_shape=jax.ShapeDtypeStruct((DEPTH, 16, n_col), F32),
        grid=(DEPTH, n_col // tn),
        in_specs=[
            pl.BlockSpec((16, D), lambda l, n: (0, 0)),
            pl.BlockSpec((None, D, tn), lambda l, n: (l, 0, n)),
            pl.BlockSpec((None, 1, tn), lambda l, n: (l, 0, n)),
        ],
        out_specs=pl.BlockSpec((None, 16, tn), lambda l, n: (l, 0, n)),
        compiler_params=_params(("parallel", "parallel")),
        name="adaln_mod",
    )(cond16, w_ada, b_ada.reshape(DEPTH, 1, n_col))


def _modulate_kernel(x_ref, mod_ref, h_ref):
    h_ref[...] = (x_ref[...] * (1.0 + mod_ref[1:2, :]) + mod_ref[0:1, :]).astype(BF16)


def _modulate(x, mod_l):
    return pl.pallas_call(
        _modulate_kernel,
        out_shape=jax.ShapeDtypeStruct((T, D), BF16),
        grid=(N_TILES,),
        in_specs=[
            pl.BlockSpec((TM, D), lambda m: (m, 0)),
            pl.BlockSpec((None, N_MOD, D), lambda m: (_cond_row(m, CTX_TILES, TILES_PER_LAT), 0, 0)),
        ],
        out_specs=pl.BlockSpec((TM, D), lambda m: (m, 0)),
        compiler_params=_params(("parallel",)),
        name="modulate",
    )(x, mod_l)


WCOL = 512


def _cast_weights(first, pairs):
    @pl.when(first)
    def _():
        for src, dst in pairs:
            dst[...] = src[...].astype(BF16)


def _proj_a_kernel(h_ref, wb_ref, wc_ref, wx_ref, ab_ref, acx_ref, wb_s, wc_s, wx_s):
    _cast_weights(pl.program_id(1) == 0, [(wb_ref, wb_s), (wc_ref, wc_s), (wx_ref, wx_s)])
    h = h_ref[...]
    ab_ref[...] = jnp.dot(h, wb_s[...], preferred_element_type=F32).astype(BF16)
    acx_ref[...] = (jnp.dot(h, wc_s[...], preferred_element_type=F32)
                    * jnp.dot(h, wx_s[...], preferred_element_type=F32))


def _proj_b_kernel(h_ref, wu_ref, wg_ref, glu_ref, wu_s, wg_s):
    _cast_weights(pl.program_id(1) == 0, [(wu_ref, wu_s), (wg_ref, wg_s)])
    h = h_ref[...]
    glu_ref[...] = (jnp.dot(h, wu_s[...], preferred_element_type=F32)
                    * _sigmoid(jnp.dot(h, wg_s[...], preferred_element_type=F32)))


def _proj_gate_kernel(h_ref, w0_ref, w1_ref, g_ref, w_s):
    _cast_weights(pl.program_id(1) == 0, [(w0_ref, w_s.at[:, 0:WCOL]), (w1_ref, w_s.at[:, WCOL:2 * WCOL])])
    g_ref[...] = _sigmoid(jnp.dot(h_ref[...], w_s[...], preferred_element_type=F32)).astype(BF16)


def _w_spec(l, col0, step=1):
    return pl.BlockSpec((None, D, WCOL), lambda c, m: (l, 0, col0 + step * c))


def _proj_a(h, w_in, l):
    out_spec = pl.BlockSpec((TMP, WCOL), lambda c, m: (m, c))
    return pl.pallas_call(
        _proj_a_kernel,
        out_shape=(jax.ShapeDtypeStruct((T, D), BF16), jax.ShapeDtypeStruct((T, D), F32)),
        grid=(D // WCOL, T // TMP),
        in_specs=[pl.BlockSpec((TMP, D), lambda c, m: (m, 0)),
                  _w_spec(l, COL_AB), _w_spec(l, COL_AC), _w_spec(l, COL_AX)],
        out_specs=(out_spec, out_spec),
        scratch_shapes=[pltpu.VMEM((D, WCOL), BF16)] * 3,
        compiler_params=_params(("parallel", "arbitrary")),
        name="proj_a",
    )(h, w_in, w_in, w_in)


def _proj_b(h, w_in, l):
    return pl.pallas_call(
        _proj_b_kernel,
        out_shape=jax.ShapeDtypeStruct((T, D), F32),
        grid=(D // WCOL, T // TMP),
        in_specs=[pl.BlockSpec((TMP, D), lambda c, m: (m, 0)), _w_spec(l, COL_BU), _w_spec(l, COL_BG)],
        out_specs=pl.BlockSpec((TMP, WCOL), lambda c, m: (m, c)),
        scratch_shapes=[pltpu.VMEM((D, WCOL), BF16)] * 2,
        compiler_params=_params(("parallel", "arbitrary")),
        name="proj_b",
    )(h, w_in, w_in)


def _proj_gates(h, w_in, l):
    tn = 2 * WCOL
    return pl.pallas_call(
        _proj_gate_kernel,
        out_shape=jax.ShapeDtypeStruct((T, 3 * D), BF16),
        grid=(3 * D // tn, T // TMP),
        in_specs=[pl.BlockSpec((TMP, D), lambda c, m: (m, 0)),
                  _w_spec(l, COL_GATES, 2), _w_spec(l, COL_GATES + 1, 2)],
        out_specs=pl.BlockSpec((TMP, tn), lambda c, m: (m, c)),
        scratch_shapes=[pltpu.VMEM((D, tn), BF16)],
        compiler_params=_params(("parallel", "arbitrary")),
        name="proj_gates",
    )(h, w_in, w_in)


def _head_mean_square(x, ones_bd):
    out = []
    for c in range(x.shape[1] // LANE):
        sq = x[:, c * LANE:(c + 1) * LANE]
        sq = sq * sq
        hi = sq.astype(BF16)
        lo = (sq - hi.astype(F32)).astype(BF16)
        out.append(jnp.dot(hi, ones_bd, preferred_element_type=F32)
                   + jnp.dot(lo, ones_bd, preferred_element_type=F32))
    return jnp.concatenate(out, axis=1) * (1.0 / HEAD)


def _rope(x, cos, sin, first_half):
    out = []
    for c in range(x.shape[1] // LANE):
        xc = x[:, c * LANE:(c + 1) * LANE]
        partner = jnp.where(first_half, pltpu.roll(xc, LANE - 16, axis=1), pltpu.roll(xc, 16, axis=1))
        out.append(xc * cos + partner * sin)
    return jnp.concatenate(out, axis=1)


def _qkv_kernel(h_ref, wq_ref, wkv_ref, gq_ref, gk_ref, cos_ref, sin_ref, ones_ref,
                q_ref, katt_ref, vatt_ref, kn_ref, vf_ref, wq_s, wkv_s):
    _cast_weights(pl.program_id(0) == 0, [(wq_ref, wq_s), (wkv_ref, wkv_s)])
    h = h_ref[...]
    ones_bd = ones_ref[...]
    cos = cos_ref[...]
    sin = sin_ref[...]
    lane = lax.broadcasted_iota(I32, (TM, LANE), 1)
    first_half = (lane & 16) == 0
    q = jnp.dot(h, wq_s[...], preferred_element_type=F32)
    qn = q * lax.rsqrt(_head_mean_square(q, ones_bd) + RMS_EPS) * gq_ref[...]
    q_ref[...] = (_rope(qn, cos, sin, first_half) * (HEAD ** -0.5)).astype(BF16)
    kv = jnp.dot(h, wkv_s[...], preferred_element_type=F32)
    k = kv[:, :D_KV]
    v = kv[:, D_KV:]
    kn = k * lax.rsqrt(_head_mean_square(k, ones_bd) + RMS_EPS) * gk_ref[...]
    kn_ref[...] = kn
    katt_ref[...] = _rope(kn, cos, sin, first_half).astype(BF16)
    vf_ref[...] = v
    vatt_ref[...] = v.astype(BF16)


def _proj_qkv(h, w_in, gq, gk, cos_tab, sin_tab, ones_bd, l):
    def tab_idx(m):
        return jnp.where(m < CTX_TILES, TILES_PER_LAT, (m - CTX_TILES) % TILES_PER_LAT)

    row = lambda m: (m, 0)
    return pl.pallas_call(
        _qkv_kernel,
        out_shape=(jax.ShapeDtypeStruct((T, D), BF16),
                   jax.ShapeDtypeStruct((T, D_KV), BF16),
                   jax.ShapeDtypeStruct((T, D_KV), BF16),
                   jax.ShapeDtypeStruct((T, D_KV), F32),
                   jax.ShapeDtypeStruct((T, D_KV), F32)),
        grid=(N_TILES,),
        in_specs=[
            pl.BlockSpec((TM, D), row),
            pl.BlockSpec((None, D, D), lambda m: (l, 0, COL_Q // 2)),
            pl.BlockSpec((None, D, 2 * D_KV), lambda m: (l, 0, COL_KV)),
            pl.BlockSpec((1, D), lambda m: (0, 0)),
            pl.BlockSpec((1, D_KV), lambda m: (0, 0)),
            pl.BlockSpec((TM, LANE), lambda m: (tab_idx(m), 0)),
            pl.BlockSpec((TM, LANE), lambda m: (tab_idx(m), 0)),
            pl.BlockSpec((LANE, LANE), lambda m: (0, 0)),
        ],
        out_specs=(pl.BlockSpec((TM, D), row), pl.BlockSpec((TM, D_KV), row),
                   pl.BlockSpec((TM, D_KV), row), pl.BlockSpec((TM, D_KV), row),
                   pl.BlockSpec((TM, D_KV), row)),
        scratch_shapes=[pltpu.VMEM((D, D), BF16), pltpu.VMEM((D, 2 * D_KV), BF16)],
        compiler_params=_params(("arbitrary",)),
        name="proj_qkv",
    )(h, w_in, w_in, gq, gk, cos_tab, sin_tab, ones_bd)


def _rope_tables():
    lane = np.arange(LANE)
    j = lane % 16
    freqs = jnp.power(ROPE_THETA, -jnp.arange(16, dtype=F32) / 16)[j]
    pos = jnp.arange(LAT_LEN, dtype=I32)
    row = (pos // GRID_W).astype(F32)
    col = (pos % GRID_W).astype(F32)
    use_row = jnp.asarray((lane % HEAD) < HEAD // 2)
    p = jnp.where(use_row[None, :], row[:, None], col[:, None])
    ang = p * freqs[None, :]
    sign = jnp.asarray(np.where((lane & 16) == 0, -1.0, 1.0), F32)
    cos = jnp.concatenate([jnp.cos(ang), jnp.ones((TM, LANE), F32)], axis=0)
    sin = jnp.concatenate([jnp.sin(ang) * sign[None, :], jnp.zeros((TM, LANE), F32)], axis=0)
    return cos, sin


def _conv_kernel(acx_ref, acx_l_ref, acx_r_ref, ab_ref, glu_ref, glu_l_ref, glu_r_ref,
                 wa_ref, wb_ref, bb_ref, ng_ref, nb_ref, ya_ref, yb_ref, pad_a, pad_b, u_ref, sh_ref):
    s = pl.program_id(0)
    lat = s >= CTX_SEGS
    pos = (s - CTX_SEGS) % SEGS_PER_LAT
    has_left = jnp.logical_and(lat, pos != 0)
    has_right = jnp.logical_and(lat, pos != SEGS_PER_LAT - 1)

    pad_a[0:HALO_A, :] = jnp.where(has_left, acx_l_ref[...], 0.0)
    pad_a[HALO_A:HALO_A + SEG, :] = acx_ref[...]
    pad_a[HALO_A + SEG:, :] = jnp.where(has_right, acx_r_ref[...], 0.0)
    pad_b[0:HALO_B, :] = jnp.where(has_left, glu_l_ref[...], 0.0)
    pad_b[HALO_B:HALO_B + SEG, :] = glu_ref[...]
    pad_b[HALO_B + SEG:, :] = jnp.where(has_right, glu_r_ref[...], 0.0)

    conv_a = (wa_ref[0:1, :] * pad_a[HALO_A - 1:HALO_A - 1 + SEG, :]
              + wa_ref[1:2, :] * pad_a[HALO_A:HALO_A + SEG, :]
              + wa_ref[2:3, :] * pad_a[HALO_A + 1:HALO_A + 1 + SEG, :])
    ya_ref[...] = (ab_ref[...].astype(F32) * conv_a).astype(BF16)

    rows = 64
    kb = wb_ref.shape[0]

    def lane_chunk(c, carry):
        lanes = pl.ds(pl.multiple_of(c * LANE, LANE), LANE)
        for b in range(1, 8):
            sh_ref[b, :, :] = pad_b[b:b + SH_ROWS, lanes]
        for r in range(SEG // rows):
            acc = jnp.zeros((rows, LANE), F32)
            for k in range(kb):
                off = HALO_B + k - kb // 2
                row0 = r * rows + 8 * (off // 8)
                if off % 8 == 0:
                    src = pad_b[row0:row0 + rows, lanes]
                else:
                    src = sh_ref[off % 8, row0:row0 + rows, :]
                acc = acc + wb_ref[k:k + 1, lanes] * src
            u_ref[r * rows:(r + 1) * rows, lanes] = acc
        return carry

    lax.fori_loop(0, D // LANE, lane_chunk, 0)
    u = _layer_norm(u_ref[...] + bb_ref[...], ng_ref[...], nb_ref[...])
    yb_ref[...] = (u * _sigmoid(u)).astype(BF16)


def _convs(acx, ab, glu, conv_a_w, conv_b_w, conv_b_b, norm_g, norm_b):
    seg = lambda s: (s, 0)
    const = lambda s: (0, 0)
    ra, rb = SEG // HALO_A, SEG // HALO_B
    left_a = lambda s: (jnp.maximum(s * ra - 1, 0), 0)
    right_a = lambda s: (jnp.minimum((s + 1) * ra, T // HALO_A - 1), 0)
    left_b = lambda s: (jnp.maximum(s * rb - 1, 0), 0)
    right_b = lambda s: (jnp.minimum((s + 1) * rb, T // HALO_B - 1), 0)
    return pl.pallas_call(
        _conv_kernel,
        out_shape=(jax.ShapeDtypeStruct((T, D), BF16), jax.ShapeDtypeStruct((T, D), BF16)),
        grid=(N_SEG,),
        in_specs=[
            pl.BlockSpec((SEG, D), seg), pl.BlockSpec((HALO_A, D), left_a), pl.BlockSpec((HALO_A, D), right_a),
            pl.BlockSpec((SEG, D), seg),
            pl.BlockSpec((SEG, D), seg), pl.BlockSpec((HALO_B, D), left_b), pl.BlockSpec((HALO_B, D), right_b),
            pl.BlockSpec(conv_a_w.shape, const), pl.BlockSpec(conv_b_w.shape, const),
            pl.BlockSpec((1, D), const), pl.BlockSpec((1, D), const), pl.BlockSpec((1, D), const),
        ],
        out_specs=(pl.BlockSpec((SEG, D), seg), pl.BlockSpec((SEG, D), seg)),
        scratch_shapes=[pltpu.VMEM((SEG + 2 * HALO_A, D), F32),
                        pltpu.VMEM((SEG + 2 * HALO_B, D), F32),
                        pltpu.VMEM((SEG, D), F32),
                        pltpu.VMEM((8, SH_ROWS, LANE), F32)],
        compiler_params=_params(("parallel",)),
        name="convs",
    )(acx, acx, acx, ab, glu, glu, glu, conv_a_w, conv_b_w, conv_b_b, norm_g, norm_b)


def _attend(q_ref, key_refs, val_refs, o_ref):
    nt = (((1,), (1,)), ((), ()))
    keys = [r[...].astype(BF16) for r in key_refs]
    vals = [r[...].astype(BF16) for r in val_refs]
    for g in range(N_KV):
        kg = [k[:, g * HEAD:(g + 1) * HEAD] for k in keys]
        vg = [v[:, g * HEAD:(g + 1) * HEAD] for v in vals]
        for hh in range(GROUP):
            hd = g * GROUP + hh
            qh = q_ref[:, hd * HEAD:(hd + 1) * HEAD]
            s = [lax.dot_general(qh, k, nt, preferred_element_type=F32) for k in kg]
            mx = functools.reduce(jnp.maximum, [jnp.max(x, axis=-1, keepdims=True) for x in s])
            p = [jnp.exp(x - mx) for x in s]
            den = functools.reduce(jnp.add, [jnp.sum(x, axis=-1, keepdims=True) for x in p])
            acc = functools.reduce(jnp.add, [jnp.dot(x.astype(BF16), v, preferred_element_type=F32)
                                             for x, v in zip(p, vg)])
            o_ref[:, hd * HEAD:(hd + 1) * HEAD] = (acc / den).astype(BF16)


def _attn_ctx_kernel(q_ref, k_ref, v_ref, o_ref):
    _attend(q_ref, [k_ref], [v_ref], o_ref)


def _attn_lat_kernel(q_ref, k_ref, v_ref, ck_ref, cv_ref, o_in_ref, o_ref):
    del o_in_ref
    _attend(q_ref, [ck_ref, k_ref], [cv_ref, v_ref], o_ref)


def _attention(q, katt, vatt, cache_k, cache_v, l):
    o = pl.pallas_call(
        _attn_ctx_kernel,
        out_shape=jax.ShapeDtypeStruct((T, D), BF16),
        grid=(N_CTX_SEQ,),
        in_specs=[pl.BlockSpec((CTX_LEN, D), lambda b: (b, 0)),
                  pl.BlockSpec((CTX_LEN, D_KV), lambda b: (b, 0)),
                  pl.BlockSpec((CTX_LEN, D_KV), lambda b: (b, 0))],
        out_specs=pl.BlockSpec((CTX_LEN, D), lambda b: (b, 0)),
        compiler_params=_params(("parallel",)),
        name="attn_ctx",
    )(q, katt, vatt)
    seg0 = CTX_SEGS
    lat0 = T_CTX // LAT_LEN
    return pl.pallas_call(
        _attn_lat_kernel,
        out_shape=jax.ShapeDtypeStruct((T, D), BF16),
        grid=(N_LAT_SEQ, SEGS_PER_LAT),
        in_specs=[pl.BlockSpec((SEG, D), lambda b, i: (seg0 + b * SEGS_PER_LAT + i, 0)),
                  pl.BlockSpec((LAT_LEN, D_KV), lambda b, i: (lat0 + b, 0)),
                  pl.BlockSpec((LAT_LEN, D_KV), lambda b, i: (lat0 + b, 0)),
                  pl.BlockSpec((None, None, PAST, D_KV), lambda b, i: (b, l, 0, 0)),
                  pl.BlockSpec((None, None, PAST, D_KV), lambda b, i: (b, l, 0, 0)),
                  pl.BlockSpec(memory_space=pl.ANY)],
        out_specs=pl.BlockSpec((SEG, D), lambda b, i: (seg0 + b * SEGS_PER_LAT + i, 0)),
        input_output_aliases={5: 0},
        compiler_params=_params(("parallel", "parallel")),
        name="attn_lat",
    )(q, katt, vatt, cache_k, cache_v, o)


def _post_kernel(ya_ref, yb_ref, o_ref, g_ref, x_ref, mod_ref, wa_ref, wb_ref, wc_ref, wo_ref,
                 lng_ref, lnb_ref, wr_hi_ref, wr_lo_ref, rb_ref,
                 x1_ref, h2_ref, route_ref, rw_ref, cnt_ref, carry_ref):
    m = pl.program_id(0)

    @pl.when(m == 0)
    def _():
        carry_ref[...] = jnp.zeros_like(carry_ref)

    ya = jnp.dot(ya_ref[...], wa_ref[...], preferred_element_type=F32)
    yb = jnp.dot(yb_ref[...], wb_ref[...], preferred_element_type=F32)
    yc = jnp.dot(o_ref[...], wc_ref[...], preferred_element_type=F32)
    merged = (g_ref[:, 0:D].astype(F32) * ya + g_ref[:, D:2 * D].astype(F32) * yb
              + g_ref[:, 2 * D:3 * D].astype(F32) * yc)
    mix = jnp.dot(merged.astype(BF16), wo_ref[...], preferred_element_type=F32)
    x1 = _layer_norm(ALPHA * x_ref[...] + mod_ref[2:3, :] * mix, lng_ref[...], lnb_ref[...])
    x1_ref[...] = x1
    h2 = x1 * (1.0 + mod_ref[4:5, :]) + mod_ref[3:4, :]
    _store_rows(h2_ref, h2)

    hi = h2.astype(BF16)
    lo = (h2 - hi.astype(F32)).astype(BF16)
    wr_hi = wr_hi_ref[...]
    logits = (jnp.dot(hi, wr_hi, preferred_element_type=F32)
              + jnp.dot(lo, wr_hi, preferred_element_type=F32)
              + jnp.dot(hi, wr_lo_ref[...], preferred_element_type=F32))
    scores = _sigmoid(logits.T[0:N_EXP, :])
    sel = scores + rb_ref[...]

    gscore = []
    for g in range(N_GRP):
        r = [sel[g * EXP_PER_GRP + j:g * EXP_PER_GRP + j + 1, :] for j in range(EXP_PER_GRP)]
        pairs = [r[a] + r[b] for a in range(EXP_PER_GRP) for b in range(a + 1, EXP_PER_GRP)]
        gscore.append(functools.reduce(jnp.maximum, pairs))
    best = functools.reduce(jnp.maximum, gscore)
    gsel = jnp.full(best.shape, N_GRP - 1, I32)
    for g in range(N_GRP - 2, -1, -1):
        gsel = jnp.where(gscore[g] == best, g, gsel)

    eidx = lax.broadcasted_iota(I32, (N_EXP, TM), 0)
    neg = jnp.float32(-jnp.inf)
    cand = jnp.where((eidx // EXP_PER_GRP) == gsel, sel, neg)
    top1 = jnp.max(cand, axis=0, keepdims=True)
    idx1 = jnp.min(jnp.where(cand == top1, eidx, N_EXP), axis=0, keepdims=True)
    cand2 = jnp.where(eidx == idx1, neg, cand)
    top2 = jnp.max(cand2, axis=0, keepdims=True)
    idx2 = jnp.min(jnp.where(cand2 == top2, eidx, N_EXP), axis=0, keepdims=True)
    is1 = eidx == idx1
    is2 = eidx == idx2
    w1 = jnp.sum(jnp.where(is1, scores, 0.0), axis=0, keepdims=True)
    w2 = jnp.sum(jnp.where(is2, scores, 0.0), axis=0, keepdims=True)
    wsum = w1 + w2
    w1 = w1 / wsum
    w2 = w2 / wsum

    onehot = jnp.where(jnp.logical_or(is1, is2), 1.0, 0.0)
    r_i = lax.broadcasted_iota(I32, (TM, TM), 0)
    c_i = lax.broadcasted_iota(I32, (TM, TM), 1)
    upper = jnp.where(r_i < c_i, 1.0, 0.0).astype(BF16)
    prefix = jnp.dot(onehot.astype(BF16), upper, preferred_element_type=F32) + carry_ref[:, 0:1]
    rank1 = jnp.sum(jnp.where(is1, prefix, 0.0), axis=0, keepdims=True)
    rank2 = jnp.sum(jnp.where(is2, prefix, 0.0), axis=0, keepdims=True)
    carry = carry_ref[...] + jnp.sum(onehot, axis=1, keepdims=True)
    carry_ref[...] = carry
    cnt_ref[...] = carry.astype(I32)

    route_ref[0:1, :] = idx1
    route_ref[1:2, :] = idx2
    route_ref[2:3, :] = rank1.astype(I32)
    route_ref[3:4, :] = rank2.astype(I32)
    route_ref[4:8, :] = jnp.zeros((4, TM), I32)
    wrow = lax.broadcasted_iota(I32, (LANE, TM), 0)
    wcols = jnp.where(wrow == 0, w1, jnp.where(wrow == 1, w2, 0.0))
    rw_ref[...] = wcols.T


def _post(ya_pre, yb_pre, o, gates, x, mod_l, wa, wb, wc, wo, ln_g, ln_b, wr_hi, wr_lo, rbias, l):
    row = lambda m: (m, 0)
    const = lambda m: (0, 0)
    wspec = pl.BlockSpec((None, D, D), lambda m: (l, 0, 0))
    return pl.pallas_call(
        _post_kernel,
        out_shape=(jax.ShapeDtypeStruct((T, D), F32),
                   jax.ShapeDtypeStruct((T * ROW_TILE, LANE), F32),
                   jax.ShapeDtypeStruct((N_TILES, 8, TM), I32),
                   jax.ShapeDtypeStruct((T, LANE), F32),
                   jax.ShapeDtypeStruct((N_EXP, LANE), I32)),
        grid=(N_TILES,),
        in_specs=[pl.BlockSpec((TM, D), row), pl.BlockSpec((TM, D), row), pl.BlockSpec((TM, D), row),
                  pl.BlockSpec((TM, 3 * D), row), pl.BlockSpec((TM, D), row),
                  pl.BlockSpec((None, N_MOD, D), lambda m: (_cond_row(m, CTX_TILES, TILES_PER_LAT), 0, 0)),
                  wspec, wspec, wspec, wspec,
                  pl.BlockSpec((1, D), const), pl.BlockSpec((1, D), const),
                  pl.BlockSpec((D, LANE), const), pl.BlockSpec((D, LANE), const),
                  pl.BlockSpec((N_EXP, 1), const)],
        out_specs=(pl.BlockSpec((TM, D), row), pl.BlockSpec((TM * ROW_TILE, LANE), row),
                   pl.BlockSpec((None, 8, TM), lambda m: (m, 0, 0)),
                   pl.BlockSpec((TM, LANE), row),
                   pl.BlockSpec((N_EXP, LANE), const)),
        scratch_shapes=[pltpu.VMEM((N_EXP, LANE), F32)],
        compiler_params=_params(("arbitrary",)),
        name="merge_ln_router",
    )(ya_pre, yb_pre, o, gates, x, mod_l, wa, wb, wc, wo, ln_g, ln_b, wr_hi, wr_lo, rbias)


def _row_copy(src, dst, src_row, dst_row, sem):
    return pltpu.make_async_copy(src.at[pl.ds(pl.multiple_of(src_row * ROW_TILE, ROW_TILE), ROW_TILE), :],
                                 dst.at[pl.ds(pl.multiple_of(dst_row * ROW_TILE, ROW_TILE), ROW_TILE), :], sem)


DMA_UNROLL = 4


def _slot_owner_kernel(slot_ref, pad_lo_ref, pad_hi_ref, owner_ref):
    def clear_expert(e, carry):
        def clear(s, c):
            owner_ref[s] = 0
            return c

        lax.fori_loop(pad_lo_ref[e], pad_hi_ref[e], clear, 0)
        return carry

    lax.fori_loop(0, N_EXP + 1, clear_expert, 0)

    def fill(i, carry):
        for u in range(DMA_UNROLL):
            t = i * DMA_UNROLL + u
            owner_ref[slot_ref[t]] = t
            owner_ref[slot_ref[T + t]] = t
        return carry

    lax.fori_loop(0, T // DMA_UNROLL, fill, 0)


def _slot_owner(slots_flat, pad_lo, pad_hi):
    smem = pl.BlockSpec(memory_space=pltpu.SMEM)
    return pl.pallas_call(
        _slot_owner_kernel,
        out_shape=jax.ShapeDtypeStruct((N_SLOTS,), I32),
        in_specs=[smem, smem, smem],
        out_specs=smem,
        name="moe_slot_owner",
    )(slots_flat, pad_lo, pad_hi)


def _ffn_kernel(blk_row_ref, blk_exp_ref, n_act_ref, owner_ref, h2_hbm, wgu_ref, wd_ref, y_ref,
                buf, sems, wgu_s, wd_s):
    i = pl.program_id(0)
    n_act = n_act_ref[0]
    half = i % 2
    new_expert = jnp.logical_or(i == 0, blk_exp_ref[i] != blk_exp_ref[jnp.maximum(i - 1, 0)])
    _cast_weights(new_expert, [(wgu_ref, wgu_s), (wd_ref, wd_s)])

    def gather_block(b, dst_half):
        base = blk_row_ref[b] * FFN_BLK

        def body(j, carry):
            for u in range(2 * DMA_UNROLL):
                r = j * 2 * DMA_UNROLL + u
                _row_copy(h2_hbm, buf.at[dst_half], owner_ref[base + r], r, sems.at[dst_half]).start()
            return carry

        lax.fori_loop(0, FFN_BLK // (2 * DMA_UNROLL), body, 0)

    @pl.when(i == 0)
    def _():
        gather_block(0, 0)

    @pl.when(i + 1 < n_act)
    def _():
        gather_block(i + 1, 1 - half)

    @pl.when(i < n_act)
    def _():
        def drain(j, carry):
            for _ in range(2 * DMA_UNROLL):
                _row_copy(h2_hbm, buf.at[half], 0, 0, sems.at[half]).wait()
            return carry

        lax.fori_loop(0, FFN_BLK // (2 * DMA_UNROLL), drain, 0)
        x = _load_rows(buf.at[half], FFN_BLK).astype(BF16)
        gu = jnp.dot(x, wgu_s[...], preferred_element_type=F32)
        gate = gu[:, :D_EXP]
        up = gu[:, D_EXP:]
        act = (gate * _sigmoid(gate) * up).astype(BF16)
        _store_rows(y_ref, jnp.dot(act, wd_s[...], preferred_element_type=F32))


def _expert_ffn(blk_row, blk_exp, n_act, owner, h2, wgu, wd, l):
    grid_spec = pltpu.PrefetchScalarGridSpec(
        num_scalar_prefetch=4,
        grid=(N_FFN_BLOCKS,),
        in_specs=[pl.BlockSpec(memory_space=pl.ANY),
                  pl.BlockSpec((None, None, D, 2 * D_EXP), lambda i, br, be, na, ow: (l, be[i], 0, 0)),
                  pl.BlockSpec((None, None, D_EXP, D), lambda i, br, be, na, ow: (l, be[i], 0, 0))],
        out_specs=pl.BlockSpec((FFN_BLK * ROW_TILE, LANE), lambda i, br, be, na, ow: (br[i], 0)),
        scratch_shapes=[pltpu.VMEM((2, FFN_BLK * ROW_TILE, LANE), F32), pltpu.SemaphoreType.DMA((2,)),
                        pltpu.VMEM((D, 2 * D_EXP), BF16), pltpu.VMEM((D_EXP, D), BF16)],
    )
    return pl.pallas_call(
        _ffn_kernel,
        out_shape=jax.ShapeDtypeStruct((N_SLOTS * ROW_TILE, LANE), F32),
        grid_spec=grid_spec,
        compiler_params=_params(("arbitrary",)),
        name="expert_ffn",
    )(blk_row, blk_exp, n_act, owner, h2, wgu, wd)


def _block_plan(counts):
    nblk = (counts + FFN_BLK - 1) // FFN_BLK
    end = jnp.cumsum(nblk)
    start = end - nblk
    n_act = end[-1]
    i = jnp.arange(N_FFN_BLOCKS, dtype=I32)
    i_eff = jnp.minimum(i, n_act - 1)
    e = jnp.minimum(jnp.sum(i_eff[:, None] >= end[None, :], axis=1), N_EXP - 1).astype(I32)
    pad_lo = jnp.concatenate([start * FFN_BLK + counts, n_act.reshape(1) * FFN_BLK]).astype(I32)
    pad_hi = jnp.concatenate([end * FFN_BLK, jnp.full((1,), N_SLOTS)]).astype(I32)
    return ((start * FFN_BLK).astype(I32), i_eff.astype(I32), e, n_act.reshape(1).astype(I32),
            pad_lo, pad_hi)


def _combine_kernel(slot_ref, slot_next_ref, y_hbm, rw_ref, x1_ref, mod_ref, modn_ref,
                    lng_ref, lnb_ref, x2_ref, hn_ref, buf, sems):
    m = pl.program_id(0)
    half = m % 2

    def gather_tile(s_ref, dst_half):
        def body(i, carry):
            for u in range(DMA_UNROLL):
                t = i * DMA_UNROLL + u
                for k in range(2):
                    _row_copy(y_hbm, buf.at[dst_half, k], s_ref[k, t], t, sems.at[dst_half]).start()
            return carry

        lax.fori_loop(0, TM // DMA_UNROLL, body, 0)

    @pl.when(m == 0)
    def _():
        gather_tile(slot_ref, 0)

    @pl.when(m + 1 < pl.num_programs(0))
    def _():
        gather_tile(slot_next_ref, 1 - half)

    def drain(i, carry):
        for _ in range(2 * DMA_UNROLL):
            _row_copy(y_hbm, buf.at[half, 0], 0, 0, sems.at[half]).wait()
        return carry

    lax.fori_loop(0, TM // DMA_UNROLL, drain, 0)
    f = rw_ref[:, 0:1] * _load_rows(buf.at[half, 0], TM) + rw_ref[:, 1:2] * _load_rows(buf.at[half, 1], TM)
    x2 = _layer_norm(ALPHA * x1_ref[...] + mod_ref[5:6, :] * f, lng_ref[...], lnb_ref[...])
    x2_ref[...] = x2
    hn_ref[...] = (x2 * (1.0 + modn_ref[1:2, :]) + modn_ref[0:1, :]).astype(BF16)


def _combine(slots, y_slots, rw, x1, mod_l, mod_next, ln_g, ln_b):
    row = lambda m: (m, 0)
    const = lambda m: (0, 0)
    mod_spec = pl.BlockSpec((None, N_MOD, D), lambda m: (_cond_row(m, CTX_TILES, TILES_PER_LAT), 0, 0))
    return pl.pallas_call(
        _combine_kernel,
        out_shape=(jax.ShapeDtypeStruct((T, D), F32), jax.ShapeDtypeStruct((T, D), BF16)),
        grid=(N_TILES,),
        in_specs=[pl.BlockSpec((None, 2, TM), lambda m: (m, 0, 0), memory_space=pltpu.SMEM),
                  pl.BlockSpec((None, 2, TM), lambda m: (jnp.minimum(m + 1, N_TILES - 1), 0, 0),
                               memory_space=pltpu.SMEM),
                  pl.BlockSpec(memory_space=pl.ANY),
                  pl.BlockSpec((TM, LANE), row), pl.BlockSpec((TM, D), row),
                  mod_spec, mod_spec,
                  pl.BlockSpec((1, D), const), pl.BlockSpec((1, D), const)],
        out_specs=(pl.BlockSpec((TM, D), row), pl.BlockSpec((TM, D), row)),
        scratch_shapes=[pltpu.VMEM((2, 2, TM * ROW_TILE, LANE), F32), pltpu.SemaphoreType.DMA((2,))],
        compiler_params=_params(("arbitrary",)),
        name="moe_combine_ln",
    )(slots, slots, y_slots, rw, x1, mod_l, mod_next, ln_g, ln_b)


def kernel(x_prompt, x_sample, c, c_ctx, cache_k, cache_v, w_in, conv_a_w, w_a_out, conv_b_w, conv_b_b,
           norm_b_g, norm_b_b, w_b_out, q_norm_g, k_norm_g, w_c_out, w_o, w_ada, b_ada, ln_g, ln_b,
           w_router, router_bias, w_gate_up, w_down):
    x = jnp.concatenate([x_prompt.reshape(T_CTX, D), x_sample.reshape(T_LAT, D)], axis=0)
    cond16 = jnp.concatenate([c_ctx[None, :], c, jnp.zeros((16 - 1 - N_LAT_SEQ, D), F32)], axis=0)
    mod = _modulation(cond16, w_ada, b_ada).reshape(DEPTH, 16, N_MOD, D)

    wa_b, wb_b, wc_b, wo_b = (w.astype(BF16) for w in (w_a_out, w_b_out, w_c_out, w_o))
    wr = jnp.pad(w_router, ((0, 0), (0, LANE - N_EXP)))
    wr_hi = wr.astype(BF16)
    wr_lo = (wr - wr_hi.astype(F32)).astype(BF16)
    rbias = router_bias.reshape(N_EXP, 1)

    cos_tab, sin_tab = _rope_tables()
    half = np.arange(LANE) // HEAD
    ones_bd = jnp.asarray(half[:, None] == half[None, :], BF16)
    ck = cache_k.reshape(N_LAT_SEQ, DEPTH, PAST, D_KV)
    cv = cache_v.reshape(N_LAT_SEQ, DEPTH, PAST, D_KV)

    h = _modulate(x, mod[0])
    new_k, new_v = [], []
    for l in range(DEPTH):
        ab, acx = _proj_a(h, w_in, l)
        glu = _proj_b(h, w_in, l)
        gates = _proj_gates(h, w_in, l)
        q, katt, vatt, kn, vf = _proj_qkv(h, w_in, jnp.tile(q_norm_g[l], N_Q)[None, :],
                                          jnp.tile(k_norm_g[l], N_KV)[None, :], cos_tab, sin_tab, ones_bd, l)
        new_k.append(kn[:T_CTX].reshape(N_CTX_SEQ, CTX_LEN, N_KV, HEAD))
        new_v.append(vf[:T_CTX].reshape(N_CTX_SEQ, CTX_LEN, N_KV, HEAD))
        ya_pre, yb_pre = _convs(acx, ab, glu, conv_a_w[l], conv_b_w[l], conv_b_b[l][None, :],
                                norm_b_g[l][None, :], norm_b_b[l][None, :])
        o = _attention(q, katt, vatt, ck, cv, l)
        x1, h2, route, rw, counts = _post(ya_pre, yb_pre, o, gates, x, mod[l], wa_b, wb_b, wc_b, wo_b,
                                          ln_g[l, 0][None, :], ln_b[l, 0][None, :], wr_hi, wr_lo, rbias, l)
        cnt = counts[:, 0]
        seg_start, blk_row, blk_exp, n_act, pad_lo, pad_hi = _block_plan(cnt)
        is_exp = route[:, 0:2, :, None] == jnp.arange(N_EXP, dtype=I32)
        slots = jnp.sum(jnp.where(is_exp, seg_start, 0), axis=-1) + route[:, 2:4, :]
        owner = _slot_owner(slots.transpose(1, 0, 2).reshape(2 * T), pad_lo, pad_hi)
        y_slots = _expert_ffn(blk_row, blk_exp, n_act, owner, h2, w_gate_up, w_down, l)
        x, h = _combine(slots, y_slots, rw, x1, mod[l], mod[min(l + 1, DEPTH - 1)],
                        ln_g[l, 1][None, :], ln_b[l, 1][None, :])
    y_prompt = x[:T_CTX].reshape(N_CTX_SEQ, CTX_LEN, D)
    y_sample = x[T_CTX:].reshape(N_LAT_SEQ, LAT_LEN, D)
    return y_prompt, y_sample, jnp.stack(new_k, axis=1), jnp.stack(new_v, axis=1)
```

```python
import functools

import numpy as np
import jax
import jax.numpy as jnp
from jax import lax
from jax.experimental import pallas as pl
from jax.experimental.pallas import tpu as pltpu

F32 = jnp.float32
BF16 = jnp.bfloat16
I32 = jnp.int32

D = 1024
DEPTH = 4
N_CTX_SEQ = 16
CTX_LEN = 256
N_LAT_SEQ = 8
LAT_LEN = 1024
PAST = 512
T_CTX = N_CTX_SEQ * CTX_LEN
T_LAT = N_LAT_SEQ * LAT_LEN
T = T_CTX + T_LAT
GRID_W = 64
HEAD = 64
N_Q = 16
N_KV = 4
GROUP = 4
D_KV = N_KV * HEAD
N_EXP = 16
N_GRP = 4
EXP_PER_GRP = 4
D_EXP = 512
IN_COLS = 9728
N_MOD = 6
ALPHA = (2 * DEPTH) ** 0.25
LN_EPS = 1e-5
RMS_EPS = 1e-6
ROPE_THETA = 10000.0

LANE = 128
TM = 512
N_TILES = T // TM
CTX_TILES = T_CTX // TM
TILES_PER_LAT = LAT_LEN // TM
TMP = 1024
SEG = 256
N_SEG = T // SEG
CTX_SEGS = T_CTX // SEG
SEGS_PER_LAT = LAT_LEN // SEG
HALO_A = 8
HALO_B = 16
SH_ROWS = SEG + 2 * HALO_B - 8
FFN_BLK = 256
N_FFN_BLOCKS = (2 * T) // FFN_BLK + N_EXP
N_SLOTS = N_FFN_BLOCKS * FFN_BLK
VMEM_LIMIT = 56 * 1024 * 1024

COL_AB, COL_AC, COL_AX, COL_BU, COL_BG, COL_Q, COL_KV, COL_GATES = 0, 2, 4, 6, 8, 10, 12, 13


def _params(sem):
    return pltpu.CompilerParams(dimension_semantics=sem, vmem_limit_bytes=VMEM_LIMIT)


def _cond_row(m, tiles_ctx, tiles_per_lat):
    return jnp.where(m < tiles_ctx, 0, 1 + (m - tiles_ctx) // tiles_per_lat)


def _layer_norm(x, g, b):
    mu = jnp.mean(x, axis=-1, keepdims=True)
    xc = x - mu
    var = jnp.mean(xc * xc, axis=-1, keepdims=True)
    return xc * lax.rsqrt(var + LN_EPS) * g + b


def _sigmoid(x):
    return 1.0 / (1.0 + jnp.exp(-x))


ROW_TILE = D // LANE


def _store_rows(ref, x):
    for j in range(ROW_TILE):
        ref[pl.ds(j, x.shape[0], stride=ROW_TILE), :] = x[:, j * LANE:(j + 1) * LANE]


def _load_rows(ref, n_rows):
    return jnp.concatenate([ref[pl.ds(j, n_rows, stride=ROW_TILE), :] for j in range(ROW_TILE)], axis=1)


def _mod_kernel(cond_ref, w_ref, b_ref, o_ref):
    cnd = cond_ref[...]
    s = (cnd * _sigmoid(cnd)).astype(BF16)
    o_ref[...] = jnp.dot(s, w_ref[...].astype(BF16), preferred_element_type=F32) + b_ref[...]


def _modulation(cond16, w_ada, b_ada):
    n_col = N_MOD * D
    tn = 1024
    return pl.pallas_call(
        _mod_kernel,
        out_shape=jax.ShapeDtypeStruct((DEPTH, 16, n_col), F32),
        grid=(DEPTH, n_col // tn),
        in_specs=[
            pl.BlockSpec((16, D), lambda l, n: (0, 0)),
            pl.BlockSpec((None, D, tn), lambda l, n: (l, 0, n)),
            pl.BlockSpec((None, 1, tn), lambda l, n: (l, 0, n)),
        ],
        out_specs=pl.BlockSpec((None, 16, tn), lambda l, n: (l, 0, n)),
        compiler_params=_params(("parallel", "parallel")),
        name="adaln_mod",
    )(cond16, w_ada, b_ada.reshape(DEPTH, 1, n_col))


def _modulate_kernel(x_ref, mod_ref, h_ref):
    h_ref[...] = (x_ref[...] * (1.0 + mod_ref[1:2, :]) + mod_ref[0:1, :]).astype(BF16)


def _modulate(x, mod_l):
    return pl.pallas_call(
        _modulate_kernel,
        out_shape=jax.ShapeDtypeStruct((T, D), BF16),
        grid=(N_TILES,),
        in_specs=[
            pl.BlockSpec((TM, D), lambda m: (m, 0)),
            pl.BlockSpec((None, N_MOD, D), lambda m: (_cond_row(m, CTX_TILES, TILES_PER_LAT), 0, 0)),
        ],
        out_specs=pl.BlockSpec((TM, D), lambda m: (m, 0)),
        compiler_params=_params(("parallel",)),
        name="modulate",
    )(x, mod_l)


WCOL = 512


def _cast_weights(first, pairs):
    @pl.when(first)
    def _():
        for src, dst in pairs:
            dst[...] = src[...].astype(BF16)


def _proj_a_kernel(h_ref, wb_ref, wc_ref, wx_ref, ab_ref, acx_ref, wb_s, wc_s, wx_s):
    _cast_weights(pl.program_id(1) == 0, [(wb_ref, wb_s), (wc_ref, wc_s), (wx_ref, wx_s)])
    h = h_ref[...]
    ab_ref[...] = jnp.dot(h, wb_s[...], preferred_element_type=F32).astype(BF16)
    acx_ref[...] = (jnp.dot(h, wc_s[...], preferred_element_type=F32)
                    * jnp.dot(h, wx_s[...], preferred_element_type=F32))


def _proj_b_kernel(h_ref, wu_ref, wg_ref, glu_ref, wu_s, wg_s):
    _cast_weights(pl.program_id(1) == 0, [(wu_ref, wu_s), (wg_ref, wg_s)])
    h = h_ref[...]
    glu_ref[...] = (jnp.dot(h, wu_s[...], preferred_element_type=F32)
                    * _sigmoid(jnp.dot(h, wg_s[...], preferred_element_type=F32)))


def _proj_gate_kernel(h_ref, w0_ref, w1_ref, g_ref, w_s):
    _cast_weights(pl.program_id(1) == 0, [(w0_ref, w_s.at[:, 0:WCOL]), (w1_ref, w_s.at[:, WCOL:2 * WCOL])])
    g_ref[...] = _sigmoid(jnp.dot(h_ref[...], w_s[...], preferred_element_type=F32)).astype(BF16)


def _w_spec(l, col0, step=1):
    return pl.BlockSpec((None, D, WCOL), lambda c, m: (l, 0, col0 + step * c))


def _proj_a(h, w_in, l):
    out_spec = pl.BlockSpec((TMP, WCOL), lambda c, m: (m, c))
    return pl.pallas_call(
        _proj_a_kernel,
        out_shape=(jax.ShapeDtypeStruct((T, D), BF16), jax.ShapeDtypeStruct((T, D), F32)),
        grid=(D // WCOL, T // TMP),
        in_specs=[pl.BlockSpec((TMP, D), lambda c, m: (m, 0)),
                  _w_spec(l, COL_AB), _w_spec(l, COL_AC), _w_spec(l, COL_AX)],
        out_specs=(out_spec, out_spec),
        scratch_shapes=[pltpu.VMEM((D, WCOL), BF16)] * 3,
        compiler_params=_params(("parallel", "arbitrary")),
        name="proj_a",
    )(h, w_in, w_in, w_in)


def _proj_b(h, w_in, l):
    return pl.pallas_call(
        _proj_b_kernel,
        out_shape=jax.ShapeDtypeStruct((T, D), F32),
        grid=(D // WCOL, T // TMP),
        in_specs=[pl.BlockSpec((TMP, D), lambda c, m: (m, 0)), _w_spec(l, COL_BU), _w_spec(l, COL_BG)],
        out_specs=pl.BlockSpec((TMP, WCOL), lambda c, m: (m, c)),
        scratch_shapes=[pltpu.VMEM((D, WCOL), BF16)] * 2,
        compiler_params=_params(("parallel", "arbitrary")),
        name="proj_b",
    )(h, w_in, w_in)


def _proj_gates(h, w_in, l):
    tn = 2 * WCOL
    return pl.pallas_call(
        _proj_gate_kernel,
        out_shape=jax.ShapeDtypeStruct((T, 3 * D), BF16),
        grid=(3 * D // tn, T // TMP),
        in_specs=[pl.BlockSpec((TMP, D), lambda c, m: (m, 0)),
                  _w_spec(l, COL_GATES, 2), _w_spec(l, COL_GATES + 1, 2)],
        out_specs=pl.BlockSpec((TMP, tn), lambda c, m: (m, c)),
        scratch_shapes=[pltpu.VMEM((D, tn), BF16)],
        compiler_params=_params(("parallel", "arbitrary")),
        name="proj_gates",
    )(h, w_in, w_in)


def _head_mean_square(x, ones_bd):
    out = []
    for c in range(x.shape[1] // LANE):
        sq = x[:, c * LANE:(c + 1) * LANE]
        sq = sq * sq
        hi = sq.astype(BF16)
        lo = (sq - hi.astype(F32)).astype(BF16)
        out.append(jnp.dot(hi, ones_bd, preferred_element_type=F32)
                   + jnp.dot(lo, ones_bd, preferred_element_type=F32))
    return jnp.concatenate(out, axis=1) * (1.0 / HEAD)


def _rope(x, cos, sin, first_half):
    out = []
    for c in range(x.shape[1] // LANE):
        xc = x[:, c * LANE:(c + 1) * LANE]
        partner = jnp.where(first_half, pltpu.roll(xc, LANE - 16, axis=1), pltpu.roll(xc, 16, axis=1))
        out.append(xc * cos + partner * sin)
    return jnp.concatenate(out, axis=1)


def _qkv_kernel(h_ref, wq_ref, wkv_ref, gq_ref, gk_ref, cos_ref, sin_ref, ones_ref,
                q_ref, katt_ref, vatt_ref, kn_ref, vf_ref, wq_s, wkv_s):
    _cast_weights(pl.program_id(0) == 0, [(wq_ref, wq_s), (wkv_ref, wkv_s)])
    h = h_ref[...]
    ones_bd = ones_ref[...]
    cos = cos_ref[...]
    sin = sin_ref[...]
    lane = lax.broadcasted_iota(I32, (TM, LANE), 1)
    first_half = (lane & 16) == 0
    q = jnp.dot(h, wq_s[...], preferred_element_type=F32)
    qn = q * lax.rsqrt(_head_mean_square(q, ones_bd) + RMS_EPS) * gq_ref[...]
    q_ref[...] = (_rope(qn, cos, sin, first_half) * (HEAD ** -0.5)).astype(BF16)
    kv = jnp.dot(h, wkv_s[...], preferred_element_type=F32)
    k = kv[:, :D_KV]
    v = kv[:, D_KV:]
    kn = k * lax.rsqrt(_head_mean_square(k, ones_bd) + RMS_EPS) * gk_ref[...]
    kn_ref[...] = kn
    katt_ref[...] = _rope(kn, cos, sin, first_half).astype(BF16)
    vf_ref[...] = v
    vatt_ref[...] = v.astype(BF16)


def _proj_qkv(h, w_in, gq, gk, cos_tab, sin_tab, ones_bd, l):
    def tab_idx(m):
        return jnp.where(m < CTX_TILES, TILES_PER_LAT, (m - CTX_TILES) % TILES_PER_LAT)

    row = lambda m: (m, 0)
    return pl.pallas_call(
        _qkv_kernel,
        out_shape=(jax.ShapeDtypeStruct((T, D), BF16),
                   jax.ShapeDtypeStruct((T, D_KV), BF16),
                   jax.ShapeDtypeStruct((T, D_KV), BF16),
                   jax.ShapeDtypeStruct((T, D_KV), F32),
                   jax.ShapeDtypeStruct((T, D_KV), F32)),
        grid=(N_TILES,),
        in_specs=[
            pl.BlockSpec((TM, D), row),
            pl.BlockSpec((None, D, D), lambda m: (l, 0, COL_Q // 2)),
            pl.BlockSpec((None, D, 2 * D_KV), lambda m: (l, 0, COL_KV)),
            pl.BlockSpec((1, D), lambda m: (0, 0)),
            pl.BlockSpec((1, D_KV), lambda m: (0, 0)),
            pl.BlockSpec((TM, LANE), lambda m: (tab_idx(m), 0)),
            pl.BlockSpec((TM, LANE), lambda m: (tab_idx(m), 0)),
            pl.BlockSpec((LANE, LANE), lambda m: (0, 0)),
        ],
        out_specs=(pl.BlockSpec((TM, D), row), pl.BlockSpec((TM, D_KV), row),
                   pl.BlockSpec((TM, D_KV), row), pl.BlockSpec((TM, D_KV), row),
                   pl.BlockSpec((TM, D_KV), row)),
        scratch_shapes=[pltpu.VMEM((D, D), BF16), pltpu.VMEM((D, 2 * D_KV), BF16)],
        compiler_params=_params(("arbitrary",)),
        name="proj_qkv",
    )(h, w_in, w_in, gq, gk, cos_tab, sin_tab, ones_bd)


def _rope_tables():
    lane = np.arange(LANE)
    j = lane % 16
    freqs = jnp.power(ROPE_THETA, -jnp.arange(16, dtype=F32) / 16)[j]
    pos = jnp.arange(LAT_LEN, dtype=I32)
    row = (pos // GRID_W).astype(F32)
    col = (pos % GRID_W).astype(F32)
    use_row = jnp.asarray((lane % HEAD) < HEAD // 2)
    p = jnp.where(use_row[None, :], row[:, None], col[:, None])
    ang = p * freqs[None, :]
    sign = jnp.asarray(np.where((lane & 16) == 0, -1.0, 1.0), F32)
    cos = jnp.concatenate([jnp.cos(ang), jnp.ones((TM, LANE), F32)], axis=0)
    sin = jnp.concatenate([jnp.sin(ang) * sign[None, :], jnp.zeros((TM, LANE), F32)], axis=0)
    return cos, sin


def _conv_kernel(acx_ref, acx_l_ref, acx_r_ref, ab_ref, glu_ref, glu_l_ref, glu_r_ref,
                 wa_ref, wb_ref, bb_ref, ng_ref, nb_ref, ya_ref, yb_ref, pad_a, pad_b, u_ref, sh_ref):
    s = pl.program_id(0)
    lat = s >= CTX_SEGS
    pos = (s - CTX_SEGS) % SEGS_PER_LAT
    has_left = jnp.logical_and(lat, pos != 0)
    has_right = jnp.logical_and(lat, pos != SEGS_PER_LAT - 1)

    pad_a[0:HALO_A, :] = jnp.where(has_left, acx_l_ref[...], 0.0)
    pad_a[HALO_A:HALO_A + SEG, :] = acx_ref[...]
    pad_a[HALO_A + SEG:, :] = jnp.where(has_right, acx_r_ref[...], 0.0)
    pad_b[0:HALO_B, :] = jnp.where(has_left, glu_l_ref[...], 0.0)
    pad_b[HALO_B:HALO_B + SEG, :] = glu_ref[...]
    pad_b[HALO_B + SEG:, :] = jnp.where(has_right, glu_r_ref[...], 0.0)

    conv_a = (wa_ref[0:1, :] * pad_a[HALO_A - 1:HALO_A - 1 + SEG, :]
              + wa_ref[1:2, :] * pad_a[HALO_A:HALO_A + SEG, :]
              + wa_ref[2:3, :] * pad_a[HALO_A + 1:HALO_A + 1 + SEG, :])
    ya_ref[...] = (ab_ref[...].astype(F32) * conv_a).astype(BF16)

    rows = 64
    kb = wb_ref.shape[0]

    def lane_chunk(c, carry):
        lanes = pl.ds(pl.multiple_of(c * LANE, LANE), LANE)
        for b in range(1, 8):
            sh_ref[b, :, :] = pad_b[b:b + SH_ROWS, lanes]
        for r in range(SEG // rows):
            acc = jnp.zeros((rows, LANE), F32)
            for k in range(kb):
                off = HALO_B + k - kb // 2
                row0 = r * rows + 8 * (off // 8)
                if off % 8 == 0:
                    src = pad_b[row0:row0 + rows, lanes]
                else:
                    src = sh_ref[off % 8, row0:row0 + rows, :]
                acc = acc + wb_ref[k:k + 1, lanes] * src
            u_ref[r * rows:(r + 1) * rows, lanes] = acc
        return carry

    lax.fori_loop(0, D // LANE, lane_chunk, 0)
    u = _layer_norm(u_ref[...] + bb_ref[...], ng_ref[...], nb_ref[...])
    yb_ref[...] = (u * _sigmoid(u)).astype(BF16)


def _convs(acx, ab, glu, conv_a_w, conv_b_w, conv_b_b, norm_g, norm_b):
    seg = lambda s: (s, 0)
    const = lambda s: (0, 0)
    ra, rb = SEG // HALO_A, SEG // HALO_B
    left_a = lambda s: (jnp.maximum(s * ra - 1, 0), 0)
    right_a = lambda s: (jnp.minimum((s + 1) * ra, T // HALO_A - 1), 0)
    left_b = lambda s: (jnp.maximum(s * rb - 1, 0), 0)
    right_b = lambda s: (jnp.minimum((s + 1) * rb, T // HALO_B - 1), 0)
    return pl.pallas_call(
        _conv_kernel,
        out_shape=(jax.ShapeDtypeStruct((T, D), BF16), jax.ShapeDtypeStruct((T, D), BF16)),
        grid=(N_SEG,),
        in_specs=[
            pl.BlockSpec((SEG, D), seg), pl.BlockSpec((HALO_A, D), left_a), pl.BlockSpec((HALO_A, D), right_a),
            pl.BlockSpec((SEG, D), seg),
            pl.BlockSpec((SEG, D), seg), pl.BlockSpec((HALO_B, D), left_b), pl.BlockSpec((HALO_B, D), right_b),
            pl.BlockSpec(conv_a_w.shape, const), pl.BlockSpec(conv_b_w.shape, const),
            pl.BlockSpec((1, D), const), pl.BlockSpec((1, D), const), pl.BlockSpec((1, D), const),
        ],
        out_specs=(pl.BlockSpec((SEG, D), seg), pl.BlockSpec((SEG, D), seg)),
        scratch_shapes=[pltpu.VMEM((SEG + 2 * HALO_A, D), F32),
                        pltpu.VMEM((SEG + 2 * HALO_B, D), F32),
                        pltpu.VMEM((SEG, D), F32),
                        pltpu.VMEM((8, SH_ROWS, LANE), F32)],
        compiler_params=_params(("parallel",)),
        name="convs",
    )(acx, acx, acx, ab, glu, glu, glu, conv_a_w, conv_b_w, conv_b_b, norm_g, norm_b)


def _attend(q_ref, key_refs, val_refs, o_ref):
    nt = (((1,), (1,)), ((), ()))
    keys = [r[...].astype(BF16) for r in key_refs]
    vals = [r[...].astype(BF16) for r in val_refs]
    for g in range(N_KV):
        kg = [k[:, g * HEAD:(g + 1) * HEAD] for k in keys]
        vg = [v[:, g * HEAD:(g + 1) * HEAD] for v in vals]
        for hh in range(GROUP):
            hd = g * GROUP + hh
            qh = q_ref[:, hd * HEAD:(hd + 1) * HEAD]
            s = [lax.dot_general(qh, k, nt, preferred_element_type=F32) for k in kg]
            mx = functools.reduce(jnp.maximum, [jnp.max(x, axis=-1, keepdims=True) for x in s])
            p = [jnp.exp(x - mx) for x in s]
            den = functools.reduce(jnp.add, [jnp.sum(x, axis=-1, keepdims=True) for x in p])
            acc = functools.reduce(jnp.add, [jnp.dot(x.astype(BF16), v, preferred_element_type=F32)
                                             for x, v in zip(p, vg)])
            o_ref[:, hd * HEAD:(hd + 1) * HEAD] = (acc / den).astype(BF16)


def _attn_ctx_kernel(q_ref, k_ref, v_ref, o_ref):
    _attend(q_ref, [k_ref], [v_ref], o_ref)


def _attn_lat_kernel(q_ref, k_ref, v_ref, ck_ref, cv_ref, o_in_ref, o_ref):
    del o_in_ref
    _attend(q_ref, [ck_ref, k_ref], [cv_ref, v_ref], o_ref)


def _attention(q, katt, vatt, cache_k, cache_v, l):
    o = pl.pallas_call(
        _attn_ctx_kernel,
        out_shape=jax.ShapeDtypeStruct((T, D), BF16),
        grid=(N_CTX_SEQ,),
        in_specs=[pl.BlockSpec((CTX_LEN, D), lambda b: (b, 0)),
                  pl.BlockSpec((CTX_LEN, D_KV), lambda b: (b, 0)),
                  pl.BlockSpec((CTX_LEN, D_KV), lambda b: (b, 0))],
        out_specs=pl.BlockSpec((CTX_LEN, D), lambda b: (b, 0)),
        compiler_params=_params(("parallel",)),
        name="attn_ctx",
    )(q, katt, vatt)
    seg0 = CTX_SEGS
    lat0 = T_CTX // LAT_LEN
    return pl.pallas_call(
        _attn_lat_kernel,
        out_shape=jax.ShapeDtypeStruct((T, D), BF16),
        grid=(N_LAT_SEQ, SEGS_PER_LAT),
        in_specs=[pl.BlockSpec((SEG, D), lambda b, i: (seg0 + b * SEGS_PER_LAT + i, 0)),
                  pl.BlockSpec((LAT_LEN, D_KV), lambda b, i: (lat0 + b, 0)),
                  pl.BlockSpec((LAT_LEN, D_KV), lambda b, i: (lat0 + b, 0)),
                  pl.BlockSpec((None, None, PAST, D_KV), lambda b, i: (b, l, 0, 0)),
                  pl.BlockSpec((None, None, PAST, D_KV), lambda b, i: (b, l, 0, 0)),
                  pl.BlockSpec(memory_space=pl.ANY)],
        out_specs=pl.BlockSpec((SEG, D), lambda b, i: (seg0 + b * SEGS_PER_LAT + i, 0)),
        input_output_aliases={5: 0},
        compiler_params=_params(("parallel", "parallel")),
        name="attn_lat",
    )(q, katt, vatt, cache_k, cache_v, o)


def _post_kernel(ya_ref, yb_ref, o_ref, g_ref, x_ref, mod_ref, wa_ref, wb_ref, wc_ref, wo_ref,
                 lng_ref, lnb_ref, wr_hi_ref, wr_lo_ref, rb_ref,
                 x1_ref, h2_ref, route_ref, rw_ref, cnt_ref, carry_ref):
    m = pl.program_id(0)

    @pl.when(m == 0)
    def _():
        carry_ref[...] = jnp.zeros_like(carry_ref)

    ya = jnp.dot(ya_ref[...], wa_ref[...], preferred_element_type=F32)
    yb = jnp.dot(yb_ref[...], wb_ref[...], preferred_element_type=F32)
    yc = jnp.dot(o_ref[...], wc_ref[...], preferred_element_type=F32)
    merged = (g_ref[:, 0:D].astype(F32) * ya + g_ref[:, D:2 * D].astype(F32) * yb
              + g_ref[:, 2 * D:3 * D].astype(F32) * yc)
    mix = jnp.dot(merged.astype(BF16), wo_ref[...], preferred_element_type=F32)
    x1 = _layer_norm(ALPHA * x_ref[...] + mod_ref[2:3, :] * mix, lng_ref[...], lnb_ref[...])
    x1_ref[...] = x1
    h2 = x1 * (1.0 + mod_ref[4:5, :]) + mod_ref[3:4, :]

    hi = h2.astype(BF16)
    h2_ref[...] = hi
    lo = (h2 - hi.astype(F32)).astype(BF16)
    wr_hi = wr_hi_ref[...]
    logits = (jnp.dot(hi, wr_hi, preferred_element_type=F32)
              + jnp.dot(lo, wr_hi, preferred_element_type=F32)
              + jnp.dot(hi, wr_lo_ref[...], preferred_element_type=F32))
    scores = _sigmoid(logits.T[0:N_EXP, :])
    sel = scores + rb_ref[...]

    gscore = []
    for g in range(N_GRP):
        r = [sel[g * EXP_PER_GRP + j:g * EXP_PER_GRP + j + 1, :] for j in range(EXP_PER_GRP)]
        pairs = [r[a] + r[b] for a in range(EXP_PER_GRP) for b in range(a + 1, EXP_PER_GRP)]
        gscore.append(functools.reduce(jnp.maximum, pairs))
    best = functools.reduce(jnp.maximum, gscore)
    gsel = jnp.full(best.shape, N_GRP - 1, I32)
    for g in range(N_GRP - 2, -1, -1):
        gsel = jnp.where(gscore[g] == best, g, gsel)

    eidx = lax.broadcasted_iota(I32, (N_EXP, TM), 0)
    neg = jnp.float32(-jnp.inf)
    cand = jnp.where((eidx // EXP_PER_GRP) == gsel, sel, neg)
    top1 = jnp.max(cand, axis=0, keepdims=True)
    idx1 = jnp.min(jnp.where(cand == top1, eidx, N_EXP), axis=0, keepdims=True)
    cand2 = jnp.where(eidx == idx1, neg, cand)
    top2 = jnp.max(cand2, axis=0, keepdims=True)
    idx2 = jnp.min(jnp.where(cand2 == top2, eidx, N_EXP), axis=0, keepdims=True)
    is1 = eidx == idx1
    is2 = eidx == idx2
    w1 = jnp.sum(jnp.where(is1, scores, 0.0), axis=0, keepdims=True)
    w2 = jnp.sum(jnp.where(is2, scores, 0.0), axis=0, keepdims=True)
    wsum = w1 + w2
    w1 = w1 / wsum
    w2 = w2 / wsum

    onehot = jnp.where(jnp.logical_or(is1, is2), 1.0, 0.0)
    r_i = lax.broadcasted_iota(I32, (TM, TM), 0)
    c_i = lax.broadcasted_iota(I32, (TM, TM), 1)
    upper = jnp.where(r_i < c_i, 1.0, 0.0).astype(BF16)
    prefix = jnp.dot(onehot.astype(BF16), upper, preferred_element_type=F32) + carry_ref[:, 0:1]
    rank1 = jnp.sum(jnp.where(is1, prefix, 0.0), axis=0, keepdims=True)
    rank2 = jnp.sum(jnp.where(is2, prefix, 0.0), axis=0, keepdims=True)
    carry = carry_ref[...] + jnp.sum(onehot, axis=1, keepdims=True)
    carry_ref[...] = carry
    cnt_ref[...] = carry.astype(I32)

    route_ref[0:1, :] = idx1
    route_ref[1:2, :] = idx2
    route_ref[2:3, :] = rank1.astype(I32)
    route_ref[3:4, :] = rank2.astype(I32)
    route_ref[4:8, :] = jnp.zeros((4, TM), I32)
    wrow = lax.broadcasted_iota(I32, (LANE, TM), 0)
    wcols = jnp.where(wrow == 0, w1, jnp.where(wrow == 1, w2, 0.0))
    rw_ref[...] = wcols.T


def _post(ya_pre, yb_pre, o, gates, x, mod_l, wa, wb, wc, wo, ln_g, ln_b, wr_hi, wr_lo, rbias, l):
    row = lambda m: (m, 0)
    const = lambda m: (0, 0)
    wspec = pl.BlockSpec((None, D, D), lambda m: (l, 0, 0))
    return pl.pallas_call(
        _post_kernel,
        out_shape=(jax.ShapeDtypeStruct((T, D), F32),
                   jax.ShapeDtypeStruct((T, D), BF16),
                   jax.ShapeDtypeStruct((N_TILES, 8, TM), I32),
                   jax.ShapeDtypeStruct((T, LANE), F32),
                   jax.ShapeDtypeStruct((N_EXP, LANE), I32)),
        grid=(N_TILES,),
        in_specs=[pl.BlockSpec((TM, D), row), pl.BlockSpec((TM, D), row), pl.BlockSpec((TM, D), row),
                  pl.BlockSpec((TM, 3 * D), row), pl.BlockSpec((TM, D), row),
                  pl.BlockSpec((None, N_MOD, D), lambda m: (_cond_row(m, CTX_TILES, TILES_PER_LAT), 0, 0)),
                  wspec, wspec, wspec, wspec,
                  pl.BlockSpec((1, D), const), pl.BlockSpec((1, D), const),
                  pl.BlockSpec((D, LANE), const), pl.BlockSpec((D, LANE), const),
                  pl.BlockSpec((N_EXP, 1), const)],
        out_specs=(pl.BlockSpec((TM, D), row), pl.BlockSpec((TM, D), row),
                   pl.BlockSpec((None, 8, TM), lambda m: (m, 0, 0)),
                   pl.BlockSpec((TM, LANE), row),
                   pl.BlockSpec((N_EXP, LANE), const)),
        scratch_shapes=[pltpu.VMEM((N_EXP, LANE), F32)],
        compiler_params=_params(("arbitrary",)),
        name="merge_ln_router",
    )(ya_pre, yb_pre, o, gates, x, mod_l, wa, wb, wc, wo, ln_g, ln_b, wr_hi, wr_lo, rbias)


def _row_copy(src, dst, src_row, dst_row, sem):
    return pltpu.make_async_copy(src.at[pl.ds(pl.multiple_of(src_row * ROW_TILE, ROW_TILE), ROW_TILE), :],
                                 dst.at[pl.ds(pl.multiple_of(dst_row * ROW_TILE, ROW_TILE), ROW_TILE), :], sem)


DMA_UNROLL = 4


RUN_CHUNK = 256


def _run_copies(src, dst, src_row, dst_row, n, sem, start):
    def piece(offset, size):
        cp = pltpu.make_async_copy(
            src.at[pl.ds(pl.multiple_of((src_row + offset) * ROW_TILE, ROW_TILE), size * ROW_TILE), :],
            dst.at[pl.ds(pl.multiple_of((dst_row + offset) * ROW_TILE, ROW_TILE), size * ROW_TILE), :], sem)
        if start:
            cp.start()
        else:
            cp.wait()

    def whole(j, carry):
        piece(j * RUN_CHUNK, RUN_CHUNK)
        return carry

    lax.fori_loop(0, n // RUN_CHUNK, whole, 0)
    size = RUN_CHUNK // 2
    while size >= 1:
        @pl.when((n & size) != 0)
        def _(size=size):
            piece(n & ~(2 * size - 1), size)
        size //= 2


def _dispatch_kernel(cnt_ref, off_ref, dst_ref, last_ref, lp_ref, h2_ref, zeros_hbm, xs_hbm, xc_ref, sem, zsem):
    m = pl.program_id(0)

    @pl.when(m == 0)
    def _():
        def pad_copy(e):
            row = pl.multiple_of(last_ref[e] * ROW_TILE, FFN_BLK * ROW_TILE)
            return pltpu.make_async_copy(zeros_hbm, xs_hbm.at[pl.ds(row, FFN_BLK * ROW_TILE), :], zsem)

        def zero_one(e, carry):
            pad_copy(e).start()
            pad_copy(e).wait()
            return carry

        lax.fori_loop(0, N_EXP, zero_one, 0)

    r_i = lax.broadcasted_iota(I32, (2 * TM, TM), 0)
    hit = jnp.logical_or(r_i == lp_ref[0:1, :], r_i == lp_ref[1:2, :])
    perm = jnp.where(hit, 1.0, 0.0).astype(BF16)
    _store_rows(xc_ref, jnp.dot(perm, h2_ref[...], preferred_element_type=F32))

    for start in (True, False):
        for e in range(N_EXP):
            j = m * N_EXP + e
            _run_copies(xc_ref, xs_hbm, off_ref[j], dst_ref[j], cnt_ref[j], sem, start)


def _dispatch(tile_cnt, tile_off, tile_dst, last_blk, lp, h2, zeros_blk):
    grid_spec = pltpu.PrefetchScalarGridSpec(
        num_scalar_prefetch=4,
        grid=(N_TILES,),
        in_specs=[pl.BlockSpec((None, 2, TM), lambda m, *_: (m, 0, 0)),
                  pl.BlockSpec((TM, D), lambda m, *_: (m, 0)),
                  pl.BlockSpec(memory_space=pl.ANY)],
        out_specs=pl.BlockSpec(memory_space=pl.ANY),
        scratch_shapes=[pltpu.VMEM((2 * TM * ROW_TILE, LANE), F32),
                        pltpu.SemaphoreType.DMA, pltpu.SemaphoreType.DMA],
    )
    return pl.pallas_call(
        _dispatch_kernel,
        out_shape=jax.ShapeDtypeStruct((N_SLOTS * ROW_TILE, LANE), F32),
        grid_spec=grid_spec,
        compiler_params=_params(("arbitrary",)),
        name="moe_dispatch",
    )(tile_cnt, tile_off, tile_dst, last_blk, lp, h2, zeros_blk)


def _ffn_kernel(blk_row_ref, blk_exp_ref, n_act_ref, x_ref, wgu_ref, wd_ref, y_ref, wgu_s, wd_s):
    del blk_row_ref
    i = pl.program_id(0)
    new_expert = jnp.logical_or(i == 0, blk_exp_ref[i] != blk_exp_ref[jnp.maximum(i - 1, 0)])
    _cast_weights(new_expert, [(wgu_ref, wgu_s), (wd_ref, wd_s)])

    @pl.when(i < n_act_ref[0])
    def _():
        x = _load_rows(x_ref, FFN_BLK).astype(BF16)
        gu = jnp.dot(x, wgu_s[...], preferred_element_type=F32)
        gate = gu[:, :D_EXP]
        up = gu[:, D_EXP:]
        act = (gate * _sigmoid(gate) * up).astype(BF16)
        _store_rows(y_ref, jnp.dot(act, wd_s[...], preferred_element_type=F32))


def _expert_ffn(blk_row, blk_exp, n_act, xs, wgu, wd, l):
    grid_spec = pltpu.PrefetchScalarGridSpec(
        num_scalar_prefetch=3,
        grid=(N_FFN_BLOCKS,),
        in_specs=[pl.BlockSpec((FFN_BLK * ROW_TILE, LANE), lambda i, br, be, na: (br[i], 0)),
                  pl.BlockSpec((None, None, D, 2 * D_EXP), lambda i, br, be, na: (l, be[i], 0, 0)),
                  pl.BlockSpec((None, None, D_EXP, D), lambda i, br, be, na: (l, be[i], 0, 0))],
        out_specs=pl.BlockSpec((FFN_BLK * ROW_TILE, LANE), lambda i, br, be, na: (br[i], 0)),
        scratch_shapes=[pltpu.VMEM((D, 2 * D_EXP), BF16), pltpu.VMEM((D_EXP, D), BF16)],
    )
    return pl.pallas_call(
        _ffn_kernel,
        out_shape=jax.ShapeDtypeStruct((N_SLOTS * ROW_TILE, LANE), F32),
        grid_spec=grid_spec,
        compiler_params=_params(("arbitrary",)),
        name="expert_ffn",
    )(blk_row, blk_exp, n_act, xs, wgu, wd)


def _block_plan(counts):
    nblk = (counts + FFN_BLK - 1) // FFN_BLK
    end = jnp.cumsum(nblk)
    start = end - nblk
    n_act = end[-1]
    i = jnp.arange(N_FFN_BLOCKS, dtype=I32)
    i_eff = jnp.minimum(i, n_act - 1)
    e = jnp.minimum(jnp.sum(i_eff[:, None] >= end[None, :], axis=1), N_EXP - 1).astype(I32)
    last_blk = jnp.minimum(start + jnp.maximum(nblk - 1, 0), N_FFN_BLOCKS - 1) * FFN_BLK
    return ((start * FFN_BLK).astype(I32), last_blk.astype(I32), i_eff.astype(I32), e,
            n_act.reshape(1).astype(I32))


def _tile_plan(route, seg_start):
    is_exp = route[:, 0:2, :, None] == jnp.arange(N_EXP, dtype=I32)
    cnt = jnp.sum(is_exp, axis=(1, 2), dtype=I32)
    before = jnp.cumsum(cnt, axis=0) - cnt
    off = jnp.cumsum(cnt, axis=1) - cnt
    dst = seg_start[None, :] + before
    pick = lambda tab: jnp.sum(jnp.where(is_exp, tab[:, None, None, :], 0), axis=-1)
    rank = route[:, 2:4, :]
    local = pick(off - before) + rank
    slots = pick(jnp.broadcast_to(seg_start, cnt.shape)) + rank
    return cnt.reshape(-1), off.reshape(-1), dst.reshape(-1), local, slots


def _combine_kernel(slot_ref, slot_next_ref, y_hbm, rw_ref, x1_ref, mod_ref, modn_ref,
                    lng_ref, lnb_ref, x2_ref, hn_ref, buf, sems):
    m = pl.program_id(0)
    half = m % 2

    def gather_tile(s_ref, dst_half):
        def body(i, carry):
            for u in range(DMA_UNROLL):
                t = i * DMA_UNROLL + u
                for k in range(2):
                    _row_copy(y_hbm, buf.at[dst_half, k], s_ref[k, t], t, sems.at[dst_half]).start()
            return carry

        lax.fori_loop(0, TM // DMA_UNROLL, body, 0)

    @pl.when(m == 0)
    def _():
        gather_tile(slot_ref, 0)

    @pl.when(m + 1 < pl.num_programs(0))
    def _():
        gather_tile(slot_next_ref, 1 - half)

    def drain(i, carry):
        for _ in range(2 * DMA_UNROLL):
            _row_copy(y_hbm, buf.at[half, 0], 0, 0, sems.at[half]).wait()
        return carry

    lax.fori_loop(0, TM // DMA_UNROLL, drain, 0)
    f = rw_ref[:, 0:1] * _load_rows(buf.at[half, 0], TM) + rw_ref[:, 1:2] * _load_rows(buf.at[half, 1], TM)
    x2 = _layer_norm(ALPHA * x1_ref[...] + mod_ref[5:6, :] * f, lng_ref[...], lnb_ref[...])
    x2_ref[...] = x2
    hn_ref[...] = (x2 * (1.0 + modn_ref[1:2, :]) + modn_ref[0:1, :]).astype(BF16)


def _combine(slots, y_slots, rw, x1, mod_l, mod_next, ln_g, ln_b):
    row = lambda m: (m, 0)
    const = lambda m: (0, 0)
    mod_spec = pl.BlockSpec((None, N_MOD, D), lambda m: (_cond_row(m, CTX_TILES, TILES_PER_LAT), 0, 0))
    return pl.pallas_call(
        _combine_kernel,
        out_shape=(jax.ShapeDtypeStruct((T, D), F32), jax.ShapeDtypeStruct((T, D), BF16)),
        grid=(N_TILES,),
        in_specs=[pl.BlockSpec((None, 2, TM), lambda m: (m, 0, 0), memory_space=pltpu.SMEM),
                  pl.BlockSpec((None, 2, TM), lambda m: (jnp.minimum(m + 1, N_TILES - 1), 0, 0),
                               memory_space=pltpu.SMEM),
                  pl.BlockSpec(memory_space=pl.ANY),
                  pl.BlockSpec((TM, LANE), row), pl.BlockSpec((TM, D), row),
                  mod_spec, mod_spec,
                  pl.BlockSpec((1, D), const), pl.BlockSpec((1, D), const)],
        out_specs=(pl.BlockSpec((TM, D), row), pl.BlockSpec((TM, D), row)),
        scratch_shapes=[pltpu.VMEM((2, 2, TM * ROW_TILE, LANE), F32), pltpu.SemaphoreType.DMA((2,))],
        compiler_params=_params(("arbitrary",)),
        name="moe_combine_ln",
    )(slots, slots, y_slots, rw, x1, mod_l, mod_next, ln_g, ln_b)


def kernel(x_prompt, x_sample, c, c_ctx, cache_k, cache_v, w_in, conv_a_w, w_a_out, conv_b_w, conv_b_b,
           norm_b_g, norm_b_b, w_b_out, q_norm_g, k_norm_g, w_c_out, w_o, w_ada, b_ada, ln_g, ln_b,
           w_router, router_bias, w_gate_up, w_down):
    x = jnp.concatenate([x_prompt.reshape(T_CTX, D), x_sample.reshape(T_LAT, D)], axis=0)
    cond16 = jnp.concatenate([c_ctx[None, :], c, jnp.zeros((16 - 1 - N_LAT_SEQ, D), F32)], axis=0)
    mod = _modulation(cond16, w_ada, b_ada).reshape(DEPTH, 16, N_MOD, D)

    wa_b, wb_b, wc_b, wo_b = (w.astype(BF16) for w in (w_a_out, w_b_out, w_c_out, w_o))
    wr = jnp.pad(w_router, ((0, 0), (0, LANE - N_EXP)))
    wr_hi = wr.astype(BF16)
    wr_lo = (wr - wr_hi.astype(F32)).astype(BF16)
    rbias = router_bias.reshape(N_EXP, 1)

    cos_tab, sin_tab = _rope_tables()
    half = np.arange(LANE) // HEAD
    ones_bd = jnp.asarray(half[:, None] == half[None, :], BF16)
    zeros_blk = jnp.zeros((FFN_BLK * ROW_TILE, LANE), F32)
    ck = cache_k.reshape(N_LAT_SEQ, DEPTH, PAST, D_KV)
    cv = cache_v.reshape(N_LAT_SEQ, DEPTH, PAST, D_KV)

    h = _modulate(x, mod[0])
    new_k, new_v = [], []
    for l in range(DEPTH):
        ab, acx = _proj_a(h, w_in, l)
        glu = _proj_b(h, w_in, l)
        gates = _proj_gates(h, w_in, l)
        q, katt, vatt, kn, vf = _proj_qkv(h, w_in, jnp.tile(q_norm_g[l], N_Q)[None, :],
                                          jnp.tile(k_norm_g[l], N_KV)[None, :], cos_tab, sin_tab, ones_bd, l)
        new_k.append(kn[:T_CTX].reshape(N_CTX_SEQ, CTX_LEN, N_KV, HEAD))
        new_v.append(vf[:T_CTX].reshape(N_CTX_SEQ, CTX_LEN, N_KV, HEAD))
        ya_pre, yb_pre = _convs(acx, ab, glu, conv_a_w[l], conv_b_w[l], conv_b_b[l][None, :],
                                norm_b_g[l][None, :], norm_b_b[l][None, :])
        o = _attention(q, katt, vatt, ck, cv, l)
        x1, h2, route, rw, counts = _post(ya_pre, yb_pre, o, gates, x, mod[l], wa_b, wb_b, wc_b, wo_b,
                                          ln_g[l, 0][None, :], ln_b[l, 0][None, :], wr_hi, wr_lo, rbias, l)
        cnt = counts[:, 0]
        seg_start, last_blk, blk_row, blk_exp, n_act = _block_plan(cnt)
        tile_cnt, tile_off, tile_dst, local, slots = _tile_plan(route, seg_start)
        xs = _dispatch(tile_cnt, tile_off, tile_dst, last_blk, local, h2, zeros_blk)
        y_slots = _expert_ffn(blk_row, blk_exp, n_act, xs, w_gate_up, w_down, l)
        x, h = _combine(slots, y_slots, rw, x1, mod[l], mod[min(l + 1, DEPTH - 1)],
                        ln_g[l, 1][None, :], ln_b[l, 1][None, :])
    y_prompt = x[:T_CTX].reshape(N_CTX_SEQ, CTX_LEN, D)
    y_sample = x[T_CTX:].reshape(N_LAT_SEQ, LAT_LEN, D)
    return y_prompt, y_sample, jnp.stack(new_k, axis=1), jnp.stack(new_v, axis=1)
```

```python
import functools

import numpy as np
import jax
import jax.numpy as jnp
from jax import lax
from jax.experimental import pallas as pl
from jax.experimental.pallas import tpu as pltpu

F32 = jnp.float32
BF16 = jnp.bfloat16
I32 = jnp.int32

D = 1024
DEPTH = 4
N_CTX_SEQ = 16
CTX_LEN = 256
N_LAT_SEQ = 8
LAT_LEN = 1024
PAST = 512
T_CTX = N_CTX_SEQ * CTX_LEN
T_LAT = N_LAT_SEQ * LAT_LEN
T = T_CTX + T_LAT
GRID_W = 64
HEAD = 64
N_Q = 16
N_KV = 4
GROUP = 4
D_KV = N_KV * HEAD
N_EXP = 16
N_GRP = 4
EXP_PER_GRP = 4
D_EXP = 512
IN_COLS = 9728
N_MOD = 6
ALPHA = (2 * DEPTH) ** 0.25
LN_EPS = 1e-5
RMS_EPS = 1e-6
ROPE_THETA = 10000.0

LANE = 128
TM = 512
N_TILES = T // TM
CTX_TILES = T_CTX // TM
TILES_PER_LAT = LAT_LEN // TM
TMP = 1024
SEG = 256
N_SEG = T // SEG
CTX_SEGS = T_CTX // SEG
SEGS_PER_LAT = LAT_LEN // SEG
HALO_A = 8
HALO_B = 16
SH_ROWS = SEG + 2 * HALO_B - 8
FFN_BLK = 256
N_FFN_BLOCKS = (2 * T) // FFN_BLK + N_EXP
N_SLOTS = N_FFN_BLOCKS * FFN_BLK
VMEM_LIMIT = 56 * 1024 * 1024

COL_AB, COL_AC, COL_AX, COL_BU, COL_BG, COL_Q, COL_KV, COL_GATES = 0, 2, 4, 6, 8, 10, 12, 13


def _params(sem):
    return pltpu.CompilerParams(dimension_semantics=sem, vmem_limit_bytes=VMEM_LIMIT)


def _cond_row(m, tiles_ctx, tiles_per_lat):
    return jnp.where(m < tiles_ctx, 0, 1 + (m - tiles_ctx) // tiles_per_lat)


def _layer_norm(x, g, b):
    mu = jnp.mean(x, axis=-1, keepdims=True)
    xc = x - mu
    var = jnp.mean(xc * xc, axis=-1, keepdims=True)
    return xc * lax.rsqrt(var + LN_EPS) * g + b


def _sigmoid(x):
    return 1.0 / (1.0 + jnp.exp(-x))


ROW_TILE = D // LANE


def _store_rows(ref, x):
    for j in range(ROW_TILE):
        ref[pl.ds(j, x.shape[0], stride=ROW_TILE), :] = x[:, j * LANE:(j + 1) * LANE]


def _load_rows(ref, n_rows):
    return jnp.concatenate([ref[pl.ds(j, n_rows, stride=ROW_TILE), :] for j in range(ROW_TILE)], axis=1)


def _mod_kernel(cond_ref, w_ref, b_ref, o_ref):
    cnd = cond_ref[...]
    s = (cnd * _sigmoid(cnd)).astype(BF16)
    o_ref[...] = jnp.dot(s, w_ref[...].astype(BF16), preferred_element_type=F32) + b_ref[...]


def _modulation(cond16, w_ada, b_ada):
    n_col = N_MOD * D
    tn = 1024
    return pl.pallas_call(
        _mod_kernel,
        out_shape=jax.ShapeDtypeStruct((DEPTH, 16, n_col), F32),
        grid=(DEPTH, n_col // tn),
        in_specs=[
            pl.BlockSpec((16, D), lambda l, n: (0, 0)),
            pl.BlockSpec((None, D, tn), lambda l, n: (l, 0, n)),
            pl.BlockSpec((None, 1, tn), lambda l, n: (l, 0, n)),
        ],
        out_specs=pl.BlockSpec((None, 16, tn), lambda l, n: (l, 0, n)),
        compiler_params=_params(("parallel", "parallel")),
        name="adaln_mod",
    )(cond16, w_ada, b_ada.reshape(DEPTH, 1, n_col))


def _modulate_kernel(x_ref, mod_ref, h_ref):
    h_ref[...] = (x_ref[...] * (1.0 + mod_ref[1:2, :]) + mod_ref[0:1, :]).astype(BF16)


def _modulate(x, mod_l):
    return pl.pallas_call(
        _modulate_kernel,
        out_shape=jax.ShapeDtypeStruct((T, D), BF16),
        grid=(N_TILES,),
        in_specs=[
            pl.BlockSpec((TM, D), lambda m: (m, 0)),
            pl.BlockSpec((None, N_MOD, D), lambda m: (_cond_row(m, CTX_TILES, TILES_PER_LAT), 0, 0)),
        ],
        out_specs=pl.BlockSpec((TM, D), lambda m: (m, 0)),
        compiler_params=_params(("parallel",)),
        name="modulate",
    )(x, mod_l)


WCOL = 512


def _cast_weights(first, pairs):
    @pl.when(first)
    def _():
        for src, dst in pairs:
            dst[...] = src[...].astype(BF16)


def _proj_a_kernel(h_ref, wb_ref, wc_ref, wx_ref, ab_ref, acx_ref, wb_s, wc_s, wx_s):
    _cast_weights(pl.program_id(1) == 0, [(wb_ref, wb_s), (wc_ref, wc_s), (wx_ref, wx_s)])
    h = h_ref[...]
    ab_ref[...] = jnp.dot(h, wb_s[...], preferred_element_type=F32).astype(BF16)
    acx_ref[...] = (jnp.dot(h, wc_s[...], preferred_element_type=F32)
                    * jnp.dot(h, wx_s[...], preferred_element_type=F32))


def _proj_b_kernel(h_ref, wu_ref, wg_ref, glu_ref, wu_s, wg_s):
    _cast_weights(pl.program_id(1) == 0, [(wu_ref, wu_s), (wg_ref, wg_s)])
    h = h_ref[...]
    glu_ref[...] = (jnp.dot(h, wu_s[...], preferred_element_type=F32)
                    * _sigmoid(jnp.dot(h, wg_s[...], preferred_element_type=F32)))


def _proj_gate_kernel(h_ref, w0_ref, w1_ref, g_ref, w_s):
    _cast_weights(pl.program_id(1) == 0, [(w0_ref, w_s.at[:, 0:WCOL]), (w1_ref, w_s.at[:, WCOL:2 * WCOL])])
    g_ref[...] = _sigmoid(jnp.dot(h_ref[...], w_s[...], preferred_element_type=F32)).astype(BF16)


def _w_spec(l, col0, step=1):
    return pl.BlockSpec((None, D, WCOL), lambda c, m: (l, 0, col0 + step * c))


def _proj_a(h, w_in, l):
    out_spec = pl.BlockSpec((TMP, WCOL), lambda c, m: (m, c))
    return pl.pallas_call(
        _proj_a_kernel,
        out_shape=(jax.ShapeDtypeStruct((T, D), BF16), jax.ShapeDtypeStruct((T, D), F32)),
        grid=(D // WCOL, T // TMP),
        in_specs=[pl.BlockSpec((TMP, D), lambda c, m: (m, 0)),
                  _w_spec(l, COL_AB), _w_spec(l, COL_AC), _w_spec(l, COL_AX)],
        out_specs=(out_spec, out_spec),
        scratch_shapes=[pltpu.VMEM((D, WCOL), BF16)] * 3,
        compiler_params=_params(("parallel", "arbitrary")),
        name="proj_a",
    )(h, w_in, w_in, w_in)


def _proj_b(h, w_in, l):
    return pl.pallas_call(
        _proj_b_kernel,
        out_shape=jax.ShapeDtypeStruct((T, D), F32),
        grid=(D // WCOL, T // TMP),
        in_specs=[pl.BlockSpec((TMP, D), lambda c, m: (m, 0)), _w_spec(l, COL_BU), _w_spec(l, COL_BG)],
        out_specs=pl.BlockSpec((TMP, WCOL), lambda c, m: (m, c)),
        scratch_shapes=[pltpu.VMEM((D, WCOL), BF16)] * 2,
        compiler_params=_params(("parallel", "arbitrary")),
        name="proj_b",
    )(h, w_in, w_in)


def _proj_gates(h, w_in, l):
    tn = 2 * WCOL
    return pl.pallas_call(
        _proj_gate_kernel,
        out_shape=jax.ShapeDtypeStruct((T, 3 * D), BF16),
        grid=(3 * D // tn, T // TMP),
        in_specs=[pl.BlockSpec((TMP, D), lambda c, m: (m, 0)),
                  _w_spec(l, COL_GATES, 2), _w_spec(l, COL_GATES + 1, 2)],
        out_specs=pl.BlockSpec((TMP, tn), lambda c, m: (m, c)),
        scratch_shapes=[pltpu.VMEM((D, tn), BF16)],
        compiler_params=_params(("parallel", "arbitrary")),
        name="proj_gates",
    )(h, w_in, w_in)


def _head_mean_square(x, ones_bd):
    out = []
    for c in range(x.shape[1] // LANE):
        sq = x[:, c * LANE:(c + 1) * LANE]
        sq = sq * sq
        hi = sq.astype(BF16)
        lo = (sq - hi.astype(F32)).astype(BF16)
        out.append(jnp.dot(hi, ones_bd, preferred_element_type=F32)
                   + jnp.dot(lo, ones_bd, preferred_element_type=F32))
    return jnp.concatenate(out, axis=1) * (1.0 / HEAD)


def _rope(x, cos, sin, first_half):
    out = []
    for c in range(x.shape[1] // LANE):
        xc = x[:, c * LANE:(c + 1) * LANE]
        partner = jnp.where(first_half, pltpu.roll(xc, LANE - 16, axis=1), pltpu.roll(xc, 16, axis=1))
        out.append(xc * cos + partner * sin)
    return jnp.concatenate(out, axis=1)


def _qkv_kernel(h_ref, wq_ref, wkv_ref, gq_ref, gk_ref, cos_ref, sin_ref, ones_ref,
                q_ref, katt_ref, vatt_ref, kn_ref, vf_ref, wq_s, wkv_s):
    _cast_weights(pl.program_id(0) == 0, [(wq_ref, wq_s), (wkv_ref, wkv_s)])
    h = h_ref[...]
    ones_bd = ones_ref[...]
    cos = cos_ref[...]
    sin = sin_ref[...]
    lane = lax.broadcasted_iota(I32, (TM, LANE), 1)
    first_half = (lane & 16) == 0
    q = jnp.dot(h, wq_s[...], preferred_element_type=F32)
    qn = q * lax.rsqrt(_head_mean_square(q, ones_bd) + RMS_EPS) * gq_ref[...]
    q_ref[...] = (_rope(qn, cos, sin, first_half) * (HEAD ** -0.5)).astype(BF16)
    kv = jnp.dot(h, wkv_s[...], preferred_element_type=F32)
    k = kv[:, :D_KV]
    v = kv[:, D_KV:]
    kn = k * lax.rsqrt(_head_mean_square(k, ones_bd) + RMS_EPS) * gk_ref[...]
    kn_ref[...] = kn
    katt_ref[...] = _rope(kn, cos, sin, first_half).astype(BF16)
    vf_ref[...] = v
    vatt_ref[...] = v.astype(BF16)


def _proj_qkv(h, w_in, gq, gk, cos_tab, sin_tab, ones_bd, l):
    def tab_idx(m):
        return jnp.where(m < CTX_TILES, TILES_PER_LAT, (m - CTX_TILES) % TILES_PER_LAT)

    row = lambda m: (m, 0)
    return pl.pallas_call(
        _qkv_kernel,
        out_shape=(jax.ShapeDtypeStruct((T, D), BF16),
                   jax.ShapeDtypeStruct((T, D_KV), BF16),
                   jax.ShapeDtypeStruct((T, D_KV), BF16),
                   jax.ShapeDtypeStruct((T, D_KV), F32),
                   jax.ShapeDtypeStruct((T, D_KV), F32)),
        grid=(N_TILES,),
        in_specs=[
            pl.BlockSpec((TM, D), row),
            pl.BlockSpec((None, D, D), lambda m: (l, 0, COL_Q // 2)),
            pl.BlockSpec((None, D, 2 * D_KV), lambda m: (l, 0, COL_KV)),
            pl.BlockSpec((1, D), lambda m: (0, 0)),
            pl.BlockSpec((1, D_KV), lambda m: (0, 0)),
            pl.BlockSpec((TM, LANE), lambda m: (tab_idx(m), 0)),
            pl.BlockSpec((TM, LANE), lambda m: (tab_idx(m), 0)),
            pl.BlockSpec((LANE, LANE), lambda m: (0, 0)),
        ],
        out_specs=(pl.BlockSpec((TM, D), row), pl.BlockSpec((TM, D_KV), row),
                   pl.BlockSpec((TM, D_KV), row), pl.BlockSpec((TM, D_KV), row),
                   pl.BlockSpec((TM, D_KV), row)),
        scratch_shapes=[pltpu.VMEM((D, D), BF16), pltpu.VMEM((D, 2 * D_KV), BF16)],
        compiler_params=_params(("arbitrary",)),
        name="proj_qkv",
    )(h, w_in, w_in, gq, gk, cos_tab, sin_tab, ones_bd)


def _rope_tables():
    lane = np.arange(LANE)
    j = lane % 16
    freqs = jnp.power(ROPE_THETA, -jnp.arange(16, dtype=F32) / 16)[j]
    pos = jnp.arange(LAT_LEN, dtype=I32)
    row = (pos // GRID_W).astype(F32)
    col = (pos % GRID_W).astype(F32)
    use_row = jnp.asarray((lane % HEAD) < HEAD // 2)
    p = jnp.where(use_row[None, :], row[:, None], col[:, None])
    ang = p * freqs[None, :]
    sign = jnp.asarray(np.where((lane & 16) == 0, -1.0, 1.0), F32)
    cos = jnp.concatenate([jnp.cos(ang), jnp.ones((TM, LANE), F32)], axis=0)
    sin = jnp.concatenate([jnp.sin(ang) * sign[None, :], jnp.zeros((TM, LANE), F32)], axis=0)
    return cos, sin


def _conv_kernel(acx_ref, acx_l_ref, acx_r_ref, ab_ref, glu_ref, glu_l_ref, glu_r_ref,
                 wa_ref, wb_ref, bb_ref, ng_ref, nb_ref, ya_ref, yb_ref, pad_a, pad_b, u_ref, sh_ref):
    s = pl.program_id(0)
    lat = s >= CTX_SEGS
    pos = (s - CTX_SEGS) % SEGS_PER_LAT
    has_left = jnp.logical_and(lat, pos != 0)
    has_right = jnp.logical_and(lat, pos != SEGS_PER_LAT - 1)

    pad_a[0:HALO_A, :] = jnp.where(has_left, acx_l_ref[...], 0.0)
    pad_a[HALO_A:HALO_A + SEG, :] = acx_ref[...]
    pad_a[HALO_A + SEG:, :] = jnp.where(has_right, acx_r_ref[...], 0.0)
    pad_b[0:HALO_B, :] = jnp.where(has_left, glu_l_ref[...], 0.0)
    pad_b[HALO_B:HALO_B + SEG, :] = glu_ref[...]
    pad_b[HALO_B + SEG:, :] = jnp.where(has_right, glu_r_ref[...], 0.0)

    conv_a = (wa_ref[0:1, :] * pad_a[HALO_A - 1:HALO_A - 1 + SEG, :]
              + wa_ref[1:2, :] * pad_a[HALO_A:HALO_A + SEG, :]
              + wa_ref[2:3, :] * pad_a[HALO_A + 1:HALO_A + 1 + SEG, :])
    ya_ref[...] = (ab_ref[...].astype(F32) * conv_a).astype(BF16)

    rows = 64
    kb = wb_ref.shape[0]

    def lane_chunk(c, carry):
        lanes = pl.ds(pl.multiple_of(c * LANE, LANE), LANE)
        for b in range(1, 8):
            sh_ref[b, :, :] = pad_b[b:b + SH_ROWS, lanes]
        for r in range(SEG // rows):
            acc = jnp.zeros((rows, LANE), F32)
            for k in range(kb):
                off = HALO_B + k - kb // 2
                row0 = r * rows + 8 * (off // 8)
                if off % 8 == 0:
                    src = pad_b[row0:row0 + rows, lanes]
                else:
                    src = sh_ref[off % 8, row0:row0 + rows, :]
                acc = acc + wb_ref[k:k + 1, lanes] * src
            u_ref[r * rows:(r + 1) * rows, lanes] = acc
        return carry

    lax.fori_loop(0, D // LANE, lane_chunk, 0)
    u = _layer_norm(u_ref[...] + bb_ref[...], ng_ref[...], nb_ref[...])
    yb_ref[...] = (u * _sigmoid(u)).astype(BF16)


def _convs(acx, ab, glu, conv_a_w, conv_b_w, conv_b_b, norm_g, norm_b):
    seg = lambda s: (s, 0)
    const = lambda s: (0, 0)
    ra, rb = SEG // HALO_A, SEG // HALO_B
    left_a = lambda s: (jnp.maximum(s * ra - 1, 0), 0)
    right_a = lambda s: (jnp.minimum((s + 1) * ra, T // HALO_A - 1), 0)
    left_b = lambda s: (jnp.maximum(s * rb - 1, 0), 0)
    right_b = lambda s: (jnp.minimum((s + 1) * rb, T // HALO_B - 1), 0)
    return pl.pallas_call(
        _conv_kernel,
        out_shape=(jax.ShapeDtypeStruct((T, D), BF16), jax.ShapeDtypeStruct((T, D), BF16)),
        grid=(N_SEG,),
        in_specs=[
            pl.BlockSpec((SEG, D), seg), pl.BlockSpec((HALO_A, D), left_a), pl.BlockSpec((HALO_A, D), right_a),
            pl.BlockSpec((SEG, D), seg),
            pl.BlockSpec((SEG, D), seg), pl.BlockSpec((HALO_B, D), left_b), pl.BlockSpec((HALO_B, D), right_b),
            pl.BlockSpec(conv_a_w.shape, const), pl.BlockSpec(conv_b_w.shape, const),
            pl.BlockSpec((1, D), const), pl.BlockSpec((1, D), const), pl.BlockSpec((1, D), const),
        ],
        out_specs=(pl.BlockSpec((SEG, D), seg), pl.BlockSpec((SEG, D), seg)),
        scratch_shapes=[pltpu.VMEM((SEG + 2 * HALO_A, D), F32),
                        pltpu.VMEM((SEG + 2 * HALO_B, D), F32),
                        pltpu.VMEM((SEG, D), F32),
                        pltpu.VMEM((8, SH_ROWS, LANE), F32)],
        compiler_params=_params(("parallel",)),
        name="convs",
    )(acx, acx, acx, ab, glu, glu, glu, conv_a_w, conv_b_w, conv_b_b, norm_g, norm_b)


def _attend(q_ref, key_refs, val_refs, o_ref):
    nt = (((1,), (1,)), ((), ()))
    keys = [r[...].astype(BF16) for r in key_refs]
    vals = [r[...].astype(BF16) for r in val_refs]
    for g in range(N_KV):
        kg = [k[:, g * HEAD:(g + 1) * HEAD] for k in keys]
        vg = [v[:, g * HEAD:(g + 1) * HEAD] for v in vals]
        for hh in range(GROUP):
            hd = g * GROUP + hh
            qh = q_ref[:, hd * HEAD:(hd + 1) * HEAD]
            s = [lax.dot_general(qh, k, nt, preferred_element_type=F32) for k in kg]
            mx = functools.reduce(jnp.maximum, [jnp.max(x, axis=-1, keepdims=True) for x in s])
            p = [jnp.exp(x - mx) for x in s]
            den = functools.reduce(jnp.add, [jnp.sum(x, axis=-1, keepdims=True) for x in p])
            acc = functools.reduce(jnp.add, [jnp.dot(x.astype(BF16), v, preferred_element_type=F32)
                                             for x, v in zip(p, vg)])
            o_ref[:, hd * HEAD:(hd + 1) * HEAD] = (acc / den).astype(BF16)


def _attn_ctx_kernel(q_ref, k_ref, v_ref, o_ref):
    _attend(q_ref, [k_ref], [v_ref], o_ref)


def _attn_lat_kernel(q_ref, k_ref, v_ref, ck_ref, cv_ref, o_in_ref, o_ref):
    del o_in_ref
    _attend(q_ref, [ck_ref, k_ref], [cv_ref, v_ref], o_ref)


def _attention(q, katt, vatt, cache_k, cache_v, l):
    o = pl.pallas_call(
        _attn_ctx_kernel,
        out_shape=jax.ShapeDtypeStruct((T, D), BF16),
        grid=(N_CTX_SEQ,),
        in_specs=[pl.BlockSpec((CTX_LEN, D), lambda b: (b, 0)),
                  pl.BlockSpec((CTX_LEN, D_KV), lambda b: (b, 0)),
                  pl.BlockSpec((CTX_LEN, D_KV), lambda b: (b, 0))],
        out_specs=pl.BlockSpec((CTX_LEN, D), lambda b: (b, 0)),
        compiler_params=_params(("parallel",)),
        name="attn_ctx",
    )(q, katt, vatt)
    seg0 = CTX_SEGS
    lat0 = T_CTX // LAT_LEN
    return pl.pallas_call(
        _attn_lat_kernel,
        out_shape=jax.ShapeDtypeStruct((T, D), BF16),
        grid=(N_LAT_SEQ, SEGS_PER_LAT),
        in_specs=[pl.BlockSpec((SEG, D), lambda b, i: (seg0 + b * SEGS_PER_LAT + i, 0)),
                  pl.BlockSpec((LAT_LEN, D_KV), lambda b, i: (lat0 + b, 0)),
                  pl.BlockSpec((LAT_LEN, D_KV), lambda b, i: (lat0 + b, 0)),
                  pl.BlockSpec((None, None, PAST, D_KV), lambda b, i: (b, l, 0, 0)),
                  pl.BlockSpec((None, None, PAST, D_KV), lambda b, i: (b, l, 0, 0)),
                  pl.BlockSpec(memory_space=pl.ANY)],
        out_specs=pl.BlockSpec((SEG, D), lambda b, i: (seg0 + b * SEGS_PER_LAT + i, 0)),
        input_output_aliases={5: 0},
        compiler_params=_params(("parallel", "parallel")),
        name="attn_lat",
    )(q, katt, vatt, cache_k, cache_v, o)


def _post_kernel(ya_ref, yb_ref, o_ref, g_ref, x_ref, mod_ref, wa_ref, wb_ref, wc_ref, wo_ref,
                 lng_ref, lnb_ref, wr_hi_ref, wr_lo_ref, rb_ref,
                 x1_ref, h2_ref, route_ref, rw_ref, cnt_ref, carry_ref):
    m = pl.program_id(0)

    @pl.when(m == 0)
    def _():
        carry_ref[...] = jnp.zeros_like(carry_ref)

    ya = jnp.dot(ya_ref[...], wa_ref[...], preferred_element_type=F32)
    yb = jnp.dot(yb_ref[...], wb_ref[...], preferred_element_type=F32)
    yc = jnp.dot(o_ref[...], wc_ref[...], preferred_element_type=F32)
    merged = (g_ref[:, 0:D].astype(F32) * ya + g_ref[:, D:2 * D].astype(F32) * yb
              + g_ref[:, 2 * D:3 * D].astype(F32) * yc)
    mix = jnp.dot(merged.astype(BF16), wo_ref[...], preferred_element_type=F32)
    x1 = _layer_norm(ALPHA * x_ref[...] + mod_ref[2:3, :] * mix, lng_ref[...], lnb_ref[...])
    x1_ref[...] = x1
    h2 = x1 * (1.0 + mod_ref[4:5, :]) + mod_ref[3:4, :]

    hi = h2.astype(BF16)
    h2_ref[...] = hi
    lo = (h2 - hi.astype(F32)).astype(BF16)
    wr_hi = wr_hi_ref[...]
    logits = (jnp.dot(hi, wr_hi, preferred_element_type=F32)
              + jnp.dot(lo, wr_hi, preferred_element_type=F32)
              + jnp.dot(hi, wr_lo_ref[...], preferred_element_type=F32))
    scores = _sigmoid(logits.T[0:N_EXP, :])
    sel = scores + rb_ref[...]

    gscore = []
    for g in range(N_GRP):
        r = [sel[g * EXP_PER_GRP + j:g * EXP_PER_GRP + j + 1, :] for j in range(EXP_PER_GRP)]
        pairs = [r[a] + r[b] for a in range(EXP_PER_GRP) for b in range(a + 1, EXP_PER_GRP)]
        gscore.append(functools.reduce(jnp.maximum, pairs))
    best = functools.reduce(jnp.maximum, gscore)
    gsel = jnp.full(best.shape, N_GRP - 1, I32)
    for g in range(N_GRP - 2, -1, -1):
        gsel = jnp.where(gscore[g] == best, g, gsel)

    eidx = lax.broadcasted_iota(I32, (N_EXP, TM), 0)
    neg = jnp.float32(-jnp.inf)
    cand = jnp.where((eidx // EXP_PER_GRP) == gsel, sel, neg)
    top1 = jnp.max(cand, axis=0, keepdims=True)
    idx1 = jnp.min(jnp.where(cand == top1, eidx, N_EXP), axis=0, keepdims=True)
    cand2 = jnp.where(eidx == idx1, neg, cand)
    top2 = jnp.max(cand2, axis=0, keepdims=True)
    idx2 = jnp.min(jnp.where(cand2 == top2, eidx, N_EXP), axis=0, keepdims=True)
    is1 = eidx == idx1
    is2 = eidx == idx2
    w1 = jnp.sum(jnp.where(is1, scores, 0.0), axis=0, keepdims=True)
    w2 = jnp.sum(jnp.where(is2, scores, 0.0), axis=0, keepdims=True)
    wsum = w1 + w2
    w1 = w1 / wsum
    w2 = w2 / wsum

    onehot = jnp.where(jnp.logical_or(is1, is2), 1.0, 0.0)
    r_i = lax.broadcasted_iota(I32, (TM, TM), 0)
    c_i = lax.broadcasted_iota(I32, (TM, TM), 1)
    upper = jnp.where(r_i < c_i, 1.0, 0.0).astype(BF16)
    prefix = jnp.dot(onehot.astype(BF16), upper, preferred_element_type=F32) + carry_ref[:, 0:1]
    rank1 = jnp.sum(jnp.where(is1, prefix, 0.0), axis=0, keepdims=True)
    rank2 = jnp.sum(jnp.where(is2, prefix, 0.0), axis=0, keepdims=True)
    carry = carry_ref[...] + jnp.sum(onehot, axis=1, keepdims=True)
    carry_ref[...] = carry
    cnt_ref[...] = carry.astype(I32)

    route_ref[0:1, :] = idx1
    route_ref[1:2, :] = idx2
    route_ref[2:3, :] = rank1.astype(I32)
    route_ref[3:4, :] = rank2.astype(I32)
    route_ref[4:8, :] = jnp.zeros((4, TM), I32)
    wrow = lax.broadcasted_iota(I32, (LANE, TM), 0)
    wcols = jnp.where(wrow == 0, w1, jnp.where(wrow == 1, w2, 0.0))
    rw_ref[...] = wcols.T


def _post(ya_pre, yb_pre, o, gates, x, mod_l, wa, wb, wc, wo, ln_g, ln_b, wr_hi, wr_lo, rbias, l):
    row = lambda m: (m, 0)
    const = lambda m: (0, 0)
    wspec = pl.BlockSpec((None, D, D), lambda m: (l, 0, 0))
    return pl.pallas_call(
        _post_kernel,
        out_shape=(jax.ShapeDtypeStruct((T, D), F32),
                   jax.ShapeDtypeStruct((T, D), BF16),
                   jax.ShapeDtypeStruct((N_TILES, 8, TM), I32),
                   jax.ShapeDtypeStruct((T, LANE), F32),
                   jax.ShapeDtypeStruct((N_EXP, LANE), I32)),
        grid=(N_TILES,),
        in_specs=[pl.BlockSpec((TM, D), row), pl.BlockSpec((TM, D), row), pl.BlockSpec((TM, D), row),
                  pl.BlockSpec((TM, 3 * D), row), pl.BlockSpec((TM, D), row),
                  pl.BlockSpec((None, N_MOD, D), lambda m: (_cond_row(m, CTX_TILES, TILES_PER_LAT), 0, 0)),
                  wspec, wspec, wspec, wspec,
                  pl.BlockSpec((1, D), const), pl.BlockSpec((1, D), const),
                  pl.BlockSpec((D, LANE), const), pl.BlockSpec((D, LANE), const),
                  pl.BlockSpec((N_EXP, 1), const)],
        out_specs=(pl.BlockSpec((TM, D), row), pl.BlockSpec((TM, D), row),
                   pl.BlockSpec((None, 8, TM), lambda m: (m, 0, 0)),
                   pl.BlockSpec((TM, LANE), row),
                   pl.BlockSpec((N_EXP, LANE), const)),
        scratch_shapes=[pltpu.VMEM((N_EXP, LANE), F32)],
        compiler_params=_params(("arbitrary",)),
        name="merge_ln_router",
    )(ya_pre, yb_pre, o, gates, x, mod_l, wa, wb, wc, wo, ln_g, ln_b, wr_hi, wr_lo, rbias)


def _row_copy(src, dst, src_row, dst_row, sem):
    return pltpu.make_async_copy(src.at[pl.ds(pl.multiple_of(src_row * ROW_TILE, ROW_TILE), ROW_TILE), :],
                                 dst.at[pl.ds(pl.multiple_of(dst_row * ROW_TILE, ROW_TILE), ROW_TILE), :], sem)


DMA_UNROLL = 4


RUN_CHUNK = 256


def _run_copies(src, dst, src_row, dst_row, n, sem, start):
    def piece(offset, size):
        cp = pltpu.make_async_copy(
            src.at[pl.ds(pl.multiple_of((src_row + offset) * ROW_TILE, ROW_TILE), size * ROW_TILE), :],
            dst.at[pl.ds(pl.multiple_of((dst_row + offset) * ROW_TILE, ROW_TILE), size * ROW_TILE), :], sem)
        if start:
            cp.start()
        else:
            cp.wait()

    def whole(j, carry):
        piece(j * RUN_CHUNK, RUN_CHUNK)
        return carry

    lax.fori_loop(0, n // RUN_CHUNK, whole, 0)
    size = RUN_CHUNK // 2
    while size >= 1:
        @pl.when((n & size) != 0)
        def _(size=size):
            piece(n & ~(2 * size - 1), size)
        size //= 2


def _dispatch_kernel(cnt_ref, off_ref, dst_ref, last_ref, lp_ref, h2_ref, xs_hbm, xc_ref, zero_ref, sem, zsem):
    m = pl.program_id(0)

    @pl.when(m == 0)
    def _():
        zero_ref[...] = jnp.zeros_like(zero_ref)

        def pad_copy(e):
            row = pl.multiple_of(last_ref[e] * ROW_TILE, FFN_BLK * ROW_TILE)
            return pltpu.make_async_copy(zero_ref, xs_hbm.at[pl.ds(row, FFN_BLK * ROW_TILE), :], zsem)

        def zero_start(e, carry):
            pad_copy(e).start()
            return carry

        def zero_wait(e, carry):
            pad_copy(e).wait()
            return carry

        lax.fori_loop(0, N_EXP, zero_start, 0)
        lax.fori_loop(0, N_EXP, zero_wait, 0)

    r_i = lax.broadcasted_iota(I32, (2 * TM, TM), 0)
    hit = jnp.logical_or(r_i == lp_ref[0:1, :], r_i == lp_ref[1:2, :])
    perm = jnp.where(hit, 1.0, 0.0).astype(BF16)
    _store_rows(xc_ref, jnp.dot(perm, h2_ref[...], preferred_element_type=F32))

    for start in (True, False):
        for e in range(N_EXP):
            j = m * N_EXP + e
            _run_copies(xc_ref, xs_hbm, off_ref[j], dst_ref[j], cnt_ref[j], sem, start)


def _dispatch(tile_cnt, tile_off, tile_dst, last_blk, lp, h2):
    grid_spec = pltpu.PrefetchScalarGridSpec(
        num_scalar_prefetch=4,
        grid=(N_TILES,),
        in_specs=[pl.BlockSpec((None, 2, TM), lambda m, *_: (m, 0, 0)),
                  pl.BlockSpec((TM, D), lambda m, *_: (m, 0))],
        out_specs=pl.BlockSpec(memory_space=pl.ANY),
        scratch_shapes=[pltpu.VMEM((2 * TM * ROW_TILE, LANE), F32),
                        pltpu.VMEM((FFN_BLK * ROW_TILE, LANE), F32),
                        pltpu.SemaphoreType.DMA, pltpu.SemaphoreType.DMA],
    )
    return pl.pallas_call(
        _dispatch_kernel,
        out_shape=jax.ShapeDtypeStruct((N_SLOTS * ROW_TILE, LANE), F32),
        grid_spec=grid_spec,
        compiler_params=_params(("arbitrary",)),
        name="moe_dispatch",
    )(tile_cnt, tile_off, tile_dst, last_blk, lp, h2)


def _ffn_kernel(blk_row_ref, blk_exp_ref, n_act_ref, x_ref, wgu_ref, wd_ref, y_ref, wgu_s, wd_s):
    del blk_row_ref
    i = pl.program_id(0)
    new_expert = jnp.logical_or(i == 0, blk_exp_ref[i] != blk_exp_ref[jnp.maximum(i - 1, 0)])
    _cast_weights(new_expert, [(wgu_ref, wgu_s), (wd_ref, wd_s)])

    @pl.when(i < n_act_ref[0])
    def _():
        x = _load_rows(x_ref, FFN_BLK).astype(BF16)
        gu = jnp.dot(x, wgu_s[...], preferred_element_type=F32)
        gate = gu[:, :D_EXP]
        up = gu[:, D_EXP:]
        act = (gate * _sigmoid(gate) * up).astype(BF16)
        _store_rows(y_ref, jnp.dot(act, wd_s[...], preferred_element_type=F32))


def _expert_ffn(blk_row, blk_exp, n_act, xs, wgu, wd, l):
    grid_spec = pltpu.PrefetchScalarGridSpec(
        num_scalar_prefetch=3,
        grid=(N_FFN_BLOCKS,),
        in_specs=[pl.BlockSpec((FFN_BLK * ROW_TILE, LANE), lambda i, br, be, na: (br[i], 0)),
                  pl.BlockSpec((None, None, D, 2 * D_EXP), lambda i, br, be, na: (l, be[i], 0, 0)),
                  pl.BlockSpec((None, None, D_EXP, D), lambda i, br, be, na: (l, be[i], 0, 0))],
        out_specs=pl.BlockSpec((FFN_BLK * ROW_TILE, LANE), lambda i, br, be, na: (br[i], 0)),
        scratch_shapes=[pltpu.VMEM((D, 2 * D_EXP), BF16), pltpu.VMEM((D_EXP, D), BF16)],
    )
    return pl.pallas_call(
        _ffn_kernel,
        out_shape=jax.ShapeDtypeStruct((N_SLOTS * ROW_TILE, LANE), F32),
        grid_spec=grid_spec,
        compiler_params=_params(("arbitrary",)),
        name="expert_ffn",
    )(blk_row, blk_exp, n_act, xs, wgu, wd)


def _block_plan(counts):
    nblk = (counts + FFN_BLK - 1) // FFN_BLK
    end = jnp.cumsum(nblk)
    start = end - nblk
    n_act = end[-1]
    i = jnp.arange(N_FFN_BLOCKS, dtype=I32)
    i_eff = jnp.minimum(i, n_act - 1)
    e = jnp.minimum(jnp.sum(i_eff[:, None] >= end[None, :], axis=1), N_EXP - 1).astype(I32)
    last_blk = jnp.minimum(start + jnp.maximum(nblk - 1, 0), N_FFN_BLOCKS - 1) * FFN_BLK
    return ((start * FFN_BLK).astype(I32), last_blk.astype(I32), i_eff.astype(I32), e,
            n_act.reshape(1).astype(I32))


def _tile_plan(route, seg_start):
    is_exp = route[:, 0:2, :, None] == jnp.arange(N_EXP, dtype=I32)
    cnt = jnp.sum(is_exp, axis=(1, 2), dtype=I32)
    before = jnp.cumsum(cnt, axis=0) - cnt
    off = jnp.cumsum(cnt, axis=1) - cnt
    dst = seg_start[None, :] + before
    pick = lambda tab: jnp.sum(jnp.where(is_exp, tab[:, None, None, :], 0), axis=-1)
    rank = route[:, 2:4, :]
    local = pick(off - before) + rank
    slots = pick(jnp.broadcast_to(seg_start, cnt.shape)) + rank
    return cnt.reshape(-1), off.reshape(-1), dst.reshape(-1), local, slots


def _combine_kernel(slot_ref, slot_next_ref, y_hbm, rw_ref, x1_ref, mod_ref, modn_ref,
                    lng_ref, lnb_ref, x2_ref, hn_ref, buf, sems):
    m = pl.program_id(0)
    half = m % 2

    def gather_tile(s_ref, dst_half):
        def body(i, carry):
            for u in range(DMA_UNROLL):
                t = i * DMA_UNROLL + u
                for k in range(2):
                    _row_copy(y_hbm, buf.at[dst_half, k], s_ref[k, t], t, sems.at[dst_half]).start()
            return carry

        lax.fori_loop(0, TM // DMA_UNROLL, body, 0)

    @pl.when(m == 0)
    def _():
        gather_tile(slot_ref, 0)

    @pl.when(m + 1 < pl.num_programs(0))
    def _():
        gather_tile(slot_next_ref, 1 - half)

    def drain(i, carry):
        for _ in range(2 * DMA_UNROLL):
            _row_copy(y_hbm, buf.at[half, 0], 0, 0, sems.at[half]).wait()
        return carry

    lax.fori_loop(0, TM // DMA_UNROLL, drain, 0)
    f = rw_ref[:, 0:1] * _load_rows(buf.at[half, 0], TM) + rw_ref[:, 1:2] * _load_rows(buf.at[half, 1], TM)
    x2 = _layer_norm(ALPHA * x1_ref[...] + mod_ref[5:6, :] * f, lng_ref[...], lnb_ref[...])
    x2_ref[...] = x2
    hn_ref[...] = (x2 * (1.0 + modn_ref[1:2, :]) + modn_ref[0:1, :]).astype(BF16)


def _combine(slots, y_slots, rw, x1, mod_l, mod_next, ln_g, ln_b):
    row = lambda m: (m, 0)
    const = lambda m: (0, 0)
    mod_spec = pl.BlockSpec((None, N_MOD, D), lambda m: (_cond_row(m, CTX_TILES, TILES_PER_LAT), 0, 0))
    return pl.pallas_call(
        _combine_kernel,
        out_shape=(jax.ShapeDtypeStruct((T, D), F32), jax.ShapeDtypeStruct((T, D), BF16)),
        grid=(N_TILES,),
        in_specs=[pl.BlockSpec((None, 2, TM), lambda m: (m, 0, 0), memory_space=pltpu.SMEM),
                  pl.BlockSpec((None, 2, TM), lambda m: (jnp.minimum(m + 1, N_TILES - 1), 0, 0),
                               memory_space=pltpu.SMEM),
                  pl.BlockSpec(memory_space=pl.ANY),
                  pl.BlockSpec((TM, LANE), row), pl.BlockSpec((TM, D), row),
                  mod_spec, mod_spec,
                  pl.BlockSpec((1, D), const), pl.BlockSpec((1, D), const)],
        out_specs=(pl.BlockSpec((TM, D), row), pl.BlockSpec((TM, D), row)),
        scratch_shapes=[pltpu.VMEM((2, 2, TM * ROW_TILE, LANE), F32), pltpu.SemaphoreType.DMA((2,))],
        compiler_params=_params(("arbitrary",)),
        name="moe_combine_ln",
    )(slots, slots, y_slots, rw, x1, mod_l, mod_next, ln_g, ln_b)


def kernel(x_prompt, x_sample, c, c_ctx, cache_k, cache_v, w_in, conv_a_w, w_a_out, conv_b_w, conv_b_b,
           norm_b_g, norm_b_b, w_b_out, q_norm_g, k_norm_g, w_c_out, w_o, w_ada, b_ada, ln_g, ln_b,
           w_router, router_bias, w_gate_up, w_down):
    x = jnp.concatenate([x_prompt.reshape(T_CTX, D), x_sample.reshape(T_LAT, D)], axis=0)
    cond16 = jnp.concatenate([c_ctx[None, :], c, jnp.zeros((16 - 1 - N_LAT_SEQ, D), F32)], axis=0)
    mod = _modulation(cond16, w_ada, b_ada).reshape(DEPTH, 16, N_MOD, D)

    wa_b, wb_b, wc_b, wo_b = (w.astype(BF16) for w in (w_a_out, w_b_out, w_c_out, w_o))
    wr = jnp.pad(w_router, ((0, 0), (0, LANE - N_EXP)))
    wr_hi = wr.astype(BF16)
    wr_lo = (wr - wr_hi.astype(F32)).astype(BF16)
    rbias = router_bias.reshape(N_EXP, 1)

    cos_tab, sin_tab = _rope_tables()
    half = np.arange(LANE) // HEAD
    ones_bd = jnp.asarray(half[:, None] == half[None, :], BF16)
    ck = cache_k.reshape(N_LAT_SEQ, DEPTH, PAST, D_KV)
    cv = cache_v.reshape(N_LAT_SEQ, DEPTH, PAST, D_KV)

    h = _modulate(x, mod[0])
    new_k, new_v = [], []
    for l in range(DEPTH):
        ab, acx = _proj_a(h, w_in, l)
        glu = _proj_b(h, w_in, l)
        gates = _proj_gates(h, w_in, l)
        q, katt, vatt, kn, vf = _proj_qkv(h, w_in, jnp.tile(q_norm_g[l], N_Q)[None, :],
                                          jnp.tile(k_norm_g[l], N_KV)[None, :], cos_tab, sin_tab, ones_bd, l)
        new_k.append(kn[:T_CTX].reshape(N_CTX_SEQ, CTX_LEN, N_KV, HEAD))
        new_v.append(vf[:T_CTX].reshape(N_CTX_SEQ, CTX_LEN, N_KV, HEAD))
        ya_pre, yb_pre = _convs(acx, ab, glu, conv_a_w[l], conv_b_w[l], conv_b_b[l][None, :],
                                norm_b_g[l][None, :], norm_b_b[l][None, :])
        o = _attention(q, katt, vatt, ck, cv, l)
        x1, h2, route, rw, counts = _post(ya_pre, yb_pre, o, gates, x, mod[l], wa_b, wb_b, wc_b, wo_b,
                                          ln_g[l, 0][None, :], ln_b[l, 0][None, :], wr_hi, wr_lo, rbias, l)
        cnt = counts[:, 0]
        seg_start, last_blk, blk_row, blk_exp, n_act = _block_plan(cnt)
        tile_cnt, tile_off, tile_dst, local, slots = _tile_plan(route, seg_start)
        xs = _dispatch(tile_cnt, tile_off, tile_dst, last_blk, local, h2)
        y_slots = _expert_ffn(blk_row, blk_exp, n_act, xs, w_gate_up, w_down, l)
        x, h = _combine(slots, y_slots, rw, x1, mod[l], mod[min(l + 1, DEPTH - 1)],
                        ln_g[l, 1][None, :], ln_b[l, 1][None, :])
    y_prompt = x[:T_CTX].reshape(N_CTX_SEQ, CTX_LEN, D)
    y_sample = x[T_CTX:].reshape(N_LAT_SEQ, LAT_LEN, D)
    return y_prompt, y_sample, jnp.stack(new_k, axis=1), jnp.stack(new_v, axis=1)
```

```python
import functools

import numpy as np
import jax
import jax.numpy as jnp
from jax import lax
from jax.experimental import pallas as pl
from jax.experimental.pallas import tpu as pltpu

F32 = jnp.float32
BF16 = jnp.bfloat16
I32 = jnp.int32

D = 1024
DEPTH = 4
N_CTX_SEQ = 16
CTX_LEN = 256
N_LAT_SEQ = 8
LAT_LEN = 1024
PAST = 512
T_CTX = N_CTX_SEQ * CTX_LEN
T_LAT = N_LAT_SEQ * LAT_LEN
T = T_CTX + T_LAT
GRID_W = 64
HEAD = 64
N_Q = 16
N_KV = 4
GROUP = 4
D_KV = N_KV * HEAD
N_EXP = 16
N_GRP = 4
EXP_PER_GRP = 4
D_EXP = 512
IN_COLS = 9728
N_MOD = 6
ALPHA = (2 * DEPTH) ** 0.25
LN_EPS = 1e-5
RMS_EPS = 1e-6
ROPE_THETA = 10000.0

LANE = 128
TM = 512
N_TILES = T // TM
CTX_TILES = T_CTX // TM
TILES_PER_LAT = LAT_LEN // TM
POST_ROWS = 256
TMP = 1024
SEG = 256
N_SEG = T // SEG
CTX_SEGS = T_CTX // SEG
SEGS_PER_LAT = LAT_LEN // SEG
HALO_A = 8
HALO_B = 16
SH_ROWS = SEG + 2 * HALO_B - 8
FFN_BLK = 256
N_FFN_BLOCKS = (2 * T) // FFN_BLK + N_EXP
N_SLOTS = N_FFN_BLOCKS * FFN_BLK
VMEM_LIMIT = 56 * 1024 * 1024

COL_AB, COL_AC, COL_AX, COL_BU, COL_BG, COL_Q, COL_KV, COL_GATES = 0, 2, 4, 6, 8, 10, 12, 13


def _params(sem):
    return pltpu.CompilerParams(dimension_semantics=sem, vmem_limit_bytes=VMEM_LIMIT)


def _cond_row(m, tiles_ctx, tiles_per_lat):
    return jnp.where(m < tiles_ctx, 0, 1 + (m - tiles_ctx) // tiles_per_lat)


def _layer_norm(x, g, b):
    mu = jnp.mean(x, axis=-1, keepdims=True)
    xc = x - mu
    var = jnp.mean(xc * xc, axis=-1, keepdims=True)
    return xc * lax.rsqrt(var + LN_EPS) * g + b


def _sigmoid(x):
    return 1.0 / (1.0 + jnp.exp(-x))


ROW_TILE = D // LANE


def _store_rows(ref, x):
    for j in range(ROW_TILE):
        ref[pl.ds(j, x.shape[0], stride=ROW_TILE), :] = x[:, j * LANE:(j + 1) * LANE]


def _load_rows(ref, n_rows):
    return jnp.concatenate([ref[pl.ds(j, n_rows, stride=ROW_TILE), :] for j in range(ROW_TILE)], axis=1)


def _mod_kernel(cond_ref, w_ref, b_ref, o_ref):
    cnd = cond_ref[...]
    s = (cnd * _sigmoid(cnd)).astype(BF16)
    o_ref[...] = jnp.dot(s, w_ref[...].astype(BF16), preferred_element_type=F32) + b_ref[...]


def _modulation(cond16, w_ada, b_ada):
    n_col = N_MOD * D
    tn = 1024
    return pl.pallas_call(
        _mod_kernel,
        out_shape=jax.ShapeDtypeStruct((DEPTH, 16, n_col), F32),
        grid=(DEPTH, n_col // tn),
        in_specs=[
            pl.BlockSpec((16, D), lambda l, n: (0, 0)),
            pl.BlockSpec((None, D, tn), lambda l, n: (l, 0, n)),
            pl.BlockSpec((None, 1, tn), lambda l, n: (l, 0, n)),
        ],
        out_specs=pl.BlockSpec((None, 16, tn), lambda l, n: (l, 0, n)),
        compiler_params=_params(("parallel", "parallel")),
        name="adaln_mod",
    )(cond16, w_ada, b_ada.reshape(DEPTH, 1, n_col))


def _modulate_kernel(x_ref, mod_ref, h_ref):
    h_ref[...] = (x_ref[...] * (1.0 + mod_ref[1:2, :]) + mod_ref[0:1, :]).astype(BF16)


def _modulate(x, mod_l):
    return pl.pallas_call(
        _modulate_kernel,
        out_shape=jax.ShapeDtypeStruct((T, D), BF16),
        grid=(N_TILES,),
        in_specs=[
            pl.BlockSpec((TM, D), lambda m: (m, 0)),
            pl.BlockSpec((None, N_MOD, D), lambda m: (_cond_row(m, CTX_TILES, TILES_PER_LAT), 0, 0)),
        ],
        out_specs=pl.BlockSpec((TM, D), lambda m: (m, 0)),
        compiler_params=_params(("parallel",)),
        name="modulate",
    )(x, mod_l)


WCOL = 512


def _cast_weights(first, pairs):
    @pl.when(first)
    def _():
        for src, dst in pairs:
            dst[...] = src[...].astype(BF16)


def _proj_a_kernel(h_ref, wb_ref, wc_ref, wx_ref, ab_ref, acx_ref, wb_s, wc_s, wx_s):
    _cast_weights(pl.program_id(1) == 0, [(wb_ref, wb_s), (wc_ref, wc_s), (wx_ref, wx_s)])
    h = h_ref[...]
    ab_ref[...] = jnp.dot(h, wb_s[...], preferred_element_type=F32).astype(BF16)
    acx_ref[...] = (jnp.dot(h, wc_s[...], preferred_element_type=F32)
                    * jnp.dot(h, wx_s[...], preferred_element_type=F32))


def _proj_b_kernel(h_ref, wu_ref, wg_ref, glu_ref, wu_s, wg_s):
    _cast_weights(pl.program_id(1) == 0, [(wu_ref, wu_s), (wg_ref, wg_s)])
    h = h_ref[...]
    glu_ref[...] = (jnp.dot(h, wu_s[...], preferred_element_type=F32)
                    * _sigmoid(jnp.dot(h, wg_s[...], preferred_element_type=F32)))


def _proj_gate_kernel(h_ref, w0_ref, w1_ref, g_ref, w_s):
    _cast_weights(pl.program_id(1) == 0, [(w0_ref, w_s.at[:, 0:WCOL]), (w1_ref, w_s.at[:, WCOL:2 * WCOL])])
    g_ref[...] = _sigmoid(jnp.dot(h_ref[...], w_s[...], preferred_element_type=F32)).astype(BF16)


def _w_spec(l, col0, step=1):
    return pl.BlockSpec((None, D, WCOL), lambda c, m: (l, 0, col0 + step * c))


def _proj_a(h, w_in, l):
    out_spec = pl.BlockSpec((TMP, WCOL), lambda c, m: (m, c))
    return pl.pallas_call(
        _proj_a_kernel,
        out_shape=(jax.ShapeDtypeStruct((T, D), BF16), jax.ShapeDtypeStruct((T, D), F32)),
        grid=(D // WCOL, T // TMP),
        in_specs=[pl.BlockSpec((TMP, D), lambda c, m: (m, 0)),
                  _w_spec(l, COL_AB), _w_spec(l, COL_AC), _w_spec(l, COL_AX)],
        out_specs=(out_spec, out_spec),
        scratch_shapes=[pltpu.VMEM((D, WCOL), BF16)] * 3,
        compiler_params=_params(("parallel", "arbitrary")),
        name="proj_a",
    )(h, w_in, w_in, w_in)


def _proj_b(h, w_in, l):
    return pl.pallas_call(
        _proj_b_kernel,
        out_shape=jax.ShapeDtypeStruct((T, D), F32),
        grid=(D // WCOL, T // TMP),
        in_specs=[pl.BlockSpec((TMP, D), lambda c, m: (m, 0)), _w_spec(l, COL_BU), _w_spec(l, COL_BG)],
        out_specs=pl.BlockSpec((TMP, WCOL), lambda c, m: (m, c)),
        scratch_shapes=[pltpu.VMEM((D, WCOL), BF16)] * 2,
        compiler_params=_params(("parallel", "arbitrary")),
        name="proj_b",
    )(h, w_in, w_in)


def _proj_gates(h, w_in, l):
    tn = 2 * WCOL
    return pl.pallas_call(
        _proj_gate_kernel,
        out_shape=jax.ShapeDtypeStruct((T, 3 * D), BF16),
        grid=(3 * D // tn, T // TMP),
        in_specs=[pl.BlockSpec((TMP, D), lambda c, m: (m, 0)),
                  _w_spec(l, COL_GATES, 2), _w_spec(l, COL_GATES + 1, 2)],
        out_specs=pl.BlockSpec((TMP, tn), lambda c, m: (m, c)),
        scratch_shapes=[pltpu.VMEM((D, tn), BF16)],
        compiler_params=_params(("parallel", "arbitrary")),
        name="proj_gates",
    )(h, w_in, w_in)


def _head_mean_square(x, ones_bd):
    out = []
    for c in range(x.shape[1] // LANE):
        sq = x[:, c * LANE:(c + 1) * LANE]
        out.append(jnp.dot((sq * sq).astype(BF16), ones_bd, preferred_element_type=F32))
    return jnp.concatenate(out, axis=1) * (1.0 / HEAD)


def _rope(x, cos, sin, first_half):
    out = []
    for c in range(x.shape[1] // LANE):
        xc = x[:, c * LANE:(c + 1) * LANE]
        partner = jnp.where(first_half, pltpu.roll(xc, LANE - 16, axis=1), pltpu.roll(xc, 16, axis=1))
        out.append(xc * cos + partner * sin)
    return jnp.concatenate(out, axis=1)


def _qkv_kernel(h_ref, wq_ref, wkv_ref, gq_ref, gk_ref, cos_ref, sin_ref, ones_ref,
                q_ref, katt_ref, vatt_ref, kn_ref, vf_ref, wq_s, wkv_s):
    _cast_weights(pl.program_id(0) == 0, [(wq_ref, wq_s), (wkv_ref, wkv_s)])
    h = h_ref[...]
    ones_bd = ones_ref[...]
    cos = cos_ref[...]
    sin = sin_ref[...]
    lane = lax.broadcasted_iota(I32, (TM, LANE), 1)
    first_half = (lane & 16) == 0
    q = jnp.dot(h, wq_s[...], preferred_element_type=F32)
    qn = q * lax.rsqrt(_head_mean_square(q, ones_bd) + RMS_EPS) * gq_ref[...]
    q_ref[...] = (_rope(qn, cos, sin, first_half) * (HEAD ** -0.5)).astype(BF16)
    kv = jnp.dot(h, wkv_s[...], preferred_element_type=F32)
    k = kv[:, :D_KV]
    v = kv[:, D_KV:]
    kn = k * lax.rsqrt(_head_mean_square(k, ones_bd) + RMS_EPS) * gk_ref[...]
    kn_ref[...] = kn
    katt_ref[...] = _rope(kn, cos, sin, first_half).astype(BF16)
    vf_ref[...] = v
    vatt_ref[...] = v.astype(BF16)


def _proj_qkv(h, w_in, gq, gk, cos_tab, sin_tab, ones_bd, l):
    def tab_idx(m):
        return jnp.where(m < CTX_TILES, TILES_PER_LAT, (m - CTX_TILES) % TILES_PER_LAT)

    row = lambda m: (m, 0)
    return pl.pallas_call(
        _qkv_kernel,
        out_shape=(jax.ShapeDtypeStruct((T, D), BF16),
                   jax.ShapeDtypeStruct((T, D_KV), BF16),
                   jax.ShapeDtypeStruct((T, D_KV), BF16),
                   jax.ShapeDtypeStruct((T, D_KV), F32),
                   jax.ShapeDtypeStruct((T, D_KV), F32)),
        grid=(N_TILES,),
        in_specs=[
            pl.BlockSpec((TM, D), row),
            pl.BlockSpec((None, D, D), lambda m: (l, 0, COL_Q // 2)),
            pl.BlockSpec((None, D, 2 * D_KV), lambda m: (l, 0, COL_KV)),
            pl.BlockSpec((1, D), lambda m: (0, 0)),
            pl.BlockSpec((1, D_KV), lambda m: (0, 0)),
            pl.BlockSpec((TM, LANE), lambda m: (tab_idx(m), 0)),
            pl.BlockSpec((TM, LANE), lambda m: (tab_idx(m), 0)),
            pl.BlockSpec((LANE, LANE), lambda m: (0, 0)),
        ],
        out_specs=(pl.BlockSpec((TM, D), row), pl.BlockSpec((TM, D_KV), row),
                   pl.BlockSpec((TM, D_KV), row), pl.BlockSpec((TM, D_KV), row),
                   pl.BlockSpec((TM, D_KV), row)),
        scratch_shapes=[pltpu.VMEM((D, D), BF16), pltpu.VMEM((D, 2 * D_KV), BF16)],
        compiler_params=_params(("arbitrary",)),
        name="proj_qkv",
    )(h, w_in, w_in, gq, gk, cos_tab, sin_tab, ones_bd)


def _rope_tables():
    lane = np.arange(LANE)
    j = lane % 16
    freqs = jnp.power(ROPE_THETA, -jnp.arange(16, dtype=F32) / 16)[j]
    pos = jnp.arange(LAT_LEN, dtype=I32)
    row = (pos // GRID_W).astype(F32)
    col = (pos % GRID_W).astype(F32)
    use_row = jnp.asarray((lane % HEAD) < HEAD // 2)
    p = jnp.where(use_row[None, :], row[:, None], col[:, None])
    ang = p * freqs[None, :]
    sign = jnp.asarray(np.where((lane & 16) == 0, -1.0, 1.0), F32)
    cos = jnp.concatenate([jnp.cos(ang), jnp.ones((TM, LANE), F32)], axis=0)
    sin = jnp.concatenate([jnp.sin(ang) * sign[None, :], jnp.zeros((TM, LANE), F32)], axis=0)
    return cos, sin


def _conv_kernel(acx_ref, acx_l_ref, acx_r_ref, ab_ref, glu_ref, glu_l_ref, glu_r_ref,
                 wa_ref, wb_ref, bb_ref, ng_ref, nb_ref, ya_ref, yb_ref, pad_a, pad_b, u_ref, sh_ref):
    s = pl.program_id(0)
    lat = s >= CTX_SEGS
    pos = (s - CTX_SEGS) % SEGS_PER_LAT
    has_left = jnp.logical_and(lat, pos != 0)
    has_right = jnp.logical_and(lat, pos != SEGS_PER_LAT - 1)

    pad_a[0:HALO_A, :] = jnp.where(has_left, acx_l_ref[...], 0.0)
    pad_a[HALO_A:HALO_A + SEG, :] = acx_ref[...]
    pad_a[HALO_A + SEG:, :] = jnp.where(has_right, acx_r_ref[...], 0.0)
    pad_b[0:HALO_B, :] = jnp.where(has_left, glu_l_ref[...], 0.0)
    pad_b[HALO_B:HALO_B + SEG, :] = glu_ref[...]
    pad_b[HALO_B + SEG:, :] = jnp.where(has_right, glu_r_ref[...], 0.0)

    conv_a = (wa_ref[0:1, :] * pad_a[HALO_A - 1:HALO_A - 1 + SEG, :]
              + wa_ref[1:2, :] * pad_a[HALO_A:HALO_A + SEG, :]
              + wa_ref[2:3, :] * pad_a[HALO_A + 1:HALO_A + 1 + SEG, :])
    ya_ref[...] = (ab_ref[...].astype(F32) * conv_a).astype(BF16)

    rows = 64
    kb = wb_ref.shape[0]

    def lane_chunk(c, carry):
        lanes = pl.ds(pl.multiple_of(c * LANE, LANE), LANE)
        for b in range(1, 8):
            sh_ref[b, :, :] = pad_b[b:b + SH_ROWS, lanes]
        for r in range(SEG // rows):
            acc = jnp.zeros((rows, LANE), F32)
            for k in range(kb):
                off = HALO_B + k - kb // 2
                row0 = r * rows + 8 * (off // 8)
                if off % 8 == 0:
                    src = pad_b[row0:row0 + rows, lanes]
                else:
                    src = sh_ref[off % 8, row0:row0 + rows, :]
                acc = acc + wb_ref[k:k + 1, lanes] * src
            u_ref[r * rows:(r + 1) * rows, lanes] = acc
        return carry

    lax.fori_loop(0, D // LANE, lane_chunk, 0)
    u = _layer_norm(u_ref[...] + bb_ref[...], ng_ref[...], nb_ref[...])
    yb_ref[...] = (u * _sigmoid(u)).astype(BF16)


def _convs(acx, ab, glu, conv_a_w, conv_b_w, conv_b_b, norm_g, norm_b):
    seg = lambda s: (s, 0)
    const = lambda s: (0, 0)
    ra, rb = SEG // HALO_A, SEG // HALO_B
    left_a = lambda s: (jnp.maximum(s * ra - 1, 0), 0)
    right_a = lambda s: (jnp.minimum((s + 1) * ra, T // HALO_A - 1), 0)
    left_b = lambda s: (jnp.maximum(s * rb - 1, 0), 0)
    right_b = lambda s: (jnp.minimum((s + 1) * rb, T // HALO_B - 1), 0)
    return pl.pallas_call(
        _conv_kernel,
        out_shape=(jax.ShapeDtypeStruct((T, D), BF16), jax.ShapeDtypeStruct((T, D), BF16)),
        grid=(N_SEG,),
        in_specs=[
            pl.BlockSpec((SEG, D), seg), pl.BlockSpec((HALO_A, D), left_a), pl.BlockSpec((HALO_A, D), right_a),
            pl.BlockSpec((SEG, D), seg),
            pl.BlockSpec((SEG, D), seg), pl.BlockSpec((HALO_B, D), left_b), pl.BlockSpec((HALO_B, D), right_b),
            pl.BlockSpec(conv_a_w.shape, const), pl.BlockSpec(conv_b_w.shape, const),
            pl.BlockSpec((1, D), const), pl.BlockSpec((1, D), const), pl.BlockSpec((1, D), const),
        ],
        out_specs=(pl.BlockSpec((SEG, D), seg), pl.BlockSpec((SEG, D), seg)),
        scratch_shapes=[pltpu.VMEM((SEG + 2 * HALO_A, D), F32),
                        pltpu.VMEM((SEG + 2 * HALO_B, D), F32),
                        pltpu.VMEM((SEG, D), F32),
                        pltpu.VMEM((8, SH_ROWS, LANE), F32)],
        compiler_params=_params(("parallel",)),
        name="convs",
    )(acx, acx, acx, ab, glu, glu, glu, conv_a_w, conv_b_w, conv_b_b, norm_g, norm_b)


def _attend(q_ref, key_refs, val_refs, o_ref):
    nt = (((1,), (1,)), ((), ()))
    keys = [r[...].astype(BF16) for r in key_refs]
    vals = [r[...].astype(BF16) for r in val_refs]
    for g in range(N_KV):
        kg = [k[:, g * HEAD:(g + 1) * HEAD] for k in keys]
        vg = [jnp.concatenate([v[:, g * HEAD:(g + 1) * HEAD], jnp.ones((v.shape[0], HEAD), BF16)], axis=1)
              for v in vals]
        for hh in range(GROUP):
            hd = g * GROUP + hh
            qh = q_ref[:, hd * HEAD:(hd + 1) * HEAD]
            s = [lax.dot_general(qh, k, nt, preferred_element_type=F32) for k in kg]
            mx = functools.reduce(jnp.maximum, [jnp.max(x, axis=-1, keepdims=True) for x in s])
            pv = functools.reduce(jnp.add, [jnp.dot(jnp.exp(x - mx).astype(BF16), v, preferred_element_type=F32)
                                            for x, v in zip(s, vg)])
            acc = pv[:, :HEAD]
            den = pv[:, HEAD:HEAD + 1]
            o_ref[:, hd * HEAD:(hd + 1) * HEAD] = (acc / den).astype(BF16)


def _attn_ctx_kernel(q_ref, k_ref, v_ref, o_ref):
    _attend(q_ref, [k_ref], [v_ref], o_ref)


def _attn_lat_kernel(q_ref, k_ref, v_ref, ck_ref, cv_ref, o_in_ref, o_ref):
    del o_in_ref
    _attend(q_ref, [ck_ref, k_ref], [cv_ref, v_ref], o_ref)


def _attention(q, katt, vatt, cache_k, cache_v, l):
    o = pl.pallas_call(
        _attn_ctx_kernel,
        out_shape=jax.ShapeDtypeStruct((T, D), BF16),
        grid=(N_CTX_SEQ,),
        in_specs=[pl.BlockSpec((CTX_LEN, D), lambda b: (b, 0)),
                  pl.BlockSpec((CTX_LEN, D_KV), lambda b: (b, 0)),
                  pl.BlockSpec((CTX_LEN, D_KV), lambda b: (b, 0))],
        out_specs=pl.BlockSpec((CTX_LEN, D), lambda b: (b, 0)),
        compiler_params=_params(("parallel",)),
        name="attn_ctx",
    )(q, katt, vatt)
    seg0 = CTX_SEGS
    lat0 = T_CTX // LAT_LEN
    return pl.pallas_call(
        _attn_lat_kernel,
        out_shape=jax.ShapeDtypeStruct((T, D), BF16),
        grid=(N_LAT_SEQ, SEGS_PER_LAT),
        in_specs=[pl.BlockSpec((SEG, D), lambda b, i: (seg0 + b * SEGS_PER_LAT + i, 0)),
                  pl.BlockSpec((LAT_LEN, D_KV), lambda b, i: (lat0 + b, 0)),
                  pl.BlockSpec((LAT_LEN, D_KV), lambda b, i: (lat0 + b, 0)),
                  pl.BlockSpec((None, None, PAST, D_KV), lambda b, i: (b, l, 0, 0)),
                  pl.BlockSpec((None, None, PAST, D_KV), lambda b, i: (b, l, 0, 0)),
                  pl.BlockSpec(memory_space=pl.ANY)],
        out_specs=pl.BlockSpec((SEG, D), lambda b, i: (seg0 + b * SEGS_PER_LAT + i, 0)),
        input_output_aliases={5: 0},
        compiler_params=_params(("parallel", "parallel")),
        name="attn_lat",
    )(q, katt, vatt, cache_k, cache_v, o)


def _post_kernel(ya_ref, yb_ref, o_ref, g_ref, x_ref, mod_ref, wa_ref, wb_ref, wc_ref, wo_ref,
                 lng_ref, lnb_ref, wr_hi_ref, wr_lo_ref, rb_ref,
                 x1_ref, h2_ref, route_ref, rw_ref, cnt_ref, carry_ref):
    m = pl.program_id(0)

    @pl.when(m == 0)
    def _():
        carry_ref[...] = jnp.zeros_like(carry_ref)

    carry = carry_ref[...]
    r_i = lax.broadcasted_iota(I32, (POST_ROWS, POST_ROWS), 0)
    c_i = lax.broadcasted_iota(I32, (POST_ROWS, POST_ROWS), 1)
    upper = jnp.where(r_i < c_i, 1.0, 0.0).astype(BF16)
    for part in range(TM // POST_ROWS):
        rows = slice(part * POST_ROWS, (part + 1) * POST_ROWS)
        carry = _post_part(rows, carry, upper, ya_ref, yb_ref, o_ref, g_ref, x_ref, mod_ref, wa_ref, wb_ref,
                           wc_ref, wo_ref, lng_ref, lnb_ref, wr_hi_ref, wr_lo_ref, rb_ref,
                           x1_ref, h2_ref, route_ref, rw_ref)
    carry_ref[...] = carry
    cnt_ref[...] = carry.astype(I32)
    route_ref[4:8, :] = jnp.zeros((4, TM), I32)


def _post_part(rows, carry, upper, ya_ref, yb_ref, o_ref, g_ref, x_ref, mod_ref, wa_ref, wb_ref, wc_ref,
               wo_ref, lng_ref, lnb_ref, wr_hi_ref, wr_lo_ref, rb_ref, x1_ref, h2_ref, route_ref, rw_ref):
    n = POST_ROWS
    ya = jnp.dot(ya_ref[rows, :], wa_ref[...], preferred_element_type=F32)
    yb = jnp.dot(yb_ref[rows, :], wb_ref[...], preferred_element_type=F32)
    yc = jnp.dot(o_ref[rows, :], wc_ref[...], preferred_element_type=F32)
    merged = (g_ref[rows, 0:D].astype(F32) * ya + g_ref[rows, D:2 * D].astype(F32) * yb
              + g_ref[rows, 2 * D:3 * D].astype(F32) * yc)
    mix = jnp.dot(merged.astype(BF16), wo_ref[...], preferred_element_type=F32)
    x1 = _layer_norm(ALPHA * x_ref[rows, :] + mod_ref[2:3, :] * mix, lng_ref[...], lnb_ref[...])
    x1_ref[rows, :] = x1
    h2 = x1 * (1.0 + mod_ref[4:5, :]) + mod_ref[3:4, :]

    hi = h2.astype(BF16)
    h2_ref[rows, :] = hi
    lo = (h2 - hi.astype(F32)).astype(BF16)
    wr_hi = wr_hi_ref[...]
    logits = (jnp.dot(hi, wr_hi, preferred_element_type=F32)
              + jnp.dot(lo, wr_hi, preferred_element_type=F32)
              + jnp.dot(hi, wr_lo_ref[...], preferred_element_type=F32))
    scores = _sigmoid(logits.T[0:N_EXP, :])
    sel = scores + rb_ref[...]

    gscore = []
    for g in range(N_GRP):
        r = [sel[g * EXP_PER_GRP + j:g * EXP_PER_GRP + j + 1, :] for j in range(EXP_PER_GRP)]
        pairs = [r[a] + r[b] for a in range(EXP_PER_GRP) for b in range(a + 1, EXP_PER_GRP)]
        gscore.append(functools.reduce(jnp.maximum, pairs))
    best = functools.reduce(jnp.maximum, gscore)
    gsel = jnp.full(best.shape, N_GRP - 1, I32)
    for g in range(N_GRP - 2, -1, -1):
        gsel = jnp.where(gscore[g] == best, g, gsel)

    eidx = lax.broadcasted_iota(I32, (N_EXP, n), 0)
    neg = jnp.float32(-jnp.inf)
    cand = jnp.where((eidx // EXP_PER_GRP) == gsel, sel, neg)
    top1 = jnp.max(cand, axis=0, keepdims=True)
    idx1 = jnp.min(jnp.where(cand == top1, eidx, N_EXP), axis=0, keepdims=True)
    cand2 = jnp.where(eidx == idx1, neg, cand)
    top2 = jnp.max(cand2, axis=0, keepdims=True)
    idx2 = jnp.min(jnp.where(cand2 == top2, eidx, N_EXP), axis=0, keepdims=True)
    is1 = eidx == idx1
    is2 = eidx == idx2
    w1 = jnp.sum(jnp.where(is1, scores, 0.0), axis=0, keepdims=True)
    w2 = jnp.sum(jnp.where(is2, scores, 0.0), axis=0, keepdims=True)
    wsum = w1 + w2
    w1 = w1 / wsum
    w2 = w2 / wsum

    onehot = jnp.where(jnp.logical_or(is1, is2), 1.0, 0.0)
    prefix = jnp.dot(onehot.astype(BF16), upper, preferred_element_type=F32) + carry[:, 0:1]
    rank1 = jnp.sum(jnp.where(is1, prefix, 0.0), axis=0, keepdims=True)
    rank2 = jnp.sum(jnp.where(is2, prefix, 0.0), axis=0, keepdims=True)

    route_ref[0:1, rows] = idx1
    route_ref[1:2, rows] = idx2
    route_ref[2:3, rows] = rank1.astype(I32)
    route_ref[3:4, rows] = rank2.astype(I32)
    wrow = lax.broadcasted_iota(I32, (LANE, n), 0)
    wcols = jnp.where(wrow == 0, w1, jnp.where(wrow == 1, w2, 0.0))
    rw_ref[rows, :] = wcols.T
    return carry + jnp.sum(onehot, axis=1, keepdims=True)


def _post(ya_pre, yb_pre, o, gates, x, mod_l, wa, wb, wc, wo, ln_g, ln_b, wr_hi, wr_lo, rbias, l):
    row = lambda m: (m, 0)
    const = lambda m: (0, 0)
    wspec = pl.BlockSpec((None, D, D), lambda m: (l, 0, 0))
    return pl.pallas_call(
        _post_kernel,
        out_shape=(jax.ShapeDtypeStruct((T, D), F32),
                   jax.ShapeDtypeStruct((T, D), BF16),
                   jax.ShapeDtypeStruct((N_TILES, 8, TM), I32),
                   jax.ShapeDtypeStruct((T, LANE), F32),
                   jax.ShapeDtypeStruct((N_EXP, LANE), I32)),
        grid=(N_TILES,),
        in_specs=[pl.BlockSpec((TM, D), row), pl.BlockSpec((TM, D), row), pl.BlockSpec((TM, D), row),
                  pl.BlockSpec((TM, 3 * D), row), pl.BlockSpec((TM, D), row),
                  pl.BlockSpec((None, N_MOD, D), lambda m: (_cond_row(m, CTX_TILES, TILES_PER_LAT), 0, 0)),
                  wspec, wspec, wspec, wspec,
                  pl.BlockSpec((1, D), const), pl.BlockSpec((1, D), const),
                  pl.BlockSpec((D, LANE), const), pl.BlockSpec((D, LANE), const),
                  pl.BlockSpec((N_EXP, 1), const)],
        out_specs=(pl.BlockSpec((TM, D), row), pl.BlockSpec((TM, D), row),
                   pl.BlockSpec((None, 8, TM), lambda m: (m, 0, 0)),
                   pl.BlockSpec((TM, LANE), row),
                   pl.BlockSpec((N_EXP, LANE), const)),
        scratch_shapes=[pltpu.VMEM((N_EXP, LANE), F32)],
        compiler_params=_params(("arbitrary",)),
        name="merge_ln_router",
    )(ya_pre, yb_pre, o, gates, x, mod_l, wa, wb, wc, wo, ln_g, ln_b, wr_hi, wr_lo, rbias)


def _row_copy(src, dst, src_row, dst_row, sem):
    return pltpu.make_async_copy(src.at[pl.ds(pl.multiple_of(src_row * ROW_TILE, ROW_TILE), ROW_TILE), :],
                                 dst.at[pl.ds(pl.multiple_of(dst_row * ROW_TILE, ROW_TILE), ROW_TILE), :], sem)


DMA_UNROLL = 4


RUN_CHUNK = 256


def _run_copies(src, dst, src_row, dst_row, n, sem, start):
    def piece(offset, size):
        cp = pltpu.make_async_copy(
            src.at[pl.ds(pl.multiple_of((src_row + offset) * ROW_TILE, ROW_TILE), size * ROW_TILE), :],
            dst.at[pl.ds(pl.multiple_of((dst_row + offset) * ROW_TILE, ROW_TILE), size * ROW_TILE), :], sem)
        if start:
            cp.start()
        else:
            cp.wait()

    def whole(j, carry):
        piece(j * RUN_CHUNK, RUN_CHUNK)
        return carry

    lax.fori_loop(0, n // RUN_CHUNK, whole, 0)
    size = RUN_CHUNK // 2
    while size >= 1:
        @pl.when((n & size) != 0)
        def _(size=size):
            piece(n & ~(2 * size - 1), size)
        size //= 2


def _dispatch_kernel(cnt_ref, off_ref, dst_ref, last_ref, lp_ref, h2_ref, xs_hbm, xc_ref, zero_ref, sem, zsem):
    m = pl.program_id(0)

    @pl.when(m == 0)
    def _():
        zero_ref[...] = jnp.zeros_like(zero_ref)

        def pad_copy(e):
            row = pl.multiple_of(last_ref[e] * ROW_TILE, FFN_BLK * ROW_TILE)
            return pltpu.make_async_copy(zero_ref, xs_hbm.at[pl.ds(row, FFN_BLK * ROW_TILE), :], zsem)

        def zero_start(e, carry):
            pad_copy(e).start()
            return carry

        def zero_wait(e, carry):
            pad_copy(e).wait()
            return carry

        lax.fori_loop(0, N_EXP, zero_start, 0)
        lax.fori_loop(0, N_EXP, zero_wait, 0)

    r_i = lax.broadcasted_iota(I32, (2 * TM, TM), 0)
    hit = jnp.logical_or(r_i == lp_ref[0:1, :], r_i == lp_ref[1:2, :])
    perm = jnp.where(hit, 1.0, 0.0).astype(BF16)
    _store_rows(xc_ref, jnp.dot(perm, h2_ref[...], preferred_element_type=F32))

    for start in (True, False):
        for e in range(N_EXP):
            j = m * N_EXP + e
            _run_copies(xc_ref, xs_hbm, off_ref[j], dst_ref[j], cnt_ref[j], sem, start)


def _dispatch(tile_cnt, tile_off, tile_dst, last_blk, lp, h2):
    grid_spec = pltpu.PrefetchScalarGridSpec(
        num_scalar_prefetch=4,
        grid=(N_TILES,),
        in_specs=[pl.BlockSpec((None, 2, TM), lambda m, *_: (m, 0, 0)),
                  pl.BlockSpec((TM, D), lambda m, *_: (m, 0))],
        out_specs=pl.BlockSpec(memory_space=pl.ANY),
        scratch_shapes=[pltpu.VMEM((2 * TM * ROW_TILE, LANE), F32),
                        pltpu.VMEM((FFN_BLK * ROW_TILE, LANE), F32),
                        pltpu.SemaphoreType.DMA, pltpu.SemaphoreType.DMA],
    )
    return pl.pallas_call(
        _dispatch_kernel,
        out_shape=jax.ShapeDtypeStruct((N_SLOTS * ROW_TILE, LANE), F32),
        grid_spec=grid_spec,
        compiler_params=_params(("arbitrary",)),
        name="moe_dispatch",
    )(tile_cnt, tile_off, tile_dst, last_blk, lp, h2)


def _ffn_kernel(blk_row_ref, blk_exp_ref, n_act_ref, x_ref, wgu_ref, wd_ref, y_ref, wgu_s, wd_s):
    del blk_row_ref
    i = pl.program_id(0)
    new_expert = jnp.logical_or(i == 0, blk_exp_ref[i] != blk_exp_ref[jnp.maximum(i - 1, 0)])
    _cast_weights(new_expert, [(wgu_ref, wgu_s), (wd_ref, wd_s)])

    @pl.when(i < n_act_ref[0])
    def _():
        x = _load_rows(x_ref, FFN_BLK).astype(BF16)
        gu = jnp.dot(x, wgu_s[...], preferred_element_type=F32)
        gate = gu[:, :D_EXP]
        up = gu[:, D_EXP:]
        act = (gate * _sigmoid(gate) * up).astype(BF16)
        _store_rows(y_ref, jnp.dot(act, wd_s[...], preferred_element_type=F32))


def _expert_ffn(blk_row, blk_exp, n_act, xs, wgu, wd, l):
    grid_spec = pltpu.PrefetchScalarGridSpec(
        num_scalar_prefetch=3,
        grid=(N_FFN_BLOCKS,),
        in_specs=[pl.BlockSpec((FFN_BLK * ROW_TILE, LANE), lambda i, br, be, na: (br[i], 0)),
                  pl.BlockSpec((None, None, D, 2 * D_EXP), lambda i, br, be, na: (l, be[i], 0, 0)),
                  pl.BlockSpec((None, None, D_EXP, D), lambda i, br, be, na: (l, be[i], 0, 0))],
        out_specs=pl.BlockSpec((FFN_BLK * ROW_TILE, LANE), lambda i, br, be, na: (br[i], 0)),
        scratch_shapes=[pltpu.VMEM((D, 2 * D_EXP), BF16), pltpu.VMEM((D_EXP, D), BF16)],
    )
    return pl.pallas_call(
        _ffn_kernel,
        out_shape=jax.ShapeDtypeStruct((N_SLOTS * ROW_TILE, LANE), F32),
        grid_spec=grid_spec,
        compiler_params=_params(("arbitrary",)),
        name="expert_ffn",
    )(blk_row, blk_exp, n_act, xs, wgu, wd)


def _block_plan(counts):
    nblk = (counts + FFN_BLK - 1) // FFN_BLK
    end = jnp.cumsum(nblk)
    start = end - nblk
    n_act = end[-1]
    i = jnp.arange(N_FFN_BLOCKS, dtype=I32)
    i_eff = jnp.minimum(i, n_act - 1)
    e = jnp.minimum(jnp.sum(i_eff[:, None] >= end[None, :], axis=1), N_EXP - 1).astype(I32)
    last_blk = jnp.minimum(start + jnp.maximum(nblk - 1, 0), N_FFN_BLOCKS - 1) * FFN_BLK
    return ((start * FFN_BLK).astype(I32), last_blk.astype(I32), i_eff.astype(I32), e,
            n_act.reshape(1).astype(I32))


def _tile_plan(route, seg_start):
    is_exp = route[:, 0:2, :, None] == jnp.arange(N_EXP, dtype=I32)
    cnt = jnp.sum(is_exp, axis=(1, 2), dtype=I32)
    before = jnp.cumsum(cnt, axis=0) - cnt
    off = jnp.cumsum(cnt, axis=1) - cnt
    dst = seg_start[None, :] + before
    pick = lambda tab: jnp.sum(jnp.where(is_exp, tab[:, None, None, :], 0), axis=-1)
    rank = route[:, 2:4, :]
    local = pick(off - before) + rank
    slots = pick(jnp.broadcast_to(seg_start, cnt.shape)) + rank
    return cnt.reshape(-1), off.reshape(-1), dst.reshape(-1), local, slots


def _combine_kernel(slot_ref, slot_next_ref, y_hbm, rw_ref, x1_ref, mod_ref, modn_ref,
                    lng_ref, lnb_ref, x2_ref, hn_ref, buf, sems):
    m = pl.program_id(0)
    half = m % 2

    def gather_tile(s_ref, dst_half):
        def body(i, carry):
            for u in range(DMA_UNROLL):
                t = i * DMA_UNROLL + u
                for k in range(2):
                    _row_copy(y_hbm, buf.at[dst_half, k], s_ref[k, t], t, sems.at[dst_half]).start()
            return carry

        lax.fori_loop(0, TM // DMA_UNROLL, body, 0)

    @pl.when(m == 0)
    def _():
        gather_tile(slot_ref, 0)

    @pl.when(m + 1 < pl.num_programs(0))
    def _():
        gather_tile(slot_next_ref, 1 - half)

    def drain(i, carry):
        for _ in range(2 * DMA_UNROLL):
            _row_copy(y_hbm, buf.at[half, 0], 0, 0, sems.at[half]).wait()
        return carry

    lax.fori_loop(0, TM // DMA_UNROLL, drain, 0)
    f = rw_ref[:, 0:1] * _load_rows(buf.at[half, 0], TM) + rw_ref[:, 1:2] * _load_rows(buf.at[half, 1], TM)
    x2 = _layer_norm(ALPHA * x1_ref[...] + mod_ref[5:6, :] * f, lng_ref[...], lnb_ref[...])
    x2_ref[...] = x2
    hn_ref[...] = (x2 * (1.0 + modn_ref[1:2, :]) + modn_ref[0:1, :]).astype(BF16)


def _combine(slots, y_slots, rw, x1, mod_l, mod_next, ln_g, ln_b):
    row = lambda m: (m, 0)
    const = lambda m: (0, 0)
    mod_spec = pl.BlockSpec((None, N_MOD, D), lambda m: (_cond_row(m, CTX_TILES, TILES_PER_LAT), 0, 0))
    return pl.pallas_call(
        _combine_kernel,
        out_shape=(jax.ShapeDtypeStruct((T, D), F32), jax.ShapeDtypeStruct((T, D), BF16)),
        grid=(N_TILES,),
        in_specs=[pl.BlockSpec((None, 2, TM), lambda m: (m, 0, 0), memory_space=pltpu.SMEM),
                  pl.BlockSpec((None, 2, TM), lambda m: (jnp.minimum(m + 1, N_TILES - 1), 0, 0),
                               memory_space=pltpu.SMEM),
                  pl.BlockSpec(memory_space=pl.ANY),
                  pl.BlockSpec((TM, LANE), row), pl.BlockSpec((TM, D), row),
                  mod_spec, mod_spec,
                  pl.BlockSpec((1, D), const), pl.BlockSpec((1, D), const)],
        out_specs=(pl.BlockSpec((TM, D), row), pl.BlockSpec((TM, D), row)),
        scratch_shapes=[pltpu.VMEM((2, 2, TM * ROW_TILE, LANE), F32), pltpu.SemaphoreType.DMA((2,))],
        compiler_params=_params(("arbitrary",)),
        name="moe_combine_ln",
    )(slots, slots, y_slots, rw, x1, mod_l, mod_next, ln_g, ln_b)


def kernel(x_prompt, x_sample, c, c_ctx, cache_k, cache_v, w_in, conv_a_w, w_a_out, conv_b_w, conv_b_b,
           norm_b_g, norm_b_b, w_b_out, q_norm_g, k_norm_g, w_c_out, w_o, w_ada, b_ada, ln_g, ln_b,
           w_router, router_bias, w_gate_up, w_down):
    x = jnp.concatenate([x_prompt.reshape(T_CTX, D), x_sample.reshape(T_LAT, D)], axis=0)
    cond16 = jnp.concatenate([c_ctx[None, :], c, jnp.zeros((16 - 1 - N_LAT_SEQ, D), F32)], axis=0)
    mod = _modulation(cond16, w_ada, b_ada).reshape(DEPTH, 16, N_MOD, D)

    wa_b, wb_b, wc_b, wo_b = (w.astype(BF16) for w in (w_a_out, w_b_out, w_c_out, w_o))
    wr = jnp.pad(w_router, ((0, 0), (0, LANE - N_EXP)))
    wr_hi = wr.astype(BF16)
    wr_lo = (wr - wr_hi.astype(F32)).astype(BF16)
    rbias = router_bias.reshape(N_EXP, 1)

    cos_tab, sin_tab = _rope_tables()
    half = np.arange(LANE) // HEAD
    ones_bd = jnp.asarray(half[:, None] == half[None, :], BF16)
    ck = cache_k.reshape(N_LAT_SEQ, DEPTH, PAST, D_KV)
    cv = cache_v.reshape(N_LAT_SEQ, DEPTH, PAST, D_KV)

    h = _modulate(x, mod[0])
    new_k, new_v = [], []
    for l in range(DEPTH):
        ab, acx = _proj_a(h, w_in, l)
        glu = _proj_b(h, w_in, l)
        gates = _proj_gates(h, w_in, l)
        q, katt, vatt, kn, vf = _proj_qkv(h, w_in, jnp.tile(q_norm_g[l], N_Q)[None, :],
                                          jnp.tile(k_norm_g[l], N_KV)[None, :], cos_tab, sin_tab, ones_bd, l)
        new_k.append(kn[:T_CTX].reshape(N_CTX_SEQ, CTX_LEN, N_KV, HEAD))
        new_v.append(vf[:T_CTX].reshape(N_CTX_SEQ, CTX_LEN, N_KV, HEAD))
        ya_pre, yb_pre = _convs(acx, ab, glu, conv_a_w[l], conv_b_w[l], conv_b_b[l][None, :],
                                norm_b_g[l][None, :], norm_b_b[l][None, :])
        o = _attention(q, katt, vatt, ck, cv, l)
        x1, h2, route, rw, counts = _post(ya_pre, yb_pre, o, gates, x, mod[l], wa_b, wb_b, wc_b, wo_b,
                                          ln_g[l, 0][None, :], ln_b[l, 0][None, :], wr_hi, wr_lo, rbias, l)
        cnt = counts[:, 0]
        seg_start, last_blk, blk_row, blk_exp, n_act = _block_plan(cnt)
        tile_cnt, tile_off, tile_dst, local, slots = _tile_plan(route, seg_start)
        xs = _dispatch(tile_cnt, tile_off, tile_dst, last_blk, local, h2)
        y_slots = _expert_ffn(blk_row, blk_exp, n_act, xs, w_gate_up, w_down, l)
        x, h = _combine(slots, y_slots, rw, x1, mod[l], mod[min(l + 1, DEPTH - 1)],
                        ln_g[l, 1][None, :], ln_b[l, 1][None, :])
    y_prompt = x[:T_CTX].reshape(N_CTX_SEQ, CTX_LEN, D)
    y_sample = x[T_CTX:].reshape(N_LAT_SEQ, LAT_LEN, D)
    return y_prompt, y_sample, jnp.stack(new_k, axis=1), jnp.stack(new_v, axis=1)
```

```python
import functools

import numpy as np
import jax
import jax.numpy as jnp
from jax import lax
from jax.experimental import pallas as pl
from jax.experimental.pallas import tpu as pltpu

F32 = jnp.float32
BF16 = jnp.bfloat16
I32 = jnp.int32

D = 1024
DEPTH = 4
N_CTX_SEQ = 16
CTX_LEN = 256
N_LAT_SEQ = 8
LAT_LEN = 1024
PAST = 512
T_CTX = N_CTX_SEQ * CTX_LEN
T_LAT = N_LAT_SEQ * LAT_LEN
T = T_CTX + T_LAT
GRID_W = 64
HEAD = 64
N_Q = 16
N_KV = 4
GROUP = 4
D_KV = N_KV * HEAD
N_EXP = 16
N_GRP = 4
EXP_PER_GRP = 4
D_EXP = 512
IN_COLS = 9728
N_MOD = 6
ALPHA = (2 * DEPTH) ** 0.25
LN_EPS = 1e-5
RMS_EPS = 1e-6
ROPE_THETA = 10000.0

LANE = 128
TM = 512
N_TILES = T // TM
CTX_TILES = T_CTX // TM
TILES_PER_LAT = LAT_LEN // TM
POST_ROWS = 256
TMP = 1024
SEG = 256
N_SEG = T // SEG
CTX_SEGS = T_CTX // SEG
SEGS_PER_LAT = LAT_LEN // SEG
HALO_A = 8
HALO_B = 16
SH_ROWS = SEG + 2 * HALO_B - 8
FFN_BLK = 256
N_FFN_BLOCKS = (2 * T) // FFN_BLK + N_EXP
N_SLOTS = N_FFN_BLOCKS * FFN_BLK
VMEM_LIMIT = 56 * 1024 * 1024

COL_AB, COL_AC, COL_AX, COL_BU, COL_BG, COL_Q, COL_KV, COL_GATES = 0, 2, 4, 6, 8, 10, 12, 13


def _params(sem):
    return pltpu.CompilerParams(dimension_semantics=sem, vmem_limit_bytes=VMEM_LIMIT)


def _cond_row(m, tiles_ctx, tiles_per_lat):
    return jnp.where(m < tiles_ctx, 0, 1 + (m - tiles_ctx) // tiles_per_lat)


def _layer_norm(x, g, b):
    mu = jnp.mean(x, axis=-1, keepdims=True)
    xc = x - mu
    var = jnp.mean(xc * xc, axis=-1, keepdims=True)
    return xc * lax.rsqrt(var + LN_EPS) * g + b


def _sigmoid(x):
    return 1.0 / (1.0 + jnp.exp(-x))


ROW_TILE = D // LANE


def _store_rows(ref, x):
    for j in range(ROW_TILE):
        ref[pl.ds(j, x.shape[0], stride=ROW_TILE), :] = x[:, j * LANE:(j + 1) * LANE]


def _load_rows(ref, n_rows):
    return jnp.concatenate([ref[pl.ds(j, n_rows, stride=ROW_TILE), :] for j in range(ROW_TILE)], axis=1)


def _mod_kernel(cond_ref, w_ref, b_ref, o_ref):
    cnd = cond_ref[...]
    s = (cnd * _sigmoid(cnd)).astype(BF16)
    o_ref[...] = jnp.dot(s, w_ref[...].astype(BF16), preferred_element_type=F32) + b_ref[...]


def _modulation(cond16, w_ada, b_ada):
    n_col = N_MOD * D
    tn = 1024
    return pl.pallas_call(
        _mod_kernel,
        out_shape=jax.ShapeDtypeStruct((DEPTH, 16, n_col), F32),
        grid=(DEPTH, n_col // tn),
        in_specs=[
            pl.BlockSpec((16, D), lambda l, n: (0, 0)),
            pl.BlockSpec((None, D, tn), lambda l, n: (l, 0, n)),
            pl.BlockSpec((None, 1, tn), lambda l, n: (l, 0, n)),
        ],
        out_specs=pl.BlockSpec((None, 16, tn), lambda l, n: (l, 0, n)),
        compiler_params=_params(("parallel", "parallel")),
        name="adaln_mod",
    )(cond16, w_ada, b_ada.reshape(DEPTH, 1, n_col))


def _modulate_kernel(x_ref, mod_ref, h_ref):
    h_ref[...] = (x_ref[...] * (1.0 + mod_ref[1:2, :]) + mod_ref[0:1, :]).astype(BF16)


def _modulate(x, mod_l):
    return pl.pallas_call(
        _modulate_kernel,
        out_shape=jax.ShapeDtypeStruct((T, D), BF16),
        grid=(N_TILES,),
        in_specs=[
            pl.BlockSpec((TM, D), lambda m: (m, 0)),
            pl.BlockSpec((None, N_MOD, D), lambda m: (_cond_row(m, CTX_TILES, TILES_PER_LAT), 0, 0)),
        ],
        out_specs=pl.BlockSpec((TM, D), lambda m: (m, 0)),
        compiler_params=_params(("parallel",)),
        name="modulate",
    )(x, mod_l)


WCOL = 512


def _cast_weights(first, pairs):
    @pl.when(first)
    def _():
        for src, dst in pairs:
            dst[...] = src[...].astype(BF16)


def _proj_a_kernel(h_ref, wb_ref, wc_ref, wx_ref, ab_ref, acx_ref, wb_s, wc_s, wx_s):
    _cast_weights(pl.program_id(1) == 0, [(wb_ref, wb_s), (wc_ref, wc_s), (wx_ref, wx_s)])
    h = h_ref[...]
    ab_ref[...] = jnp.dot(h, wb_s[...], preferred_element_type=F32).astype(BF16)
    acx_ref[...] = (jnp.dot(h, wc_s[...], preferred_element_type=F32)
                    * jnp.dot(h, wx_s[...], preferred_element_type=F32))


def _proj_b_kernel(h_ref, wu_ref, wg_ref, glu_ref, wu_s, wg_s):
    _cast_weights(pl.program_id(1) == 0, [(wu_ref, wu_s), (wg_ref, wg_s)])
    h = h_ref[...]
    glu_ref[...] = (jnp.dot(h, wu_s[...], preferred_element_type=F32)
                    * _sigmoid(jnp.dot(h, wg_s[...], preferred_element_type=F32)))


def _proj_gate_kernel(h_ref, w0_ref, w1_ref, g_ref, w_s):
    _cast_weights(pl.program_id(1) == 0, [(w0_ref, w_s.at[:, 0:WCOL]), (w1_ref, w_s.at[:, WCOL:2 * WCOL])])
    g_ref[...] = _sigmoid(jnp.dot(h_ref[...], w_s[...], preferred_element_type=F32)).astype(BF16)


def _w_spec(l, col0, step=1):
    return pl.BlockSpec((None, D, WCOL), lambda c, m: (l, 0, col0 + step * c))


def _proj_a(h, w_in, l):
    out_spec = pl.BlockSpec((TMP, WCOL), lambda c, m: (m, c))
    return pl.pallas_call(
        _proj_a_kernel,
        out_shape=(jax.ShapeDtypeStruct((T, D), BF16), jax.ShapeDtypeStruct((T, D), F32)),
        grid=(D // WCOL, T // TMP),
        in_specs=[pl.BlockSpec((TMP, D), lambda c, m: (m, 0)),
                  _w_spec(l, COL_AB), _w_spec(l, COL_AC), _w_spec(l, COL_AX)],
        out_specs=(out_spec, out_spec),
        scratch_shapes=[pltpu.VMEM((D, WCOL), BF16)] * 3,
        compiler_params=_params(("parallel", "arbitrary")),
        name="proj_a",
    )(h, w_in, w_in, w_in)


def _proj_b(h, w_in, l):
    return pl.pallas_call(
        _proj_b_kernel,
        out_shape=jax.ShapeDtypeStruct((T, D), F32),
        grid=(D // WCOL, T // TMP),
        in_specs=[pl.BlockSpec((TMP, D), lambda c, m: (m, 0)), _w_spec(l, COL_BU), _w_spec(l, COL_BG)],
        out_specs=pl.BlockSpec((TMP, WCOL), lambda c, m: (m, c)),
        scratch_shapes=[pltpu.VMEM((D, WCOL), BF16)] * 2,
        compiler_params=_params(("parallel", "arbitrary")),
        name="proj_b",
    )(h, w_in, w_in)


def _proj_gates(h, w_in, l):
    tn = 2 * WCOL
    return pl.pallas_call(
        _proj_gate_kernel,
        out_shape=jax.ShapeDtypeStruct((T, 3 * D), BF16),
        grid=(3 * D // tn, T // TMP),
        in_specs=[pl.BlockSpec((TMP, D), lambda c, m: (m, 0)),
                  _w_spec(l, COL_GATES, 2), _w_spec(l, COL_GATES + 1, 2)],
        out_specs=pl.BlockSpec((TMP, tn), lambda c, m: (m, c)),
        scratch_shapes=[pltpu.VMEM((D, tn), BF16)],
        compiler_params=_params(("parallel", "arbitrary")),
        name="proj_gates",
    )(h, w_in, w_in)


def _head_mean_square(x, ones_bd):
    out = []
    for c in range(x.shape[1] // LANE):
        sq = x[:, c * LANE:(c + 1) * LANE]
        out.append(jnp.dot((sq * sq).astype(BF16), ones_bd, preferred_element_type=F32))
    return jnp.concatenate(out, axis=1) * (1.0 / HEAD)


def _rope(x, cos, sin, first_half):
    out = []
    for c in range(x.shape[1] // LANE):
        xc = x[:, c * LANE:(c + 1) * LANE]
        partner = jnp.where(first_half, pltpu.roll(xc, LANE - 16, axis=1), pltpu.roll(xc, 16, axis=1))
        out.append(xc * cos + partner * sin)
    return jnp.concatenate(out, axis=1)


def _qkv_kernel(h_ref, wq_ref, wkv_ref, gq_ref, gk_ref, cos_ref, sin_ref, ones_ref,
                q_ref, katt_ref, vatt_ref, kn_ref, vf_ref, wq_s, wkv_s):
    _cast_weights(pl.program_id(0) == 0, [(wq_ref, wq_s), (wkv_ref, wkv_s)])
    h = h_ref[...]
    ones_bd = ones_ref[...]
    cos = cos_ref[...]
    sin = sin_ref[...]
    lane = lax.broadcasted_iota(I32, (TM, LANE), 1)
    first_half = (lane & 16) == 0
    q = jnp.dot(h, wq_s[...], preferred_element_type=F32)
    qn = q * lax.rsqrt(_head_mean_square(q, ones_bd) + RMS_EPS) * gq_ref[...]
    q_ref[...] = (_rope(qn, cos, sin, first_half) * (HEAD ** -0.5)).astype(BF16)
    kv = jnp.dot(h, wkv_s[...], preferred_element_type=F32)
    k = kv[:, :D_KV]
    v = kv[:, D_KV:]
    kn = k * lax.rsqrt(_head_mean_square(k, ones_bd) + RMS_EPS) * gk_ref[...]
    kn_ref[...] = kn
    katt_ref[...] = _rope(kn, cos, sin, first_half).astype(BF16)
    vf_ref[...] = v
    vatt_ref[...] = v.astype(BF16)


def _proj_qkv(h, w_in, gq, gk, cos_tab, sin_tab, ones_bd, l):
    def tab_idx(m):
        return jnp.where(m < CTX_TILES, TILES_PER_LAT, (m - CTX_TILES) % TILES_PER_LAT)

    row = lambda m: (m, 0)
    return pl.pallas_call(
        _qkv_kernel,
        out_shape=(jax.ShapeDtypeStruct((T, D), BF16),
                   jax.ShapeDtypeStruct((T, D_KV), BF16),
                   jax.ShapeDtypeStruct((T, D_KV), BF16),
                   jax.ShapeDtypeStruct((T, D_KV), F32),
                   jax.ShapeDtypeStruct((T, D_KV), F32)),
        grid=(N_TILES,),
        in_specs=[
            pl.BlockSpec((TM, D), row),
            pl.BlockSpec((None, D, D), lambda m: (l, 0, COL_Q // 2)),
            pl.BlockSpec((None, D, 2 * D_KV), lambda m: (l, 0, COL_KV)),
            pl.BlockSpec((1, D), lambda m: (0, 0)),
            pl.BlockSpec((1, D_KV), lambda m: (0, 0)),
            pl.BlockSpec((TM, LANE), lambda m: (tab_idx(m), 0)),
            pl.BlockSpec((TM, LANE), lambda m: (tab_idx(m), 0)),
            pl.BlockSpec((LANE, LANE), lambda m: (0, 0)),
        ],
        out_specs=(pl.BlockSpec((TM, D), row), pl.BlockSpec((TM, D_KV), row),
                   pl.BlockSpec((TM, D_KV), row), pl.BlockSpec((TM, D_KV), row),
                   pl.BlockSpec((TM, D_KV), row)),
        scratch_shapes=[pltpu.VMEM((D, D), BF16), pltpu.VMEM((D, 2 * D_KV), BF16)],
        compiler_params=_params(("arbitrary",)),
        name="proj_qkv",
    )(h, w_in, w_in, gq, gk, cos_tab, sin_tab, ones_bd)


def _rope_tables():
    lane = np.arange(LANE)
    j = lane % 16
    freqs = jnp.power(ROPE_THETA, -jnp.arange(16, dtype=F32) / 16)[j]
    pos = jnp.arange(LAT_LEN, dtype=I32)
    row = (pos // GRID_W).astype(F32)
    col = (pos % GRID_W).astype(F32)
    use_row = jnp.asarray((lane % HEAD) < HEAD // 2)
    p = jnp.where(use_row[None, :], row[:, None], col[:, None])
    ang = p * freqs[None, :]
    sign = jnp.asarray(np.where((lane & 16) == 0, -1.0, 1.0), F32)
    cos = jnp.concatenate([jnp.cos(ang), jnp.ones((TM, LANE), F32)], axis=0)
    sin = jnp.concatenate([jnp.sin(ang) * sign[None, :], jnp.zeros((TM, LANE), F32)], axis=0)
    return cos, sin


def _conv_kernel(acx_ref, acx_l_ref, acx_r_ref, ab_ref, glu_ref, glu_l_ref, glu_r_ref,
                 wa_ref, wb_ref, bb_ref, ng_ref, nb_ref, ya_ref, yb_ref, pad_a, pad_b, u_ref, sh_ref):
    s = pl.program_id(0)
    lat = s >= CTX_SEGS
    pos = (s - CTX_SEGS) % SEGS_PER_LAT
    has_left = jnp.logical_and(lat, pos != 0)
    has_right = jnp.logical_and(lat, pos != SEGS_PER_LAT - 1)

    pad_a[0:HALO_A, :] = jnp.where(has_left, acx_l_ref[...], 0.0)
    pad_a[HALO_A:HALO_A + SEG, :] = acx_ref[...]
    pad_a[HALO_A + SEG:, :] = jnp.where(has_right, acx_r_ref[...], 0.0)
    pad_b[0:HALO_B, :] = jnp.where(has_left, glu_l_ref[...], 0.0)
    pad_b[HALO_B:HALO_B + SEG, :] = glu_ref[...]
    pad_b[HALO_B + SEG:, :] = jnp.where(has_right, glu_r_ref[...], 0.0)

    conv_a = (wa_ref[0:1, :] * pad_a[HALO_A - 1:HALO_A - 1 + SEG, :]
              + wa_ref[1:2, :] * pad_a[HALO_A:HALO_A + SEG, :]
              + wa_ref[2:3, :] * pad_a[HALO_A + 1:HALO_A + 1 + SEG, :])
    ya_ref[...] = (ab_ref[...].astype(F32) * conv_a).astype(BF16)

    rows = 64
    kb = wb_ref.shape[0]

    def lane_chunk(c, carry):
        lanes = pl.ds(pl.multiple_of(c * LANE, LANE), LANE)
        for b in range(1, 8):
            sh_ref[b, :, :] = pad_b[b:b + SH_ROWS, lanes]
        for r in range(SEG // rows):
            acc = jnp.zeros((rows, LANE), F32)
            for k in range(kb):
                off = HALO_B + k - kb // 2
                row0 = r * rows + 8 * (off // 8)
                if off % 8 == 0:
                    src = pad_b[row0:row0 + rows, lanes]
                else:
                    src = sh_ref[off % 8, row0:row0 + rows, :]
                acc = acc + wb_ref[k:k + 1, lanes] * src
            u_ref[r * rows:(r + 1) * rows, lanes] = acc
        return carry

    lax.fori_loop(0, D // LANE, lane_chunk, 0)
    u = _layer_norm(u_ref[...] + bb_ref[...], ng_ref[...], nb_ref[...])
    yb_ref[...] = (u * _sigmoid(u)).astype(BF16)


def _convs(acx, ab, glu, conv_a_w, conv_b_w, conv_b_b, norm_g, norm_b):
    seg = lambda s: (s, 0)
    const = lambda s: (0, 0)
    ra, rb = SEG // HALO_A, SEG // HALO_B
    left_a = lambda s: (jnp.maximum(s * ra - 1, 0), 0)
    right_a = lambda s: (jnp.minimum((s + 1) * ra, T // HALO_A - 1), 0)
    left_b = lambda s: (jnp.maximum(s * rb - 1, 0), 0)
    right_b = lambda s: (jnp.minimum((s + 1) * rb, T // HALO_B - 1), 0)
    return pl.pallas_call(
        _conv_kernel,
        out_shape=(jax.ShapeDtypeStruct((T, D), BF16), jax.ShapeDtypeStruct((T, D), BF16)),
        grid=(N_SEG,),
        in_specs=[
            pl.BlockSpec((SEG, D), seg), pl.BlockSpec((HALO_A, D), left_a), pl.BlockSpec((HALO_A, D), right_a),
            pl.BlockSpec((SEG, D), seg),
            pl.BlockSpec((SEG, D), seg), pl.BlockSpec((HALO_B, D), left_b), pl.BlockSpec((HALO_B, D), right_b),
            pl.BlockSpec(conv_a_w.shape, const), pl.BlockSpec(conv_b_w.shape, const),
            pl.BlockSpec((1, D), const), pl.BlockSpec((1, D), const), pl.BlockSpec((1, D), const),
        ],
        out_specs=(pl.BlockSpec((SEG, D), seg), pl.BlockSpec((SEG, D), seg)),
        scratch_shapes=[pltpu.VMEM((SEG + 2 * HALO_A, D), F32),
                        pltpu.VMEM((SEG + 2 * HALO_B, D), F32),
                        pltpu.VMEM((SEG, D), F32),
                        pltpu.VMEM((8, SH_ROWS, LANE), F32)],
        compiler_params=_params(("parallel",)),
        name="convs",
    )(acx, acx, acx, ab, glu, glu, glu, conv_a_w, conv_b_w, conv_b_b, norm_g, norm_b)


def _attend(q_ref, key_refs, val_refs, o_ref, ones_denominator):
    nt = (((1,), (1,)), ((), ()))
    keys = [r[...].astype(BF16) for r in key_refs]
    vals = [r[...].astype(BF16) for r in val_refs]
    n_q = q_ref.shape[0]
    for g in range(N_KV):
        kg = [k[:, g * HEAD:(g + 1) * HEAD] for k in keys]
        vg = [v[:, g * HEAD:(g + 1) * HEAD] for v in vals]
        if ones_denominator:
            vg = [jnp.concatenate([v, jnp.ones_like(v)], axis=1) for v in vg]
        qg = jnp.concatenate([q_ref[:, (g * GROUP + hh) * HEAD:(g * GROUP + hh + 1) * HEAD]
                              for hh in range(GROUP)], axis=0)
        s = [lax.dot_general(qg, k, nt, preferred_element_type=F32) for k in kg]
        mx = functools.reduce(jnp.maximum, [jnp.max(x, axis=-1, keepdims=True) for x in s])
        p = [jnp.exp(x - mx) for x in s]
        acc = functools.reduce(jnp.add, [jnp.dot(x.astype(BF16), v, preferred_element_type=F32)
                                         for x, v in zip(p, vg)])
        if ones_denominator:
            den = acc[:, HEAD:HEAD + 1]
            acc = acc[:, :HEAD]
        else:
            den = functools.reduce(jnp.add, [jnp.sum(x, axis=-1, keepdims=True) for x in p])
        out = (acc / den).astype(BF16)
        for hh in range(GROUP):
            hd = g * GROUP + hh
            o_ref[:, hd * HEAD:(hd + 1) * HEAD] = out[hh * n_q:(hh + 1) * n_q, :]


def _attn_ctx_kernel(q_ref, k_ref, v_ref, o_ref):
    _attend(q_ref, [k_ref], [v_ref], o_ref, ones_denominator=False)


def _attn_lat_kernel(q_ref, k_ref, v_ref, ck_ref, cv_ref, o_in_ref, o_ref):
    del o_in_ref
    _attend(q_ref, [ck_ref, k_ref], [cv_ref, v_ref], o_ref, ones_denominator=True)


def _attention(q, katt, vatt, cache_k, cache_v, l):
    o = pl.pallas_call(
        _attn_ctx_kernel,
        out_shape=jax.ShapeDtypeStruct((T, D), BF16),
        grid=(N_CTX_SEQ,),
        in_specs=[pl.BlockSpec((CTX_LEN, D), lambda b: (b, 0)),
                  pl.BlockSpec((CTX_LEN, D_KV), lambda b: (b, 0)),
                  pl.BlockSpec((CTX_LEN, D_KV), lambda b: (b, 0))],
        out_specs=pl.BlockSpec((CTX_LEN, D), lambda b: (b, 0)),
        compiler_params=_params(("parallel",)),
        name="attn_ctx",
    )(q, katt, vatt)
    seg0 = CTX_SEGS
    lat0 = T_CTX // LAT_LEN
    return pl.pallas_call(
        _attn_lat_kernel,
        out_shape=jax.ShapeDtypeStruct((T, D), BF16),
        grid=(N_LAT_SEQ, SEGS_PER_LAT),
        in_specs=[pl.BlockSpec((SEG, D), lambda b, i: (seg0 + b * SEGS_PER_LAT + i, 0)),
                  pl.BlockSpec((LAT_LEN, D_KV), lambda b, i: (lat0 + b, 0)),
                  pl.BlockSpec((LAT_LEN, D_KV), lambda b, i: (lat0 + b, 0)),
                  pl.BlockSpec((None, None, PAST, D_KV), lambda b, i: (b, l, 0, 0)),
                  pl.BlockSpec((None, None, PAST, D_KV), lambda b, i: (b, l, 0, 0)),
                  pl.BlockSpec(memory_space=pl.ANY)],
        out_specs=pl.BlockSpec((SEG, D), lambda b, i: (seg0 + b * SEGS_PER_LAT + i, 0)),
        input_output_aliases={5: 0},
        compiler_params=_params(("parallel", "parallel")),
        name="attn_lat",
    )(q, katt, vatt, cache_k, cache_v, o)


def _post_kernel(ya_ref, yb_ref, o_ref, g_ref, x_ref, mod_ref, wa_ref, wb_ref, wc_ref, wo_ref,
                 lng_ref, lnb_ref, wr_hi_ref, wr_lo_ref, rb_ref,
                 x1_ref, h2_ref, route_ref, rw_ref, cnt_ref, carry_ref):
    m = pl.program_id(0)

    @pl.when(m == 0)
    def _():
        carry_ref[...] = jnp.zeros_like(carry_ref)

    carry = carry_ref[...]
    r_i = lax.broadcasted_iota(I32, (POST_ROWS, POST_ROWS), 0)
    c_i = lax.broadcasted_iota(I32, (POST_ROWS, POST_ROWS), 1)
    upper = jnp.where(r_i < c_i, 1.0, 0.0).astype(BF16)
    for part in range(TM // POST_ROWS):
        rows = slice(part * POST_ROWS, (part + 1) * POST_ROWS)
        carry = _post_part(rows, carry, upper, ya_ref, yb_ref, o_ref, g_ref, x_ref, mod_ref, wa_ref, wb_ref,
                           wc_ref, wo_ref, lng_ref, lnb_ref, wr_hi_ref, wr_lo_ref, rb_ref,
                           x1_ref, h2_ref, route_ref, rw_ref)
    carry_ref[...] = carry
    cnt_ref[...] = carry.astype(I32)
    route_ref[4:8, :] = jnp.zeros((4, TM), I32)


def _post_part(rows, carry, upper, ya_ref, yb_ref, o_ref, g_ref, x_ref, mod_ref, wa_ref, wb_ref, wc_ref,
               wo_ref, lng_ref, lnb_ref, wr_hi_ref, wr_lo_ref, rb_ref, x1_ref, h2_ref, route_ref, rw_ref):
    n = POST_ROWS
    ya = jnp.dot(ya_ref[rows, :], wa_ref[...], preferred_element_type=F32)
    yb = jnp.dot(yb_ref[rows, :], wb_ref[...], preferred_element_type=F32)
    yc = jnp.dot(o_ref[rows, :], wc_ref[...], preferred_element_type=F32)
    merged = (g_ref[rows, 0:D].astype(F32) * ya + g_ref[rows, D:2 * D].astype(F32) * yb
              + g_ref[rows, 2 * D:3 * D].astype(F32) * yc)
    mix = jnp.dot(merged.astype(BF16), wo_ref[...], preferred_element_type=F32)
    x1 = _layer_norm(ALPHA * x_ref[rows, :] + mod_ref[2:3, :] * mix, lng_ref[...], lnb_ref[...])
    x1_ref[rows, :] = x1
    h2 = x1 * (1.0 + mod_ref[4:5, :]) + mod_ref[3:4, :]

    hi = h2.astype(BF16)
    h2_ref[rows, :] = hi
    lo = (h2 - hi.astype(F32)).astype(BF16)
    wr_hi = wr_hi_ref[...]
    logits = (jnp.dot(hi, wr_hi, preferred_element_type=F32)
              + jnp.dot(lo, wr_hi, preferred_element_type=F32)
              + jnp.dot(hi, wr_lo_ref[...], preferred_element_type=F32))
    scores = _sigmoid(logits.T[0:N_EXP, :])
    sel = scores + rb_ref[...]

    gscore = []
    for g in range(N_GRP):
        r = [sel[g * EXP_PER_GRP + j:g * EXP_PER_GRP + j + 1, :] for j in range(EXP_PER_GRP)]
        pairs = [r[a] + r[b] for a in range(EXP_PER_GRP) for b in range(a + 1, EXP_PER_GRP)]
        gscore.append(functools.reduce(jnp.maximum, pairs))
    best = functools.reduce(jnp.maximum, gscore)
    gsel = jnp.full(best.shape, N_GRP - 1, I32)
    for g in range(N_GRP - 2, -1, -1):
        gsel = jnp.where(gscore[g] == best, g, gsel)

    eidx = lax.broadcasted_iota(I32, (N_EXP, n), 0)
    neg = jnp.float32(-jnp.inf)
    cand = jnp.where((eidx // EXP_PER_GRP) == gsel, sel, neg)
    top1 = jnp.max(cand, axis=0, keepdims=True)
    idx1 = jnp.min(jnp.where(cand == top1, eidx, N_EXP), axis=0, keepdims=True)
    cand2 = jnp.where(eidx == idx1, neg, cand)
    top2 = jnp.max(cand2, axis=0, keepdims=True)
    idx2 = jnp.min(jnp.where(cand2 == top2, eidx, N_EXP), axis=0, keepdims=True)
    is1 = eidx == idx1
    is2 = eidx == idx2
    w1 = jnp.sum(jnp.where(is1, scores, 0.0), axis=0, keepdims=True)
    w2 = jnp.sum(jnp.where(is2, scores, 0.0), axis=0, keepdims=True)
    wsum = w1 + w2
    w1 = w1 / wsum
    w2 = w2 / wsum

    onehot = jnp.where(jnp.logical_or(is1, is2), 1.0, 0.0)
    prefix = jnp.dot(onehot.astype(BF16), upper, preferred_element_type=F32) + carry[:, 0:1]
    rank1 = jnp.sum(jnp.where(is1, prefix, 0.0), axis=0, keepdims=True)
    rank2 = jnp.sum(jnp.where(is2, prefix, 0.0), axis=0, keepdims=True)

    route_ref[0:1, rows] = idx1
    route_ref[1:2, rows] = idx2
    route_ref[2:3, rows] = rank1.astype(I32)
    route_ref[3:4, rows] = rank2.astype(I32)
    wrow = lax.broadcasted_iota(I32, (LANE, n), 0)
    wcols = jnp.where(wrow == 0, w1, jnp.where(wrow == 1, w2, 0.0))
    rw_ref[rows, :] = wcols.T
    return carry + jnp.sum(onehot, axis=1, keepdims=True)


def _post(ya_pre, yb_pre, o, gates, x, mod_l, wa, wb, wc, wo, ln_g, ln_b, wr_hi, wr_lo, rbias, l):
    row = lambda m: (m, 0)
    const = lambda m: (0, 0)
    wspec = pl.BlockSpec((None, D, D), lambda m: (l, 0, 0))
    return pl.pallas_call(
        _post_kernel,
        out_shape=(jax.ShapeDtypeStruct((T, D), F32),
                   jax.ShapeDtypeStruct((T, D), BF16),
                   jax.ShapeDtypeStruct((N_TILES, 8, TM), I32),
                   jax.ShapeDtypeStruct((T, LANE), F32),
                   jax.ShapeDtypeStruct((N_EXP, LANE), I32)),
        grid=(N_TILES,),
        in_specs=[pl.BlockSpec((TM, D), row), pl.BlockSpec((TM, D), row), pl.BlockSpec((TM, D), row),
                  pl.BlockSpec((TM, 3 * D), row), pl.BlockSpec((TM, D), row),
                  pl.BlockSpec((None, N_MOD, D), lambda m: (_cond_row(m, CTX_TILES, TILES_PER_LAT), 0, 0)),
                  wspec, wspec, wspec, wspec,
                  pl.BlockSpec((1, D), const), pl.BlockSpec((1, D), const),
                  pl.BlockSpec((D, LANE), const), pl.BlockSpec((D, LANE), const),
                  pl.BlockSpec((N_EXP, 1), const)],
        out_specs=(pl.BlockSpec((TM, D), row), pl.BlockSpec((TM, D), row),
                   pl.BlockSpec((None, 8, TM), lambda m: (m, 0, 0)),
                   pl.BlockSpec((TM, LANE), row),
                   pl.BlockSpec((N_EXP, LANE), const)),
        scratch_shapes=[pltpu.VMEM((N_EXP, LANE), F32)],
        compiler_params=_params(("arbitrary",)),
        name="merge_ln_router",
    )(ya_pre, yb_pre, o, gates, x, mod_l, wa, wb, wc, wo, ln_g, ln_b, wr_hi, wr_lo, rbias)


def _row_copy(src, dst, src_row, dst_row, sem):
    return pltpu.make_async_copy(src.at[pl.ds(pl.multiple_of(src_row * ROW_TILE, ROW_TILE), ROW_TILE), :],
                                 dst.at[pl.ds(pl.multiple_of(dst_row * ROW_TILE, ROW_TILE), ROW_TILE), :], sem)


DMA_UNROLL = 4


RUN_CHUNK = 256


def _run_copies(src, dst, src_row, dst_row, n, sem, start):
    def piece(offset, size):
        cp = pltpu.make_async_copy(
            src.at[pl.ds(pl.multiple_of((src_row + offset) * ROW_TILE, ROW_TILE), size * ROW_TILE), :],
            dst.at[pl.ds(pl.multiple_of((dst_row + offset) * ROW_TILE, ROW_TILE), size * ROW_TILE), :], sem)
        if start:
            cp.start()
        else:
            cp.wait()

    def whole(j, carry):
        piece(j * RUN_CHUNK, RUN_CHUNK)
        return carry

    lax.fori_loop(0, n // RUN_CHUNK, whole, 0)
    size = RUN_CHUNK // 2
    while size >= 1:
        @pl.when((n & size) != 0)
        def _(size=size):
            piece(n & ~(2 * size - 1), size)
        size //= 2


def _dispatch_kernel(cnt_ref, off_ref, dst_ref, last_ref, lp_ref, h2_ref, xs_hbm, xc_ref, zero_ref, sem, zsem):
    m = pl.program_id(0)

    @pl.when(m == 0)
    def _():
        zero_ref[...] = jnp.zeros_like(zero_ref)

        def pad_copy(e):
            row = pl.multiple_of(last_ref[e] * ROW_TILE, FFN_BLK * ROW_TILE)
            return pltpu.make_async_copy(zero_ref, xs_hbm.at[pl.ds(row, FFN_BLK * ROW_TILE), :], zsem)

        def zero_start(e, carry):
            pad_copy(e).start()
            return carry

        def zero_wait(e, carry):
            pad_copy(e).wait()
            return carry

        lax.fori_loop(0, N_EXP, zero_start, 0)
        lax.fori_loop(0, N_EXP, zero_wait, 0)

    r_i = lax.broadcasted_iota(I32, (2 * TM, TM), 0)
    hit = jnp.logical_or(r_i == lp_ref[0:1, :], r_i == lp_ref[1:2, :])
    perm = jnp.where(hit, 1.0, 0.0).astype(BF16)
    _store_rows(xc_ref, jnp.dot(perm, h2_ref[...], preferred_element_type=F32))

    for start in (True, False):
        for e in range(N_EXP):
            j = m * N_EXP + e
            _run_copies(xc_ref, xs_hbm, off_ref[j], dst_ref[j], cnt_ref[j], sem, start)


def _dispatch(tile_cnt, tile_off, tile_dst, last_blk, lp, h2):
    grid_spec = pltpu.PrefetchScalarGridSpec(
        num_scalar_prefetch=4,
        grid=(N_TILES,),
        in_specs=[pl.BlockSpec((None, 2, TM), lambda m, *_: (m, 0, 0)),
                  pl.BlockSpec((TM, D), lambda m, *_: (m, 0))],
        out_specs=pl.BlockSpec(memory_space=pl.ANY),
        scratch_shapes=[pltpu.VMEM((2 * TM * ROW_TILE, LANE), F32),
                        pltpu.VMEM((FFN_BLK * ROW_TILE, LANE), F32),
                        pltpu.SemaphoreType.DMA, pltpu.SemaphoreType.DMA],
    )
    return pl.pallas_call(
        _dispatch_kernel,
        out_shape=jax.ShapeDtypeStruct((N_SLOTS * ROW_TILE, LANE), F32),
        grid_spec=grid_spec,
        compiler_params=_params(("arbitrary",)),
        name="moe_dispatch",
    )(tile_cnt, tile_off, tile_dst, last_blk, lp, h2)


def _ffn_kernel(blk_row_ref, blk_exp_ref, n_act_ref, x_ref, wgu_ref, wd_ref, y_ref, wgu_s, wd_s):
    del blk_row_ref
    i = pl.program_id(0)
    new_expert = jnp.logical_or(i == 0, blk_exp_ref[i] != blk_exp_ref[jnp.maximum(i - 1, 0)])
    _cast_weights(new_expert, [(wgu_ref, wgu_s), (wd_ref, wd_s)])

    @pl.when(i < n_act_ref[0])
    def _():
        x = _load_rows(x_ref, FFN_BLK).astype(BF16)
        gu = jnp.dot(x, wgu_s[...], preferred_element_type=F32)
        gate = gu[:, :D_EXP]
        up = gu[:, D_EXP:]
        act = (gate * _sigmoid(gate) * up).astype(BF16)
        _store_rows(y_ref, jnp.dot(act, wd_s[...], preferred_element_type=F32))


def _expert_ffn(blk_row, blk_exp, n_act, xs, wgu, wd, l):
    grid_spec = pltpu.PrefetchScalarGridSpec(
        num_scalar_prefetch=3,
        grid=(N_FFN_BLOCKS,),
        in_specs=[pl.BlockSpec((FFN_BLK * ROW_TILE, LANE), lambda i, br, be, na: (br[i], 0)),
                  pl.BlockSpec((None, None, D, 2 * D_EXP), lambda i, br, be, na: (l, be[i], 0, 0)),
                  pl.BlockSpec((None, None, D_EXP, D), lambda i, br, be, na: (l, be[i], 0, 0))],
        out_specs=pl.BlockSpec((FFN_BLK * ROW_TILE, LANE), lambda i, br, be, na: (br[i], 0)),
        scratch_shapes=[pltpu.VMEM((D, 2 * D_EXP), BF16), pltpu.VMEM((D_EXP, D), BF16)],
    )
    return pl.pallas_call(
        _ffn_kernel,
        out_shape=jax.ShapeDtypeStruct((N_SLOTS * ROW_TILE, LANE), F32),
        grid_spec=grid_spec,
        compiler_params=_params(("arbitrary",)),
        name="expert_ffn",
    )(blk_row, blk_exp, n_act, xs, wgu, wd)


def _block_plan(counts):
    nblk = (counts + FFN_BLK - 1) // FFN_BLK
    end = jnp.cumsum(nblk)
    start = end - nblk
    n_act = end[-1]
    i = jnp.arange(N_FFN_BLOCKS, dtype=I32)
    i_eff = jnp.minimum(i, n_act - 1)
    e = jnp.minimum(jnp.sum(i_eff[:, None] >= end[None, :], axis=1), N_EXP - 1).astype(I32)
    last_blk = jnp.minimum(start + jnp.maximum(nblk - 1, 0), N_FFN_BLOCKS - 1) * FFN_BLK
    return ((start * FFN_BLK).astype(I32), last_blk.astype(I32), i_eff.astype(I32), e,
            n_act.reshape(1).astype(I32))


def _tile_plan(route, seg_start):
    is_exp = route[:, 0:2, :, None] == jnp.arange(N_EXP, dtype=I32)
    cnt = jnp.sum(is_exp, axis=(1, 2), dtype=I32)
    before = jnp.cumsum(cnt, axis=0) - cnt
    off = jnp.cumsum(cnt, axis=1) - cnt
    dst = seg_start[None, :] + before
    pick = lambda tab: jnp.sum(jnp.where(is_exp, tab[:, None, None, :], 0), axis=-1)
    rank = route[:, 2:4, :]
    local = pick(off - before) + rank
    slots = pick(jnp.broadcast_to(seg_start, cnt.shape)) + rank
    return cnt.reshape(-1), off.reshape(-1), dst.reshape(-1), local, slots


def _combine_kernel(slot_ref, slot_next_ref, y_hbm, rw_ref, x1_ref, mod_ref, modn_ref,
                    lng_ref, lnb_ref, x2_ref, hn_ref, buf, sems):
    m = pl.program_id(0)
    half = m % 2

    def gather_tile(s_ref, dst_half):
        def body(i, carry):
            for u in range(DMA_UNROLL):
                t = i * DMA_UNROLL + u
                for k in range(2):
                    _row_copy(y_hbm, buf.at[dst_half, k], s_ref[k, t], t, sems.at[dst_half]).start()
            return carry

        lax.fori_loop(0, TM // DMA_UNROLL, body, 0)

    @pl.when(m == 0)
    def _():
        gather_tile(slot_ref, 0)

    @pl.when(m + 1 < pl.num_programs(0))
    def _():
        gather_tile(slot_next_ref, 1 - half)

    def drain(i, carry):
        for _ in range(2 * DMA_UNROLL):
            _row_copy(y_hbm, buf.at[half, 0], 0, 0, sems.at[half]).wait()
        return carry

    lax.fori_loop(0, TM // DMA_UNROLL, drain, 0)
    f = rw_ref[:, 0:1] * _load_rows(buf.at[half, 0], TM) + rw_ref[:, 1:2] * _load_rows(buf.at[half, 1], TM)
    x2 = _layer_norm(ALPHA * x1_ref[...] + mod_ref[5:6, :] * f, lng_ref[...], lnb_ref[...])
    x2_ref[...] = x2
    hn_ref[...] = (x2 * (1.0 + modn_ref[1:2, :]) + modn_ref[0:1, :]).astype(BF16)


def _combine(slots, y_slots, rw, x1, mod_l, mod_next, ln_g, ln_b):
    row = lambda m: (m, 0)
    const = lambda m: (0, 0)
    mod_spec = pl.BlockSpec((None, N_MOD, D), lambda m: (_cond_row(m, CTX_TILES, TILES_PER_LAT), 0, 0))
    return pl.pallas_call(
        _combine_kernel,
        out_shape=(jax.ShapeDtypeStruct((T, D), F32), jax.ShapeDtypeStruct((T, D), BF16)),
        grid=(N_TILES,),
        in_specs=[pl.BlockSpec((None, 2, TM), lambda m: (m, 0, 0), memory_space=pltpu.SMEM),
                  pl.BlockSpec((None, 2, TM), lambda m: (jnp.minimum(m + 1, N_TILES - 1), 0, 0),
                               memory_space=pltpu.SMEM),
                  pl.BlockSpec(memory_space=pl.ANY),
                  pl.BlockSpec((TM, LANE), row), pl.BlockSpec((TM, D), row),
                  mod_spec, mod_spec,
                  pl.BlockSpec((1, D), const), pl.BlockSpec((1, D), const)],
        out_specs=(pl.BlockSpec((TM, D), row), pl.BlockSpec((TM, D), row)),
        scratch_shapes=[pltpu.VMEM((2, 2, TM * ROW_TILE, LANE), F32), pltpu.SemaphoreType.DMA((2,))],
        compiler_params=_params(("arbitrary",)),
        name="moe_combine_ln",
    )(slots, slots, y_slots, rw, x1, mod_l, mod_next, ln_g, ln_b)


def kernel(x_prompt, x_sample, c, c_ctx, cache_k, cache_v, w_in, conv_a_w, w_a_out, conv_b_w, conv_b_b,
           norm_b_g, norm_b_b, w_b_out, q_norm_g, k_norm_g, w_c_out, w_o, w_ada, b_ada, ln_g, ln_b,
           w_router, router_bias, w_gate_up, w_down):
    x = jnp.concatenate([x_prompt.reshape(T_CTX, D), x_sample.reshape(T_LAT, D)], axis=0)
    cond16 = jnp.concatenate([c_ctx[None, :], c, jnp.zeros((16 - 1 - N_LAT_SEQ, D), F32)], axis=0)
    mod = _modulation(cond16, w_ada, b_ada).reshape(DEPTH, 16, N_MOD, D)

    wa_b, wb_b, wc_b, wo_b = (w.astype(BF16) for w in (w_a_out, w_b_out, w_c_out, w_o))
    wr = jnp.pad(w_router, ((0, 0), (0, LANE - N_EXP)))
    wr_hi = wr.astype(BF16)
    wr_lo = (wr - wr_hi.astype(F32)).astype(BF16)
    rbias = router_bias.reshape(N_EXP, 1)

    cos_tab, sin_tab = _rope_tables()
    half = np.arange(LANE) // HEAD
    ones_bd = jnp.asarray(half[:, None] == half[None, :], BF16)
    ck = cache_k.reshape(N_LAT_SEQ, DEPTH, PAST, D_KV)
    cv = cache_v.reshape(N_LAT_SEQ, DEPTH, PAST, D_KV)

    h = _modulate(x, mod[0])
    new_k, new_v = [], []
    for l in range(DEPTH):
        ab, acx = _proj_a(h, w_in, l)
        glu = _proj_b(h, w_in, l)
        gates = _proj_gates(h, w_in, l)
        q, katt, vatt, kn, vf = _proj_qkv(h, w_in, jnp.tile(q_norm_g[l], N_Q)[None, :],
                                          jnp.tile(k_norm_g[l], N_KV)[None, :], cos_tab, sin_tab, ones_bd, l)
        new_k.append(kn[:T_CTX].reshape(N_CTX_SEQ, CTX_LEN, N_KV, HEAD))
        new_v.append(vf[:T_CTX].reshape(N_CTX_SEQ, CTX_LEN, N_KV, HEAD))
        ya_pre, yb_pre = _convs(acx, ab, glu, conv_a_w[l], conv_b_w[l], conv_b_b[l][None, :],
                                norm_b_g[l][None, :], norm_b_b[l][None, :])
        o = _attention(q, katt, vatt, ck, cv, l)
        x1, h2, route, rw, counts = _post(ya_pre, yb_pre, o, gates, x, mod[l], wa_b, wb_b, wc_b, wo_b,
                                          ln_g[l, 0][None, :], ln_b[l, 0][None, :], wr_hi, wr_lo, rbias, l)
        cnt = counts[:, 0]
        seg_start, last_blk, blk_row, blk_exp, n_act = _block_plan(cnt)
        tile_cnt, tile_off, tile_dst, local, slots = _tile_plan(route, seg_start)
        xs = _dispatch(tile_cnt, tile_off, tile_dst, last_blk, local, h2)
        y_slots = _expert_ffn(blk_row, blk_exp, n_act, xs, w_gate_up, w_down, l)
        x, h = _combine(slots, y_slots, rw, x1, mod[l], mod[min(l + 1, DEPTH - 1)],
                        ln_g[l, 1][None, :], ln_b[l, 1][None, :])
    y_prompt = x[:T_CTX].reshape(N_CTX_SEQ, CTX_LEN, D)
    y_sample = x[T_CTX:].reshape(N_LAT_SEQ, LAT_LEN, D)
    return y_prompt, y_sample, jnp.stack(new_k, axis=1), jnp.stack(new_v, axis=1)
```

```python
import functools

import numpy as np
import jax
import jax.numpy as jnp
from jax import lax
from jax.experimental import pallas as pl
from jax.experimental.pallas import tpu as pltpu

F32 = jnp.float32
BF16 = jnp.bfloat16
I32 = jnp.int32

D = 1024
DEPTH = 4
N_CTX_SEQ = 16
CTX_LEN = 256
N_LAT_SEQ = 8
LAT_LEN = 1024
PAST = 512
T_CTX = N_CTX_SEQ * CTX_LEN
T_LAT = N_LAT_SEQ * LAT_LEN
T = T_CTX + T_LAT
GRID_W = 64
HEAD = 64
N_Q = 16
N_KV = 4
GROUP = 4
D_KV = N_KV * HEAD
N_EXP = 16
N_GRP = 4
EXP_PER_GRP = 4
D_EXP = 512
IN_COLS = 9728
N_MOD = 6
ALPHA = (2 * DEPTH) ** 0.25
LN_EPS = 1e-5
RMS_EPS = 1e-6
ROPE_THETA = 10000.0

LANE = 128
TM = 512
N_TILES = T // TM
CTX_TILES = T_CTX // TM
TILES_PER_LAT = LAT_LEN // TM
POST_ROWS = 256
TMP = 1024
SEG = 256
N_SEG = T // SEG
CTX_SEGS = T_CTX // SEG
SEGS_PER_LAT = LAT_LEN // SEG
HALO_A = 8
HALO_B = 16
SH_ROWS = SEG + 2 * HALO_B - 8
FFN_BLK = 256
N_FFN_BLOCKS = (2 * T) // FFN_BLK + N_EXP
N_SLOTS = N_FFN_BLOCKS * FFN_BLK
VMEM_LIMIT = 56 * 1024 * 1024

COL_AB, COL_AC, COL_AX, COL_BU, COL_BG, COL_Q, COL_KV, COL_GATES = 0, 2, 4, 6, 8, 10, 12, 13


def _params(sem):
    return pltpu.CompilerParams(dimension_semantics=sem, vmem_limit_bytes=VMEM_LIMIT)


def _cond_row(m, tiles_ctx, tiles_per_lat):
    return jnp.where(m < tiles_ctx, 0, 1 + (m - tiles_ctx) // tiles_per_lat)


def _layer_norm(x, g, b):
    mu = jnp.mean(x, axis=-1, keepdims=True)
    xc = x - mu
    var = jnp.mean(xc * xc, axis=-1, keepdims=True)
    return xc * lax.rsqrt(var + LN_EPS) * g + b


def _sigmoid(x):
    return 1.0 / (1.0 + jnp.exp(-x))


ROW_TILE = D // LANE


def _store_rows(ref, x):
    for j in range(ROW_TILE):
        ref[pl.ds(j, x.shape[0], stride=ROW_TILE), :] = x[:, j * LANE:(j + 1) * LANE]


def _load_rows(ref, n_rows):
    return jnp.concatenate([ref[pl.ds(j, n_rows, stride=ROW_TILE), :] for j in range(ROW_TILE)], axis=1)


def _mod_kernel(cond_ref, w_ref, b_ref, o_ref):
    cnd = cond_ref[...]
    s = (cnd * _sigmoid(cnd)).astype(BF16)
    o_ref[...] = jnp.dot(s, w_ref[...].astype(BF16), preferred_element_type=F32) + b_ref[...]


def _modulation(cond16, w_ada, b_ada):
    n_col = N_MOD * D
    tn = 1024
    return pl.pallas_call(
        _mod_kernel,
        out_shape=jax.ShapeDtypeStruct((DEPTH, 16, n_col), F32),
        grid=(DEPTH, n_col // tn),
        in_specs=[
            pl.BlockSpec((16, D), lambda l, n: (0, 0)),
            pl.BlockSpec((None, D, tn), lambda l, n: (l, 0, n)),
            pl.BlockSpec((None, 1, tn), lambda l, n: (l, 0, n)),
        ],
        out_specs=pl.BlockSpec((None, 16, tn), lambda l, n: (l, 0, n)),
        compiler_params=_params(("parallel", "parallel")),
        name="adaln_mod",
    )(cond16, w_ada, b_ada.reshape(DEPTH, 1, n_col))


def _modulate_kernel(x_ref, mod_ref, h_ref):
    h_ref[...] = (x_ref[...] * (1.0 + mod_ref[1:2, :]) + mod_ref[0:1, :]).astype(BF16)


def _modulate(x, mod_l):
    return pl.pallas_call(
        _modulate_kernel,
        out_shape=jax.ShapeDtypeStruct((T, D), BF16),
        grid=(N_TILES,),
        in_specs=[
            pl.BlockSpec((TM, D), lambda m: (m, 0)),
            pl.BlockSpec((None, N_MOD, D), lambda m: (_cond_row(m, CTX_TILES, TILES_PER_LAT), 0, 0)),
        ],
        out_specs=pl.BlockSpec((TM, D), lambda m: (m, 0)),
        compiler_params=_params(("parallel",)),
        name="modulate",
    )(x, mod_l)


WCOL = 512


def _cast_weights(first, pairs):
    @pl.when(first)
    def _():
        for src, dst in pairs:
            dst[...] = src[...].astype(BF16)


def _proj_a_kernel(h_ref, wb_ref, wc_ref, wx_ref, ab_ref, acx_ref, wb_s, wc_s, wx_s):
    _cast_weights(pl.program_id(1) == 0, [(wb_ref, wb_s), (wc_ref, wc_s), (wx_ref, wx_s)])
    h = h_ref[...]
    ab_ref[...] = jnp.dot(h, wb_s[...], preferred_element_type=F32).astype(BF16)
    acx_ref[...] = (jnp.dot(h, wc_s[...], preferred_element_type=F32)
                    * jnp.dot(h, wx_s[...], preferred_element_type=F32))


def _proj_b_kernel(h_ref, wu_ref, wg_ref, glu_ref, wu_s, wg_s):
    _cast_weights(pl.program_id(1) == 0, [(wu_ref, wu_s), (wg_ref, wg_s)])
    h = h_ref[...]
    glu_ref[...] = (jnp.dot(h, wu_s[...], preferred_element_type=F32)
                    * _sigmoid(jnp.dot(h, wg_s[...], preferred_element_type=F32)))


def _proj_gate_kernel(h_ref, w0_ref, w1_ref, g_ref, w_s):
    _cast_weights(pl.program_id(1) == 0, [(w0_ref, w_s.at[:, 0:WCOL]), (w1_ref, w_s.at[:, WCOL:2 * WCOL])])
    g_ref[...] = _sigmoid(jnp.dot(h_ref[...], w_s[...], preferred_element_type=F32)).astype(BF16)


def _w_spec(l, col0, step=1):
    return pl.BlockSpec((None, D, WCOL), lambda c, m: (l, 0, col0 + step * c))


def _proj_a(h, w_in, l):
    out_spec = pl.BlockSpec((TMP, WCOL), lambda c, m: (m, c))
    return pl.pallas_call(
        _proj_a_kernel,
        out_shape=(jax.ShapeDtypeStruct((T, D), BF16), jax.ShapeDtypeStruct((T, D), F32)),
        grid=(D // WCOL, T // TMP),
        in_specs=[pl.BlockSpec((TMP, D), lambda c, m: (m, 0)),
                  _w_spec(l, COL_AB), _w_spec(l, COL_AC), _w_spec(l, COL_AX)],
        out_specs=(out_spec, out_spec),
        scratch_shapes=[pltpu.VMEM((D, WCOL), BF16)] * 3,
        compiler_params=_params(("parallel", "arbitrary")),
        name="proj_a",
    )(h, w_in, w_in, w_in)


def _proj_b(h, w_in, l):
    return pl.pallas_call(
        _proj_b_kernel,
        out_shape=jax.ShapeDtypeStruct((T, D), F32),
        grid=(D // WCOL, T // TMP),
        in_specs=[pl.BlockSpec((TMP, D), lambda c, m: (m, 0)), _w_spec(l, COL_BU), _w_spec(l, COL_BG)],
        out_specs=pl.BlockSpec((TMP, WCOL), lambda c, m: (m, c)),
        scratch_shapes=[pltpu.VMEM((D, WCOL), BF16)] * 2,
        compiler_params=_params(("parallel", "arbitrary")),
        name="proj_b",
    )(h, w_in, w_in)


def _proj_gates(h, w_in, l):
    tn = 2 * WCOL
    return pl.pallas_call(
        _proj_gate_kernel,
        out_shape=jax.ShapeDtypeStruct((T, 3 * D), BF16),
        grid=(3 * D // tn, T // TMP),
        in_specs=[pl.BlockSpec((TMP, D), lambda c, m: (m, 0)),
                  _w_spec(l, COL_GATES, 2), _w_spec(l, COL_GATES + 1, 2)],
        out_specs=pl.BlockSpec((TMP, tn), lambda c, m: (m, c)),
        scratch_shapes=[pltpu.VMEM((D, tn), BF16)],
        compiler_params=_params(("parallel", "arbitrary")),
        name="proj_gates",
    )(h, w_in, w_in)


def _head_mean_square(x, ones_bd):
    out = []
    for c in range(x.shape[1] // LANE):
        sq = x[:, c * LANE:(c + 1) * LANE]
        out.append(jnp.dot((sq * sq).astype(BF16), ones_bd, preferred_element_type=F32))
    return jnp.concatenate(out, axis=1) * (1.0 / HEAD)


def _rope(x, cos, sin, first_half):
    out = []
    for c in range(x.shape[1] // LANE):
        xc = x[:, c * LANE:(c + 1) * LANE]
        partner = jnp.where(first_half, pltpu.roll(xc, LANE - 16, axis=1), pltpu.roll(xc, 16, axis=1))
        out.append(xc * cos + partner * sin)
    return jnp.concatenate(out, axis=1)


def _qkv_kernel(h_ref, wq_ref, wkv_ref, gq_ref, gk_ref, cos_ref, sin_ref, ones_ref,
                q_ref, katt_ref, vatt_ref, kn_ref, vf_ref, wq_s, wkv_s):
    _cast_weights(pl.program_id(0) == 0, [(wq_ref, wq_s), (wkv_ref, wkv_s)])
    h = h_ref[...]
    ones_bd = ones_ref[...]
    cos = cos_ref[...]
    sin = sin_ref[...]
    lane = lax.broadcasted_iota(I32, (TM, LANE), 1)
    first_half = (lane & 16) == 0
    q = jnp.dot(h, wq_s[...], preferred_element_type=F32)
    qn = q * lax.rsqrt(_head_mean_square(q, ones_bd) + RMS_EPS) * gq_ref[...]
    q_ref[...] = (_rope(qn, cos, sin, first_half) * (HEAD ** -0.5)).astype(BF16)
    kv = jnp.dot(h, wkv_s[...], preferred_element_type=F32)
    k = kv[:, :D_KV]
    v = kv[:, D_KV:]
    kn = k * lax.rsqrt(_head_mean_square(k, ones_bd) + RMS_EPS) * gk_ref[...]
    katt_ref[...] = _rope(kn, cos, sin, first_half).astype(BF16)
    vatt_ref[...] = v.astype(BF16)

    @pl.when(pl.program_id(0) < CTX_TILES)
    def _():
        kn_ref[...] = kn
        vf_ref[...] = v


def _proj_qkv(h, w_in, gq, gk, cos_tab, sin_tab, ones_bd, l):
    def tab_idx(m):
        return jnp.where(m < CTX_TILES, TILES_PER_LAT, (m - CTX_TILES) % TILES_PER_LAT)

    row = lambda m: (m, 0)
    ctx_row = lambda m: (jnp.minimum(m, CTX_TILES - 1), 0)
    return pl.pallas_call(
        _qkv_kernel,
        out_shape=(jax.ShapeDtypeStruct((T, D), BF16),
                   jax.ShapeDtypeStruct((T, D_KV), BF16),
                   jax.ShapeDtypeStruct((T, D_KV), BF16),
                   jax.ShapeDtypeStruct((T_CTX, D_KV), F32),
                   jax.ShapeDtypeStruct((T_CTX, D_KV), F32)),
        grid=(N_TILES,),
        in_specs=[
            pl.BlockSpec((TM, D), row),
            pl.BlockSpec((None, D, D), lambda m: (l, 0, COL_Q // 2)),
            pl.BlockSpec((None, D, 2 * D_KV), lambda m: (l, 0, COL_KV)),
            pl.BlockSpec((1, D), lambda m: (0, 0)),
            pl.BlockSpec((1, D_KV), lambda m: (0, 0)),
            pl.BlockSpec((TM, LANE), lambda m: (tab_idx(m), 0)),
            pl.BlockSpec((TM, LANE), lambda m: (tab_idx(m), 0)),
            pl.BlockSpec((LANE, LANE), lambda m: (0, 0)),
        ],
        out_specs=(pl.BlockSpec((TM, D), row), pl.BlockSpec((TM, D_KV), row),
                   pl.BlockSpec((TM, D_KV), row), pl.BlockSpec((TM, D_KV), ctx_row),
                   pl.BlockSpec((TM, D_KV), ctx_row)),
        scratch_shapes=[pltpu.VMEM((D, D), BF16), pltpu.VMEM((D, 2 * D_KV), BF16)],
        compiler_params=_params(("arbitrary",)),
        name="proj_qkv",
    )(h, w_in, w_in, gq, gk, cos_tab, sin_tab, ones_bd)


def _rope_tables():
    lane = np.arange(LANE)
    j = lane % 16
    freqs = jnp.power(ROPE_THETA, -jnp.arange(16, dtype=F32) / 16)[j]
    pos = jnp.arange(LAT_LEN, dtype=I32)
    row = (pos // GRID_W).astype(F32)
    col = (pos % GRID_W).astype(F32)
    use_row = jnp.asarray((lane % HEAD) < HEAD // 2)
    p = jnp.where(use_row[None, :], row[:, None], col[:, None])
    ang = p * freqs[None, :]
    sign = jnp.asarray(np.where((lane & 16) == 0, -1.0, 1.0), F32)
    cos = jnp.concatenate([jnp.cos(ang), jnp.ones((TM, LANE), F32)], axis=0)
    sin = jnp.concatenate([jnp.sin(ang) * sign[None, :], jnp.zeros((TM, LANE), F32)], axis=0)
    return cos, sin


def _conv_kernel(acx_ref, acx_l_ref, acx_r_ref, ab_ref, glu_ref, glu_l_ref, glu_r_ref,
                 wa_ref, wb_ref, bb_ref, ng_ref, nb_ref, ya_ref, yb_ref, pad_a, pad_b, u_ref, sh_ref):
    s = pl.program_id(0)
    lat = s >= CTX_SEGS
    pos = (s - CTX_SEGS) % SEGS_PER_LAT
    has_left = jnp.logical_and(lat, pos != 0)
    has_right = jnp.logical_and(lat, pos != SEGS_PER_LAT - 1)

    pad_a[0:HALO_A, :] = jnp.where(has_left, acx_l_ref[...], 0.0)
    pad_a[HALO_A:HALO_A + SEG, :] = acx_ref[...]
    pad_a[HALO_A + SEG:, :] = jnp.where(has_right, acx_r_ref[...], 0.0)
    pad_b[0:HALO_B, :] = jnp.where(has_left, glu_l_ref[...], 0.0)
    pad_b[HALO_B:HALO_B + SEG, :] = glu_ref[...]
    pad_b[HALO_B + SEG:, :] = jnp.where(has_right, glu_r_ref[...], 0.0)

    conv_a = (wa_ref[0:1, :] * pad_a[HALO_A - 1:HALO_A - 1 + SEG, :]
              + wa_ref[1:2, :] * pad_a[HALO_A:HALO_A + SEG, :]
              + wa_ref[2:3, :] * pad_a[HALO_A + 1:HALO_A + 1 + SEG, :])
    ya_ref[...] = (ab_ref[...].astype(F32) * conv_a).astype(BF16)

    rows = 64
    kb = wb_ref.shape[0]

    def lane_chunk(c, carry):
        lanes = pl.ds(pl.multiple_of(c * LANE, LANE), LANE)
        for b in range(1, 8):
            sh_ref[b, :, :] = pad_b[b:b + SH_ROWS, lanes]
        for r in range(SEG // rows):
            acc = jnp.zeros((rows, LANE), F32)
            for k in range(kb):
                off = HALO_B + k - kb // 2
                row0 = r * rows + 8 * (off // 8)
                if off % 8 == 0:
                    src = pad_b[row0:row0 + rows, lanes]
                else:
                    src = sh_ref[off % 8, row0:row0 + rows, :]
                acc = acc + wb_ref[k:k + 1, lanes] * src
            u_ref[r * rows:(r + 1) * rows, lanes] = acc
        return carry

    lax.fori_loop(0, D // LANE, lane_chunk, 0)
    u = _layer_norm(u_ref[...] + bb_ref[...], ng_ref[...], nb_ref[...])
    yb_ref[...] = (u * _sigmoid(u)).astype(BF16)


def _convs(acx, ab, glu, conv_a_w, conv_b_w, conv_b_b, norm_g, norm_b):
    seg = lambda s: (s, 0)
    const = lambda s: (0, 0)
    ra, rb = SEG // HALO_A, SEG // HALO_B
    left_a = lambda s: (jnp.maximum(s * ra - 1, 0), 0)
    right_a = lambda s: (jnp.minimum((s + 1) * ra, T // HALO_A - 1), 0)
    left_b = lambda s: (jnp.maximum(s * rb - 1, 0), 0)
    right_b = lambda s: (jnp.minimum((s + 1) * rb, T // HALO_B - 1), 0)
    return pl.pallas_call(
        _conv_kernel,
        out_shape=(jax.ShapeDtypeStruct((T, D), BF16), jax.ShapeDtypeStruct((T, D), BF16)),
        grid=(N_SEG,),
        in_specs=[
            pl.BlockSpec((SEG, D), seg), pl.BlockSpec((HALO_A, D), left_a), pl.BlockSpec((HALO_A, D), right_a),
            pl.BlockSpec((SEG, D), seg),
            pl.BlockSpec((SEG, D), seg), pl.BlockSpec((HALO_B, D), left_b), pl.BlockSpec((HALO_B, D), right_b),
            pl.BlockSpec(conv_a_w.shape, const), pl.BlockSpec(conv_b_w.shape, const),
            pl.BlockSpec((1, D), const), pl.BlockSpec((1, D), const), pl.BlockSpec((1, D), const),
        ],
        out_specs=(pl.BlockSpec((SEG, D), seg), pl.BlockSpec((SEG, D), seg)),
        scratch_shapes=[pltpu.VMEM((SEG + 2 * HALO_A, D), F32),
                        pltpu.VMEM((SEG + 2 * HALO_B, D), F32),
                        pltpu.VMEM((SEG, D), F32),
                        pltpu.VMEM((8, SH_ROWS, LANE), F32)],
        compiler_params=_params(("parallel",)),
        name="convs",
    )(acx, acx, acx, ab, glu, glu, glu, conv_a_w, conv_b_w, conv_b_b, norm_g, norm_b)


def _attend(q_ref, key_refs, val_refs, o_ref, ones_denominator):
    nt = (((1,), (1,)), ((), ()))
    keys = [r[...].astype(BF16) for r in key_refs]
    vals = [r[...].astype(BF16) for r in val_refs]
    n_q = q_ref.shape[0]
    for g in range(N_KV):
        kg = [k[:, g * HEAD:(g + 1) * HEAD] for k in keys]
        vg = [v[:, g * HEAD:(g + 1) * HEAD] for v in vals]
        if ones_denominator:
            vg = [jnp.concatenate([v, jnp.ones_like(v)], axis=1) for v in vg]
        qg = jnp.concatenate([q_ref[:, (g * GROUP + hh) * HEAD:(g * GROUP + hh + 1) * HEAD]
                              for hh in range(GROUP)], axis=0)
        s = [lax.dot_general(qg, k, nt, preferred_element_type=F32) for k in kg]
        mx = functools.reduce(jnp.maximum, [jnp.max(x, axis=-1, keepdims=True) for x in s])
        p = [jnp.exp(x - mx) for x in s]
        acc = functools.reduce(jnp.add, [jnp.dot(x.astype(BF16), v, preferred_element_type=F32)
                                         for x, v in zip(p, vg)])
        if ones_denominator:
            den = acc[:, HEAD:HEAD + 1]
            acc = acc[:, :HEAD]
        else:
            den = functools.reduce(jnp.add, [jnp.sum(x, axis=-1, keepdims=True) for x in p])
        out = (acc / den).astype(BF16)
        for hh in range(GROUP):
            hd = g * GROUP + hh
            o_ref[:, hd * HEAD:(hd + 1) * HEAD] = out[hh * n_q:(hh + 1) * n_q, :]


def _attn_ctx_kernel(q_ref, k_ref, v_ref, o_ref):
    _attend(q_ref, [k_ref], [v_ref], o_ref, ones_denominator=False)


def _attn_lat_kernel(q_ref, k_ref, v_ref, ck_ref, cv_ref, o_in_ref, o_ref):
    del o_in_ref
    _attend(q_ref, [ck_ref, k_ref], [cv_ref, v_ref], o_ref, ones_denominator=True)


def _attention(q, katt, vatt, cache_k, cache_v, l):
    o = pl.pallas_call(
        _attn_ctx_kernel,
        out_shape=jax.ShapeDtypeStruct((T, D), BF16),
        grid=(N_CTX_SEQ,),
        in_specs=[pl.BlockSpec((CTX_LEN, D), lambda b: (b, 0)),
                  pl.BlockSpec((CTX_LEN, D_KV), lambda b: (b, 0)),
                  pl.BlockSpec((CTX_LEN, D_KV), lambda b: (b, 0))],
        out_specs=pl.BlockSpec((CTX_LEN, D), lambda b: (b, 0)),
        compiler_params=_params(("parallel",)),
        name="attn_ctx",
    )(q, katt, vatt)
    seg0 = CTX_SEGS
    lat0 = T_CTX // LAT_LEN
    return pl.pallas_call(
        _attn_lat_kernel,
        out_shape=jax.ShapeDtypeStruct((T, D), BF16),
        grid=(N_LAT_SEQ, SEGS_PER_LAT),
        in_specs=[pl.BlockSpec((SEG, D), lambda b, i: (seg0 + b * SEGS_PER_LAT + i, 0)),
                  pl.BlockSpec((LAT_LEN, D_KV), lambda b, i: (lat0 + b, 0)),
                  pl.BlockSpec((LAT_LEN, D_KV), lambda b, i: (lat0 + b, 0)),
                  pl.BlockSpec((None, None, PAST, D_KV), lambda b, i: (b, l, 0, 0)),
                  pl.BlockSpec((None, None, PAST, D_KV), lambda b, i: (b, l, 0, 0)),
                  pl.BlockSpec(memory_space=pl.ANY)],
        out_specs=pl.BlockSpec((SEG, D), lambda b, i: (seg0 + b * SEGS_PER_LAT + i, 0)),
        input_output_aliases={5: 0},
        compiler_params=_params(("parallel", "parallel")),
        name="attn_lat",
    )(q, katt, vatt, cache_k, cache_v, o)


def _post_kernel(ya_ref, yb_ref, o_ref, g_ref, x_ref, mod_ref, wa_ref, wb_ref, wc_ref, wo_ref,
                 lng_ref, lnb_ref, wr_hi_ref, wr_lo_ref, rb_ref,
                 x1_ref, h2_ref, route_ref, rw_ref, cnt_ref, carry_ref):
    m = pl.program_id(0)

    @pl.when(m == 0)
    def _():
        carry_ref[...] = jnp.zeros_like(carry_ref)

    carry = carry_ref[...]
    r_i = lax.broadcasted_iota(I32, (POST_ROWS, POST_ROWS), 0)
    c_i = lax.broadcasted_iota(I32, (POST_ROWS, POST_ROWS), 1)
    upper = jnp.where(r_i < c_i, 1.0, 0.0).astype(BF16)
    for part in range(TM // POST_ROWS):
        rows = slice(part * POST_ROWS, (part + 1) * POST_ROWS)
        carry = _post_part(rows, carry, upper, ya_ref, yb_ref, o_ref, g_ref, x_ref, mod_ref, wa_ref, wb_ref,
                           wc_ref, wo_ref, lng_ref, lnb_ref, wr_hi_ref, wr_lo_ref, rb_ref,
                           x1_ref, h2_ref, route_ref, rw_ref)
    carry_ref[...] = carry
    cnt_ref[...] = carry.astype(I32)
    route_ref[4:8, :] = jnp.zeros((4, TM), I32)


def _post_part(rows, carry, upper, ya_ref, yb_ref, o_ref, g_ref, x_ref, mod_ref, wa_ref, wb_ref, wc_ref,
               wo_ref, lng_ref, lnb_ref, wr_hi_ref, wr_lo_ref, rb_ref, x1_ref, h2_ref, route_ref, rw_ref):
    n = POST_ROWS
    ya = jnp.dot(ya_ref[rows, :], wa_ref[...], preferred_element_type=F32)
    yb = jnp.dot(yb_ref[rows, :], wb_ref[...], preferred_element_type=F32)
    yc = jnp.dot(o_ref[rows, :], wc_ref[...], preferred_element_type=F32)
    merged = (g_ref[rows, 0:D].astype(F32) * ya + g_ref[rows, D:2 * D].astype(F32) * yb
              + g_ref[rows, 2 * D:3 * D].astype(F32) * yc)
    mix = jnp.dot(merged.astype(BF16), wo_ref[...], preferred_element_type=F32)
    x1 = _layer_norm(ALPHA * x_ref[rows, :] + mod_ref[2:3, :] * mix, lng_ref[...], lnb_ref[...])
    x1_ref[rows, :] = x1
    h2 = x1 * (1.0 + mod_ref[4:5, :]) + mod_ref[3:4, :]

    hi = h2.astype(BF16)
    h2_ref[rows, :] = hi
    lo = (h2 - hi.astype(F32)).astype(BF16)
    wr_hi = wr_hi_ref[...]
    logits = (jnp.dot(hi, wr_hi, preferred_element_type=F32)
              + jnp.dot(lo, wr_hi, preferred_element_type=F32)
              + jnp.dot(hi, wr_lo_ref[...], preferred_element_type=F32))
    scores = _sigmoid(logits.T[0:N_EXP, :])
    sel = scores + rb_ref[...]

    gscore = []
    for g in range(N_GRP):
        r = [sel[g * EXP_PER_GRP + j:g * EXP_PER_GRP + j + 1, :] for j in range(EXP_PER_GRP)]
        pairs = [r[a] + r[b] for a in range(EXP_PER_GRP) for b in range(a + 1, EXP_PER_GRP)]
        gscore.append(functools.reduce(jnp.maximum, pairs))
    best = functools.reduce(jnp.maximum, gscore)
    gsel = jnp.full(best.shape, N_GRP - 1, I32)
    for g in range(N_GRP - 2, -1, -1):
        gsel = jnp.where(gscore[g] == best, g, gsel)

    eidx = lax.broadcasted_iota(I32, (N_EXP, n), 0)
    neg = jnp.float32(-jnp.inf)
    cand = jnp.where((eidx // EXP_PER_GRP) == gsel, sel, neg)
    top1 = jnp.max(cand, axis=0, keepdims=True)
    idx1 = jnp.min(jnp.where(cand == top1, eidx, N_EXP), axis=0, keepdims=True)
    cand2 = jnp.where(eidx == idx1, neg, cand)
    top2 = jnp.max(cand2, axis=0, keepdims=True)
    idx2 = jnp.min(jnp.where(cand2 == top2, eidx, N_EXP), axis=0, keepdims=True)
    is1 = eidx == idx1
    is2 = eidx == idx2
    w1 = jnp.sum(jnp.where(is1, scores, 0.0), axis=0, keepdims=True)
    w2 = jnp.sum(jnp.where(is2, scores, 0.0), axis=0, keepdims=True)
    wsum = w1 + w2
    w1 = w1 / wsum
    w2 = w2 / wsum

    onehot = jnp.where(jnp.logical_or(is1, is2), 1.0, 0.0)
    prefix = jnp.dot(onehot.astype(BF16), upper, preferred_element_type=F32) + carry[:, 0:1]
    rank1 = jnp.sum(jnp.where(is1, prefix, 0.0), axis=0, keepdims=True)
    rank2 = jnp.sum(jnp.where(is2, prefix, 0.0), axis=0, keepdims=True)

    route_ref[0:1, rows] = idx1
    route_ref[1:2, rows] = idx2
    route_ref[2:3, rows] = rank1.astype(I32)
    route_ref[3:4, rows] = rank2.astype(I32)
    wrow = lax.broadcasted_iota(I32, (LANE, n), 0)
    wcols = jnp.where(wrow == 0, w1, jnp.where(wrow == 1, w2, 0.0))
    rw_ref[rows, :] = wcols.T
    return carry + jnp.sum(onehot, axis=1, keepdims=True)


def _post(ya_pre, yb_pre, o, gates, x, mod_l, wa, wb, wc, wo, ln_g, ln_b, wr_hi, wr_lo, rbias, l):
    row = lambda m: (m, 0)
    const = lambda m: (0, 0)
    wspec = pl.BlockSpec((None, D, D), lambda m: (l, 0, 0))
    return pl.pallas_call(
        _post_kernel,
        out_shape=(jax.ShapeDtypeStruct((T, D), F32),
                   jax.ShapeDtypeStruct((T, D), BF16),
                   jax.ShapeDtypeStruct((N_TILES, 8, TM), I32),
                   jax.ShapeDtypeStruct((T, LANE), F32),
                   jax.ShapeDtypeStruct((N_EXP, LANE), I32)),
        grid=(N_TILES,),
        in_specs=[pl.BlockSpec((TM, D), row), pl.BlockSpec((TM, D), row), pl.BlockSpec((TM, D), row),
                  pl.BlockSpec((TM, 3 * D), row), pl.BlockSpec((TM, D), row),
                  pl.BlockSpec((None, N_MOD, D), lambda m: (_cond_row(m, CTX_TILES, TILES_PER_LAT), 0, 0)),
                  wspec, wspec, wspec, wspec,
                  pl.BlockSpec((1, D), const), pl.BlockSpec((1, D), const),
                  pl.BlockSpec((D, LANE), const), pl.BlockSpec((D, LANE), const),
                  pl.BlockSpec((N_EXP, 1), const)],
        out_specs=(pl.BlockSpec((TM, D), row), pl.BlockSpec((TM, D), row),
                   pl.BlockSpec((None, 8, TM), lambda m: (m, 0, 0)),
                   pl.BlockSpec((TM, LANE), row),
                   pl.BlockSpec((N_EXP, LANE), const)),
        scratch_shapes=[pltpu.VMEM((N_EXP, LANE), F32)],
        compiler_params=_params(("arbitrary",)),
        name="merge_ln_router",
    )(ya_pre, yb_pre, o, gates, x, mod_l, wa, wb, wc, wo, ln_g, ln_b, wr_hi, wr_lo, rbias)


def _row_copy(src, dst, src_row, dst_row, sem):
    return pltpu.make_async_copy(src.at[pl.ds(pl.multiple_of(src_row * ROW_TILE, ROW_TILE), ROW_TILE), :],
                                 dst.at[pl.ds(pl.multiple_of(dst_row * ROW_TILE, ROW_TILE), ROW_TILE), :], sem)


DMA_UNROLL = 4


RUN_CHUNK = 256


def _run_copies(src, dst, src_row, dst_row, n, sem, start):
    def piece(offset, size):
        cp = pltpu.make_async_copy(
            src.at[pl.ds(pl.multiple_of((src_row + offset) * ROW_TILE, ROW_TILE), size * ROW_TILE), :],
            dst.at[pl.ds(pl.multiple_of((dst_row + offset) * ROW_TILE, ROW_TILE), size * ROW_TILE), :], sem)
        if start:
            cp.start()
        else:
            cp.wait()

    def whole(j, carry):
        piece(j * RUN_CHUNK, RUN_CHUNK)
        return carry

    lax.fori_loop(0, n // RUN_CHUNK, whole, 0)
    size = RUN_CHUNK // 2
    while size >= 1:
        @pl.when((n & size) != 0)
        def _(size=size):
            piece(n & ~(2 * size - 1), size)
        size //= 2


def _dispatch_kernel(cnt_ref, off_ref, dst_ref, last_ref, lp_ref, h2_ref, xs_hbm, xc_ref, zero_ref, sem, zsem):
    m = pl.program_id(0)

    @pl.when(m == 0)
    def _():
        zero_ref[...] = jnp.zeros_like(zero_ref)

        def pad_copy(e):
            row = pl.multiple_of(last_ref[e] * ROW_TILE, FFN_BLK * ROW_TILE)
            return pltpu.make_async_copy(zero_ref, xs_hbm.at[pl.ds(row, FFN_BLK * ROW_TILE), :], zsem)

        def zero_start(e, carry):
            pad_copy(e).start()
            return carry

        def zero_wait(e, carry):
            pad_copy(e).wait()
            return carry

        lax.fori_loop(0, N_EXP, zero_start, 0)
        lax.fori_loop(0, N_EXP, zero_wait, 0)

    r_i = lax.broadcasted_iota(I32, (2 * TM, TM), 0)
    hit = jnp.logical_or(r_i == lp_ref[0:1, :], r_i == lp_ref[1:2, :])
    perm = jnp.where(hit, 1.0, 0.0).astype(BF16)
    _store_rows(xc_ref, jnp.dot(perm, h2_ref[...], preferred_element_type=F32))

    for start in (True, False):
        for e in range(N_EXP):
            j = m * N_EXP + e
            _run_copies(xc_ref, xs_hbm, off_ref[j], dst_ref[j], cnt_ref[j], sem, start)


def _dispatch(tile_cnt, tile_off, tile_dst, last_blk, lp, h2):
    grid_spec = pltpu.PrefetchScalarGridSpec(
        num_scalar_prefetch=4,
        grid=(N_TILES,),
        in_specs=[pl.BlockSpec((None, 2, TM), lambda m, *_: (m, 0, 0)),
                  pl.BlockSpec((TM, D), lambda m, *_: (m, 0))],
        out_specs=pl.BlockSpec(memory_space=pl.ANY),
        scratch_shapes=[pltpu.VMEM((2 * TM * ROW_TILE, LANE), F32),
                        pltpu.VMEM((FFN_BLK * ROW_TILE, LANE), F32),
                        pltpu.SemaphoreType.DMA, pltpu.SemaphoreType.DMA],
    )
    return pl.pallas_call(
        _dispatch_kernel,
        out_shape=jax.ShapeDtypeStruct((N_SLOTS * ROW_TILE, LANE), F32),
        grid_spec=grid_spec,
        compiler_params=_params(("arbitrary",)),
        name="moe_dispatch",
    )(tile_cnt, tile_off, tile_dst, last_blk, lp, h2)


def _ffn_kernel(blk_row_ref, blk_exp_ref, n_act_ref, x_ref, wgu_ref, wd_ref, y_ref, wgu_s, wd_s):
    del blk_row_ref
    i = pl.program_id(0)
    new_expert = jnp.logical_or(i == 0, blk_exp_ref[i] != blk_exp_ref[jnp.maximum(i - 1, 0)])
    _cast_weights(new_expert, [(wgu_ref, wgu_s), (wd_ref, wd_s)])

    @pl.when(i < n_act_ref[0])
    def _():
        x = _load_rows(x_ref, FFN_BLK).astype(BF16)
        gu = jnp.dot(x, wgu_s[...], preferred_element_type=F32)
        gate = gu[:, :D_EXP]
        up = gu[:, D_EXP:]
        act = (gate * _sigmoid(gate) * up).astype(BF16)
        _store_rows(y_ref, jnp.dot(act, wd_s[...], preferred_element_type=F32))


def _expert_ffn(blk_row, blk_exp, n_act, xs, wgu, wd, l):
    grid_spec = pltpu.PrefetchScalarGridSpec(
        num_scalar_prefetch=3,
        grid=(N_FFN_BLOCKS,),
        in_specs=[pl.BlockSpec((FFN_BLK * ROW_TILE, LANE), lambda i, br, be, na: (br[i], 0)),
                  pl.BlockSpec((None, None, D, 2 * D_EXP), lambda i, br, be, na: (l, be[i], 0, 0)),
                  pl.BlockSpec((None, None, D_EXP, D), lambda i, br, be, na: (l, be[i], 0, 0))],
        out_specs=pl.BlockSpec((FFN_BLK * ROW_TILE, LANE), lambda i, br, be, na: (br[i], 0)),
        scratch_shapes=[pltpu.VMEM((D, 2 * D_EXP), BF16), pltpu.VMEM((D_EXP, D), BF16)],
    )
    return pl.pallas_call(
        _ffn_kernel,
        out_shape=jax.ShapeDtypeStruct((N_SLOTS * ROW_TILE, LANE), F32),
        grid_spec=grid_spec,
        compiler_params=_params(("arbitrary",)),
        name="expert_ffn",
    )(blk_row, blk_exp, n_act, xs, wgu, wd)


def _block_plan(counts):
    nblk = (counts + FFN_BLK - 1) // FFN_BLK
    end = jnp.cumsum(nblk)
    start = end - nblk
    n_act = end[-1]
    i = jnp.arange(N_FFN_BLOCKS, dtype=I32)
    i_eff = jnp.minimum(i, n_act - 1)
    e = jnp.minimum(jnp.sum(i_eff[:, None] >= end[None, :], axis=1), N_EXP - 1).astype(I32)
    last_blk = jnp.minimum(start + jnp.maximum(nblk - 1, 0), N_FFN_BLOCKS - 1) * FFN_BLK
    return ((start * FFN_BLK).astype(I32), last_blk.astype(I32), i_eff.astype(I32), e,
            n_act.reshape(1).astype(I32))


def _tile_plan(route, seg_start):
    is_exp = route[:, 0:2, :, None] == jnp.arange(N_EXP, dtype=I32)
    cnt = jnp.sum(is_exp, axis=(1, 2), dtype=I32)
    before = jnp.cumsum(cnt, axis=0) - cnt
    off = jnp.cumsum(cnt, axis=1) - cnt
    dst = seg_start[None, :] + before
    pick = lambda tab: jnp.sum(jnp.where(is_exp, tab[:, None, None, :], 0), axis=-1)
    rank = route[:, 2:4, :]
    local = pick(off - before) + rank
    slots = pick(jnp.broadcast_to(seg_start, cnt.shape)) + rank
    return cnt.reshape(-1), off.reshape(-1), dst.reshape(-1), local, slots


def _combine_kernel(final, slot_ref, slot_next_ref, y_hbm, rw_ref, x1_ref, mod_ref, modn_ref,
                    lng_ref, lnb_ref, out_a_ref, out_b_ref, buf, sems):
    m = pl.program_id(0)
    half = m % 2

    def gather_tile(s_ref, dst_half):
        def body(i, carry):
            for u in range(DMA_UNROLL):
                t = i * DMA_UNROLL + u
                for k in range(2):
                    _row_copy(y_hbm, buf.at[dst_half, k], s_ref[k, t], t, sems.at[dst_half]).start()
            return carry

        lax.fori_loop(0, TM // DMA_UNROLL, body, 0)

    @pl.when(m == 0)
    def _():
        gather_tile(slot_ref, 0)

    @pl.when(m + 1 < pl.num_programs(0))
    def _():
        gather_tile(slot_next_ref, 1 - half)

    def drain(i, carry):
        for _ in range(2 * DMA_UNROLL):
            _row_copy(y_hbm, buf.at[half, 0], 0, 0, sems.at[half]).wait()
        return carry

    lax.fori_loop(0, TM // DMA_UNROLL, drain, 0)
    f = rw_ref[:, 0:1] * _load_rows(buf.at[half, 0], TM) + rw_ref[:, 1:2] * _load_rows(buf.at[half, 1], TM)
    x2 = _layer_norm(ALPHA * x1_ref[...] + mod_ref[5:6, :] * f, lng_ref[...], lnb_ref[...])
    if final:
        @pl.when(m < CTX_TILES)
        def _():
            out_a_ref[...] = x2

        @pl.when(m >= CTX_TILES)
        def _():
            out_b_ref[...] = x2
    else:
        out_a_ref[...] = x2
        out_b_ref[...] = (x2 * (1.0 + modn_ref[1:2, :]) + modn_ref[0:1, :]).astype(BF16)


def _combine(slots, y_slots, rw, x1, mod_l, mod_next, ln_g, ln_b, final):
    row = lambda m: (m, 0)
    const = lambda m: (0, 0)
    mod_spec = pl.BlockSpec((None, N_MOD, D), lambda m: (_cond_row(m, CTX_TILES, TILES_PER_LAT), 0, 0))
    if final:
        out_shape = (jax.ShapeDtypeStruct((T_CTX, D), F32), jax.ShapeDtypeStruct((T_LAT, D), F32))
        out_specs = (pl.BlockSpec((TM, D), lambda m: (jnp.minimum(m, CTX_TILES - 1), 0)),
                     pl.BlockSpec((TM, D), lambda m: (jnp.maximum(m - CTX_TILES, 0), 0)))
    else:
        out_shape = (jax.ShapeDtypeStruct((T, D), F32), jax.ShapeDtypeStruct((T, D), BF16))
        out_specs = (pl.BlockSpec((TM, D), row), pl.BlockSpec((TM, D), row))
    return pl.pallas_call(
        functools.partial(_combine_kernel, final),
        out_shape=out_shape,
        grid=(N_TILES,),
        in_specs=[pl.BlockSpec((None, 2, TM), lambda m: (m, 0, 0), memory_space=pltpu.SMEM),
                  pl.BlockSpec((None, 2, TM), lambda m: (jnp.minimum(m + 1, N_TILES - 1), 0, 0),
                               memory_space=pltpu.SMEM),
                  pl.BlockSpec(memory_space=pl.ANY),
                  pl.BlockSpec((TM, LANE), row), pl.BlockSpec((TM, D), row),
                  mod_spec, mod_spec,
                  pl.BlockSpec((1, D), const), pl.BlockSpec((1, D), const)],
        out_specs=out_specs,
        scratch_shapes=[pltpu.VMEM((2, 2, TM * ROW_TILE, LANE), F32), pltpu.SemaphoreType.DMA((2,))],
        compiler_params=_params(("arbitrary",)),
        name="moe_combine_ln",
    )(slots, slots, y_slots, rw, x1, mod_l, mod_next, ln_g, ln_b)


def kernel(x_prompt, x_sample, c, c_ctx, cache_k, cache_v, w_in, conv_a_w, w_a_out, conv_b_w, conv_b_b,
           norm_b_g, norm_b_b, w_b_out, q_norm_g, k_norm_g, w_c_out, w_o, w_ada, b_ada, ln_g, ln_b,
           w_router, router_bias, w_gate_up, w_down):
    x = jnp.concatenate([x_prompt.reshape(T_CTX, D), x_sample.reshape(T_LAT, D)], axis=0)
    cond16 = jnp.concatenate([c_ctx[None, :], c, jnp.zeros((16 - 1 - N_LAT_SEQ, D), F32)], axis=0)
    mod = _modulation(cond16, w_ada, b_ada).reshape(DEPTH, 16, N_MOD, D)

    wa_b, wb_b, wc_b, wo_b = (w.astype(BF16) for w in (w_a_out, w_b_out, w_c_out, w_o))
    wr = jnp.pad(w_router, ((0, 0), (0, LANE - N_EXP)))
    wr_hi = wr.astype(BF16)
    wr_lo = (wr - wr_hi.astype(F32)).astype(BF16)
    rbias = router_bias.reshape(N_EXP, 1)

    cos_tab, sin_tab = _rope_tables()
    half = np.arange(LANE) // HEAD
    ones_bd = jnp.asarray(half[:, None] == half[None, :], BF16)
    ck = cache_k.reshape(N_LAT_SEQ, DEPTH, PAST, D_KV)
    cv = cache_v.reshape(N_LAT_SEQ, DEPTH, PAST, D_KV)

    h = _modulate(x, mod[0])
    new_k, new_v = [], []
    for l in range(DEPTH):
        ab, acx = _proj_a(h, w_in, l)
        glu = _proj_b(h, w_in, l)
        gates = _proj_gates(h, w_in, l)
        q, katt, vatt, kn, vf = _proj_qkv(h, w_in, jnp.tile(q_norm_g[l], N_Q)[None, :],
                                          jnp.tile(k_norm_g[l], N_KV)[None, :], cos_tab, sin_tab, ones_bd, l)
        new_k.append(kn.reshape(N_CTX_SEQ, CTX_LEN, N_KV, HEAD))
        new_v.append(vf.reshape(N_CTX_SEQ, CTX_LEN, N_KV, HEAD))
        ya_pre, yb_pre = _convs(acx, ab, glu, conv_a_w[l], conv_b_w[l], conv_b_b[l][None, :],
                                norm_b_g[l][None, :], norm_b_b[l][None, :])
        o = _attention(q, katt, vatt, ck, cv, l)
        x1, h2, route, rw, counts = _post(ya_pre, yb_pre, o, gates, x, mod[l], wa_b, wb_b, wc_b, wo_b,
                                          ln_g[l, 0][None, :], ln_b[l, 0][None, :], wr_hi, wr_lo, rbias, l)
        cnt = counts[:, 0]
        seg_start, last_blk, blk_row, blk_exp, n_act = _block_plan(cnt)
        tile_cnt, tile_off, tile_dst, local, slots = _tile_plan(route, seg_start)
        xs = _dispatch(tile_cnt, tile_off, tile_dst, last_blk, local, h2)
        y_slots = _expert_ffn(blk_row, blk_exp, n_act, xs, w_gate_up, w_down, l)
        x, h = _combine(slots, y_slots, rw, x1, mod[l], mod[min(l + 1, DEPTH - 1)],
                        ln_g[l, 1][None, :], ln_b[l, 1][None, :], final=l == DEPTH - 1)
    y_prompt = x.reshape(N_CTX_SEQ, CTX_LEN, D)
    y_sample = h.reshape(N_LAT_SEQ, LAT_LEN, D)
    return y_prompt, y_sample, jnp.stack(new_k, axis=1), jnp.stack(new_v, axis=1)
```

```python
import functools

import numpy as np
import jax
import jax.numpy as jnp
from jax import lax
from jax.experimental import pallas as pl
from jax.experimental.pallas import tpu as pltpu

F32 = jnp.float32
BF16 = jnp.bfloat16
I32 = jnp.int32

D = 1024
DEPTH = 4
N_CTX_SEQ = 16
CTX_LEN = 256
N_LAT_SEQ = 8
LAT_LEN = 1024
PAST = 512
T_CTX = N_CTX_SEQ * CTX_LEN
T_LAT = N_LAT_SEQ * LAT_LEN
T = T_CTX + T_LAT
GRID_W = 64
HEAD = 64
N_Q = 16
N_KV = 4
GROUP = 4
D_KV = N_KV * HEAD
N_EXP = 16
N_GRP = 4
EXP_PER_GRP = 4
D_EXP = 512
IN_COLS = 9728
N_MOD = 6
ALPHA = (2 * DEPTH) ** 0.25
LN_EPS = 1e-5
RMS_EPS = 1e-6
ROPE_THETA = 10000.0

LANE = 128
TM = 512
N_TILES = T // TM
CTX_TILES = T_CTX // TM
TILES_PER_LAT = LAT_LEN // TM
POST_ROWS = 256
TMP = 1024
SEG = 256
N_SEG = T // SEG
CTX_SEGS = T_CTX // SEG
SEGS_PER_LAT = LAT_LEN // SEG
HALO_A = 8
HALO_B = 16
SH_ROWS = SEG + 2 * HALO_B - 8
FFN_BLK = 256
N_FFN_BLOCKS = (2 * T) // FFN_BLK + N_EXP
N_SLOTS = N_FFN_BLOCKS * FFN_BLK
VMEM_LIMIT = 56 * 1024 * 1024

COL_AB, COL_AC, COL_AX, COL_BU, COL_BG, COL_Q, COL_KV, COL_GATES = 0, 2, 4, 6, 8, 10, 12, 13


def _params(sem):
    return pltpu.CompilerParams(dimension_semantics=sem, vmem_limit_bytes=VMEM_LIMIT)


def _cond_row(m, tiles_ctx, tiles_per_lat):
    return jnp.where(m < tiles_ctx, 0, 1 + (m - tiles_ctx) // tiles_per_lat)


def _layer_norm(x, g, b):
    mu = jnp.mean(x, axis=-1, keepdims=True)
    xc = x - mu
    var = jnp.mean(xc * xc, axis=-1, keepdims=True)
    return xc * lax.rsqrt(var + LN_EPS) * g + b


def _sigmoid(x):
    return 1.0 / (1.0 + jnp.exp(-x))


ROW_TILE = D // LANE


def _store_rows(ref, x):
    for j in range(ROW_TILE):
        ref[pl.ds(j, x.shape[0], stride=ROW_TILE), :] = x[:, j * LANE:(j + 1) * LANE]


def _load_rows(ref, n_rows):
    return jnp.concatenate([ref[pl.ds(j, n_rows, stride=ROW_TILE), :] for j in range(ROW_TILE)], axis=1)


def _mod_kernel(cond_ref, w_ref, b_ref, o_ref):
    cnd = cond_ref[...]
    s = (cnd * _sigmoid(cnd)).astype(BF16)
    o_ref[...] = jnp.dot(s, w_ref[...].astype(BF16), preferred_element_type=F32) + b_ref[...]


def _modulation(cond16, w_ada, b_ada):
    n_col = N_MOD * D
    tn = 1024
    return pl.pallas_call(
        _mod_kernel,
        out_shape=jax.ShapeDtypeStruct((DEPTH, 16, n_col), F32),
        grid=(DEPTH, n_col // tn),
        in_specs=[
            pl.BlockSpec((16, D), lambda l, n: (0, 0)),
            pl.BlockSpec((None, D, tn), lambda l, n: (l, 0, n)),
            pl.BlockSpec((None, 1, tn), lambda l, n: (l, 0, n)),
        ],
        out_specs=pl.BlockSpec((None, 16, tn), lambda l, n: (l, 0, n)),
        compiler_params=_params(("parallel", "parallel")),
        name="adaln_mod",
    )(cond16, w_ada, b_ada.reshape(DEPTH, 1, n_col))


def _modulate_kernel(x_ref, mod_ref, h_ref):
    h_ref[...] = (x_ref[...] * (1.0 + mod_ref[1:2, :]) + mod_ref[0:1, :]).astype(BF16)


def _modulate(x, mod_l):
    return pl.pallas_call(
        _modulate_kernel,
        out_shape=jax.ShapeDtypeStruct((T, D), BF16),
        grid=(N_TILES,),
        in_specs=[
            pl.BlockSpec((TM, D), lambda m: (m, 0)),
            pl.BlockSpec((None, N_MOD, D), lambda m: (_cond_row(m, CTX_TILES, TILES_PER_LAT), 0, 0)),
        ],
        out_specs=pl.BlockSpec((TM, D), lambda m: (m, 0)),
        compiler_params=_params(("parallel",)),
        name="modulate",
    )(x, mod_l)


WCOL = 512


def _cast_weights(first, pairs):
    @pl.when(first)
    def _():
        for src, dst in pairs:
            dst[...] = src[...].astype(BF16)


def _proj_a_kernel(h_ref, wb_ref, wc_ref, wx_ref, ab_ref, acx_ref, wb_s, wc_s, wx_s):
    _cast_weights(pl.program_id(1) == 0, [(wb_ref, wb_s), (wc_ref, wc_s), (wx_ref, wx_s)])
    h = h_ref[...]
    ab_ref[...] = jnp.dot(h, wb_s[...], preferred_element_type=F32).astype(BF16)
    acx_ref[...] = (jnp.dot(h, wc_s[...], preferred_element_type=F32)
                    * jnp.dot(h, wx_s[...], preferred_element_type=F32))


def _proj_b_kernel(h_ref, wu_ref, wg_ref, glu_ref, wu_s, wg_s):
    _cast_weights(pl.program_id(1) == 0, [(wu_ref, wu_s), (wg_ref, wg_s)])
    h = h_ref[...]
    glu_ref[...] = (jnp.dot(h, wu_s[...], preferred_element_type=F32)
                    * _sigmoid(jnp.dot(h, wg_s[...], preferred_element_type=F32)))


def _proj_gate_kernel(h_ref, w0_ref, w1_ref, g_ref, w_s):
    _cast_weights(pl.program_id(1) == 0, [(w0_ref, w_s.at[:, 0:WCOL]), (w1_ref, w_s.at[:, WCOL:2 * WCOL])])
    g_ref[...] = _sigmoid(jnp.dot(h_ref[...], w_s[...], preferred_element_type=F32)).astype(BF16)


def _w_spec(l, col0, step=1):
    return pl.BlockSpec((None, D, WCOL), lambda c, m: (l, 0, col0 + step * c))


def _proj_a(h, w_in, l):
    out_spec = pl.BlockSpec((TMP, WCOL), lambda c, m: (m, c))
    return pl.pallas_call(
        _proj_a_kernel,
        out_shape=(jax.ShapeDtypeStruct((T, D), BF16), jax.ShapeDtypeStruct((T, D), F32)),
        grid=(D // WCOL, T // TMP),
        in_specs=[pl.BlockSpec((TMP, D), lambda c, m: (m, 0)),
                  _w_spec(l, COL_AB), _w_spec(l, COL_AC), _w_spec(l, COL_AX)],
        out_specs=(out_spec, out_spec),
        scratch_shapes=[pltpu.VMEM((D, WCOL), BF16)] * 3,
        compiler_params=_params(("parallel", "arbitrary")),
        name="proj_a",
    )(h, w_in, w_in, w_in)


def _proj_b(h, w_in, l):
    return pl.pallas_call(
        _proj_b_kernel,
        out_shape=jax.ShapeDtypeStruct((T, D), F32),
        grid=(D // WCOL, T // TMP),
        in_specs=[pl.BlockSpec((TMP, D), lambda c, m: (m, 0)), _w_spec(l, COL_BU), _w_spec(l, COL_BG)],
        out_specs=pl.BlockSpec((TMP, WCOL), lambda c, m: (m, c)),
        scratch_shapes=[pltpu.VMEM((D, WCOL), BF16)] * 2,
        compiler_params=_params(("parallel", "arbitrary")),
        name="proj_b",
    )(h, w_in, w_in)


def _proj_gates(h, w_in, l):
    tn = 2 * WCOL
    return pl.pallas_call(
        _proj_gate_kernel,
        out_shape=jax.ShapeDtypeStruct((T, 3 * D), BF16),
        grid=(3 * D // tn, T // TMP),
        in_specs=[pl.BlockSpec((TMP, D), lambda c, m: (m, 0)),
                  _w_spec(l, COL_GATES, 2), _w_spec(l, COL_GATES + 1, 2)],
        out_specs=pl.BlockSpec((TMP, tn), lambda c, m: (m, c)),
        scratch_shapes=[pltpu.VMEM((D, tn), BF16)],
        compiler_params=_params(("parallel", "arbitrary")),
        name="proj_gates",
    )(h, w_in, w_in)


def _head_mean_square(x, ones_bd):
    out = []
    for c in range(x.shape[1] // LANE):
        sq = x[:, c * LANE:(c + 1) * LANE]
        out.append(jnp.dot((sq * sq).astype(BF16), ones_bd, preferred_element_type=F32))
    return jnp.concatenate(out, axis=1) * (1.0 / HEAD)


def _rope(x, cos, sin, first_half):
    out = []
    for c in range(x.shape[1] // LANE):
        xc = x[:, c * LANE:(c + 1) * LANE]
        partner = jnp.where(first_half, pltpu.roll(xc, LANE - 16, axis=1), pltpu.roll(xc, 16, axis=1))
        out.append(xc * cos + partner * sin)
    return jnp.concatenate(out, axis=1)


def _qkv_kernel(h_ref, wq_ref, wkv_ref, gq_ref, gk_ref, cos_ref, sin_ref, ones_ref,
                q_ref, katt_ref, vatt_ref, kn_ref, vf_ref, wq_s, wkv_s):
    _cast_weights(pl.program_id(0) == 0, [(wq_ref, wq_s), (wkv_ref, wkv_s)])
    h = h_ref[...]
    ones_bd = ones_ref[...]
    cos = cos_ref[...]
    sin = sin_ref[...]
    lane = lax.broadcasted_iota(I32, (TM, LANE), 1)
    first_half = (lane & 16) == 0
    q = jnp.dot(h, wq_s[...], preferred_element_type=F32)
    qn = q * lax.rsqrt(_head_mean_square(q, ones_bd) + RMS_EPS) * gq_ref[...]
    q_ref[...] = (_rope(qn, cos, sin, first_half) * (HEAD ** -0.5)).astype(BF16)
    kv = jnp.dot(h, wkv_s[...], preferred_element_type=F32)
    k = kv[:, :D_KV]
    v = kv[:, D_KV:]
    kn = k * lax.rsqrt(_head_mean_square(k, ones_bd) + RMS_EPS) * gk_ref[...]
    katt_ref[...] = _rope(kn, cos, sin, first_half).astype(BF16)
    vatt_ref[...] = v.astype(BF16)

    @pl.when(pl.program_id(0) < CTX_TILES)
    def _():
        kn_ref[...] = kn
        vf_ref[...] = v


def _proj_qkv(h, w_in, gq, gk, cos_tab, sin_tab, ones_bd, l):
    def tab_idx(m):
        return jnp.where(m < CTX_TILES, TILES_PER_LAT, (m - CTX_TILES) % TILES_PER_LAT)

    row = lambda m: (m, 0)
    ctx_row = lambda m: (jnp.minimum(m, CTX_TILES - 1), 0)
    return pl.pallas_call(
        _qkv_kernel,
        out_shape=(jax.ShapeDtypeStruct((T, D), BF16),
                   jax.ShapeDtypeStruct((T, D_KV), BF16),
                   jax.ShapeDtypeStruct((T, D_KV), BF16),
                   jax.ShapeDtypeStruct((T_CTX, D_KV), F32),
                   jax.ShapeDtypeStruct((T_CTX, D_KV), F32)),
        grid=(N_TILES,),
        in_specs=[
            pl.BlockSpec((TM, D), row),
            pl.BlockSpec((None, D, D), lambda m: (l, 0, COL_Q // 2)),
            pl.BlockSpec((None, D, 2 * D_KV), lambda m: (l, 0, COL_KV)),
            pl.BlockSpec((1, D), lambda m: (0, 0)),
            pl.BlockSpec((1, D_KV), lambda m: (0, 0)),
            pl.BlockSpec((TM, LANE), lambda m: (tab_idx(m), 0)),
            pl.BlockSpec((TM, LANE), lambda m: (tab_idx(m), 0)),
            pl.BlockSpec((LANE, LANE), lambda m: (0, 0)),
        ],
        out_specs=(pl.BlockSpec((TM, D), row), pl.BlockSpec((TM, D_KV), row),
                   pl.BlockSpec((TM, D_KV), row), pl.BlockSpec((TM, D_KV), ctx_row),
                   pl.BlockSpec((TM, D_KV), ctx_row)),
        scratch_shapes=[pltpu.VMEM((D, D), BF16), pltpu.VMEM((D, 2 * D_KV), BF16)],
        compiler_params=_params(("arbitrary",)),
        name="proj_qkv",
    )(h, w_in, w_in, gq, gk, cos_tab, sin_tab, ones_bd)


def _rope_tables():
    lane = np.arange(LANE)
    j = lane % 16
    freqs = jnp.power(ROPE_THETA, -jnp.arange(16, dtype=F32) / 16)[j]
    pos = jnp.arange(LAT_LEN, dtype=I32)
    row = (pos // GRID_W).astype(F32)
    col = (pos % GRID_W).astype(F32)
    use_row = jnp.asarray((lane % HEAD) < HEAD // 2)
    p = jnp.where(use_row[None, :], row[:, None], col[:, None])
    ang = p * freqs[None, :]
    sign = jnp.asarray(np.where((lane & 16) == 0, -1.0, 1.0), F32)
    cos = jnp.concatenate([jnp.cos(ang), jnp.ones((TM, LANE), F32)], axis=0)
    sin = jnp.concatenate([jnp.sin(ang) * sign[None, :], jnp.zeros((TM, LANE), F32)], axis=0)
    return cos, sin


def _conv_kernel(acx_ref, acx_l_ref, acx_r_ref, ab_ref, glu_ref, glu_l_ref, glu_r_ref,
                 wa_ref, wb_ref, bb_ref, ng_ref, nb_ref, ya_ref, yb_ref, pad_a, pad_b, u_ref, sh_ref):
    s = pl.program_id(0)
    lat = s >= CTX_SEGS
    pos = (s - CTX_SEGS) % SEGS_PER_LAT
    has_left = jnp.logical_and(lat, pos != 0)
    has_right = jnp.logical_and(lat, pos != SEGS_PER_LAT - 1)

    pad_a[0:HALO_A, :] = jnp.where(has_left, acx_l_ref[...], 0.0)
    pad_a[HALO_A:HALO_A + SEG, :] = acx_ref[...]
    pad_a[HALO_A + SEG:, :] = jnp.where(has_right, acx_r_ref[...], 0.0)
    pad_b[0:HALO_B, :] = jnp.where(has_left, glu_l_ref[...], 0.0)
    pad_b[HALO_B:HALO_B + SEG, :] = glu_ref[...]
    pad_b[HALO_B + SEG:, :] = jnp.where(has_right, glu_r_ref[...], 0.0)

    conv_a = (wa_ref[0:1, :] * pad_a[HALO_A - 1:HALO_A - 1 + SEG, :]
              + wa_ref[1:2, :] * pad_a[HALO_A:HALO_A + SEG, :]
              + wa_ref[2:3, :] * pad_a[HALO_A + 1:HALO_A + 1 + SEG, :])
    ya_ref[...] = (ab_ref[...].astype(F32) * conv_a).astype(BF16)

    rows = 64
    kb = wb_ref.shape[0]

    def lane_chunk(c, carry):
        lanes = pl.ds(pl.multiple_of(c * LANE, LANE), LANE)
        for b in range(1, 8):
            sh_ref[b, :, :] = pad_b[b:b + SH_ROWS, lanes]
        for r in range(SEG // rows):
            acc = jnp.zeros((rows, LANE), F32)
            for k in range(kb):
                off = HALO_B + k - kb // 2
                row0 = r * rows + 8 * (off // 8)
                if off % 8 == 0:
                    src = pad_b[row0:row0 + rows, lanes]
                else:
                    src = sh_ref[off % 8, row0:row0 + rows, :]
                acc = acc + wb_ref[k:k + 1, lanes] * src
            u_ref[r * rows:(r + 1) * rows, lanes] = acc
        return carry

    lax.fori_loop(0, D // LANE, lane_chunk, 0)
    u = _layer_norm(u_ref[...] + bb_ref[...], ng_ref[...], nb_ref[...])
    yb_ref[...] = (u * _sigmoid(u)).astype(BF16)


def _convs(acx, ab, glu, conv_a_w, conv_b_w, conv_b_b, norm_g, norm_b):
    seg = lambda s: (s, 0)
    const = lambda s: (0, 0)
    ra, rb = SEG // HALO_A, SEG // HALO_B
    left_a = lambda s: (jnp.maximum(s * ra - 1, 0), 0)
    right_a = lambda s: (jnp.minimum((s + 1) * ra, T // HALO_A - 1), 0)
    left_b = lambda s: (jnp.maximum(s * rb - 1, 0), 0)
    right_b = lambda s: (jnp.minimum((s + 1) * rb, T // HALO_B - 1), 0)
    return pl.pallas_call(
        _conv_kernel,
        out_shape=(jax.ShapeDtypeStruct((T, D), BF16), jax.ShapeDtypeStruct((T, D), BF16)),
        grid=(N_SEG,),
        in_specs=[
            pl.BlockSpec((SEG, D), seg), pl.BlockSpec((HALO_A, D), left_a), pl.BlockSpec((HALO_A, D), right_a),
            pl.BlockSpec((SEG, D), seg),
            pl.BlockSpec((SEG, D), seg), pl.BlockSpec((HALO_B, D), left_b), pl.BlockSpec((HALO_B, D), right_b),
            pl.BlockSpec(conv_a_w.shape, const), pl.BlockSpec(conv_b_w.shape, const),
            pl.BlockSpec((1, D), const), pl.BlockSpec((1, D), const), pl.BlockSpec((1, D), const),
        ],
        out_specs=(pl.BlockSpec((SEG, D), seg), pl.BlockSpec((SEG, D), seg)),
        scratch_shapes=[pltpu.VMEM((SEG + 2 * HALO_A, D), F32),
                        pltpu.VMEM((SEG + 2 * HALO_B, D), F32),
                        pltpu.VMEM((SEG, D), F32),
                        pltpu.VMEM((8, SH_ROWS, LANE), F32)],
        compiler_params=_params(("parallel",)),
        name="convs",
    )(acx, acx, acx, ab, glu, glu, glu, conv_a_w, conv_b_w, conv_b_b, norm_g, norm_b)


def _attend(q_ref, key_refs, val_refs, o_ref, ones_denominator):
    nt = (((1,), (1,)), ((), ()))
    keys = [r[...].astype(BF16) for r in key_refs]
    vals = [r[...].astype(BF16) for r in val_refs]
    n_q = q_ref.shape[0]
    for g in range(N_KV):
        kg = [k[:, g * HEAD:(g + 1) * HEAD] for k in keys]
        vg = [v[:, g * HEAD:(g + 1) * HEAD] for v in vals]
        if ones_denominator:
            vg = [jnp.concatenate([v, jnp.ones_like(v)], axis=1) for v in vg]
        qg = jnp.concatenate([q_ref[:, (g * GROUP + hh) * HEAD:(g * GROUP + hh + 1) * HEAD]
                              for hh in range(GROUP)], axis=0)
        s = [lax.dot_general(qg, k, nt, preferred_element_type=F32) for k in kg]
        mx = functools.reduce(jnp.maximum, [jnp.max(x, axis=-1, keepdims=True) for x in s])
        p = [jnp.exp(x - mx) for x in s]
        acc = functools.reduce(jnp.add, [jnp.dot(x.astype(BF16), v, preferred_element_type=F32)
                                         for x, v in zip(p, vg)])
        if ones_denominator:
            den = acc[:, HEAD:HEAD + 1]
            acc = acc[:, :HEAD]
        else:
            den = functools.reduce(jnp.add, [jnp.sum(x, axis=-1, keepdims=True) for x in p])
        out = (acc / den).astype(BF16)
        for hh in range(GROUP):
            hd = g * GROUP + hh
            o_ref[:, hd * HEAD:(hd + 1) * HEAD] = out[hh * n_q:(hh + 1) * n_q, :]


def _attn_ctx_kernel(q_ref, k_ref, v_ref, o_ref):
    _attend(q_ref, [k_ref], [v_ref], o_ref, ones_denominator=False)


def _attn_lat_kernel(q_ref, k_ref, v_ref, ck_ref, cv_ref, o_in_ref, o_ref):
    del o_in_ref
    _attend(q_ref, [ck_ref, k_ref], [cv_ref, v_ref], o_ref, ones_denominator=True)


def _attention(q, katt, vatt, cache_k, cache_v, l):
    o = pl.pallas_call(
        _attn_ctx_kernel,
        out_shape=jax.ShapeDtypeStruct((T, D), BF16),
        grid=(N_CTX_SEQ,),
        in_specs=[pl.BlockSpec((CTX_LEN, D), lambda b: (b, 0)),
                  pl.BlockSpec((CTX_LEN, D_KV), lambda b: (b, 0)),
                  pl.BlockSpec((CTX_LEN, D_KV), lambda b: (b, 0))],
        out_specs=pl.BlockSpec((CTX_LEN, D), lambda b: (b, 0)),
        compiler_params=_params(("parallel",)),
        name="attn_ctx",
    )(q, katt, vatt)
    seg0 = CTX_SEGS
    lat0 = T_CTX // LAT_LEN
    return pl.pallas_call(
        _attn_lat_kernel,
        out_shape=jax.ShapeDtypeStruct((T, D), BF16),
        grid=(N_LAT_SEQ, SEGS_PER_LAT),
        in_specs=[pl.BlockSpec((SEG, D), lambda b, i: (seg0 + b * SEGS_PER_LAT + i, 0)),
                  pl.BlockSpec((LAT_LEN, D_KV), lambda b, i: (lat0 + b, 0)),
                  pl.BlockSpec((LAT_LEN, D_KV), lambda b, i: (lat0 + b, 0)),
                  pl.BlockSpec((None, None, PAST, D_KV), lambda b, i: (b, l, 0, 0)),
                  pl.BlockSpec((None, None, PAST, D_KV), lambda b, i: (b, l, 0, 0)),
                  pl.BlockSpec(memory_space=pl.ANY)],
        out_specs=pl.BlockSpec((SEG, D), lambda b, i: (seg0 + b * SEGS_PER_LAT + i, 0)),
        input_output_aliases={5: 0},
        compiler_params=_params(("parallel", "parallel")),
        name="attn_lat",
    )(q, katt, vatt, cache_k, cache_v, o)


def _post_kernel(ya_ref, yb_ref, o_ref, g_ref, x_ref, mod_ref, wa_ref, wb_ref, wc_ref, wo_ref,
                 lng_ref, lnb_ref, wr_hi_ref, wr_lo_ref, rb_ref,
                 x1_ref, h2_ref, route_ref, rw_ref, cnt_ref, carry_ref):
    m = pl.program_id(0)

    @pl.when(m == 0)
    def _():
        carry_ref[...] = jnp.zeros_like(carry_ref)

    carry = carry_ref[...]
    r_i = lax.broadcasted_iota(I32, (POST_ROWS, POST_ROWS), 0)
    c_i = lax.broadcasted_iota(I32, (POST_ROWS, POST_ROWS), 1)
    upper = jnp.where(r_i < c_i, 1.0, 0.0).astype(BF16)
    for part in range(TM // POST_ROWS):
        rows = slice(part * POST_ROWS, (part + 1) * POST_ROWS)
        carry = _post_part(rows, carry, upper, ya_ref, yb_ref, o_ref, g_ref, x_ref, mod_ref, wa_ref, wb_ref,
                           wc_ref, wo_ref, lng_ref, lnb_ref, wr_hi_ref, wr_lo_ref, rb_ref,
                           x1_ref, h2_ref, route_ref, rw_ref)
    carry_ref[...] = carry
    cnt_ref[...] = carry.astype(I32)
    route_ref[4:8, :] = jnp.zeros((4, TM), I32)


def _post_part(rows, carry, upper, ya_ref, yb_ref, o_ref, g_ref, x_ref, mod_ref, wa_ref, wb_ref, wc_ref,
               wo_ref, lng_ref, lnb_ref, wr_hi_ref, wr_lo_ref, rb_ref, x1_ref, h2_ref, route_ref, rw_ref):
    n = POST_ROWS
    ya = jnp.dot(ya_ref[rows, :], wa_ref[...], preferred_element_type=F32)
    yb = jnp.dot(yb_ref[rows, :], wb_ref[...], preferred_element_type=F32)
    yc = jnp.dot(o_ref[rows, :], wc_ref[...], preferred_element_type=F32)
    merged = (g_ref[rows, 0:D].astype(F32) * ya + g_ref[rows, D:2 * D].astype(F32) * yb
              + g_ref[rows, 2 * D:3 * D].astype(F32) * yc)
    mix = jnp.dot(merged.astype(BF16), wo_ref[...], preferred_element_type=F32)
    x1 = _layer_norm(ALPHA * x_ref[rows, :] + mod_ref[2:3, :] * mix, lng_ref[...], lnb_ref[...])
    x1_ref[rows, :] = x1
    h2 = x1 * (1.0 + mod_ref[4:5, :]) + mod_ref[3:4, :]

    hi = h2.astype(BF16)
    h2_ref[rows, :] = hi
    lo = (h2 - hi.astype(F32)).astype(BF16)
    wr_hi = wr_hi_ref[...]
    logits = (jnp.dot(hi, wr_hi, preferred_element_type=F32)
              + jnp.dot(lo, wr_hi, preferred_element_type=F32)
              + jnp.dot(hi, wr_lo_ref[...], preferred_element_type=F32))
    scores = _sigmoid(logits.T[0:N_EXP, :])
    sel = scores + rb_ref[...]

    gscore = []
    for g in range(N_GRP):
        r = [sel[g * EXP_PER_GRP + j:g * EXP_PER_GRP + j + 1, :] for j in range(EXP_PER_GRP)]
        pairs = [r[a] + r[b] for a in range(EXP_PER_GRP) for b in range(a + 1, EXP_PER_GRP)]
        gscore.append(functools.reduce(jnp.maximum, pairs))
    best = functools.reduce(jnp.maximum, gscore)
    gsel = jnp.full(best.shape, N_GRP - 1, I32)
    for g in range(N_GRP - 2, -1, -1):
        gsel = jnp.where(gscore[g] == best, g, gsel)

    eidx = lax.broadcasted_iota(I32, (N_EXP, n), 0)
    neg = jnp.float32(-jnp.inf)
    cand = jnp.where((eidx // EXP_PER_GRP) == gsel, sel, neg)
    top1 = jnp.max(cand, axis=0, keepdims=True)
    idx1 = jnp.min(jnp.where(cand == top1, eidx, N_EXP), axis=0, keepdims=True)
    cand2 = jnp.where(eidx == idx1, neg, cand)
    top2 = jnp.max(cand2, axis=0, keepdims=True)
    idx2 = jnp.min(jnp.where(cand2 == top2, eidx, N_EXP), axis=0, keepdims=True)
    is1 = eidx == idx1
    is2 = eidx == idx2
    w1 = jnp.sum(jnp.where(is1, scores, 0.0), axis=0, keepdims=True)
    w2 = jnp.sum(jnp.where(is2, scores, 0.0), axis=0, keepdims=True)
    wsum = w1 + w2
    w1 = w1 / wsum
    w2 = w2 / wsum

    onehot = jnp.where(jnp.logical_or(is1, is2), 1.0, 0.0)
    prefix = jnp.dot(onehot.astype(BF16), upper, preferred_element_type=F32) + carry[:, 0:1]
    rank1 = jnp.sum(jnp.where(is1, prefix, 0.0), axis=0, keepdims=True)
    rank2 = jnp.sum(jnp.where(is2, prefix, 0.0), axis=0, keepdims=True)

    route_ref[0:1, rows] = idx1
    route_ref[1:2, rows] = idx2
    route_ref[2:3, rows] = rank1.astype(I32)
    route_ref[3:4, rows] = rank2.astype(I32)
    wrow = lax.broadcasted_iota(I32, (LANE, n), 0)
    wcols = jnp.where(wrow == 0, w1, jnp.where(wrow == 1, w2, 0.0))
    rw_ref[rows, :] = wcols.T
    return carry + jnp.sum(onehot, axis=1, keepdims=True)


def _post(ya_pre, yb_pre, o, gates, x, mod_l, wa, wb, wc, wo, ln_g, ln_b, wr_hi, wr_lo, rbias, l):
    row = lambda m: (m, 0)
    const = lambda m: (0, 0)
    wspec = pl.BlockSpec((None, D, D), lambda m: (l, 0, 0))
    return pl.pallas_call(
        _post_kernel,
        out_shape=(jax.ShapeDtypeStruct((T, D), F32),
                   jax.ShapeDtypeStruct((T, D), BF16),
                   jax.ShapeDtypeStruct((N_TILES, 8, TM), I32),
                   jax.ShapeDtypeStruct((T, LANE), F32),
                   jax.ShapeDtypeStruct((N_EXP, LANE), I32)),
        grid=(N_TILES,),
        in_specs=[pl.BlockSpec((TM, D), row), pl.BlockSpec((TM, D), row), pl.BlockSpec((TM, D), row),
                  pl.BlockSpec((TM, 3 * D), row), pl.BlockSpec((TM, D), row),
                  pl.BlockSpec((None, N_MOD, D), lambda m: (_cond_row(m, CTX_TILES, TILES_PER_LAT), 0, 0)),
                  wspec, wspec, wspec, wspec,
                  pl.BlockSpec((1, D), const), pl.BlockSpec((1, D), const),
                  pl.BlockSpec((D, LANE), const), pl.BlockSpec((D, LANE), const),
                  pl.BlockSpec((N_EXP, 1), const)],
        out_specs=(pl.BlockSpec((TM, D), row), pl.BlockSpec((TM, D), row),
                   pl.BlockSpec((None, 8, TM), lambda m: (m, 0, 0)),
                   pl.BlockSpec((TM, LANE), row),
                   pl.BlockSpec((N_EXP, LANE), const)),
        scratch_shapes=[pltpu.VMEM((N_EXP, LANE), F32)],
        compiler_params=_params(("arbitrary",)),
        name="merge_ln_router",
    )(ya_pre, yb_pre, o, gates, x, mod_l, wa, wb, wc, wo, ln_g, ln_b, wr_hi, wr_lo, rbias)


def _row_copy(src, dst, src_row, dst_row, sem):
    return pltpu.make_async_copy(src.at[pl.ds(pl.multiple_of(src_row * ROW_TILE, ROW_TILE), ROW_TILE), :],
                                 dst.at[pl.ds(pl.multiple_of(dst_row * ROW_TILE, ROW_TILE), ROW_TILE), :], sem)


DMA_UNROLL = 4


RUN_CHUNK = 256


def _run_copies(src, dst, src_row, dst_row, n, sem, start):
    def piece(offset, size):
        cp = pltpu.make_async_copy(
            src.at[pl.ds(pl.multiple_of((src_row + offset) * ROW_TILE, ROW_TILE), size * ROW_TILE), :],
            dst.at[pl.ds(pl.multiple_of((dst_row + offset) * ROW_TILE, ROW_TILE), size * ROW_TILE), :], sem)
        if start:
            cp.start()
        else:
            cp.wait()

    def whole(j, carry):
        piece(j * RUN_CHUNK, RUN_CHUNK)
        return carry

    lax.fori_loop(0, n // RUN_CHUNK, whole, 0)
    size = RUN_CHUNK // 2
    while size >= 1:
        @pl.when((n & size) != 0)
        def _(size=size):
            piece(n & ~(2 * size - 1), size)
        size //= 2


def _dispatch_kernel(cnt_ref, off_ref, dst_ref, last_ref, lp_ref, h2_ref, xs_hbm, xc_ref, zero_ref, sem, zsem):
    m = pl.program_id(0)

    @pl.when(m == 0)
    def _():
        zero_ref[...] = jnp.zeros_like(zero_ref)

        def pad_copy(e):
            row = pl.multiple_of(last_ref[e] * ROW_TILE, FFN_BLK * ROW_TILE)
            return pltpu.make_async_copy(zero_ref, xs_hbm.at[pl.ds(row, FFN_BLK * ROW_TILE), :], zsem)

        def zero_start(e, carry):
            pad_copy(e).start()
            return carry

        def zero_wait(e, carry):
            pad_copy(e).wait()
            return carry

        lax.fori_loop(0, N_EXP, zero_start, 0)
        lax.fori_loop(0, N_EXP, zero_wait, 0)

    r_i = lax.broadcasted_iota(I32, (2 * TM, TM), 0)
    hit = jnp.logical_or(r_i == lp_ref[0:1, :], r_i == lp_ref[1:2, :])
    perm = jnp.where(hit, 1.0, 0.0).astype(BF16)
    _store_rows(xc_ref, jnp.dot(perm, h2_ref[...], preferred_element_type=F32))

    for start in (True, False):
        for e in range(N_EXP):
            j = m * N_EXP + e
            _run_copies(xc_ref, xs_hbm, off_ref[j], dst_ref[j], cnt_ref[j], sem, start)


def _dispatch(tile_cnt, tile_off, tile_dst, last_blk, lp, h2):
    grid_spec = pltpu.PrefetchScalarGridSpec(
        num_scalar_prefetch=4,
        grid=(N_TILES,),
        in_specs=[pl.BlockSpec((None, 2, TM), lambda m, *_: (m, 0, 0)),
                  pl.BlockSpec((TM, D), lambda m, *_: (m, 0))],
        out_specs=pl.BlockSpec(memory_space=pl.ANY),
        scratch_shapes=[pltpu.VMEM((2 * TM * ROW_TILE, LANE), F32),
                        pltpu.VMEM((FFN_BLK * ROW_TILE, LANE), F32),
                        pltpu.SemaphoreType.DMA, pltpu.SemaphoreType.DMA],
    )
    return pl.pallas_call(
        _dispatch_kernel,
        out_shape=jax.ShapeDtypeStruct((N_SLOTS * ROW_TILE, LANE), F32),
        grid_spec=grid_spec,
        compiler_params=_params(("arbitrary",)),
        name="moe_dispatch",
    )(tile_cnt, tile_off, tile_dst, last_blk, lp, h2)


FFN_AHEAD = 2


def _ffn_kernel(blk_row_ref, blk_exp_ref, n_act_ref, xs_hbm, wgu_ref, wd_ref, y_ref, wgu_s, wd_s, xbuf, sems):
    i = pl.program_id(0)
    n_act = n_act_ref[0]
    new_expert = jnp.logical_or(i == 0, blk_exp_ref[i] != blk_exp_ref[jnp.maximum(i - 1, 0)])
    _cast_weights(new_expert, [(wgu_ref, wgu_s), (wd_ref, wd_s)])
    rows = FFN_BLK * ROW_TILE

    def fetch(b):
        slot = b % (FFN_AHEAD + 1)
        return pltpu.make_async_copy(xs_hbm.at[pl.ds(pl.multiple_of(blk_row_ref[b] * rows, rows), rows), :],
                                     xbuf.at[slot], sems.at[slot])

    for b in range(FFN_AHEAD):
        @pl.when(jnp.logical_and(i == 0, b < n_act))
        def _(b=b):
            fetch(b).start()

    @pl.when(i + FFN_AHEAD < n_act)
    def _():
        fetch(i + FFN_AHEAD).start()

    @pl.when(i < n_act)
    def _():
        fetch(i).wait()
        x = _load_rows(xbuf.at[i % (FFN_AHEAD + 1)], FFN_BLK).astype(BF16)
        gu = jnp.dot(x, wgu_s[...], preferred_element_type=F32)
        gate = gu[:, :D_EXP]
        up = gu[:, D_EXP:]
        act = (gate * _sigmoid(gate) * up).astype(BF16)
        _store_rows(y_ref, jnp.dot(act, wd_s[...], preferred_element_type=F32))


def _expert_ffn(blk_row, blk_exp, n_act, xs, wgu, wd, l):
    grid_spec = pltpu.PrefetchScalarGridSpec(
        num_scalar_prefetch=3,
        grid=(N_FFN_BLOCKS,),
        in_specs=[pl.BlockSpec(memory_space=pl.ANY),
                  pl.BlockSpec((None, None, D, 2 * D_EXP), lambda i, br, be, na: (l, be[i], 0, 0)),
                  pl.BlockSpec((None, None, D_EXP, D), lambda i, br, be, na: (l, be[i], 0, 0))],
        out_specs=pl.BlockSpec((FFN_BLK * ROW_TILE, LANE), lambda i, br, be, na: (br[i], 0)),
        scratch_shapes=[pltpu.VMEM((D, 2 * D_EXP), BF16), pltpu.VMEM((D_EXP, D), BF16),
                        pltpu.VMEM((FFN_AHEAD + 1, FFN_BLK * ROW_TILE, LANE), F32),
                        pltpu.SemaphoreType.DMA((FFN_AHEAD + 1,))],
    )
    return pl.pallas_call(
        _ffn_kernel,
        out_shape=jax.ShapeDtypeStruct((N_SLOTS * ROW_TILE, LANE), F32),
        grid_spec=grid_spec,
        compiler_params=_params(("arbitrary",)),
        name="expert_ffn",
    )(blk_row, blk_exp, n_act, xs, wgu, wd)


def _block_plan(counts):
    nblk = (counts + FFN_BLK - 1) // FFN_BLK
    end = jnp.cumsum(nblk)
    start = end - nblk
    n_act = end[-1]
    i = jnp.arange(N_FFN_BLOCKS, dtype=I32)
    i_eff = jnp.minimum(i, n_act - 1)
    e = jnp.minimum(jnp.sum(i_eff[:, None] >= end[None, :], axis=1), N_EXP - 1).astype(I32)
    last_blk = jnp.minimum(start + jnp.maximum(nblk - 1, 0), N_FFN_BLOCKS - 1) * FFN_BLK
    return ((start * FFN_BLK).astype(I32), last_blk.astype(I32), i_eff.astype(I32), e,
            n_act.reshape(1).astype(I32))


def _tile_plan(route, seg_start):
    is_exp = route[:, 0:2, :, None] == jnp.arange(N_EXP, dtype=I32)
    cnt = jnp.sum(is_exp, axis=(1, 2), dtype=I32)
    before = jnp.cumsum(cnt, axis=0) - cnt
    off = jnp.cumsum(cnt, axis=1) - cnt
    dst = seg_start[None, :] + before
    pick = lambda tab: jnp.sum(jnp.where(is_exp, tab[:, None, None, :], 0), axis=-1)
    rank = route[:, 2:4, :]
    local = pick(off - before) + rank
    slots = pick(jnp.broadcast_to(seg_start, cnt.shape)) + rank
    return cnt.reshape(-1), off.reshape(-1), dst.reshape(-1), local, slots


def _combine_kernel(final, slot_ref, slot_next_ref, y_hbm, rw_ref, x1_ref, mod_ref, modn_ref,
                    lng_ref, lnb_ref, out_a_ref, out_b_ref, buf, sems):
    m = pl.program_id(0)
    half = m % 2

    def gather_tile(s_ref, dst_half):
        def body(i, carry):
            for u in range(DMA_UNROLL):
                t = i * DMA_UNROLL + u
                for k in range(2):
                    _row_copy(y_hbm, buf.at[dst_half, k], s_ref[k, t], t, sems.at[dst_half]).start()
            return carry

        lax.fori_loop(0, TM // DMA_UNROLL, body, 0)

    @pl.when(m == 0)
    def _():
        gather_tile(slot_ref, 0)

    @pl.when(m + 1 < pl.num_programs(0))
    def _():
        gather_tile(slot_next_ref, 1 - half)

    def drain(i, carry):
        for _ in range(2 * DMA_UNROLL):
            _row_copy(y_hbm, buf.at[half, 0], 0, 0, sems.at[half]).wait()
        return carry

    lax.fori_loop(0, TM // DMA_UNROLL, drain, 0)
    f = rw_ref[:, 0:1] * _load_rows(buf.at[half, 0], TM) + rw_ref[:, 1:2] * _load_rows(buf.at[half, 1], TM)
    x2 = _layer_norm(ALPHA * x1_ref[...] + mod_ref[5:6, :] * f, lng_ref[...], lnb_ref[...])
    if final:
        @pl.when(m < CTX_TILES)
        def _():
            out_a_ref[...] = x2

        @pl.when(m >= CTX_TILES)
        def _():
            out_b_ref[...] = x2
    else:
        out_a_ref[...] = x2
        out_b_ref[...] = (x2 * (1.0 + modn_ref[1:2, :]) + modn_ref[0:1, :]).astype(BF16)


def _combine(slots, y_slots, rw, x1, mod_l, mod_next, ln_g, ln_b, final):
    row = lambda m: (m, 0)
    const = lambda m: (0, 0)
    mod_spec = pl.BlockSpec((None, N_MOD, D), lambda m: (_cond_row(m, CTX_TILES, TILES_PER_LAT), 0, 0))
    if final:
        out_shape = (jax.ShapeDtypeStruct((T_CTX, D), F32), jax.ShapeDtypeStruct((T_LAT, D), F32))
        out_specs = (pl.BlockSpec((TM, D), lambda m: (jnp.minimum(m, CTX_TILES - 1), 0)),
                     pl.BlockSpec((TM, D), lambda m: (jnp.maximum(m - CTX_TILES, 0), 0)))
    else:
        out_shape = (jax.ShapeDtypeStruct((T, D), F32), jax.ShapeDtypeStruct((T, D), BF16))
        out_specs = (pl.BlockSpec((TM, D), row), pl.BlockSpec((TM, D), row))
    return pl.pallas_call(
        functools.partial(_combine_kernel, final),
        out_shape=out_shape,
        grid=(N_TILES,),
        in_specs=[pl.BlockSpec((None, 2, TM), lambda m: (m, 0, 0), memory_space=pltpu.SMEM),
                  pl.BlockSpec((None, 2, TM), lambda m: (jnp.minimum(m + 1, N_TILES - 1), 0, 0),
                               memory_space=pltpu.SMEM),
                  pl.BlockSpec(memory_space=pl.ANY),
                  pl.BlockSpec((TM, LANE), row), pl.BlockSpec((TM, D), row),
                  mod_spec, mod_spec,
                  pl.BlockSpec((1, D), const), pl.BlockSpec((1, D), const)],
        out_specs=out_specs,
        scratch_shapes=[pltpu.VMEM((2, 2, TM * ROW_TILE, LANE), F32), pltpu.SemaphoreType.DMA((2,))],
        compiler_params=_params(("arbitrary",)),
        name="moe_combine_ln",
    )(slots, slots, y_slots, rw, x1, mod_l, mod_next, ln_g, ln_b)


def kernel(x_prompt, x_sample, c, c_ctx, cache_k, cache_v, w_in, conv_a_w, w_a_out, conv_b_w, conv_b_b,
           norm_b_g, norm_b_b, w_b_out, q_norm_g, k_norm_g, w_c_out, w_o, w_ada, b_ada, ln_g, ln_b,
           w_router, router_bias, w_gate_up, w_down):
    x = jnp.concatenate([x_prompt.reshape(T_CTX, D), x_sample.reshape(T_LAT, D)], axis=0)
    cond16 = jnp.concatenate([c_ctx[None, :], c, jnp.zeros((16 - 1 - N_LAT_SEQ, D), F32)], axis=0)
    mod = _modulation(cond16, w_ada, b_ada).reshape(DEPTH, 16, N_MOD, D)

    wa_b, wb_b, wc_b, wo_b = (w.astype(BF16) for w in (w_a_out, w_b_out, w_c_out, w_o))
    wr = jnp.pad(w_router, ((0, 0), (0, LANE - N_EXP)))
    wr_hi = wr.astype(BF16)
    wr_lo = (wr - wr_hi.astype(F32)).astype(BF16)
    rbias = router_bias.reshape(N_EXP, 1)

    cos_tab, sin_tab = _rope_tables()
    half = np.arange(LANE) // HEAD
    ones_bd = jnp.asarray(half[:, None] == half[None, :], BF16)
    ck = cache_k.reshape(N_LAT_SEQ, DEPTH, PAST, D_KV)
    cv = cache_v.reshape(N_LAT_SEQ, DEPTH, PAST, D_KV)

    h = _modulate(x, mod[0])
    new_k, new_v = [], []
    for l in range(DEPTH):
        ab, acx = _proj_a(h, w_in, l)
        glu = _proj_b(h, w_in, l)
        gates = _proj_gates(h, w_in, l)
        q, katt, vatt, kn, vf = _proj_qkv(h, w_in, jnp.tile(q_norm_g[l], N_Q)[None, :],
                                          jnp.tile(k_norm_g[l], N_KV)[None, :], cos_tab, sin_tab, ones_bd, l)
        new_k.append(kn.reshape(N_CTX_SEQ, CTX_LEN, N_KV, HEAD))
        new_v.append(vf.reshape(N_CTX_SEQ, CTX_LEN, N_KV, HEAD))
        ya_pre, yb_pre = _convs(acx, ab, glu, conv_a_w[l], conv_b_w[l], conv_b_b[l][None, :],
                                norm_b_g[l][None, :], norm_b_b[l][None, :])
        o = _attention(q, katt, vatt, ck, cv, l)
        x1, h2, route, rw, counts = _post(ya_pre, yb_pre, o, gates, x, mod[l], wa_b, wb_b, wc_b, wo_b,
                                          ln_g[l, 0][None, :], ln_b[l, 0][None, :], wr_hi, wr_lo, rbias, l)
        cnt = counts[:, 0]
        seg_start, last_blk, blk_row, blk_exp, n_act = _block_plan(cnt)
        tile_cnt, tile_off, tile_dst, local, slots = _tile_plan(route, seg_start)
        xs = _dispatch(tile_cnt, tile_off, tile_dst, last_blk, local, h2)
        y_slots = _expert_ffn(blk_row, blk_exp, n_act, xs, w_gate_up, w_down, l)
        x, h = _combine(slots, y_slots, rw, x1, mod[l], mod[min(l + 1, DEPTH - 1)],
                        ln_g[l, 1][None, :], ln_b[l, 1][None, :], final=l == DEPTH - 1)
    y_prompt = x.reshape(N_CTX_SEQ, CTX_LEN, D)
    y_sample = h.reshape(N_LAT_SEQ, LAT_LEN, D)
    return y_prompt, y_sample, jnp.stack(new_k, axis=1), jnp.stack(new_v, axis=1)
```

```python
import functools

import numpy as np
import jax
import jax.numpy as jnp
from jax import lax
from jax.experimental import pallas as pl
from jax.experimental.pallas import tpu as pltpu

F32 = jnp.float32
BF16 = jnp.bfloat16
I32 = jnp.int32

D = 1024
DEPTH = 4
N_CTX_SEQ = 16
CTX_LEN = 256
N_LAT_SEQ = 8
LAT_LEN = 1024
PAST = 512
T_CTX = N_CTX_SEQ * CTX_LEN
T_LAT = N_LAT_SEQ * LAT_LEN
T = T_CTX + T_LAT
GRID_W = 64
HEAD = 64
N_Q = 16
N_KV = 4
GROUP = 4
D_KV = N_KV * HEAD
N_EXP = 16
N_GRP = 4
EXP_PER_GRP = 4
D_EXP = 512
N_MOD = 6
ALPHA = (2 * DEPTH) ** 0.25
LN_EPS = 1e-5
RMS_EPS = 1e-6
ROPE_THETA = 10000.0

LANE = 128
TM = 512
N_TILES = T // TM
CTX_TILES = T_CTX // TM
TILES_PER_LAT = LAT_LEN // TM
POST_ROWS = 256
TMP = 1024
SEG = 256
N_SEG = T // SEG
CTX_SEGS = T_CTX // SEG
SEGS_PER_LAT = LAT_LEN // SEG
HALO_A = 8
HALO_B = 16
SH_ROWS = SEG + 2 * HALO_B - 8
FFN_BLK = 256
N_FFN_BLOCKS = (2 * T) // FFN_BLK + N_EXP
N_SLOTS = N_FFN_BLOCKS * FFN_BLK
VMEM_LIMIT = 56 * 1024 * 1024

COL_AB, COL_AC, COL_AX, COL_BU, COL_BG, COL_Q, COL_KV, COL_GATES = 0, 2, 4, 6, 8, 10, 12, 13


def _params(sem):
    return pltpu.CompilerParams(dimension_semantics=sem, vmem_limit_bytes=VMEM_LIMIT)


def _cond_row(m, tiles_ctx, tiles_per_lat):
    return jnp.where(m < tiles_ctx, 0, 1 + (m - tiles_ctx) // tiles_per_lat)


def _layer_norm(x, g, b):
    mu = jnp.mean(x, axis=-1, keepdims=True)
    xc = x - mu
    var = jnp.mean(xc * xc, axis=-1, keepdims=True)
    return xc * lax.rsqrt(var + LN_EPS) * g + b


def _sigmoid(x):
    return 1.0 / (1.0 + jnp.exp(-x))


ROW_TILE = D // LANE


def _store_rows(ref, x):
    for j in range(ROW_TILE):
        ref[pl.ds(j, x.shape[0], stride=ROW_TILE), :] = x[:, j * LANE:(j + 1) * LANE]


def _load_rows(ref, n_rows):
    return jnp.concatenate([ref[pl.ds(j, n_rows, stride=ROW_TILE), :] for j in range(ROW_TILE)], axis=1)


def _mod_kernel(cond_ref, w_ref, b_ref, o_ref):
    cnd = cond_ref[...]
    s = (cnd * _sigmoid(cnd)).astype(BF16)
    o_ref[...] = jnp.dot(s, w_ref[...].astype(BF16), preferred_element_type=F32) + b_ref[...]


def _modulation(cond16, w_ada, b_ada):
    n_col = N_MOD * D
    tn = 1024
    return pl.pallas_call(
        _mod_kernel,
        out_shape=jax.ShapeDtypeStruct((DEPTH, 16, n_col), F32),
        grid=(DEPTH, n_col // tn),
        in_specs=[
            pl.BlockSpec((16, D), lambda l, n: (0, 0)),
            pl.BlockSpec((None, D, tn), lambda l, n: (l, 0, n)),
            pl.BlockSpec((None, 1, tn), lambda l, n: (l, 0, n)),
        ],
        out_specs=pl.BlockSpec((None, 16, tn), lambda l, n: (l, 0, n)),
        compiler_params=_params(("parallel", "parallel")),
        name="adaln_mod",
    )(cond16, w_ada, b_ada.reshape(DEPTH, 1, n_col))


def _modulate_kernel(x_ref, mod_ref, h_ref):
    h_ref[...] = (x_ref[...] * (1.0 + mod_ref[1:2, :]) + mod_ref[0:1, :]).astype(BF16)


def _modulate(x, mod_l):
    return pl.pallas_call(
        _modulate_kernel,
        out_shape=jax.ShapeDtypeStruct((T, D), BF16),
        grid=(N_TILES,),
        in_specs=[
            pl.BlockSpec((TM, D), lambda m: (m, 0)),
            pl.BlockSpec((None, N_MOD, D), lambda m: (_cond_row(m, CTX_TILES, TILES_PER_LAT), 0, 0)),
        ],
        out_specs=pl.BlockSpec((TM, D), lambda m: (m, 0)),
        compiler_params=_params(("parallel",)),
        name="modulate",
    )(x, mod_l)


WCOL = 512


def _cast_weights(first, pairs):
    @pl.when(first)
    def _():
        for src, dst in pairs:
            dst[...] = src[...].astype(BF16)


def _proj_a_kernel(h_ref, wb_ref, wc_ref, wx_ref, ab_ref, acx_ref, wb_s, wc_s, wx_s):
    _cast_weights(pl.program_id(1) == 0, [(wb_ref, wb_s), (wc_ref, wc_s), (wx_ref, wx_s)])
    h = h_ref[...]
    ab_ref[...] = jnp.dot(h, wb_s[...], preferred_element_type=F32).astype(BF16)
    acx_ref[...] = (jnp.dot(h, wc_s[...], preferred_element_type=F32)
                    * jnp.dot(h, wx_s[...], preferred_element_type=F32))


def _proj_b_kernel(h_ref, wu_ref, wg_ref, glu_ref, wu_s, wg_s):
    _cast_weights(pl.program_id(1) == 0, [(wu_ref, wu_s), (wg_ref, wg_s)])
    h = h_ref[...]
    glu_ref[...] = (jnp.dot(h, wu_s[...], preferred_element_type=F32)
                    * _sigmoid(jnp.dot(h, wg_s[...], preferred_element_type=F32)))


def _proj_gate_kernel(h_ref, w0_ref, w1_ref, g_ref, w_s):
    _cast_weights(pl.program_id(1) == 0, [(w0_ref, w_s.at[:, 0:WCOL]), (w1_ref, w_s.at[:, WCOL:2 * WCOL])])
    g_ref[...] = _sigmoid(jnp.dot(h_ref[...], w_s[...], preferred_element_type=F32)).astype(BF16)


def _w_spec(l, col0, step=1):
    return pl.BlockSpec((None, D, WCOL), lambda c, m: (l, 0, col0 + step * c))


def _proj_a(h, w_in, l):
    out_spec = pl.BlockSpec((TMP, WCOL), lambda c, m: (m, c))
    return pl.pallas_call(
        _proj_a_kernel,
        out_shape=(jax.ShapeDtypeStruct((T, D), BF16), jax.ShapeDtypeStruct((T, D), F32)),
        grid=(D // WCOL, T // TMP),
        in_specs=[pl.BlockSpec((TMP, D), lambda c, m: (m, 0)),
                  _w_spec(l, COL_AB), _w_spec(l, COL_AC), _w_spec(l, COL_AX)],
        out_specs=(out_spec, out_spec),
        scratch_shapes=[pltpu.VMEM((D, WCOL), BF16)] * 3,
        compiler_params=_params(("parallel", "arbitrary")),
        name="proj_a",
    )(h, w_in, w_in, w_in)


def _proj_b(h, w_in, l):
    return pl.pallas_call(
        _proj_b_kernel,
        out_shape=jax.ShapeDtypeStruct((T, D), F32),
        grid=(D // WCOL, T // TMP),
        in_specs=[pl.BlockSpec((TMP, D), lambda c, m: (m, 0)), _w_spec(l, COL_BU), _w_spec(l, COL_BG)],
        out_specs=pl.BlockSpec((TMP, WCOL), lambda c, m: (m, c)),
        scratch_shapes=[pltpu.VMEM((D, WCOL), BF16)] * 2,
        compiler_params=_params(("parallel", "arbitrary")),
        name="proj_b",
    )(h, w_in, w_in)


def _proj_gates(h, w_in, l):
    tn = 2 * WCOL
    return pl.pallas_call(
        _proj_gate_kernel,
        out_shape=jax.ShapeDtypeStruct((T, 3 * D), BF16),
        grid=(3 * D // tn, T // TMP),
        in_specs=[pl.BlockSpec((TMP, D), lambda c, m: (m, 0)),
                  _w_spec(l, COL_GATES, 2), _w_spec(l, COL_GATES + 1, 2)],
        out_specs=pl.BlockSpec((TMP, tn), lambda c, m: (m, c)),
        scratch_shapes=[pltpu.VMEM((D, tn), BF16)],
        compiler_params=_params(("parallel", "arbitrary")),
        name="proj_gates",
    )(h, w_in, w_in)


def _head_mean_square(x, ones_bd):
    out = []
    for c in range(x.shape[1] // LANE):
        sq = x[:, c * LANE:(c + 1) * LANE]
        out.append(jnp.dot((sq * sq).astype(BF16), ones_bd, preferred_element_type=F32))
    return jnp.concatenate(out, axis=1) * (1.0 / HEAD)


def _rope(x, cos, sin, first_half):
    out = []
    for c in range(x.shape[1] // LANE):
        xc = x[:, c * LANE:(c + 1) * LANE]
        partner = jnp.where(first_half, pltpu.roll(xc, LANE - 16, axis=1), pltpu.roll(xc, 16, axis=1))
        out.append(xc * cos + partner * sin)
    return jnp.concatenate(out, axis=1)


def _qkv_kernel(h_ref, wq_ref, wkv_ref, gq_ref, gk_ref, cos_ref, sin_ref, ones_ref,
                q_ref, katt_ref, vatt_ref, kn_ref, vf_ref, wq_s, wkv_s):
    _cast_weights(pl.program_id(0) == 0, [(wq_ref, wq_s), (wkv_ref, wkv_s)])
    h = h_ref[...]
    ones_bd = ones_ref[...]
    cos = cos_ref[...]
    sin = sin_ref[...]
    lane = lax.broadcasted_iota(I32, (TM, LANE), 1)
    first_half = (lane & 16) == 0
    q = jnp.dot(h, wq_s[...], preferred_element_type=F32)
    qn = q * lax.rsqrt(_head_mean_square(q, ones_bd) + RMS_EPS) * gq_ref[...]
    q_ref[...] = (_rope(qn, cos, sin, first_half) * (HEAD ** -0.5)).astype(BF16)
    kv = jnp.dot(h, wkv_s[...], preferred_element_type=F32)
    k = kv[:, :D_KV]
    v = kv[:, D_KV:]
    kn = k * lax.rsqrt(_head_mean_square(k, ones_bd) + RMS_EPS) * gk_ref[...]
    katt_ref[...] = _rope(kn, cos, sin, first_half).astype(BF16)
    vatt_ref[...] = v.astype(BF16)

    @pl.when(pl.program_id(0) < CTX_TILES)
    def _():
        kn_ref[...] = kn
        vf_ref[...] = v


def _proj_qkv(h, w_in, gq, gk, cos_tab, sin_tab, ones_bd, l):
    def tab_idx(m):
        return jnp.where(m < CTX_TILES, TILES_PER_LAT, (m - CTX_TILES) % TILES_PER_LAT)

    row = lambda m: (m, 0)
    ctx_row = lambda m: (jnp.minimum(m, CTX_TILES - 1), 0)
    return pl.pallas_call(
        _qkv_kernel,
        out_shape=(jax.ShapeDtypeStruct((T, D), BF16),
                   jax.ShapeDtypeStruct((T, D_KV), BF16),
                   jax.ShapeDtypeStruct((T, D_KV), BF16),
                   jax.ShapeDtypeStruct((T_CTX, D_KV), F32),
                   jax.ShapeDtypeStruct((T_CTX, D_KV), F32)),
        grid=(N_TILES,),
        in_specs=[
            pl.BlockSpec((TM, D), row),
            pl.BlockSpec((None, D, D), lambda m: (l, 0, COL_Q // 2)),
            pl.BlockSpec((None, D, 2 * D_KV), lambda m: (l, 0, COL_KV)),
            pl.BlockSpec((1, D), lambda m: (0, 0)),
            pl.BlockSpec((1, D_KV), lambda m: (0, 0)),
            pl.BlockSpec((TM, LANE), lambda m: (tab_idx(m), 0)),
            pl.BlockSpec((TM, LANE), lambda m: (tab_idx(m), 0)),
            pl.BlockSpec((LANE, LANE), lambda m: (0, 0)),
        ],
        out_specs=(pl.BlockSpec((TM, D), row), pl.BlockSpec((TM, D_KV), row),
                   pl.BlockSpec((TM, D_KV), row), pl.BlockSpec((TM, D_KV), ctx_row),
                   pl.BlockSpec((TM, D_KV), ctx_row)),
        scratch_shapes=[pltpu.VMEM((D, D), BF16), pltpu.VMEM((D, 2 * D_KV), BF16)],
        compiler_params=_params(("arbitrary",)),
        name="proj_qkv",
    )(h, w_in, w_in, gq, gk, cos_tab, sin_tab, ones_bd)


def _rope_tables():
    lane = np.arange(LANE)
    j = lane % 16
    freqs = jnp.power(ROPE_THETA, -jnp.arange(16, dtype=F32) / 16)[j]
    pos = jnp.arange(LAT_LEN, dtype=I32)
    row = (pos // GRID_W).astype(F32)
    col = (pos % GRID_W).astype(F32)
    use_row = jnp.asarray((lane % HEAD) < HEAD // 2)
    p = jnp.where(use_row[None, :], row[:, None], col[:, None])
    ang = p * freqs[None, :]
    sign = jnp.asarray(np.where((lane & 16) == 0, -1.0, 1.0), F32)
    cos = jnp.concatenate([jnp.cos(ang), jnp.ones((TM, LANE), F32)], axis=0)
    sin = jnp.concatenate([jnp.sin(ang) * sign[None, :], jnp.zeros((TM, LANE), F32)], axis=0)
    return cos, sin


def _conv_kernel(acx_ref, acx_l_ref, acx_r_ref, ab_ref, glu_ref, glu_l_ref, glu_r_ref,
                 wa_ref, wb_ref, bb_ref, ng_ref, nb_ref, ya_ref, yb_ref, pad_a, pad_b, u_ref, sh_ref):
    s = pl.program_id(0)
    lat = s >= CTX_SEGS
    pos = (s - CTX_SEGS) % SEGS_PER_LAT
    has_left = jnp.logical_and(lat, pos != 0)
    has_right = jnp.logical_and(lat, pos != SEGS_PER_LAT - 1)

    pad_a[0:HALO_A, :] = jnp.where(has_left, acx_l_ref[...], 0.0)
    pad_a[HALO_A:HALO_A + SEG, :] = acx_ref[...]
    pad_a[HALO_A + SEG:, :] = jnp.where(has_right, acx_r_ref[...], 0.0)
    pad_b[0:HALO_B, :] = jnp.where(has_left, glu_l_ref[...], 0.0)
    pad_b[HALO_B:HALO_B + SEG, :] = glu_ref[...]
    pad_b[HALO_B + SEG:, :] = jnp.where(has_right, glu_r_ref[...], 0.0)

    conv_a = (wa_ref[0:1, :] * pad_a[HALO_A - 1:HALO_A - 1 + SEG, :]
              + wa_ref[1:2, :] * pad_a[HALO_A:HALO_A + SEG, :]
              + wa_ref[2:3, :] * pad_a[HALO_A + 1:HALO_A + 1 + SEG, :])
    ya_ref[...] = (ab_ref[...].astype(F32) * conv_a).astype(BF16)

    rows = 64
    kb = wb_ref.shape[0]

    def lane_chunk(c, carry):
        lanes = pl.ds(pl.multiple_of(c * LANE, LANE), LANE)
        for b in range(1, 8):
            sh_ref[b, :, :] = pad_b[b:b + SH_ROWS, lanes]
        for r in range(SEG // rows):
            acc = jnp.zeros((rows, LANE), F32)
            for k in range(kb):
                off = HALO_B + k - kb // 2
                row0 = r * rows + 8 * (off // 8)
                if off % 8 == 0:
                    src = pad_b[row0:row0 + rows, lanes]
                else:
                    src = sh_ref[off % 8, row0:row0 + rows, :]
                acc = acc + wb_ref[k:k + 1, lanes] * src
            u_ref[r * rows:(r + 1) * rows, lanes] = acc
        return carry

    lax.fori_loop(0, D // LANE, lane_chunk, 0)
    u = _layer_norm(u_ref[...] + bb_ref[...], ng_ref[...], nb_ref[...])
    yb_ref[...] = (u * _sigmoid(u)).astype(BF16)


def _convs(acx, ab, glu, conv_a_w, conv_b_w, conv_b_b, norm_g, norm_b):
    seg = lambda s: (s, 0)
    const = lambda s: (0, 0)
    ra, rb = SEG // HALO_A, SEG // HALO_B
    left_a = lambda s: (jnp.maximum(s * ra - 1, 0), 0)
    right_a = lambda s: (jnp.minimum((s + 1) * ra, T // HALO_A - 1), 0)
    left_b = lambda s: (jnp.maximum(s * rb - 1, 0), 0)
    right_b = lambda s: (jnp.minimum((s + 1) * rb, T // HALO_B - 1), 0)
    return pl.pallas_call(
        _conv_kernel,
        out_shape=(jax.ShapeDtypeStruct((T, D), BF16), jax.ShapeDtypeStruct((T, D), BF16)),
        grid=(N_SEG,),
        in_specs=[
            pl.BlockSpec((SEG, D), seg), pl.BlockSpec((HALO_A, D), left_a), pl.BlockSpec((HALO_A, D), right_a),
            pl.BlockSpec((SEG, D), seg),
            pl.BlockSpec((SEG, D), seg), pl.BlockSpec((HALO_B, D), left_b), pl.BlockSpec((HALO_B, D), right_b),
            pl.BlockSpec(conv_a_w.shape, const), pl.BlockSpec(conv_b_w.shape, const),
            pl.BlockSpec((1, D), const), pl.BlockSpec((1, D), const), pl.BlockSpec((1, D), const),
        ],
        out_specs=(pl.BlockSpec((SEG, D), seg), pl.BlockSpec((SEG, D), seg)),
        scratch_shapes=[pltpu.VMEM((SEG + 2 * HALO_A, D), F32),
                        pltpu.VMEM((SEG + 2 * HALO_B, D), F32),
                        pltpu.VMEM((SEG, D), F32),
                        pltpu.VMEM((8, SH_ROWS, LANE), F32)],
        compiler_params=_params(("parallel",)),
        name="convs",
    )(acx, acx, acx, ab, glu, glu, glu, conv_a_w, conv_b_w, conv_b_b, norm_g, norm_b)


def _attend(q_ref, key_refs, val_refs, o_ref, ones_denominator):
    nt = (((1,), (1,)), ((), ()))
    keys = [r[...].astype(BF16) for r in key_refs]
    vals = [r[...].astype(BF16) for r in val_refs]
    n_q = q_ref.shape[0]
    for g in range(N_KV):
        kg = [k[:, g * HEAD:(g + 1) * HEAD] for k in keys]
        vg = [v[:, g * HEAD:(g + 1) * HEAD] for v in vals]
        if ones_denominator:
            vg = [jnp.concatenate([v, jnp.ones_like(v)], axis=1) for v in vg]
        qg = jnp.concatenate([q_ref[:, (g * GROUP + hh) * HEAD:(g * GROUP + hh + 1) * HEAD]
                              for hh in range(GROUP)], axis=0)
        s = [lax.dot_general(qg, k, nt, preferred_element_type=F32) for k in kg]
        mx = functools.reduce(jnp.maximum, [jnp.max(x, axis=-1, keepdims=True) for x in s])
        p = [jnp.exp(x - mx) for x in s]
        acc = functools.reduce(jnp.add, [jnp.dot(x.astype(BF16), v, preferred_element_type=F32)
                                         for x, v in zip(p, vg)])
        if ones_denominator:
            den = acc[:, HEAD:HEAD + 1]
            acc = acc[:, :HEAD]
        else:
            den = functools.reduce(jnp.add, [jnp.sum(x, axis=-1, keepdims=True) for x in p])
        out = (acc / den).astype(BF16)
        for hh in range(GROUP):
            hd = g * GROUP + hh
            o_ref[:, hd * HEAD:(hd + 1) * HEAD] = out[hh * n_q:(hh + 1) * n_q, :]


def _attn_ctx_kernel(q_ref, k_ref, v_ref, o_ref):
    _attend(q_ref, [k_ref], [v_ref], o_ref, ones_denominator=False)


def _attn_lat_kernel(q_ref, k_ref, v_ref, ck_ref, cv_ref, o_in_ref, o_ref):
    del o_in_ref
    _attend(q_ref, [ck_ref, k_ref], [cv_ref, v_ref], o_ref, ones_denominator=True)


def _attention(q, katt, vatt, cache_k, cache_v, l):
    o = pl.pallas_call(
        _attn_ctx_kernel,
        out_shape=jax.ShapeDtypeStruct((T, D), BF16),
        grid=(N_CTX_SEQ,),
        in_specs=[pl.BlockSpec((CTX_LEN, D), lambda b: (b, 0)),
                  pl.BlockSpec((CTX_LEN, D_KV), lambda b: (b, 0)),
                  pl.BlockSpec((CTX_LEN, D_KV), lambda b: (b, 0))],
        out_specs=pl.BlockSpec((CTX_LEN, D), lambda b: (b, 0)),
        compiler_params=_params(("parallel",)),
        name="attn_ctx",
    )(q, katt, vatt)
    seg0 = CTX_SEGS
    lat0 = T_CTX // LAT_LEN
    return pl.pallas_call(
        _attn_lat_kernel,
        out_shape=jax.ShapeDtypeStruct((T, D), BF16),
        grid=(N_LAT_SEQ, SEGS_PER_LAT),
        in_specs=[pl.BlockSpec((SEG, D), lambda b, i: (seg0 + b * SEGS_PER_LAT + i, 0)),
                  pl.BlockSpec((LAT_LEN, D_KV), lambda b, i: (lat0 + b, 0)),
                  pl.BlockSpec((LAT_LEN, D_KV), lambda b, i: (lat0 + b, 0)),
                  pl.BlockSpec((None, None, PAST, D_KV), lambda b, i: (b, l, 0, 0)),
                  pl.BlockSpec((None, None, PAST, D_KV), lambda b, i: (b, l, 0, 0)),
                  pl.BlockSpec(memory_space=pl.ANY)],
        out_specs=pl.BlockSpec((SEG, D), lambda b, i: (seg0 + b * SEGS_PER_LAT + i, 0)),
        input_output_aliases={5: 0},
        compiler_params=_params(("parallel", "parallel")),
        name="attn_lat",
    )(q, katt, vatt, cache_k, cache_v, o)


def _post_kernel(ya_ref, yb_ref, o_ref, g_ref, x_ref, mod_ref, wa_ref, wb_ref, wc_ref, wo_ref,
                 lng_ref, lnb_ref, wr_hi_ref, wr_lo_ref, rb_ref,
                 x1_ref, h2_ref, route_ref, rw_ref, cnt_ref, carry_ref):
    m = pl.program_id(0)

    @pl.when(m == 0)
    def _():
        carry_ref[...] = jnp.zeros_like(carry_ref)

    carry = carry_ref[...]
    r_i = lax.broadcasted_iota(I32, (POST_ROWS, POST_ROWS), 0)
    c_i = lax.broadcasted_iota(I32, (POST_ROWS, POST_ROWS), 1)
    upper = jnp.where(r_i < c_i, 1.0, 0.0).astype(BF16)
    for part in range(TM // POST_ROWS):
        rows = slice(part * POST_ROWS, (part + 1) * POST_ROWS)
        carry = _post_part(rows, carry, upper, ya_ref, yb_ref, o_ref, g_ref, x_ref, mod_ref, wa_ref, wb_ref,
                           wc_ref, wo_ref, lng_ref, lnb_ref, wr_hi_ref, wr_lo_ref, rb_ref,
                           x1_ref, h2_ref, route_ref, rw_ref)
    carry_ref[...] = carry
    cnt_ref[...] = carry.astype(I32)
    route_ref[4:8, :] = jnp.zeros((4, TM), I32)


def _post_part(rows, carry, upper, ya_ref, yb_ref, o_ref, g_ref, x_ref, mod_ref, wa_ref, wb_ref, wc_ref,
               wo_ref, lng_ref, lnb_ref, wr_hi_ref, wr_lo_ref, rb_ref, x1_ref, h2_ref, route_ref, rw_ref):
    n = POST_ROWS
    ya = jnp.dot(ya_ref[rows, :], wa_ref[...], preferred_element_type=F32)
    yb = jnp.dot(yb_ref[rows, :], wb_ref[...], preferred_element_type=F32)
    yc = jnp.dot(o_ref[rows, :], wc_ref[...], preferred_element_type=F32)
    merged = (g_ref[rows, 0:D].astype(F32) * ya + g_ref[rows, D:2 * D].astype(F32) * yb
              + g_ref[rows, 2 * D:3 * D].astype(F32) * yc)
    mix = jnp.dot(merged.astype(BF16), wo_ref[...], preferred_element_type=F32)
    x1 = _layer_norm(ALPHA * x_ref[rows, :] + mod_ref[2:3, :] * mix, lng_ref[...], lnb_ref[...])
    x1_ref[rows, :] = x1
    h2 = x1 * (1.0 + mod_ref[4:5, :]) + mod_ref[3:4, :]

    hi = h2.astype(BF16)
    h2_ref[rows, :] = hi
    lo = (h2 - hi.astype(F32)).astype(BF16)
    wr_hi = wr_hi_ref[...]
    logits = (jnp.dot(hi, wr_hi, preferred_element_type=F32)
              + jnp.dot(lo, wr_hi, preferred_element_type=F32)
              + jnp.dot(hi, wr_lo_ref[...], preferred_element_type=F32))
    scores = _sigmoid(logits.T[0:N_EXP, :])
    sel = scores + rb_ref[...]

    gscore = []
    for g in range(N_GRP):
        r = [sel[g * EXP_PER_GRP + j:g * EXP_PER_GRP + j + 1, :] for j in range(EXP_PER_GRP)]
        pairs = [r[a] + r[b] for a in range(EXP_PER_GRP) for b in range(a + 1, EXP_PER_GRP)]
        gscore.append(functools.reduce(jnp.maximum, pairs))
    best = functools.reduce(jnp.maximum, gscore)
    gsel = jnp.full(best.shape, N_GRP - 1, I32)
    for g in range(N_GRP - 2, -1, -1):
        gsel = jnp.where(gscore[g] == best, g, gsel)

    eidx = lax.broadcasted_iota(I32, (N_EXP, n), 0)
    neg = jnp.float32(-jnp.inf)
    cand = jnp.where((eidx // EXP_PER_GRP) == gsel, sel, neg)
    top1 = jnp.max(cand, axis=0, keepdims=True)
    idx1 = jnp.min(jnp.where(cand == top1, eidx, N_EXP), axis=0, keepdims=True)
    cand2 = jnp.where(eidx == idx1, neg, cand)
    top2 = jnp.max(cand2, axis=0, keepdims=True)
    idx2 = jnp.min(jnp.where(cand2 == top2, eidx, N_EXP), axis=0, keepdims=True)
    is1 = eidx == idx1
    is2 = eidx == idx2
    w1 = jnp.sum(jnp.where(is1, scores, 0.0), axis=0, keepdims=True)
    w2 = jnp.sum(jnp.where(is2, scores, 0.0), axis=0, keepdims=True)
    wsum = w1 + w2
    w1 = w1 / wsum
    w2 = w2 / wsum

    onehot = jnp.where(jnp.logical_or(is1, is2), 1.0, 0.0)
    prefix = jnp.dot(onehot.astype(BF16), upper, preferred_element_type=F32) + carry[:, 0:1]
    rank1 = jnp.sum(jnp.where(is1, prefix, 0.0), axis=0, keepdims=True)
    rank2 = jnp.sum(jnp.where(is2, prefix, 0.0), axis=0, keepdims=True)

    route_ref[0:1, rows] = idx1
    route_ref[1:2, rows] = idx2
    route_ref[2:3, rows] = rank1.astype(I32)
    route_ref[3:4, rows] = rank2.astype(I32)
    wrow = lax.broadcasted_iota(I32, (LANE, n), 0)
    wcols = jnp.where(wrow == 0, w1, jnp.where(wrow == 1, w2, 0.0))
    rw_ref[rows, :] = wcols.T
    return carry + jnp.sum(onehot, axis=1, keepdims=True)


def _post(ya_pre, yb_pre, o, gates, x, mod_l, wa, wb, wc, wo, ln_g, ln_b, wr_hi, wr_lo, rbias, l):
    row = lambda m: (m, 0)
    const = lambda m: (0, 0)
    wspec = pl.BlockSpec((None, D, D), lambda m: (l, 0, 0))
    return pl.pallas_call(
        _post_kernel,
        out_shape=(jax.ShapeDtypeStruct((T, D), F32),
                   jax.ShapeDtypeStruct((T, D), BF16),
                   jax.ShapeDtypeStruct((N_TILES, 8, TM), I32),
                   jax.ShapeDtypeStruct((T, LANE), F32),
                   jax.ShapeDtypeStruct((N_EXP, LANE), I32)),
        grid=(N_TILES,),
        in_specs=[pl.BlockSpec((TM, D), row), pl.BlockSpec((TM, D), row), pl.BlockSpec((TM, D), row),
                  pl.BlockSpec((TM, 3 * D), row), pl.BlockSpec((TM, D), row),
                  pl.BlockSpec((None, N_MOD, D), lambda m: (_cond_row(m, CTX_TILES, TILES_PER_LAT), 0, 0)),
                  wspec, wspec, wspec, wspec,
                  pl.BlockSpec((1, D), const), pl.BlockSpec((1, D), const),
                  pl.BlockSpec((D, LANE), const), pl.BlockSpec((D, LANE), const),
                  pl.BlockSpec((N_EXP, 1), const)],
        out_specs=(pl.BlockSpec((TM, D), row), pl.BlockSpec((TM, D), row),
                   pl.BlockSpec((None, 8, TM), lambda m: (m, 0, 0)),
                   pl.BlockSpec((TM, LANE), row),
                   pl.BlockSpec((N_EXP, LANE), const)),
        scratch_shapes=[pltpu.VMEM((N_EXP, LANE), F32)],
        compiler_params=_params(("arbitrary",)),
        name="merge_ln_router",
    )(ya_pre, yb_pre, o, gates, x, mod_l, wa, wb, wc, wo, ln_g, ln_b, wr_hi, wr_lo, rbias)


def _row_copy(src, dst, src_row, dst_row, sem):
    return pltpu.make_async_copy(src.at[pl.ds(pl.multiple_of(src_row * ROW_TILE, ROW_TILE), ROW_TILE), :],
                                 dst.at[pl.ds(pl.multiple_of(dst_row * ROW_TILE, ROW_TILE), ROW_TILE), :], sem)


DMA_UNROLL = 8


RUN_CHUNK = 256


def _run_copies(src, dst, src_row, dst_row, n, sem, start):
    def piece(offset, size):
        cp = pltpu.make_async_copy(
            src.at[pl.ds(pl.multiple_of((src_row + offset) * ROW_TILE, ROW_TILE), size * ROW_TILE), :],
            dst.at[pl.ds(pl.multiple_of((dst_row + offset) * ROW_TILE, ROW_TILE), size * ROW_TILE), :], sem)
        if start:
            cp.start()
        else:
            cp.wait()

    def whole(j, carry):
        piece(j * RUN_CHUNK, RUN_CHUNK)
        return carry

    lax.fori_loop(0, n // RUN_CHUNK, whole, 0)
    size = RUN_CHUNK // 2
    while size >= 1:
        @pl.when((n & size) != 0)
        def _(size=size):
            piece(n & ~(2 * size - 1), size)
        size //= 2


def _dispatch_kernel(cnt_ref, off_ref, dst_ref, last_ref, lp_ref, h2_ref, xs_hbm, xc_ref, zero_ref, sem, zsem):
    m = pl.program_id(0)

    @pl.when(m == 0)
    def _():
        zero_ref[...] = jnp.zeros_like(zero_ref)

        def pad_copy(e):
            row = pl.multiple_of(last_ref[e] * ROW_TILE, FFN_BLK * ROW_TILE)
            return pltpu.make_async_copy(zero_ref, xs_hbm.at[pl.ds(row, FFN_BLK * ROW_TILE), :], zsem)

        def zero_start(e, carry):
            pad_copy(e).start()
            return carry

        def zero_wait(e, carry):
            pad_copy(e).wait()
            return carry

        lax.fori_loop(0, N_EXP, zero_start, 0)
        lax.fori_loop(0, N_EXP, zero_wait, 0)

    r_i = lax.broadcasted_iota(I32, (2 * TM, TM), 0)
    hit = jnp.logical_or(r_i == lp_ref[0:1, :], r_i == lp_ref[1:2, :])
    perm = jnp.where(hit, 1.0, 0.0).astype(BF16)
    _store_rows(xc_ref, jnp.dot(perm, h2_ref[...], preferred_element_type=F32))

    for start in (True, False):
        for e in range(N_EXP):
            j = m * N_EXP + e
            _run_copies(xc_ref, xs_hbm, off_ref[j], dst_ref[j], cnt_ref[j], sem, start)


def _dispatch(tile_cnt, tile_off, tile_dst, last_blk, lp, h2):
    grid_spec = pltpu.PrefetchScalarGridSpec(
        num_scalar_prefetch=4,
        grid=(N_TILES,),
        in_specs=[pl.BlockSpec((None, 2, TM), lambda m, *_: (m, 0, 0)),
                  pl.BlockSpec((TM, D), lambda m, *_: (m, 0))],
        out_specs=pl.BlockSpec(memory_space=pl.ANY),
        scratch_shapes=[pltpu.VMEM((2 * TM * ROW_TILE, LANE), F32),
                        pltpu.VMEM((FFN_BLK * ROW_TILE, LANE), F32),
                        pltpu.SemaphoreType.DMA, pltpu.SemaphoreType.DMA],
    )
    return pl.pallas_call(
        _dispatch_kernel,
        out_shape=jax.ShapeDtypeStruct((N_SLOTS * ROW_TILE, LANE), F32),
        grid_spec=grid_spec,
        compiler_params=_params(("arbitrary",)),
        name="moe_dispatch",
    )(tile_cnt, tile_off, tile_dst, last_blk, lp, h2)


FFN_AHEAD = 2


def _ffn_kernel(blk_row_ref, blk_exp_ref, n_act_ref, xs_hbm, wgu_ref, wd_ref, y_ref, wgu_s, wd_s, xbuf, sems):
    i = pl.program_id(0)
    n_act = n_act_ref[0]
    new_expert = jnp.logical_or(i == 0, blk_exp_ref[i] != blk_exp_ref[jnp.maximum(i - 1, 0)])
    _cast_weights(new_expert, [(wgu_ref, wgu_s), (wd_ref, wd_s)])
    rows = FFN_BLK * ROW_TILE

    def fetch(b):
        slot = b % (FFN_AHEAD + 1)
        return pltpu.make_async_copy(xs_hbm.at[pl.ds(pl.multiple_of(blk_row_ref[b] * rows, rows), rows), :],
                                     xbuf.at[slot], sems.at[slot])

    for b in range(FFN_AHEAD):
        @pl.when(jnp.logical_and(i == 0, b < n_act))
        def _(b=b):
            fetch(b).start()

    @pl.when(i + FFN_AHEAD < n_act)
    def _():
        fetch(i + FFN_AHEAD).start()

    @pl.when(i < n_act)
    def _():
        fetch(i).wait()
        x = _load_rows(xbuf.at[i % (FFN_AHEAD + 1)], FFN_BLK).astype(BF16)
        gu = jnp.dot(x, wgu_s[...], preferred_element_type=F32)
        gate = gu[:, :D_EXP]
        up = gu[:, D_EXP:]
        act = (gate * _sigmoid(gate) * up).astype(BF16)
        _store_rows(y_ref, jnp.dot(act, wd_s[...], preferred_element_type=F32))


def _expert_ffn(blk_row, blk_exp, n_act, xs, wgu, wd, l):
    grid_spec = pltpu.PrefetchScalarGridSpec(
        num_scalar_prefetch=3,
        grid=(N_FFN_BLOCKS,),
        in_specs=[pl.BlockSpec(memory_space=pl.ANY),
                  pl.BlockSpec((None, None, D, 2 * D_EXP), lambda i, br, be, na: (l, be[i], 0, 0)),
                  pl.BlockSpec((None, None, D_EXP, D), lambda i, br, be, na: (l, be[i], 0, 0))],
        out_specs=pl.BlockSpec((FFN_BLK * ROW_TILE, LANE), lambda i, br, be, na: (br[i], 0)),
        scratch_shapes=[pltpu.VMEM((D, 2 * D_EXP), BF16), pltpu.VMEM((D_EXP, D), BF16),
                        pltpu.VMEM((FFN_AHEAD + 1, FFN_BLK * ROW_TILE, LANE), F32),
                        pltpu.SemaphoreType.DMA((FFN_AHEAD + 1,))],
    )
    return pl.pallas_call(
        _ffn_kernel,
        out_shape=jax.ShapeDtypeStruct((N_SLOTS * ROW_TILE, LANE), F32),
        grid_spec=grid_spec,
        compiler_params=_params(("arbitrary",)),
        name="expert_ffn",
    )(blk_row, blk_exp, n_act, xs, wgu, wd)


def _block_plan(counts):
    nblk = (counts + FFN_BLK - 1) // FFN_BLK
    end = jnp.cumsum(nblk)
    start = end - nblk
    n_act = end[-1]
    i = jnp.arange(N_FFN_BLOCKS, dtype=I32)
    i_eff = jnp.minimum(i, n_act - 1)
    e = jnp.minimum(jnp.sum(i_eff[:, None] >= end[None, :], axis=1), N_EXP - 1).astype(I32)
    last_blk = jnp.minimum(start + jnp.maximum(nblk - 1, 0), N_FFN_BLOCKS - 1) * FFN_BLK
    return ((start * FFN_BLK).astype(I32), last_blk.astype(I32), i_eff.astype(I32), e,
            n_act.reshape(1).astype(I32))


def _tile_plan(route, seg_start):
    is_exp = route[:, 0:2, :, None] == jnp.arange(N_EXP, dtype=I32)
    cnt = jnp.sum(is_exp, axis=(1, 2), dtype=I32)
    before = jnp.cumsum(cnt, axis=0) - cnt
    off = jnp.cumsum(cnt, axis=1) - cnt
    dst = seg_start[None, :] + before
    pick = lambda tab: jnp.sum(jnp.where(is_exp, tab[:, None, None, :], 0), axis=-1)
    rank = route[:, 2:4, :]
    local = pick(off - before) + rank
    slots = pick(jnp.broadcast_to(seg_start, cnt.shape)) + rank
    return cnt.reshape(-1), off.reshape(-1), dst.reshape(-1), local, slots


def _combine_kernel(final, slot_ref, slot_next_ref, y_hbm, rw_ref, x1_ref, mod_ref, modn_ref,
                    lng_ref, lnb_ref, out_a_ref, out_b_ref, buf, sems):
    m = pl.program_id(0)
    half = m % 2

    def gather_tile(s_ref, dst_half):
        def body(i, carry):
            for u in range(DMA_UNROLL):
                t = i * DMA_UNROLL + u
                for k in range(2):
                    _row_copy(y_hbm, buf.at[dst_half, k], s_ref[k, t], t, sems.at[dst_half]).start()
            return carry

        lax.fori_loop(0, TM // DMA_UNROLL, body, 0)

    @pl.when(m == 0)
    def _():
        gather_tile(slot_ref, 0)

    @pl.when(m + 1 < pl.num_programs(0))
    def _():
        gather_tile(slot_next_ref, 1 - half)

    def drain(i, carry):
        for _ in range(2 * DMA_UNROLL):
            _row_copy(y_hbm, buf.at[half, 0], 0, 0, sems.at[half]).wait()
        return carry

    lax.fori_loop(0, TM // DMA_UNROLL, drain, 0)
    f = rw_ref[:, 0:1] * _load_rows(buf.at[half, 0], TM) + rw_ref[:, 1:2] * _load_rows(buf.at[half, 1], TM)
    x2 = _layer_norm(ALPHA * x1_ref[...] + mod_ref[5:6, :] * f, lng_ref[...], lnb_ref[...])
    if final:
        @pl.when(m < CTX_TILES)
        def _():
            out_a_ref[...] = x2

        @pl.when(m >= CTX_TILES)
        def _():
            out_b_ref[...] = x2
    else:
        out_a_ref[...] = x2
        out_b_ref[...] = (x2 * (1.0 + modn_ref[1:2, :]) + modn_ref[0:1, :]).astype(BF16)


def _combine(slots, y_slots, rw, x1, mod_l, mod_next, ln_g, ln_b, final):
    row = lambda m: (m, 0)
    const = lambda m: (0, 0)
    mod_spec = pl.BlockSpec((None, N_MOD, D), lambda m: (_cond_row(m, CTX_TILES, TILES_PER_LAT), 0, 0))
    if final:
        out_shape = (jax.ShapeDtypeStruct((T_CTX, D), F32), jax.ShapeDtypeStruct((T_LAT, D), F32))
        out_specs = (pl.BlockSpec((TM, D), lambda m: (jnp.minimum(m, CTX_TILES - 1), 0)),
                     pl.BlockSpec((TM, D), lambda m: (jnp.maximum(m - CTX_TILES, 0), 0)))
    else:
        out_shape = (jax.ShapeDtypeStruct((T, D), F32), jax.ShapeDtypeStruct((T, D), BF16))
        out_specs = (pl.BlockSpec((TM, D), row), pl.BlockSpec((TM, D), row))
    return pl.pallas_call(
        functools.partial(_combine_kernel, final),
        out_shape=out_shape,
        grid=(N_TILES,),
        in_specs=[pl.BlockSpec((None, 2, TM), lambda m: (m, 0, 0), memory_space=pltpu.SMEM),
                  pl.BlockSpec((None, 2, TM), lambda m: (jnp.minimum(m + 1, N_TILES - 1), 0, 0),
                               memory_space=pltpu.SMEM),
                  pl.BlockSpec(memory_space=pl.ANY),
                  pl.BlockSpec((TM, LANE), row), pl.BlockSpec((TM, D), row),
                  mod_spec, mod_spec,
                  pl.BlockSpec((1, D), const), pl.BlockSpec((1, D), const)],
        out_specs=out_specs,
        scratch_shapes=[pltpu.VMEM((2, 2, TM * ROW_TILE, LANE), F32), pltpu.SemaphoreType.DMA((2,))],
        compiler_params=_params(("arbitrary",)),
        name="moe_combine_ln",
    )(slots, slots, y_slots, rw, x1, mod_l, mod_next, ln_g, ln_b)


def kernel(x_prompt, x_sample, c, c_ctx, cache_k, cache_v, w_in, conv_a_w, w_a_out, conv_b_w, conv_b_b,
           norm_b_g, norm_b_b, w_b_out, q_norm_g, k_norm_g, w_c_out, w_o, w_ada, b_ada, ln_g, ln_b,
           w_router, router_bias, w_gate_up, w_down):
    x = jnp.concatenate([x_prompt.reshape(T_CTX, D), x_sample.reshape(T_LAT, D)], axis=0)
    cond16 = jnp.concatenate([c_ctx[None, :], c, jnp.zeros((16 - 1 - N_LAT_SEQ, D), F32)], axis=0)
    mod = _modulation(cond16, w_ada, b_ada).reshape(DEPTH, 16, N_MOD, D)

    wa_b, wb_b, wc_b, wo_b = (w.astype(BF16) for w in (w_a_out, w_b_out, w_c_out, w_o))
    wr = jnp.pad(w_router, ((0, 0), (0, LANE - N_EXP)))
    wr_hi = wr.astype(BF16)
    wr_lo = (wr - wr_hi.astype(F32)).astype(BF16)
    rbias = router_bias.reshape(N_EXP, 1)

    cos_tab, sin_tab = _rope_tables()
    half = np.arange(LANE) // HEAD
    ones_bd = jnp.asarray(half[:, None] == half[None, :], BF16)
    ck = cache_k.reshape(N_LAT_SEQ, DEPTH, PAST, D_KV)
    cv = cache_v.reshape(N_LAT_SEQ, DEPTH, PAST, D_KV)

    h = _modulate(x, mod[0])
    new_k, new_v = [], []
    for l in range(DEPTH):
        ab, acx = _proj_a(h, w_in, l)
        glu = _proj_b(h, w_in, l)
        gates = _proj_gates(h, w_in, l)
        q, katt, vatt, kn, vf = _proj_qkv(h, w_in, jnp.tile(q_norm_g[l], N_Q)[None, :],
                                          jnp.tile(k_norm_g[l], N_KV)[None, :], cos_tab, sin_tab, ones_bd, l)
        new_k.append(kn.reshape(N_CTX_SEQ, CTX_LEN, N_KV, HEAD))
        new_v.append(vf.reshape(N_CTX_SEQ, CTX_LEN, N_KV, HEAD))
        ya_pre, yb_pre = _convs(acx, ab, glu, conv_a_w[l], conv_b_w[l], conv_b_b[l][None, :],
                                norm_b_g[l][None, :], norm_b_b[l][None, :])
        o = _attention(q, katt, vatt, ck, cv, l)
        x1, h2, route, rw, counts = _post(ya_pre, yb_pre, o, gates, x, mod[l], wa_b, wb_b, wc_b, wo_b,
                                          ln_g[l, 0][None, :], ln_b[l, 0][None, :], wr_hi, wr_lo, rbias, l)
        cnt = counts[:, 0]
        seg_start, last_blk, blk_row, blk_exp, n_act = _block_plan(cnt)
        tile_cnt, tile_off, tile_dst, local, slots = _tile_plan(route, seg_start)
        xs = _dispatch(tile_cnt, tile_off, tile_dst, last_blk, local, h2)
        y_slots = _expert_ffn(blk_row, blk_exp, n_act, xs, w_gate_up, w_down, l)
        x, h = _combine(slots, y_slots, rw, x1, mod[l], mod[min(l + 1, DEPTH - 1)],
                        ln_g[l, 1][None, :], ln_b[l, 1][None, :], final=l == DEPTH - 1)
    y_prompt = x.reshape(N_CTX_SEQ, CTX_LEN, D)
    y_sample = h.reshape(N_LAT_SEQ, LAT_LEN, D)
    return y_prompt, y_sample, jnp.stack(new_k, axis=1), jnp.stack(new_v, axis=1)
```

```python
import functools

import numpy as np
import jax
import jax.numpy as jnp
from jax import lax
from jax.experimental import pallas as pl
from jax.experimental.pallas import tpu as pltpu

F32 = jnp.float32
BF16 = jnp.bfloat16
I32 = jnp.int32

D = 1024
DEPTH = 4
N_CTX_SEQ = 16
CTX_LEN = 256
N_LAT_SEQ = 8
LAT_LEN = 1024
PAST = 512
T_CTX = N_CTX_SEQ * CTX_LEN
T_LAT = N_LAT_SEQ * LAT_LEN
T = T_CTX + T_LAT
GRID_W = 64
HEAD = 64
N_Q = 16
N_KV = 4
GROUP = 4
D_KV = N_KV * HEAD
N_EXP = 16
N_GRP = 4
EXP_PER_GRP = 4
D_EXP = 512
N_MOD = 6
ALPHA = (2 * DEPTH) ** 0.25
LN_EPS = 1e-5
RMS_EPS = 1e-6
ROPE_THETA = 10000.0
Q_SCALE = HEAD ** -0.5 * float(np.log2(np.e))

LANE = 128
TM = 512
N_TILES = T // TM
CTX_TILES = T_CTX // TM
TILES_PER_LAT = LAT_LEN // TM
POST_ROWS = 256
TMP = 1024
SEG = 256
N_SEG = T // SEG
CTX_SEGS = T_CTX // SEG
SEGS_PER_LAT = LAT_LEN // SEG
HALO_A = 8
HALO_B = 16
SH_ROWS = SEG + 2 * HALO_B - 8
FFN_BLK = 256
N_FFN_BLOCKS = (2 * T) // FFN_BLK + N_EXP
N_SLOTS = N_FFN_BLOCKS * FFN_BLK
VMEM_LIMIT = 56 * 1024 * 1024

COL_AB, COL_AC, COL_AX, COL_BU, COL_BG, COL_Q, COL_KV, COL_GATES = 0, 2, 4, 6, 8, 10, 12, 13


def _params(sem):
    return pltpu.CompilerParams(dimension_semantics=sem, vmem_limit_bytes=VMEM_LIMIT)


def _cond_row(m, tiles_ctx, tiles_per_lat):
    return jnp.where(m < tiles_ctx, 0, 1 + (m - tiles_ctx) // tiles_per_lat)


def _layer_norm(x, g, b):
    mu = jnp.mean(x, axis=-1, keepdims=True)
    xc = x - mu
    var = jnp.mean(xc * xc, axis=-1, keepdims=True)
    return xc * lax.rsqrt(var + LN_EPS) * g + b


def _sigmoid(x):
    return 1.0 / (1.0 + jnp.exp(-x))


ROW_TILE = D // LANE


def _store_rows(ref, x):
    for j in range(ROW_TILE):
        ref[pl.ds(j, x.shape[0], stride=ROW_TILE), :] = x[:, j * LANE:(j + 1) * LANE]


def _load_rows(ref, n_rows):
    return jnp.concatenate([ref[pl.ds(j, n_rows, stride=ROW_TILE), :] for j in range(ROW_TILE)], axis=1)


def _mod_kernel(cond_ref, w_ref, b_ref, o_ref):
    cnd = cond_ref[...]
    s = (cnd * _sigmoid(cnd)).astype(BF16)
    o_ref[...] = jnp.dot(s, w_ref[...].astype(BF16), preferred_element_type=F32) + b_ref[...]


def _modulation(cond16, w_ada, b_ada):
    n_col = N_MOD * D
    tn = 1024
    return pl.pallas_call(
        _mod_kernel,
        out_shape=jax.ShapeDtypeStruct((DEPTH, 16, n_col), F32),
        grid=(DEPTH, n_col // tn),
        in_specs=[
            pl.BlockSpec((16, D), lambda l, n: (0, 0)),
            pl.BlockSpec((None, D, tn), lambda l, n: (l, 0, n)),
            pl.BlockSpec((None, 1, tn), lambda l, n: (l, 0, n)),
        ],
        out_specs=pl.BlockSpec((None, 16, tn), lambda l, n: (l, 0, n)),
        compiler_params=_params(("parallel", "parallel")),
        name="adaln_mod",
    )(cond16, w_ada, b_ada.reshape(DEPTH, 1, n_col))


def _modulate_kernel(x_ref, mod_ref, h_ref):
    h_ref[...] = (x_ref[...] * (1.0 + mod_ref[1:2, :]) + mod_ref[0:1, :]).astype(BF16)


def _modulate(x, mod_l):
    return pl.pallas_call(
        _modulate_kernel,
        out_shape=jax.ShapeDtypeStruct((T, D), BF16),
        grid=(N_TILES,),
        in_specs=[
            pl.BlockSpec((TM, D), lambda m: (m, 0)),
            pl.BlockSpec((None, N_MOD, D), lambda m: (_cond_row(m, CTX_TILES, TILES_PER_LAT), 0, 0)),
        ],
        out_specs=pl.BlockSpec((TM, D), lambda m: (m, 0)),
        compiler_params=_params(("parallel",)),
        name="modulate",
    )(x, mod_l)


WCOL = 512


def _cast_weights(first, pairs):
    @pl.when(first)
    def _():
        for src, dst in pairs:
            dst[...] = src[...].astype(BF16)


def _proj_a_kernel(h_ref, wb_ref, wc_ref, wx_ref, ab_ref, acx_ref, wb_s, wc_s, wx_s):
    _cast_weights(pl.program_id(1) == 0, [(wb_ref, wb_s), (wc_ref, wc_s), (wx_ref, wx_s)])
    h = h_ref[...]
    ab_ref[...] = jnp.dot(h, wb_s[...], preferred_element_type=F32).astype(BF16)
    acx_ref[...] = (jnp.dot(h, wc_s[...], preferred_element_type=F32)
                    * jnp.dot(h, wx_s[...], preferred_element_type=F32))


def _proj_b_kernel(h_ref, wu_ref, wg_ref, glu_ref, wu_s, wg_s):
    _cast_weights(pl.program_id(1) == 0, [(wu_ref, wu_s), (wg_ref, wg_s)])
    h = h_ref[...]
    glu_ref[...] = (jnp.dot(h, wu_s[...], preferred_element_type=F32)
                    * _sigmoid(jnp.dot(h, wg_s[...], preferred_element_type=F32)))


def _proj_gate_kernel(h_ref, w0_ref, w1_ref, g_ref, w_s):
    _cast_weights(pl.program_id(1) == 0, [(w0_ref, w_s.at[:, 0:WCOL]), (w1_ref, w_s.at[:, WCOL:2 * WCOL])])
    g_ref[...] = _sigmoid(jnp.dot(h_ref[...], w_s[...], preferred_element_type=F32)).astype(BF16)


def _w_spec(l, col0, step=1):
    return pl.BlockSpec((None, D, WCOL), lambda c, m: (l, 0, col0 + step * c))


def _proj_a(h, w_in, l):
    out_spec = pl.BlockSpec((TMP, WCOL), lambda c, m: (m, c))
    return pl.pallas_call(
        _proj_a_kernel,
        out_shape=(jax.ShapeDtypeStruct((T, D), BF16), jax.ShapeDtypeStruct((T, D), F32)),
        grid=(D // WCOL, T // TMP),
        in_specs=[pl.BlockSpec((TMP, D), lambda c, m: (m, 0)),
                  _w_spec(l, COL_AB), _w_spec(l, COL_AC), _w_spec(l, COL_AX)],
        out_specs=(out_spec, out_spec),
        scratch_shapes=[pltpu.VMEM((D, WCOL), BF16)] * 3,
        compiler_params=_params(("parallel", "arbitrary")),
        name="proj_a",
    )(h, w_in, w_in, w_in)


def _proj_b(h, w_in, l):
    return pl.pallas_call(
        _proj_b_kernel,
        out_shape=jax.ShapeDtypeStruct((T, D), F32),
        grid=(D // WCOL, T // TMP),
        in_specs=[pl.BlockSpec((TMP, D), lambda c, m: (m, 0)), _w_spec(l, COL_BU), _w_spec(l, COL_BG)],
        out_specs=pl.BlockSpec((TMP, WCOL), lambda c, m: (m, c)),
        scratch_shapes=[pltpu.VMEM((D, WCOL), BF16)] * 2,
        compiler_params=_params(("parallel", "arbitrary")),
        name="proj_b",
    )(h, w_in, w_in)


def _proj_gates(h, w_in, l):
    tn = 2 * WCOL
    return pl.pallas_call(
        _proj_gate_kernel,
        out_shape=jax.ShapeDtypeStruct((T, 3 * D), BF16),
        grid=(3 * D // tn, T // TMP),
        in_specs=[pl.BlockSpec((TMP, D), lambda c, m: (m, 0)),
                  _w_spec(l, COL_GATES, 2), _w_spec(l, COL_GATES + 1, 2)],
        out_specs=pl.BlockSpec((TMP, tn), lambda c, m: (m, c)),
        scratch_shapes=[pltpu.VMEM((D, tn), BF16)],
        compiler_params=_params(("parallel", "arbitrary")),
        name="proj_gates",
    )(h, w_in, w_in)


def _head_mean_square(x, ones_bd):
    out = []
    for c in range(x.shape[1] // LANE):
        sq = x[:, c * LANE:(c + 1) * LANE]
        out.append(jnp.dot((sq * sq).astype(BF16), ones_bd, preferred_element_type=F32))
    return jnp.concatenate(out, axis=1) * (1.0 / HEAD)


def _rope(x, cos, sin, first_half):
    out = []
    for c in range(x.shape[1] // LANE):
        xc = x[:, c * LANE:(c + 1) * LANE]
        partner = jnp.where(first_half, pltpu.roll(xc, LANE - 16, axis=1), pltpu.roll(xc, 16, axis=1))
        out.append(xc * cos + partner * sin)
    return jnp.concatenate(out, axis=1)


def _qkv_kernel(h_ref, wq_ref, wkv_ref, gq_ref, gk_ref, cos_ref, sin_ref, ones_ref,
                q_ref, katt_ref, vatt_ref, kn_ref, vf_ref, wq_s, wkv_s):
    _cast_weights(pl.program_id(0) == 0, [(wq_ref, wq_s), (wkv_ref, wkv_s)])
    h = h_ref[...]
    ones_bd = ones_ref[...]
    cos = cos_ref[...]
    sin = sin_ref[...]
    lane = lax.broadcasted_iota(I32, (TM, LANE), 1)
    first_half = (lane & 16) == 0
    q = jnp.dot(h, wq_s[...], preferred_element_type=F32)
    qn = q * lax.rsqrt(_head_mean_square(q, ones_bd) + RMS_EPS) * gq_ref[...]
    q_ref[...] = (_rope(qn, cos, sin, first_half) * Q_SCALE).astype(BF16)
    kv = jnp.dot(h, wkv_s[...], preferred_element_type=F32)
    k = kv[:, :D_KV]
    v = kv[:, D_KV:]
    kn = k * lax.rsqrt(_head_mean_square(k, ones_bd) + RMS_EPS) * gk_ref[...]
    katt_ref[...] = _rope(kn, cos, sin, first_half).astype(BF16)
    vatt_ref[...] = v.astype(BF16)

    @pl.when(pl.program_id(0) < CTX_TILES)
    def _():
        kn_ref[...] = kn
        vf_ref[...] = v


def _proj_qkv(h, w_in, gq, gk, cos_tab, sin_tab, ones_bd, l):
    def tab_idx(m):
        return jnp.where(m < CTX_TILES, TILES_PER_LAT, (m - CTX_TILES) % TILES_PER_LAT)

    row = lambda m: (m, 0)
    ctx_row = lambda m: (jnp.minimum(m, CTX_TILES - 1), 0)
    return pl.pallas_call(
        _qkv_kernel,
        out_shape=(jax.ShapeDtypeStruct((T, D), BF16),
                   jax.ShapeDtypeStruct((T, D_KV), BF16),
                   jax.ShapeDtypeStruct((T, D_KV), BF16),
                   jax.ShapeDtypeStruct((T_CTX, D_KV), F32),
                   jax.ShapeDtypeStruct((T_CTX, D_KV), F32)),
        grid=(N_TILES,),
        in_specs=[
            pl.BlockSpec((TM, D), row),
            pl.BlockSpec((None, D, D), lambda m: (l, 0, COL_Q // 2)),
            pl.BlockSpec((None, D, 2 * D_KV), lambda m: (l, 0, COL_KV)),
            pl.BlockSpec((1, D), lambda m: (0, 0)),
            pl.BlockSpec((1, D_KV), lambda m: (0, 0)),
            pl.BlockSpec((TM, LANE), lambda m: (tab_idx(m), 0)),
            pl.BlockSpec((TM, LANE), lambda m: (tab_idx(m), 0)),
            pl.BlockSpec((LANE, LANE), lambda m: (0, 0)),
        ],
        out_specs=(pl.BlockSpec((TM, D), row), pl.BlockSpec((TM, D_KV), row),
                   pl.BlockSpec((TM, D_KV), row), pl.BlockSpec((TM, D_KV), ctx_row),
                   pl.BlockSpec((TM, D_KV), ctx_row)),
        scratch_shapes=[pltpu.VMEM((D, D), BF16), pltpu.VMEM((D, 2 * D_KV), BF16)],
        compiler_params=_params(("arbitrary",)),
        name="proj_qkv",
    )(h, w_in, w_in, gq, gk, cos_tab, sin_tab, ones_bd)


def _rope_tables():
    lane = np.arange(LANE)
    j = lane % 16
    freqs = jnp.power(ROPE_THETA, -jnp.arange(16, dtype=F32) / 16)[j]
    pos = jnp.arange(LAT_LEN, dtype=I32)
    row = (pos // GRID_W).astype(F32)
    col = (pos % GRID_W).astype(F32)
    use_row = jnp.asarray((lane % HEAD) < HEAD // 2)
    p = jnp.where(use_row[None, :], row[:, None], col[:, None])
    ang = p * freqs[None, :]
    sign = jnp.asarray(np.where((lane & 16) == 0, -1.0, 1.0), F32)
    cos = jnp.concatenate([jnp.cos(ang), jnp.ones((TM, LANE), F32)], axis=0)
    sin = jnp.concatenate([jnp.sin(ang) * sign[None, :], jnp.zeros((TM, LANE), F32)], axis=0)
    return cos, sin


def _conv_kernel(acx_ref, acx_l_ref, acx_r_ref, ab_ref, glu_ref, glu_l_ref, glu_r_ref,
                 wa_ref, wb_ref, bb_ref, ng_ref, nb_ref, ya_ref, yb_ref, pad_a, pad_b, u_ref, sh_ref):
    s = pl.program_id(0)
    lat = s >= CTX_SEGS
    pos = (s - CTX_SEGS) % SEGS_PER_LAT
    has_left = jnp.logical_and(lat, pos != 0)
    has_right = jnp.logical_and(lat, pos != SEGS_PER_LAT - 1)

    pad_a[0:HALO_A, :] = jnp.where(has_left, acx_l_ref[...], 0.0)
    pad_a[HALO_A:HALO_A + SEG, :] = acx_ref[...]
    pad_a[HALO_A + SEG:, :] = jnp.where(has_right, acx_r_ref[...], 0.0)
    pad_b[0:HALO_B, :] = jnp.where(has_left, glu_l_ref[...], 0.0)
    pad_b[HALO_B:HALO_B + SEG, :] = glu_ref[...]
    pad_b[HALO_B + SEG:, :] = jnp.where(has_right, glu_r_ref[...], 0.0)

    conv_a = (wa_ref[0:1, :] * pad_a[HALO_A - 1:HALO_A - 1 + SEG, :]
              + wa_ref[1:2, :] * pad_a[HALO_A:HALO_A + SEG, :]
              + wa_ref[2:3, :] * pad_a[HALO_A + 1:HALO_A + 1 + SEG, :])
    ya_ref[...] = (ab_ref[...].astype(F32) * conv_a).astype(BF16)

    rows = 64
    kb = wb_ref.shape[0]

    def lane_chunk(c, carry):
        lanes = pl.ds(pl.multiple_of(c * LANE, LANE), LANE)
        for b in range(1, 8):
            sh_ref[b, :, :] = pad_b[b:b + SH_ROWS, lanes]
        for r in range(SEG // rows):
            acc = jnp.zeros((rows, LANE), F32)
            for k in range(kb):
                off = HALO_B + k - kb // 2
                row0 = r * rows + 8 * (off // 8)
                if off % 8 == 0:
                    src = pad_b[row0:row0 + rows, lanes]
                else:
                    src = sh_ref[off % 8, row0:row0 + rows, :]
                acc = acc + wb_ref[k:k + 1, lanes] * src
            u_ref[r * rows:(r + 1) * rows, lanes] = acc
        return carry

    lax.fori_loop(0, D // LANE, lane_chunk, 0)
    u = _layer_norm(u_ref[...] + bb_ref[...], ng_ref[...], nb_ref[...])
    yb_ref[...] = (u * _sigmoid(u)).astype(BF16)


def _convs(acx, ab, glu, conv_a_w, conv_b_w, conv_b_b, norm_g, norm_b):
    seg = lambda s: (s, 0)
    const = lambda s: (0, 0)
    ra, rb = SEG // HALO_A, SEG // HALO_B
    left_a = lambda s: (jnp.maximum(s * ra - 1, 0), 0)
    right_a = lambda s: (jnp.minimum((s + 1) * ra, T // HALO_A - 1), 0)
    left_b = lambda s: (jnp.maximum(s * rb - 1, 0), 0)
    right_b = lambda s: (jnp.minimum((s + 1) * rb, T // HALO_B - 1), 0)
    return pl.pallas_call(
        _conv_kernel,
        out_shape=(jax.ShapeDtypeStruct((T, D), BF16), jax.ShapeDtypeStruct((T, D), BF16)),
        grid=(N_SEG,),
        in_specs=[
            pl.BlockSpec((SEG, D), seg), pl.BlockSpec((HALO_A, D), left_a), pl.BlockSpec((HALO_A, D), right_a),
            pl.BlockSpec((SEG, D), seg),
            pl.BlockSpec((SEG, D), seg), pl.BlockSpec((HALO_B, D), left_b), pl.BlockSpec((HALO_B, D), right_b),
            pl.BlockSpec(conv_a_w.shape, const), pl.BlockSpec(conv_b_w.shape, const),
            pl.BlockSpec((1, D), const), pl.BlockSpec((1, D), const), pl.BlockSpec((1, D), const),
        ],
        out_specs=(pl.BlockSpec((SEG, D), seg), pl.BlockSpec((SEG, D), seg)),
        scratch_shapes=[pltpu.VMEM((SEG + 2 * HALO_A, D), F32),
                        pltpu.VMEM((SEG + 2 * HALO_B, D), F32),
                        pltpu.VMEM((SEG, D), F32),
                        pltpu.VMEM((8, SH_ROWS, LANE), F32)],
        compiler_params=_params(("parallel",)),
        name="convs",
    )(acx, acx, acx, ab, glu, glu, glu, conv_a_w, conv_b_w, conv_b_b, norm_g, norm_b)


def _attend(q_ref, key_refs, val_refs, o_ref, ones_denominator):
    nt = (((1,), (1,)), ((), ()))
    keys = [r[...].astype(BF16) for r in key_refs]
    vals = [r[...].astype(BF16) for r in val_refs]
    n_q = q_ref.shape[0]
    for g in range(N_KV):
        kg = [k[:, g * HEAD:(g + 1) * HEAD] for k in keys]
        vg = [v[:, g * HEAD:(g + 1) * HEAD] for v in vals]
        if ones_denominator:
            vg = [jnp.concatenate([v, jnp.ones_like(v)], axis=1) for v in vg]
        qg = jnp.concatenate([q_ref[:, (g * GROUP + hh) * HEAD:(g * GROUP + hh + 1) * HEAD]
                              for hh in range(GROUP)], axis=0)
        s = [lax.dot_general(qg, k, nt, preferred_element_type=F32) for k in kg]
        mx = functools.reduce(jnp.maximum, [jnp.max(x, axis=-1, keepdims=True) for x in s])
        p = [jnp.exp2(x - mx) for x in s]
        acc = functools.reduce(jnp.add, [jnp.dot(x.astype(BF16), v, preferred_element_type=F32)
                                         for x, v in zip(p, vg)])
        if ones_denominator:
            den = acc[:, HEAD:HEAD + 1]
            acc = acc[:, :HEAD]
        else:
            den = functools.reduce(jnp.add, [jnp.sum(x, axis=-1, keepdims=True) for x in p])
        out = (acc / den).astype(BF16)
        for hh in range(GROUP):
            hd = g * GROUP + hh
            o_ref[:, hd * HEAD:(hd + 1) * HEAD] = out[hh * n_q:(hh + 1) * n_q, :]


def _attn_ctx_kernel(q_ref, k_ref, v_ref, o_ref):
    _attend(q_ref, [k_ref], [v_ref], o_ref, ones_denominator=False)


def _attn_lat_kernel(q_ref, k_ref, v_ref, ck_ref, cv_ref, o_in_ref, o_ref):
    del o_in_ref
    _attend(q_ref, [ck_ref, k_ref], [cv_ref, v_ref], o_ref, ones_denominator=True)


def _attention(q, katt, vatt, cache_k, cache_v, l):
    o = pl.pallas_call(
        _attn_ctx_kernel,
        out_shape=jax.ShapeDtypeStruct((T, D), BF16),
        grid=(N_CTX_SEQ,),
        in_specs=[pl.BlockSpec((CTX_LEN, D), lambda b: (b, 0)),
                  pl.BlockSpec((CTX_LEN, D_KV), lambda b: (b, 0)),
                  pl.BlockSpec((CTX_LEN, D_KV), lambda b: (b, 0))],
        out_specs=pl.BlockSpec((CTX_LEN, D), lambda b: (b, 0)),
        compiler_params=_params(("parallel",)),
        name="attn_ctx",
    )(q, katt, vatt)
    seg0 = CTX_SEGS
    lat0 = T_CTX // LAT_LEN
    return pl.pallas_call(
        _attn_lat_kernel,
        out_shape=jax.ShapeDtypeStruct((T, D), BF16),
        grid=(N_LAT_SEQ, SEGS_PER_LAT),
        in_specs=[pl.BlockSpec((SEG, D), lambda b, i: (seg0 + b * SEGS_PER_LAT + i, 0)),
                  pl.BlockSpec((LAT_LEN, D_KV), lambda b, i: (lat0 + b, 0)),
                  pl.BlockSpec((LAT_LEN, D_KV), lambda b, i: (lat0 + b, 0)),
                  pl.BlockSpec((None, None, PAST, D_KV), lambda b, i: (b, l, 0, 0)),
                  pl.BlockSpec((None, None, PAST, D_KV), lambda b, i: (b, l, 0, 0)),
                  pl.BlockSpec(memory_space=pl.ANY)],
        out_specs=pl.BlockSpec((SEG, D), lambda b, i: (seg0 + b * SEGS_PER_LAT + i, 0)),
        input_output_aliases={5: 0},
        compiler_params=_params(("parallel", "parallel")),
        name="attn_lat",
    )(q, katt, vatt, cache_k, cache_v, o)


def _post_kernel(ya_ref, yb_ref, o_ref, g_ref, x_ref, mod_ref, wa_ref, wb_ref, wc_ref, wo_ref,
                 lng_ref, lnb_ref, wr_hi_ref, wr_lo_ref, rb_ref,
                 x1_ref, h2_ref, route_ref, rw_ref, cnt_ref, carry_ref):
    m = pl.program_id(0)

    @pl.when(m == 0)
    def _():
        carry_ref[...] = jnp.zeros_like(carry_ref)

    carry = carry_ref[...]
    r_i = lax.broadcasted_iota(I32, (POST_ROWS, POST_ROWS), 0)
    c_i = lax.broadcasted_iota(I32, (POST_ROWS, POST_ROWS), 1)
    upper = jnp.where(r_i < c_i, 1.0, 0.0).astype(BF16)
    for part in range(TM // POST_ROWS):
        rows = slice(part * POST_ROWS, (part + 1) * POST_ROWS)
        carry = _post_part(rows, carry, upper, ya_ref, yb_ref, o_ref, g_ref, x_ref, mod_ref, wa_ref, wb_ref,
                           wc_ref, wo_ref, lng_ref, lnb_ref, wr_hi_ref, wr_lo_ref, rb_ref,
                           x1_ref, h2_ref, route_ref, rw_ref)
    carry_ref[...] = carry
    cnt_ref[...] = carry.astype(I32)
    route_ref[4:8, :] = jnp.zeros((4, TM), I32)


def _post_part(rows, carry, upper, ya_ref, yb_ref, o_ref, g_ref, x_ref, mod_ref, wa_ref, wb_ref, wc_ref,
               wo_ref, lng_ref, lnb_ref, wr_hi_ref, wr_lo_ref, rb_ref, x1_ref, h2_ref, route_ref, rw_ref):
    n = POST_ROWS
    ya = jnp.dot(ya_ref[rows, :], wa_ref[...], preferred_element_type=F32)
    yb = jnp.dot(yb_ref[rows, :], wb_ref[...], preferred_element_type=F32)
    yc = jnp.dot(o_ref[rows, :], wc_ref[...], preferred_element_type=F32)
    merged = (g_ref[rows, 0:D].astype(F32) * ya + g_ref[rows, D:2 * D].astype(F32) * yb
              + g_ref[rows, 2 * D:3 * D].astype(F32) * yc)
    mix = jnp.dot(merged.astype(BF16), wo_ref[...], preferred_element_type=F32)
    x1 = _layer_norm(ALPHA * x_ref[rows, :] + mod_ref[2:3, :] * mix, lng_ref[...], lnb_ref[...])
    x1_ref[rows, :] = x1
    h2 = x1 * (1.0 + mod_ref[4:5, :]) + mod_ref[3:4, :]

    hi = h2.astype(BF16)
    h2_ref[rows, :] = hi
    lo = (h2 - hi.astype(F32)).astype(BF16)
    wr_hi = wr_hi_ref[...]
    logits = (jnp.dot(hi, wr_hi, preferred_element_type=F32)
              + jnp.dot(lo, wr_hi, preferred_element_type=F32)
              + jnp.dot(hi, wr_lo_ref[...], preferred_element_type=F32))
    scores = _sigmoid(logits.T[0:N_EXP, :])
    sel = scores + rb_ref[...]

    gscore = []
    for g in range(N_GRP):
        r = [sel[g * EXP_PER_GRP + j:g * EXP_PER_GRP + j + 1, :] for j in range(EXP_PER_GRP)]
        pairs = [r[a] + r[b] for a in range(EXP_PER_GRP) for b in range(a + 1, EXP_PER_GRP)]
        gscore.append(functools.reduce(jnp.maximum, pairs))
    best = functools.reduce(jnp.maximum, gscore)
    gsel = jnp.full(best.shape, N_GRP - 1, I32)
    for g in range(N_GRP - 2, -1, -1):
        gsel = jnp.where(gscore[g] == best, g, gsel)

    eidx = lax.broadcasted_iota(I32, (N_EXP, n), 0)
    neg = jnp.float32(-jnp.inf)
    cand = jnp.where((eidx // EXP_PER_GRP) == gsel, sel, neg)
    top1 = jnp.max(cand, axis=0, keepdims=True)
    idx1 = jnp.min(jnp.where(cand == top1, eidx, N_EXP), axis=0, keepdims=True)
    cand2 = jnp.where(eidx == idx1, neg, cand)
    top2 = jnp.max(cand2, axis=0, keepdims=True)
    idx2 = jnp.min(jnp.where(cand2 == top2, eidx, N_EXP), axis=0, keepdims=True)
    is1 = eidx == idx1
    is2 = eidx == idx2
    w1 = jnp.sum(jnp.where(is1, scores, 0.0), axis=0, keepdims=True)
    w2 = jnp.sum(jnp.where(is2, scores, 0.0), axis=0, keepdims=True)
    wsum = w1 + w2
    w1 = w1 / wsum
    w2 = w2 / wsum

    onehot = jnp.where(jnp.logical_or(is1, is2), 1.0, 0.0)
    prefix = jnp.dot(onehot.astype(BF16), upper, preferred_element_type=F32) + carry[:, 0:1]
    rank1 = jnp.sum(jnp.where(is1, prefix, 0.0), axis=0, keepdims=True)
    rank2 = jnp.sum(jnp.where(is2, prefix, 0.0), axis=0, keepdims=True)

    route_ref[0:1, rows] = idx1
    route_ref[1:2, rows] = idx2
    route_ref[2:3, rows] = rank1.astype(I32)
    route_ref[3:4, rows] = rank2.astype(I32)
    wrow = lax.broadcasted_iota(I32, (LANE, n), 0)
    wcols = jnp.where(wrow == 0, w1, jnp.where(wrow == 1, w2, 0.0))
    rw_ref[rows, :] = wcols.T
    return carry + jnp.sum(onehot, axis=1, keepdims=True)


def _post(ya_pre, yb_pre, o, gates, x, mod_l, wa, wb, wc, wo, ln_g, ln_b, wr_hi, wr_lo, rbias, l):
    row = lambda m: (m, 0)
    const = lambda m: (0, 0)
    wspec = pl.BlockSpec((None, D, D), lambda m: (l, 0, 0))
    return pl.pallas_call(
        _post_kernel,
        out_shape=(jax.ShapeDtypeStruct((T, D), F32),
                   jax.ShapeDtypeStruct((T, D), BF16),
                   jax.ShapeDtypeStruct((N_TILES, 8, TM), I32),
                   jax.ShapeDtypeStruct((T, LANE), F32),
                   jax.ShapeDtypeStruct((N_EXP, LANE), I32)),
        grid=(N_TILES,),
        in_specs=[pl.BlockSpec((TM, D), row), pl.BlockSpec((TM, D), row), pl.BlockSpec((TM, D), row),
                  pl.BlockSpec((TM, 3 * D), row), pl.BlockSpec((TM, D), row),
                  pl.BlockSpec((None, N_MOD, D), lambda m: (_cond_row(m, CTX_TILES, TILES_PER_LAT), 0, 0)),
                  wspec, wspec, wspec, wspec,
                  pl.BlockSpec((1, D), const), pl.BlockSpec((1, D), const),
                  pl.BlockSpec((D, LANE), const), pl.BlockSpec((D, LANE), const),
                  pl.BlockSpec((N_EXP, 1), const)],
        out_specs=(pl.BlockSpec((TM, D), row), pl.BlockSpec((TM, D), row),
                   pl.BlockSpec((None, 8, TM), lambda m: (m, 0, 0)),
                   pl.BlockSpec((TM, LANE), row),
                   pl.BlockSpec((N_EXP, LANE), const)),
        scratch_shapes=[pltpu.VMEM((N_EXP, LANE), F32)],
        compiler_params=_params(("arbitrary",)),
        name="merge_ln_router",
    )(ya_pre, yb_pre, o, gates, x, mod_l, wa, wb, wc, wo, ln_g, ln_b, wr_hi, wr_lo, rbias)


def _row_copy(src, dst, src_row, dst_row, sem):
    return pltpu.make_async_copy(src.at[pl.ds(pl.multiple_of(src_row * ROW_TILE, ROW_TILE), ROW_TILE), :],
                                 dst.at[pl.ds(pl.multiple_of(dst_row * ROW_TILE, ROW_TILE), ROW_TILE), :], sem)


DMA_UNROLL = 8


RUN_CHUNK = 256


def _run_copies(src, dst, src_row, dst_row, n, sem, start):
    def piece(offset, size):
        cp = pltpu.make_async_copy(
            src.at[pl.ds(pl.multiple_of((src_row + offset) * ROW_TILE, ROW_TILE), size * ROW_TILE), :],
            dst.at[pl.ds(pl.multiple_of((dst_row + offset) * ROW_TILE, ROW_TILE), size * ROW_TILE), :], sem)
        if start:
            cp.start()
        else:
            cp.wait()

    def whole(j, carry):
        piece(j * RUN_CHUNK, RUN_CHUNK)
        return carry

    lax.fori_loop(0, n // RUN_CHUNK, whole, 0)
    size = RUN_CHUNK // 2
    while size >= 1:
        @pl.when((n & size) != 0)
        def _(size=size):
            piece(n & ~(2 * size - 1), size)
        size //= 2


def _dispatch_kernel(cnt_ref, off_ref, dst_ref, last_ref, lp_ref, h2_ref, xs_hbm, xc_ref, zero_ref, sem, zsem):
    m = pl.program_id(0)

    @pl.when(m == 0)
    def _():
        zero_ref[...] = jnp.zeros_like(zero_ref)

        def pad_copy(e):
            row = pl.multiple_of(last_ref[e] * ROW_TILE, FFN_BLK * ROW_TILE)
            return pltpu.make_async_copy(zero_ref, xs_hbm.at[pl.ds(row, FFN_BLK * ROW_TILE), :], zsem)

        def zero_start(e, carry):
            pad_copy(e).start()
            return carry

        def zero_wait(e, carry):
            pad_copy(e).wait()
            return carry

        lax.fori_loop(0, N_EXP, zero_start, 0)
        lax.fori_loop(0, N_EXP, zero_wait, 0)

    r_i = lax.broadcasted_iota(I32, (2 * TM, TM), 0)
    hit = jnp.logical_or(r_i == lp_ref[0:1, :], r_i == lp_ref[1:2, :])
    perm = jnp.where(hit, 1.0, 0.0).astype(BF16)
    _store_rows(xc_ref, jnp.dot(perm, h2_ref[...], preferred_element_type=F32))

    for start in (True, False):
        for e in range(N_EXP):
            j = m * N_EXP + e
            _run_copies(xc_ref, xs_hbm, off_ref[j], dst_ref[j], cnt_ref[j], sem, start)


def _dispatch(tile_cnt, tile_off, tile_dst, last_blk, lp, h2):
    grid_spec = pltpu.PrefetchScalarGridSpec(
        num_scalar_prefetch=4,
        grid=(N_TILES,),
        in_specs=[pl.BlockSpec((None, 2, TM), lambda m, *_: (m, 0, 0)),
                  pl.BlockSpec((TM, D), lambda m, *_: (m, 0))],
        out_specs=pl.BlockSpec(memory_space=pl.ANY),
        scratch_shapes=[pltpu.VMEM((2 * TM * ROW_TILE, LANE), F32),
                        pltpu.VMEM((FFN_BLK * ROW_TILE, LANE), F32),
                        pltpu.SemaphoreType.DMA, pltpu.SemaphoreType.DMA],
    )
    return pl.pallas_call(
        _dispatch_kernel,
        out_shape=jax.ShapeDtypeStruct((N_SLOTS * ROW_TILE, LANE), F32),
        grid_spec=grid_spec,
        compiler_params=_params(("arbitrary",)),
        name="moe_dispatch",
    )(tile_cnt, tile_off, tile_dst, last_blk, lp, h2)


FFN_AHEAD = 2


def _ffn_kernel(blk_row_ref, blk_exp_ref, n_act_ref, xs_hbm, wgu_ref, wd_ref, y_ref, wgu_s, wd_s, xbuf, sems):
    i = pl.program_id(0)
    n_act = n_act_ref[0]
    new_expert = jnp.logical_or(i == 0, blk_exp_ref[i] != blk_exp_ref[jnp.maximum(i - 1, 0)])
    _cast_weights(new_expert, [(wgu_ref, wgu_s), (wd_ref, wd_s)])
    rows = FFN_BLK * ROW_TILE

    def fetch(b):
        slot = b % (FFN_AHEAD + 1)
        return pltpu.make_async_copy(xs_hbm.at[pl.ds(pl.multiple_of(blk_row_ref[b] * rows, rows), rows), :],
                                     xbuf.at[slot], sems.at[slot])

    for b in range(FFN_AHEAD):
        @pl.when(jnp.logical_and(i == 0, b < n_act))
        def _(b=b):
            fetch(b).start()

    @pl.when(i + FFN_AHEAD < n_act)
    def _():
        fetch(i + FFN_AHEAD).start()

    @pl.when(i < n_act)
    def _():
        fetch(i).wait()
        x = _load_rows(xbuf.at[i % (FFN_AHEAD + 1)], FFN_BLK).astype(BF16)
        gu = jnp.dot(x, wgu_s[...], preferred_element_type=F32)
        gate = gu[:, :D_EXP]
        up = gu[:, D_EXP:]
        act = (gate * _sigmoid(gate) * up).astype(BF16)
        _store_rows(y_ref, jnp.dot(act, wd_s[...], preferred_element_type=F32))


def _expert_ffn(blk_row, blk_exp, n_act, xs, wgu, wd, l):
    grid_spec = pltpu.PrefetchScalarGridSpec(
        num_scalar_prefetch=3,
        grid=(N_FFN_BLOCKS,),
        in_specs=[pl.BlockSpec(memory_space=pl.ANY),
                  pl.BlockSpec((None, None, D, 2 * D_EXP), lambda i, br, be, na: (l, be[i], 0, 0)),
                  pl.BlockSpec((None, None, D_EXP, D), lambda i, br, be, na: (l, be[i], 0, 0))],
        out_specs=pl.BlockSpec((FFN_BLK * ROW_TILE, LANE), lambda i, br, be, na: (br[i], 0)),
        scratch_shapes=[pltpu.VMEM((D, 2 * D_EXP), BF16), pltpu.VMEM((D_EXP, D), BF16),
                        pltpu.VMEM((FFN_AHEAD + 1, FFN_BLK * ROW_TILE, LANE), F32),
                        pltpu.SemaphoreType.DMA((FFN_AHEAD + 1,))],
    )
    return pl.pallas_call(
        _ffn_kernel,
        out_shape=jax.ShapeDtypeStruct((N_SLOTS * ROW_TILE, LANE), F32),
        grid_spec=grid_spec,
        compiler_params=_params(("arbitrary",)),
        name="expert_ffn",
    )(blk_row, blk_exp, n_act, xs, wgu, wd)


def _block_plan(counts):
    nblk = (counts + FFN_BLK - 1) // FFN_BLK
    end = jnp.cumsum(nblk)
    start = end - nblk
    n_act = end[-1]
    i = jnp.arange(N_FFN_BLOCKS, dtype=I32)
    i_eff = jnp.minimum(i, n_act - 1)
    e = jnp.minimum(jnp.sum(i_eff[:, None] >= end[None, :], axis=1), N_EXP - 1).astype(I32)
    last_blk = jnp.minimum(start + jnp.maximum(nblk - 1, 0), N_FFN_BLOCKS - 1) * FFN_BLK
    return ((start * FFN_BLK).astype(I32), last_blk.astype(I32), i_eff.astype(I32), e,
            n_act.reshape(1).astype(I32))


def _tile_plan(route, seg_start):
    is_exp = route[:, 0:2, :, None] == jnp.arange(N_EXP, dtype=I32)
    cnt = jnp.sum(is_exp, axis=(1, 2), dtype=I32)
    before = jnp.cumsum(cnt, axis=0) - cnt
    off = jnp.cumsum(cnt, axis=1) - cnt
    dst = seg_start[None, :] + before
    pick = lambda tab: jnp.sum(jnp.where(is_exp, tab[:, None, None, :], 0), axis=-1)
    rank = route[:, 2:4, :]
    local = pick(off - before) + rank
    slots = pick(jnp.broadcast_to(seg_start, cnt.shape)) + rank
    return cnt.reshape(-1), off.reshape(-1), dst.reshape(-1), local, slots


def _combine_kernel(final, slot_ref, slot_next_ref, y_hbm, rw_ref, x1_ref, mod_ref, modn_ref,
                    lng_ref, lnb_ref, out_a_ref, out_b_ref, buf, sems):
    m = pl.program_id(0)
    half = m % 2

    def gather_tile(s_ref, dst_half):
        def body(i, carry):
            for u in range(DMA_UNROLL):
                t = i * DMA_UNROLL + u
                for k in range(2):
                    _row_copy(y_hbm, buf.at[dst_half, k], s_ref[k, t], t, sems.at[dst_half]).start()
            return carry

        lax.fori_loop(0, TM // DMA_UNROLL, body, 0)

    @pl.when(m == 0)
    def _():
        gather_tile(slot_ref, 0)

    @pl.when(m + 1 < pl.num_programs(0))
    def _():
        gather_tile(slot_next_ref, 1 - half)

    def drain(i, carry):
        for _ in range(2 * DMA_UNROLL):
            _row_copy(y_hbm, buf.at[half, 0], 0, 0, sems.at[half]).wait()
        return carry

    lax.fori_loop(0, TM // DMA_UNROLL, drain, 0)
    f = rw_ref[:, 0:1] * _load_rows(buf.at[half, 0], TM) + rw_ref[:, 1:2] * _load_rows(buf.at[half, 1], TM)
    x2 = _layer_norm(ALPHA * x1_ref[...] + mod_ref[5:6, :] * f, lng_ref[...], lnb_ref[...])
    if final:
        @pl.when(m < CTX_TILES)
        def _():
            out_a_ref[...] = x2

        @pl.when(m >= CTX_TILES)
        def _():
            out_b_ref[...] = x2
    else:
        out_a_ref[...] = x2
        out_b_ref[...] = (x2 * (1.0 + modn_ref[1:2, :]) + modn_ref[0:1, :]).astype(BF16)


def _combine(slots, y_slots, rw, x1, mod_l, mod_next, ln_g, ln_b, final):
    row = lambda m: (m, 0)
    const = lambda m: (0, 0)
    mod_spec = pl.BlockSpec((None, N_MOD, D), lambda m: (_cond_row(m, CTX_TILES, TILES_PER_LAT), 0, 0))
    if final:
        out_shape = (jax.ShapeDtypeStruct((T_CTX, D), F32), jax.ShapeDtypeStruct((T_LAT, D), F32))
        out_specs = (pl.BlockSpec((TM, D), lambda m: (jnp.minimum(m, CTX_TILES - 1), 0)),
                     pl.BlockSpec((TM, D), lambda m: (jnp.maximum(m - CTX_TILES, 0), 0)))
    else:
        out_shape = (jax.ShapeDtypeStruct((T, D), F32), jax.ShapeDtypeStruct((T, D), BF16))
        out_specs = (pl.BlockSpec((TM, D), row), pl.BlockSpec((TM, D), row))
    return pl.pallas_call(
        functools.partial(_combine_kernel, final),
        out_shape=out_shape,
        grid=(N_TILES,),
        in_specs=[pl.BlockSpec((None, 2, TM), lambda m: (m, 0, 0), memory_space=pltpu.SMEM),
                  pl.BlockSpec((None, 2, TM), lambda m: (jnp.minimum(m + 1, N_TILES - 1), 0, 0),
                               memory_space=pltpu.SMEM),
                  pl.BlockSpec(memory_space=pl.ANY),
                  pl.BlockSpec((TM, LANE), row), pl.BlockSpec((TM, D), row),
                  mod_spec, mod_spec,
                  pl.BlockSpec((1, D), const), pl.BlockSpec((1, D), const)],
        out_specs=out_specs,
        scratch_shapes=[pltpu.VMEM((2, 2, TM * ROW_TILE, LANE), F32), pltpu.SemaphoreType.DMA((2,))],
        compiler_params=_params(("arbitrary",)),
        name="moe_combine_ln",
    )(slots, slots, y_slots, rw, x1, mod_l, mod_next, ln_g, ln_b)


def kernel(x_prompt, x_sample, c, c_ctx, cache_k, cache_v, w_in, conv_a_w, w_a_out, conv_b_w, conv_b_b,
           norm_b_g, norm_b_b, w_b_out, q_norm_g, k_norm_g, w_c_out, w_o, w_ada, b_ada, ln_g, ln_b,
           w_router, router_bias, w_gate_up, w_down):
    x = jnp.concatenate([x_prompt.reshape(T_CTX, D), x_sample.reshape(T_LAT, D)], axis=0)
    cond16 = jnp.concatenate([c_ctx[None, :], c, jnp.zeros((16 - 1 - N_LAT_SEQ, D), F32)], axis=0)
    mod = _modulation(cond16, w_ada, b_ada).reshape(DEPTH, 16, N_MOD, D)

    wa_b, wb_b, wc_b, wo_b = (w.astype(BF16) for w in (w_a_out, w_b_out, w_c_out, w_o))
    wr = jnp.pad(w_router, ((0, 0), (0, LANE - N_EXP)))
    wr_hi = wr.astype(BF16)
    wr_lo = (wr - wr_hi.astype(F32)).astype(BF16)
    rbias = router_bias.reshape(N_EXP, 1)

    cos_tab, sin_tab = _rope_tables()
    half = np.arange(LANE) // HEAD
    ones_bd = jnp.asarray(half[:, None] == half[None, :], BF16)
    ck = cache_k.reshape(N_LAT_SEQ, DEPTH, PAST, D_KV)
    cv = cache_v.reshape(N_LAT_SEQ, DEPTH, PAST, D_KV)

    h = _modulate(x, mod[0])
    new_k, new_v = [], []
    for l in range(DEPTH):
        ab, acx = _proj_a(h, w_in, l)
        glu = _proj_b(h, w_in, l)
        gates = _proj_gates(h, w_in, l)
        q, katt, vatt, kn, vf = _proj_qkv(h, w_in, jnp.tile(q_norm_g[l], N_Q)[None, :],
                                          jnp.tile(k_norm_g[l], N_KV)[None, :], cos_tab, sin_tab, ones_bd, l)
        new_k.append(kn.reshape(N_CTX_SEQ, CTX_LEN, N_KV, HEAD))
        new_v.append(vf.reshape(N_CTX_SEQ, CTX_LEN, N_KV, HEAD))
        ya_pre, yb_pre = _convs(acx, ab, glu, conv_a_w[l], conv_b_w[l], conv_b_b[l][None, :],
                                norm_b_g[l][None, :], norm_b_b[l][None, :])
        o = _attention(q, katt, vatt, ck, cv, l)
        x1, h2, route, rw, counts = _post(ya_pre, yb_pre, o, gates, x, mod[l], wa_b, wb_b, wc_b, wo_b,
                                          ln_g[l, 0][None, :], ln_b[l, 0][None, :], wr_hi, wr_lo, rbias, l)
        cnt = counts[:, 0]
        seg_start, last_blk, blk_row, blk_exp, n_act = _block_plan(cnt)
        tile_cnt, tile_off, tile_dst, local, slots = _tile_plan(route, seg_start)
        xs = _dispatch(tile_cnt, tile_off, tile_dst, last_blk, local, h2)
        y_slots = _expert_ffn(blk_row, blk_exp, n_act, xs, w_gate_up, w_down, l)
        x, h = _combine(slots, y_slots, rw, x1, mod[l], mod[min(l + 1, DEPTH - 1)],
                        ln_g[l, 1][None, :], ln_b[l, 1][None, :], final=l == DEPTH - 1)
    y_prompt = x.reshape(N_CTX_SEQ, CTX_LEN, D)
    y_sample = h.reshape(N_LAT_SEQ, LAT_LEN, D)
    return y_prompt, y_sample, jnp.stack(new_k, axis=1), jnp.stack(new_v, axis=1)
```

```python
import functools

import numpy as np
import jax
import jax.numpy as jnp
from jax import lax
from jax.experimental import pallas as pl
from jax.experimental.pallas import tpu as pltpu

F32 = jnp.float32
BF16 = jnp.bfloat16
I32 = jnp.int32

D = 1024
DEPTH = 4
N_CTX_SEQ = 16
CTX_LEN = 256
N_LAT_SEQ = 8
LAT_LEN = 1024
PAST = 512
T_CTX = N_CTX_SEQ * CTX_LEN
T_LAT = N_LAT_SEQ * LAT_LEN
T = T_CTX + T_LAT
GRID_W = 64
HEAD = 64
N_Q = 16
N_KV = 4
GROUP = 4
D_KV = N_KV * HEAD
N_EXP = 16
N_GRP = 4
EXP_PER_GRP = 4
D_EXP = 512
N_MOD = 6
ALPHA = (2 * DEPTH) ** 0.25
LN_EPS = 1e-5
RMS_EPS = 1e-6
ROPE_THETA = 10000.0
Q_SCALE = HEAD ** -0.5 * float(np.log2(np.e))

LANE = 128
TM = 512
N_TILES = T // TM
CTX_TILES = T_CTX // TM
TILES_PER_LAT = LAT_LEN // TM
POST_ROWS = 256
TMP = 1024
SEG = 256
N_SEG = T // SEG
CTX_SEGS = T_CTX // SEG
SEGS_PER_LAT = LAT_LEN // SEG
HALO_A = 8
HALO_B = 16
SH_ROWS = SEG + 2 * HALO_B - 8
FFN_BLK = 256
N_FFN_BLOCKS = (2 * T) // FFN_BLK + N_EXP
N_SLOTS = N_FFN_BLOCKS * FFN_BLK
VMEM_LIMIT = 56 * 1024 * 1024

COL_AB, COL_AC, COL_AX, COL_BU, COL_BG, COL_Q, COL_KV, COL_GATES = 0, 2, 4, 6, 8, 10, 12, 13


def _params(sem):
    return pltpu.CompilerParams(dimension_semantics=sem, vmem_limit_bytes=VMEM_LIMIT)


def _cond_row(m, tiles_ctx, tiles_per_lat):
    return jnp.where(m < tiles_ctx, 0, 1 + (m - tiles_ctx) // tiles_per_lat)


def _layer_norm(x, g, b):
    mu = jnp.mean(x, axis=-1, keepdims=True)
    xc = x - mu
    var = jnp.mean(xc * xc, axis=-1, keepdims=True)
    return xc * lax.rsqrt(var + LN_EPS) * g + b


def _sigmoid(x):
    return 1.0 / (1.0 + jnp.exp(-x))


ROW_TILE = D // LANE


def _store_rows(ref, x):
    for j in range(ROW_TILE):
        ref[pl.ds(j, x.shape[0], stride=ROW_TILE), :] = x[:, j * LANE:(j + 1) * LANE]


def _load_rows(ref, n_rows):
    return jnp.concatenate([ref[pl.ds(j, n_rows, stride=ROW_TILE), :] for j in range(ROW_TILE)], axis=1)


def _mod_kernel(cond_ref, w_ref, b_ref, o_ref):
    cnd = cond_ref[...]
    s = (cnd * _sigmoid(cnd)).astype(BF16)
    o_ref[...] = jnp.dot(s, w_ref[...].astype(BF16), preferred_element_type=F32) + b_ref[...]


def _modulation(cond16, w_ada, b_ada):
    n_col = N_MOD * D
    tn = 1024
    return pl.pallas_call(
        _mod_kernel,
        out_shape=jax.ShapeDtypeStruct((DEPTH, 16, n_col), F32),
        grid=(DEPTH, n_col // tn),
        in_specs=[
            pl.BlockSpec((16, D), lambda l, n: (0, 0)),
            pl.BlockSpec((None, D, tn), lambda l, n: (l, 0, n)),
            pl.BlockSpec((None, 1, tn), lambda l, n: (l, 0, n)),
        ],
        out_specs=pl.BlockSpec((None, 16, tn), lambda l, n: (l, 0, n)),
        compiler_params=_params(("parallel", "parallel")),
        name="adaln_mod",
    )(cond16, w_ada, b_ada.reshape(DEPTH, 1, n_col))


def _modulate_kernel(x_ref, mod_ref, h_ref):
    h_ref[...] = (x_ref[...] * (1.0 + mod_ref[1:2, :]) + mod_ref[0:1, :]).astype(BF16)


def _modulate(x, mod_l):
    return pl.pallas_call(
        _modulate_kernel,
        out_shape=jax.ShapeDtypeStruct((T, D), BF16),
        grid=(N_TILES,),
        in_specs=[
            pl.BlockSpec((TM, D), lambda m: (m, 0)),
            pl.BlockSpec((None, N_MOD, D), lambda m: (_cond_row(m, CTX_TILES, TILES_PER_LAT), 0, 0)),
        ],
        out_specs=pl.BlockSpec((TM, D), lambda m: (m, 0)),
        compiler_params=_params(("parallel",)),
        name="modulate",
    )(x, mod_l)


WCOL = 512


def _cast_weights(first, pairs):
    @pl.when(first)
    def _():
        for src, dst in pairs:
            dst[...] = src[...].astype(BF16)


def _proj_a_kernel(h_ref, wb_ref, wc_ref, wx_ref, ab_ref, acx_ref, wb_s, wc_s, wx_s):
    _cast_weights(pl.program_id(1) == 0, [(wb_ref, wb_s), (wc_ref, wc_s), (wx_ref, wx_s)])
    h = h_ref[...]
    ab_ref[...] = jnp.dot(h, wb_s[...], preferred_element_type=F32).astype(BF16)
    acx_ref[...] = (jnp.dot(h, wc_s[...], preferred_element_type=F32)
                    * jnp.dot(h, wx_s[...], preferred_element_type=F32))


def _proj_b_kernel(h_ref, wu_ref, wg_ref, glu_ref, wu_s, wg_s):
    _cast_weights(pl.program_id(1) == 0, [(wu_ref, wu_s), (wg_ref, wg_s)])
    h = h_ref[...]
    glu_ref[...] = (jnp.dot(h, wu_s[...], preferred_element_type=F32)
                    * _sigmoid(jnp.dot(h, wg_s[...], preferred_element_type=F32)))


def _proj_gate_kernel(h_ref, w0_ref, w1_ref, g_ref, w_s):
    _cast_weights(pl.program_id(1) == 0, [(w0_ref, w_s.at[:, 0:WCOL]), (w1_ref, w_s.at[:, WCOL:2 * WCOL])])
    g_ref[...] = _sigmoid(jnp.dot(h_ref[...], w_s[...], preferred_element_type=F32)).astype(BF16)


def _w_spec(l, col0, step=1):
    return pl.BlockSpec((None, D, WCOL), lambda c, m: (l, 0, col0 + step * c))


def _proj_a(h, w_in, l):
    out_spec = pl.BlockSpec((TMP, WCOL), lambda c, m: (m, c))
    return pl.pallas_call(
        _proj_a_kernel,
        out_shape=(jax.ShapeDtypeStruct((T, D), BF16), jax.ShapeDtypeStruct((T, D), F32)),
        grid=(D // WCOL, T // TMP),
        in_specs=[pl.BlockSpec((TMP, D), lambda c, m: (m, 0)),
                  _w_spec(l, COL_AB), _w_spec(l, COL_AC), _w_spec(l, COL_AX)],
        out_specs=(out_spec, out_spec),
        scratch_shapes=[pltpu.VMEM((D, WCOL), BF16)] * 3,
        compiler_params=_params(("parallel", "arbitrary")),
        name="proj_a",
    )(h, w_in, w_in, w_in)


def _proj_b(h, w_in, l):
    return pl.pallas_call(
        _proj_b_kernel,
        out_shape=jax.ShapeDtypeStruct((T, D), F32),
        grid=(D // WCOL, T // TMP),
        in_specs=[pl.BlockSpec((TMP, D), lambda c, m: (m, 0)), _w_spec(l, COL_BU), _w_spec(l, COL_BG)],
        out_specs=pl.BlockSpec((TMP, WCOL), lambda c, m: (m, c)),
        scratch_shapes=[pltpu.VMEM((D, WCOL), BF16)] * 2,
        compiler_params=_params(("parallel", "arbitrary")),
        name="proj_b",
    )(h, w_in, w_in)


def _proj_gates(h, w_in, l):
    tn = 2 * WCOL
    return pl.pallas_call(
        _proj_gate_kernel,
        out_shape=jax.ShapeDtypeStruct((T, 3 * D), BF16),
        grid=(3 * D // tn, T // TMP),
        in_specs=[pl.BlockSpec((TMP, D), lambda c, m: (m, 0)),
                  _w_spec(l, COL_GATES, 2), _w_spec(l, COL_GATES + 1, 2)],
        out_specs=pl.BlockSpec((TMP, tn), lambda c, m: (m, c)),
        scratch_shapes=[pltpu.VMEM((D, tn), BF16)],
        compiler_params=_params(("parallel", "arbitrary")),
        name="proj_gates",
    )(h, w_in, w_in)


def _head_mean_square(x, ones_bd):
    out = []
    for c in range(x.shape[1] // LANE):
        sq = x[:, c * LANE:(c + 1) * LANE]
        out.append(jnp.dot((sq * sq).astype(BF16), ones_bd, preferred_element_type=F32))
    return jnp.concatenate(out, axis=1) * (1.0 / HEAD)


def _rope(x, cos, sin, first_half):
    out = []
    for c in range(x.shape[1] // LANE):
        xc = x[:, c * LANE:(c + 1) * LANE]
        partner = jnp.where(first_half, pltpu.roll(xc, LANE - 16, axis=1), pltpu.roll(xc, 16, axis=1))
        out.append(xc * cos + partner * sin)
    return jnp.concatenate(out, axis=1)


def _qkv_kernel(h_ref, wq_ref, wkv_ref, gq_ref, gk_ref, cos_ref, sin_ref, ones_ref,
                q_ref, katt_ref, vatt_ref, kn_ref, vf_ref, wq_s, wkv_s):
    _cast_weights(pl.program_id(0) == 0, [(wq_ref, wq_s), (wkv_ref, wkv_s)])
    h = h_ref[...]
    ones_bd = ones_ref[...]
    cos = cos_ref[...]
    sin = sin_ref[...]
    lane = lax.broadcasted_iota(I32, (TM, LANE), 1)
    first_half = (lane & 16) == 0
    q = jnp.dot(h, wq_s[...], preferred_element_type=F32)
    qn = q * lax.rsqrt(_head_mean_square(q, ones_bd) + RMS_EPS) * gq_ref[...]
    q_ref[...] = (_rope(qn, cos, sin, first_half) * Q_SCALE).astype(BF16)
    kv = jnp.dot(h, wkv_s[...], preferred_element_type=F32)
    k = kv[:, :D_KV]
    v = kv[:, D_KV:]
    kn = k * lax.rsqrt(_head_mean_square(k, ones_bd) + RMS_EPS) * gk_ref[...]
    katt_ref[...] = _rope(kn, cos, sin, first_half).astype(BF16)
    vatt_ref[...] = v.astype(BF16)

    @pl.when(pl.program_id(0) < CTX_TILES)
    def _():
        kn_ref[...] = kn
        vf_ref[...] = v


def _proj_qkv(h, w_in, gq, gk, cos_tab, sin_tab, ones_bd, l):
    def tab_idx(m):
        return jnp.where(m < CTX_TILES, TILES_PER_LAT, (m - CTX_TILES) % TILES_PER_LAT)

    row = lambda m: (m, 0)
    ctx_row = lambda m: (jnp.minimum(m, CTX_TILES - 1), 0)
    return pl.pallas_call(
        _qkv_kernel,
        out_shape=(jax.ShapeDtypeStruct((T, D), BF16),
                   jax.ShapeDtypeStruct((T, D_KV), BF16),
                   jax.ShapeDtypeStruct((T, D_KV), BF16),
                   jax.ShapeDtypeStruct((T_CTX, D_KV), F32),
                   jax.ShapeDtypeStruct((T_CTX, D_KV), F32)),
        grid=(N_TILES,),
        in_specs=[
            pl.BlockSpec((TM, D), row),
            pl.BlockSpec((None, D, D), lambda m: (l, 0, COL_Q // 2)),
            pl.BlockSpec((None, D, 2 * D_KV), lambda m: (l, 0, COL_KV)),
            pl.BlockSpec((1, D), lambda m: (0, 0)),
            pl.BlockSpec((1, D_KV), lambda m: (0, 0)),
            pl.BlockSpec((TM, LANE), lambda m: (tab_idx(m), 0)),
            pl.BlockSpec((TM, LANE), lambda m: (tab_idx(m), 0)),
            pl.BlockSpec((LANE, LANE), lambda m: (0, 0)),
        ],
        out_specs=(pl.BlockSpec((TM, D), row), pl.BlockSpec((TM, D_KV), row),
                   pl.BlockSpec((TM, D_KV), row), pl.BlockSpec((TM, D_KV), ctx_row),
                   pl.BlockSpec((TM, D_KV), ctx_row)),
        scratch_shapes=[pltpu.VMEM((D, D), BF16), pltpu.VMEM((D, 2 * D_KV), BF16)],
        compiler_params=_params(("arbitrary",)),
        name="proj_qkv",
    )(h, w_in, w_in, gq, gk, cos_tab, sin_tab, ones_bd)


def _rope_tables():
    lane = np.arange(LANE)
    j = lane % 16
    freqs = jnp.power(ROPE_THETA, -jnp.arange(16, dtype=F32) / 16)[j]
    pos = jnp.arange(LAT_LEN, dtype=I32)
    row = (pos // GRID_W).astype(F32)
    col = (pos % GRID_W).astype(F32)
    use_row = jnp.asarray((lane % HEAD) < HEAD // 2)
    p = jnp.where(use_row[None, :], row[:, None], col[:, None])
    ang = p * freqs[None, :]
    sign = jnp.asarray(np.where((lane & 16) == 0, -1.0, 1.0), F32)
    cos = jnp.concatenate([jnp.cos(ang), jnp.ones((TM, LANE), F32)], axis=0)
    sin = jnp.concatenate([jnp.sin(ang) * sign[None, :], jnp.zeros((TM, LANE), F32)], axis=0)
    return cos, sin


def _conv_kernel(acx_ref, acx_l_ref, acx_r_ref, ab_ref, glu_ref, glu_l_ref, glu_r_ref,
                 wa_ref, wb_ref, bb_ref, ng_ref, nb_ref, ya_ref, yb_ref, pad_a, pad_b, u_ref, sh_ref):
    s = pl.program_id(0)
    lat = s >= CTX_SEGS
    pos = (s - CTX_SEGS) % SEGS_PER_LAT
    has_left = jnp.logical_and(lat, pos != 0)
    has_right = jnp.logical_and(lat, pos != SEGS_PER_LAT - 1)

    pad_a[0:HALO_A, :] = jnp.where(has_left, acx_l_ref[...], 0.0)
    pad_a[HALO_A:HALO_A + SEG, :] = acx_ref[...]
    pad_a[HALO_A + SEG:, :] = jnp.where(has_right, acx_r_ref[...], 0.0)
    pad_b[0:HALO_B, :] = jnp.where(has_left, glu_l_ref[...], 0.0)
    pad_b[HALO_B:HALO_B + SEG, :] = glu_ref[...]
    pad_b[HALO_B + SEG:, :] = jnp.where(has_right, glu_r_ref[...], 0.0)

    conv_a = (wa_ref[0:1, :] * pad_a[HALO_A - 1:HALO_A - 1 + SEG, :]
              + wa_ref[1:2, :] * pad_a[HALO_A:HALO_A + SEG, :]
              + wa_ref[2:3, :] * pad_a[HALO_A + 1:HALO_A + 1 + SEG, :])
    ya_ref[...] = (ab_ref[...].astype(F32) * conv_a).astype(BF16)

    rows = 64
    kb = wb_ref.shape[0]

    def lane_chunk(c, carry):
        lanes = pl.ds(pl.multiple_of(c * LANE, LANE), LANE)
        for b in range(1, 8):
            sh_ref[b, :, :] = pad_b[b:b + SH_ROWS, lanes]
        for r in range(SEG // rows):
            acc = jnp.zeros((rows, LANE), F32)
            for k in range(kb):
                off = HALO_B + k - kb // 2
                row0 = r * rows + 8 * (off // 8)
                if off % 8 == 0:
                    src = pad_b[row0:row0 + rows, lanes]
                else:
                    src = sh_ref[off % 8, row0:row0 + rows, :]
                acc = acc + wb_ref[k:k + 1, lanes] * src
            u_ref[r * rows:(r + 1) * rows, lanes] = acc
        return carry

    lax.fori_loop(0, D // LANE, lane_chunk, 0)
    u = _layer_norm(u_ref[...] + bb_ref[...], ng_ref[...], nb_ref[...])
    yb_ref[...] = (u * _sigmoid(u)).astype(BF16)


def _convs(acx, ab, glu, conv_a_w, conv_b_w, conv_b_b, norm_g, norm_b):
    seg = lambda s: (s, 0)
    const = lambda s: (0, 0)
    ra, rb = SEG // HALO_A, SEG // HALO_B
    left_a = lambda s: (jnp.maximum(s * ra - 1, 0), 0)
    right_a = lambda s: (jnp.minimum((s + 1) * ra, T // HALO_A - 1), 0)
    left_b = lambda s: (jnp.maximum(s * rb - 1, 0), 0)
    right_b = lambda s: (jnp.minimum((s + 1) * rb, T // HALO_B - 1), 0)
    return pl.pallas_call(
        _conv_kernel,
        out_shape=(jax.ShapeDtypeStruct((T, D), BF16), jax.ShapeDtypeStruct((T, D), BF16)),
        grid=(N_SEG,),
        in_specs=[
            pl.BlockSpec((SEG, D), seg), pl.BlockSpec((HALO_A, D), left_a), pl.BlockSpec((HALO_A, D), right_a),
            pl.BlockSpec((SEG, D), seg),
            pl.BlockSpec((SEG, D), seg), pl.BlockSpec((HALO_B, D), left_b), pl.BlockSpec((HALO_B, D), right_b),
            pl.BlockSpec(conv_a_w.shape, const), pl.BlockSpec(conv_b_w.shape, const),
            pl.BlockSpec((1, D), const), pl.BlockSpec((1, D), const), pl.BlockSpec((1, D), const),
        ],
        out_specs=(pl.BlockSpec((SEG, D), seg), pl.BlockSpec((SEG, D), seg)),
        scratch_shapes=[pltpu.VMEM((SEG + 2 * HALO_A, D), F32),
                        pltpu.VMEM((SEG + 2 * HALO_B, D), F32),
                        pltpu.VMEM((SEG, D), F32),
                        pltpu.VMEM((8, SH_ROWS, LANE), F32)],
        compiler_params=_params(("parallel",)),
        name="convs",
    )(acx, acx, acx, ab, glu, glu, glu, conv_a_w, conv_b_w, conv_b_b, norm_g, norm_b)


def _attend(q_ref, key_refs, val_refs, o_ref, ones_denominator):
    nt = (((1,), (1,)), ((), ()))
    keys = [r[...].astype(BF16) for r in key_refs]
    vals = [r[...].astype(BF16) for r in val_refs]
    n_q = q_ref.shape[0]
    for g in range(N_KV):
        kg = [k[:, g * HEAD:(g + 1) * HEAD] for k in keys]
        vg = [v[:, g * HEAD:(g + 1) * HEAD] for v in vals]
        if ones_denominator:
            vg = [jnp.concatenate([v, jnp.ones_like(v)], axis=1) for v in vg]
        qg = jnp.concatenate([q_ref[:, (g * GROUP + hh) * HEAD:(g * GROUP + hh + 1) * HEAD]
                              for hh in range(GROUP)], axis=0)
        s = [lax.dot_general(qg, k, nt, preferred_element_type=F32) for k in kg]
        mx = functools.reduce(jnp.maximum, [jnp.max(x, axis=-1, keepdims=True) for x in s])
        p = [jnp.exp2(x - mx) for x in s]
        acc = functools.reduce(jnp.add, [jnp.dot(x.astype(BF16), v, preferred_element_type=F32)
                                         for x, v in zip(p, vg)])
        if ones_denominator:
            den = acc[:, HEAD:HEAD + 1]
            acc = acc[:, :HEAD]
        else:
            den = functools.reduce(jnp.add, [jnp.sum(x, axis=-1, keepdims=True) for x in p])
        out = (acc / den).astype(BF16)
        for hh in range(GROUP):
            hd = g * GROUP + hh
            o_ref[:, hd * HEAD:(hd + 1) * HEAD] = out[hh * n_q:(hh + 1) * n_q, :]


def _attn_ctx_kernel(q_ref, k_ref, v_ref, o_ref):
    _attend(q_ref, [k_ref], [v_ref], o_ref, ones_denominator=False)


def _attn_lat_kernel(q_ref, k_ref, v_ref, ck_ref, cv_ref, o_in_ref, o_ref):
    del o_in_ref
    _attend(q_ref, [ck_ref, k_ref], [cv_ref, v_ref], o_ref, ones_denominator=True)


def _attention(q, katt, vatt, cache_k, cache_v, l):
    o = pl.pallas_call(
        _attn_ctx_kernel,
        out_shape=jax.ShapeDtypeStruct((T, D), BF16),
        grid=(N_CTX_SEQ,),
        in_specs=[pl.BlockSpec((CTX_LEN, D), lambda b: (b, 0)),
                  pl.BlockSpec((CTX_LEN, D_KV), lambda b: (b, 0)),
                  pl.BlockSpec((CTX_LEN, D_KV), lambda b: (b, 0))],
        out_specs=pl.BlockSpec((CTX_LEN, D), lambda b: (b, 0)),
        compiler_params=_params(("parallel",)),
        name="attn_ctx",
    )(q, katt, vatt)
    seg0 = CTX_SEGS
    lat0 = T_CTX // LAT_LEN
    return pl.pallas_call(
        _attn_lat_kernel,
        out_shape=jax.ShapeDtypeStruct((T, D), BF16),
        grid=(N_LAT_SEQ, SEGS_PER_LAT),
        in_specs=[pl.BlockSpec((SEG, D), lambda b, i: (seg0 + b * SEGS_PER_LAT + i, 0)),
                  pl.BlockSpec((LAT_LEN, D_KV), lambda b, i: (lat0 + b, 0)),
                  pl.BlockSpec((LAT_LEN, D_KV), lambda b, i: (lat0 + b, 0)),
                  pl.BlockSpec((None, None, PAST, D_KV), lambda b, i: (b, l, 0, 0)),
                  pl.BlockSpec((None, None, PAST, D_KV), lambda b, i: (b, l, 0, 0)),
                  pl.BlockSpec(memory_space=pl.ANY)],
        out_specs=pl.BlockSpec((SEG, D), lambda b, i: (seg0 + b * SEGS_PER_LAT + i, 0)),
        input_output_aliases={5: 0},
        compiler_params=_params(("parallel", "parallel")),
        name="attn_lat",
    )(q, katt, vatt, cache_k, cache_v, o)


def _post_kernel(ya_ref, yb_ref, o_ref, g_ref, x_ref, mod_ref, wa_ref, wb_ref, wc_ref, wo_ref,
                 lng_ref, lnb_ref, wr_hi_ref, wr_lo_ref, rb_ref,
                 x1_ref, h2_ref, route_ref, rw_ref, cnt_ref, carry_ref):
    m = pl.program_id(0)

    @pl.when(m == 0)
    def _():
        carry_ref[...] = jnp.zeros_like(carry_ref)

    carry = carry_ref[...]
    r_i = lax.broadcasted_iota(I32, (POST_ROWS, POST_ROWS), 0)
    c_i = lax.broadcasted_iota(I32, (POST_ROWS, POST_ROWS), 1)
    upper = jnp.where(r_i < c_i, 1.0, 0.0).astype(BF16)
    for part in range(TM // POST_ROWS):
        rows = slice(part * POST_ROWS, (part + 1) * POST_ROWS)
        carry = _post_part(rows, carry, upper, ya_ref, yb_ref, o_ref, g_ref, x_ref, mod_ref, wa_ref, wb_ref,
                           wc_ref, wo_ref, lng_ref, lnb_ref, wr_hi_ref, wr_lo_ref, rb_ref,
                           x1_ref, h2_ref, route_ref, rw_ref)
    carry_ref[...] = carry
    cnt_ref[...] = carry.astype(I32)
    route_ref[4:8, :] = jnp.zeros((4, TM), I32)


def _post_part(rows, carry, upper, ya_ref, yb_ref, o_ref, g_ref, x_ref, mod_ref, wa_ref, wb_ref, wc_ref,
               wo_ref, lng_ref, lnb_ref, wr_hi_ref, wr_lo_ref, rb_ref, x1_ref, h2_ref, route_ref, rw_ref):
    n = POST_ROWS
    ya = jnp.dot(ya_ref[rows, :], wa_ref[...], preferred_element_type=F32)
    yb = jnp.dot(yb_ref[rows, :], wb_ref[...], preferred_element_type=F32)
    yc = jnp.dot(o_ref[rows, :], wc_ref[...], preferred_element_type=F32)
    merged = (g_ref[rows, 0:D].astype(F32) * ya + g_ref[rows, D:2 * D].astype(F32) * yb
              + g_ref[rows, 2 * D:3 * D].astype(F32) * yc)
    mix = jnp.dot(merged.astype(BF16), wo_ref[...], preferred_element_type=F32)
    x1 = _layer_norm(ALPHA * x_ref[rows, :] + mod_ref[2:3, :] * mix, lng_ref[...], lnb_ref[...])
    x1_ref[rows, :] = x1
    h2 = x1 * (1.0 + mod_ref[4:5, :]) + mod_ref[3:4, :]

    hi = h2.astype(BF16)
    h2_ref[rows, :] = hi
    lo = (h2 - hi.astype(F32)).astype(BF16)
    wr_hi = wr_hi_ref[...]
    logits = (jnp.dot(hi, wr_hi, preferred_element_type=F32)
              + jnp.dot(lo, wr_hi, preferred_element_type=F32)
              + jnp.dot(hi, wr_lo_ref[...], preferred_element_type=F32))
    scores = _sigmoid(logits.T[0:N_EXP, :])
    sel = scores + rb_ref[...]

    gscore = []
    for g in range(N_GRP):
        r = [sel[g * EXP_PER_GRP + j:g * EXP_PER_GRP + j + 1, :] for j in range(EXP_PER_GRP)]
        pairs = [r[a] + r[b] for a in range(EXP_PER_GRP) for b in range(a + 1, EXP_PER_GRP)]
        gscore.append(functools.reduce(jnp.maximum, pairs))
    best = functools.reduce(jnp.maximum, gscore)
    gsel = jnp.full(best.shape, N_GRP - 1, I32)
    for g in range(N_GRP - 2, -1, -1):
        gsel = jnp.where(gscore[g] == best, g, gsel)

    eidx = lax.broadcasted_iota(I32, (N_EXP, n), 0)
    neg = jnp.float32(-jnp.inf)
    cand = jnp.where((eidx // EXP_PER_GRP) == gsel, sel, neg)
    top1 = jnp.max(cand, axis=0, keepdims=True)
    idx1 = jnp.min(jnp.where(cand == top1, eidx, N_EXP), axis=0, keepdims=True)
    cand2 = jnp.where(eidx == idx1, neg, cand)
    top2 = jnp.max(cand2, axis=0, keepdims=True)
    idx2 = jnp.min(jnp.where(cand2 == top2, eidx, N_EXP), axis=0, keepdims=True)
    is1 = eidx == idx1
    is2 = eidx == idx2
    w1 = jnp.sum(jnp.where(is1, scores, 0.0), axis=0, keepdims=True)
    w2 = jnp.sum(jnp.where(is2, scores, 0.0), axis=0, keepdims=True)
    wsum = w1 + w2
    w1 = w1 / wsum
    w2 = w2 / wsum

    onehot = jnp.where(jnp.logical_or(is1, is2), 1.0, 0.0)
    prefix = jnp.dot(onehot.astype(BF16), upper, preferred_element_type=F32) + carry[:, 0:1]
    rank1 = jnp.sum(jnp.where(is1, prefix, 0.0), axis=0, keepdims=True)
    rank2 = jnp.sum(jnp.where(is2, prefix, 0.0), axis=0, keepdims=True)

    route_ref[0:1, rows] = idx1
    route_ref[1:2, rows] = idx2
    route_ref[2:3, rows] = rank1.astype(I32)
    route_ref[3:4, rows] = rank2.astype(I32)
    wrow = lax.broadcasted_iota(I32, (LANE, n), 0)
    wcols = jnp.where(wrow == 0, w1, jnp.where(wrow == 1, w2, 0.0))
    rw_ref[rows, :] = wcols.T
    return carry + jnp.sum(onehot, axis=1, keepdims=True)


def _post(ya_pre, yb_pre, o, gates, x, mod_l, wa, wb, wc, wo, ln_g, ln_b, wr_hi, wr_lo, rbias, l):
    row = lambda m: (m, 0)
    const = lambda m: (0, 0)
    wspec = pl.BlockSpec((None, D, D), lambda m: (l, 0, 0))
    return pl.pallas_call(
        _post_kernel,
        out_shape=(jax.ShapeDtypeStruct((T, D), F32),
                   jax.ShapeDtypeStruct((T, D), BF16),
                   jax.ShapeDtypeStruct((N_TILES, 8, TM), I32),
                   jax.ShapeDtypeStruct((T, LANE), F32),
                   jax.ShapeDtypeStruct((N_EXP, LANE), I32)),
        grid=(N_TILES,),
        in_specs=[pl.BlockSpec((TM, D), row), pl.BlockSpec((TM, D), row), pl.BlockSpec((TM, D), row),
                  pl.BlockSpec((TM, 3 * D), row), pl.BlockSpec((TM, D), row),
                  pl.BlockSpec((None, N_MOD, D), lambda m: (_cond_row(m, CTX_TILES, TILES_PER_LAT), 0, 0)),
                  wspec, wspec, wspec, wspec,
                  pl.BlockSpec((1, D), const), pl.BlockSpec((1, D), const),
                  pl.BlockSpec((D, LANE), const), pl.BlockSpec((D, LANE), const),
                  pl.BlockSpec((N_EXP, 1), const)],
        out_specs=(pl.BlockSpec((TM, D), row), pl.BlockSpec((TM, D), row),
                   pl.BlockSpec((None, 8, TM), lambda m: (m, 0, 0)),
                   pl.BlockSpec((TM, LANE), row),
                   pl.BlockSpec((N_EXP, LANE), const)),
        scratch_shapes=[pltpu.VMEM((N_EXP, LANE), F32)],
        compiler_params=_params(("arbitrary",)),
        name="merge_ln_router",
    )(ya_pre, yb_pre, o, gates, x, mod_l, wa, wb, wc, wo, ln_g, ln_b, wr_hi, wr_lo, rbias)


def _row_copy(src, dst, src_row, dst_row, sem):
    return pltpu.make_async_copy(src.at[pl.ds(pl.multiple_of(src_row * ROW_TILE, ROW_TILE), ROW_TILE), :],
                                 dst.at[pl.ds(pl.multiple_of(dst_row * ROW_TILE, ROW_TILE), ROW_TILE), :], sem)


DMA_UNROLL = 8


RUN_CHUNK = 256


def _run_copies(src, dst, src_row, dst_row, n, sem, start):
    def piece(offset, size):
        cp = pltpu.make_async_copy(
            src.at[pl.ds(pl.multiple_of((src_row + offset) * ROW_TILE, ROW_TILE), size * ROW_TILE), :],
            dst.at[pl.ds(pl.multiple_of((dst_row + offset) * ROW_TILE, ROW_TILE), size * ROW_TILE), :], sem)
        if start:
            cp.start()
        else:
            cp.wait()

    def whole(j, carry):
        piece(j * RUN_CHUNK, RUN_CHUNK)
        return carry

    lax.fori_loop(0, n // RUN_CHUNK, whole, 0)
    size = RUN_CHUNK // 2
    while size >= 1:
        @pl.when((n & size) != 0)
        def _(size=size):
            piece(n & ~(2 * size - 1), size)
        size //= 2


def _dispatch_kernel(cnt_ref, off_ref, dst_ref, last_ref, lp_ref, h2_ref, xs_hbm, xc_ref, zero_ref, sem, zsem):
    m = pl.program_id(0)

    @pl.when(m == 0)
    def _():
        zero_ref[...] = jnp.zeros_like(zero_ref)

        def pad_copy(e):
            row = pl.multiple_of(last_ref[e] * ROW_TILE, FFN_BLK * ROW_TILE)
            return pltpu.make_async_copy(zero_ref, xs_hbm.at[pl.ds(row, FFN_BLK * ROW_TILE), :], zsem)

        def zero_start(e, carry):
            pad_copy(e).start()
            return carry

        def zero_wait(e, carry):
            pad_copy(e).wait()
            return carry

        lax.fori_loop(0, N_EXP, zero_start, 0)
        lax.fori_loop(0, N_EXP, zero_wait, 0)

    r_i = lax.broadcasted_iota(I32, (2 * TM, TM), 0)
    hit = jnp.logical_or(r_i == lp_ref[0:1, :], r_i == lp_ref[1:2, :])
    perm = jnp.where(hit, 1.0, 0.0).astype(BF16)
    _store_rows(xc_ref, jnp.dot(perm, h2_ref[...], preferred_element_type=F32))

    for start in (True, False):
        for e in range(N_EXP):
            j = m * N_EXP + e
            _run_copies(xc_ref, xs_hbm, off_ref[j], dst_ref[j], cnt_ref[j], sem, start)


def _dispatch(tile_cnt, tile_off, tile_dst, last_blk, lp, h2):
    grid_spec = pltpu.PrefetchScalarGridSpec(
        num_scalar_prefetch=4,
        grid=(N_TILES,),
        in_specs=[pl.BlockSpec((None, 2, TM), lambda m, *_: (m, 0, 0)),
                  pl.BlockSpec((TM, D), lambda m, *_: (m, 0))],
        out_specs=pl.BlockSpec(memory_space=pl.ANY),
        scratch_shapes=[pltpu.VMEM((2 * TM * ROW_TILE, LANE), F32),
                        pltpu.VMEM((FFN_BLK * ROW_TILE, LANE), F32),
                        pltpu.SemaphoreType.DMA, pltpu.SemaphoreType.DMA],
    )
    return pl.pallas_call(
        _dispatch_kernel,
        out_shape=jax.ShapeDtypeStruct((N_SLOTS * ROW_TILE, LANE), F32),
        grid_spec=grid_spec,
        compiler_params=_params(("arbitrary",)),
        name="moe_dispatch",
    )(tile_cnt, tile_off, tile_dst, last_blk, lp, h2)


FFN_AHEAD = 2


def _ffn_kernel(blk_row_ref, blk_exp_ref, n_act_ref, xs_hbm, wgu_ref, wd_ref, y_ref, wgu_s, wd_s, xbuf, sems):
    i = pl.program_id(0)
    n_act = n_act_ref[0]
    new_expert = jnp.logical_or(i == 0, blk_exp_ref[i] != blk_exp_ref[jnp.maximum(i - 1, 0)])
    _cast_weights(new_expert, [(wgu_ref, wgu_s), (wd_ref, wd_s)])
    rows = FFN_BLK * ROW_TILE

    def fetch(b):
        slot = b % (FFN_AHEAD + 1)
        return pltpu.make_async_copy(xs_hbm.at[pl.ds(pl.multiple_of(blk_row_ref[b] * rows, rows), rows), :],
                                     xbuf.at[slot], sems.at[slot])

    for b in range(FFN_AHEAD):
        @pl.when(jnp.logical_and(i == 0, b < n_act))
        def _(b=b):
            fetch(b).start()

    @pl.when(i + FFN_AHEAD < n_act)
    def _():
        fetch(i + FFN_AHEAD).start()

    @pl.when(i < n_act)
    def _():
        fetch(i).wait()
        x = _load_rows(xbuf.at[i % (FFN_AHEAD + 1)], FFN_BLK).astype(BF16)
        gu = jnp.dot(x, wgu_s[...], preferred_element_type=F32)
        gate = gu[:, :D_EXP]
        up = gu[:, D_EXP:]
        act = (gate * _sigmoid(gate) * up).astype(BF16)
        _store_rows(y_ref, jnp.dot(act, wd_s[...], preferred_element_type=F32))


def _expert_ffn(blk_row, blk_exp, n_act, xs, wgu, wd, l):
    grid_spec = pltpu.PrefetchScalarGridSpec(
        num_scalar_prefetch=3,
        grid=(N_FFN_BLOCKS,),
        in_specs=[pl.BlockSpec(memory_space=pl.ANY),
                  pl.BlockSpec((None, None, D, 2 * D_EXP), lambda i, br, be, na: (l, be[i], 0, 0)),
                  pl.BlockSpec((None, None, D_EXP, D), lambda i, br, be, na: (l, be[i], 0, 0))],
        out_specs=pl.BlockSpec((FFN_BLK * ROW_TILE, LANE), lambda i, br, be, na: (br[i], 0)),
        scratch_shapes=[pltpu.VMEM((D, 2 * D_EXP), BF16), pltpu.VMEM((D_EXP, D), BF16),
                        pltpu.VMEM((FFN_AHEAD + 1, FFN_BLK * ROW_TILE, LANE), F32),
                        pltpu.SemaphoreType.DMA((FFN_AHEAD + 1,))],
    )
    return pl.pallas_call(
        _ffn_kernel,
        out_shape=jax.ShapeDtypeStruct((N_SLOTS * ROW_TILE, LANE), F32),
        grid_spec=grid_spec,
        compiler_params=_params(("arbitrary",)),
        name="expert_ffn",
    )(blk_row, blk_exp, n_act, xs, wgu, wd)


def _block_plan(counts):
    nblk = (counts + FFN_BLK - 1) // FFN_BLK
    end = jnp.cumsum(nblk)
    start = end - nblk
    n_act = end[-1]
    i = jnp.arange(N_FFN_BLOCKS, dtype=I32)
    i_eff = jnp.minimum(i, n_act - 1)
    e = jnp.minimum(jnp.sum(i_eff[:, None] >= end[None, :], axis=1), N_EXP - 1).astype(I32)
    last_blk = jnp.minimum(start + jnp.maximum(nblk - 1, 0), N_FFN_BLOCKS - 1) * FFN_BLK
    return ((start * FFN_BLK).astype(I32), last_blk.astype(I32), i_eff.astype(I32), e,
            n_act.reshape(1).astype(I32))


def _tile_plan(route, seg_start):
    is_exp = route[:, 0:2, :, None] == jnp.arange(N_EXP, dtype=I32)
    cnt = jnp.sum(is_exp, axis=(1, 2), dtype=I32)
    before = jnp.cumsum(cnt, axis=0) - cnt
    off = jnp.cumsum(cnt, axis=1) - cnt
    dst = seg_start[None, :] + before
    pick = lambda tab: jnp.sum(jnp.where(is_exp, tab[:, None, None, :], 0), axis=-1)
    rank = route[:, 2:4, :]
    local = pick(off - before) + rank
    slots = pick(jnp.broadcast_to(seg_start, cnt.shape)) + rank
    return cnt.reshape(-1), off.reshape(-1), dst.reshape(-1), local, slots


def _combine_kernel(final, slot_ref, slot_next_ref, y_hbm, rw_ref, x1_ref, mod_ref, modn_ref,
                    lng_ref, lnb_ref, out_a_ref, out_b_ref, buf, sems):
    m = pl.program_id(0)
    half = m % 2

    def gather_tile(s_ref, dst_half):
        def body(i, carry):
            for u in range(DMA_UNROLL):
                t = i * DMA_UNROLL + u
                for k in range(2):
                    _row_copy(y_hbm, buf.at[dst_half, k], s_ref[k, t], t, sems.at[dst_half]).start(priority=k)
            return carry

        lax.fori_loop(0, TM // DMA_UNROLL, body, 0)

    @pl.when(m == 0)
    def _():
        gather_tile(slot_ref, 0)

    @pl.when(m + 1 < pl.num_programs(0))
    def _():
        gather_tile(slot_next_ref, 1 - half)

    def drain(i, carry):
        for _ in range(2 * DMA_UNROLL):
            _row_copy(y_hbm, buf.at[half, 0], 0, 0, sems.at[half]).wait()
        return carry

    lax.fori_loop(0, TM // DMA_UNROLL, drain, 0)
    f = rw_ref[:, 0:1] * _load_rows(buf.at[half, 0], TM) + rw_ref[:, 1:2] * _load_rows(buf.at[half, 1], TM)
    x2 = _layer_norm(ALPHA * x1_ref[...] + mod_ref[5:6, :] * f, lng_ref[...], lnb_ref[...])
    if final:
        @pl.when(m < CTX_TILES)
        def _():
            out_a_ref[...] = x2

        @pl.when(m >= CTX_TILES)
        def _():
            out_b_ref[...] = x2
    else:
        out_a_ref[...] = x2
        out_b_ref[...] = (x2 * (1.0 + modn_ref[1:2, :]) + modn_ref[0:1, :]).astype(BF16)


def _combine(slots, y_slots, rw, x1, mod_l, mod_next, ln_g, ln_b, final):
    row = lambda m: (m, 0)
    const = lambda m: (0, 0)
    mod_spec = pl.BlockSpec((None, N_MOD, D), lambda m: (_cond_row(m, CTX_TILES, TILES_PER_LAT), 0, 0))
    if final:
        out_shape = (jax.ShapeDtypeStruct((T_CTX, D), F32), jax.ShapeDtypeStruct((T_LAT, D), F32))
        out_specs = (pl.BlockSpec((TM, D), lambda m: (jnp.minimum(m, CTX_TILES - 1), 0)),
                     pl.BlockSpec((TM, D), lambda m: (jnp.maximum(m - CTX_TILES, 0), 0)))
    else:
        out_shape = (jax.ShapeDtypeStruct((T, D), F32), jax.ShapeDtypeStruct((T, D), BF16))
        out_specs = (pl.BlockSpec((TM, D), row), pl.BlockSpec((TM, D), row))
    return pl.pallas_call(
        functools.partial(_combine_kernel, final),
        out_shape=out_shape,
        grid=(N_TILES,),
        in_specs=[pl.BlockSpec((None, 2, TM), lambda m: (m, 0, 0), memory_space=pltpu.SMEM),
                  pl.BlockSpec((None, 2, TM), lambda m: (jnp.minimum(m + 1, N_TILES - 1), 0, 0),
                               memory_space=pltpu.SMEM),
                  pl.BlockSpec(memory_space=pl.ANY),
                  pl.BlockSpec((TM, LANE), row), pl.BlockSpec((TM, D), row),
                  mod_spec, mod_spec,
                  pl.BlockSpec((1, D), const), pl.BlockSpec((1, D), const)],
        out_specs=out_specs,
        scratch_shapes=[pltpu.VMEM((2, 2, TM * ROW_TILE, LANE), F32), pltpu.SemaphoreType.DMA((2,))],
        compiler_params=_params(("arbitrary",)),
        name="moe_combine_ln",
    )(slots, slots, y_slots, rw, x1, mod_l, mod_next, ln_g, ln_b)


def kernel(x_prompt, x_sample, c, c_ctx, cache_k, cache_v, w_in, conv_a_w, w_a_out, conv_b_w, conv_b_b,
           norm_b_g, norm_b_b, w_b_out, q_norm_g, k_norm_g, w_c_out, w_o, w_ada, b_ada, ln_g, ln_b,
           w_router, router_bias, w_gate_up, w_down):
    x = jnp.concatenate([x_prompt.reshape(T_CTX, D), x_sample.reshape(T_LAT, D)], axis=0)
    cond16 = jnp.concatenate([c_ctx[None, :], c, jnp.zeros((16 - 1 - N_LAT_SEQ, D), F32)], axis=0)
    mod = _modulation(cond16, w_ada, b_ada).reshape(DEPTH, 16, N_MOD, D)

    wa_b, wb_b, wc_b, wo_b = (w.astype(BF16) for w in (w_a_out, w_b_out, w_c_out, w_o))
    wr = jnp.pad(w_router, ((0, 0), (0, LANE - N_EXP)))
    wr_hi = wr.astype(BF16)
    wr_lo = (wr - wr_hi.astype(F32)).astype(BF16)
    rbias = router_bias.reshape(N_EXP, 1)

    cos_tab, sin_tab = _rope_tables()
    half = np.arange(LANE) // HEAD
    ones_bd = jnp.asarray(half[:, None] == half[None, :], BF16)
    ck = cache_k.reshape(N_LAT_SEQ, DEPTH, PAST, D_KV)
    cv = cache_v.reshape(N_LAT_SEQ, DEPTH, PAST, D_KV)

    h = _modulate(x, mod[0])
    new_k, new_v = [], []
    for l in range(DEPTH):
        ab, acx = _proj_a(h, w_in, l)
        glu = _proj_b(h, w_in, l)
        gates = _proj_gates(h, w_in, l)
        q, katt, vatt, kn, vf = _proj_qkv(h, w_in, jnp.tile(q_norm_g[l], N_Q)[None, :],
                                          jnp.tile(k_norm_g[l], N_KV)[None, :], cos_tab, sin_tab, ones_bd, l)
        new_k.append(kn.reshape(N_CTX_SEQ, CTX_LEN, N_KV, HEAD))
        new_v.append(vf.reshape(N_CTX_SEQ, CTX_LEN, N_KV, HEAD))
        ya_pre, yb_pre = _convs(acx, ab, glu, conv_a_w[l], conv_b_w[l], conv_b_b[l][None, :],
                                norm_b_g[l][None, :], norm_b_b[l][None, :])
        o = _attention(q, katt, vatt, ck, cv, l)
        x1, h2, route, rw, counts = _post(ya_pre, yb_pre, o, gates, x, mod[l], wa_b, wb_b, wc_b, wo_b,
                                          ln_g[l, 0][None, :], ln_b[l, 0][None, :], wr_hi, wr_lo, rbias, l)
        cnt = counts[:, 0]
        seg_start, last_blk, blk_row, blk_exp, n_act = _block_plan(cnt)
        tile_cnt, tile_off, tile_dst, local, slots = _tile_plan(route, seg_start)
        xs = _dispatch(tile_cnt, tile_off, tile_dst, last_blk, local, h2)
        y_slots = _expert_ffn(blk_row, blk_exp, n_act, xs, w_gate_up, w_down, l)
        x, h = _combine(slots, y_slots, rw, x1, mod[l], mod[min(l + 1, DEPTH - 1)],
                        ln_g[l, 1][None, :], ln_b[l, 1][None, :], final=l == DEPTH - 1)
    y_prompt = x.reshape(N_CTX_SEQ, CTX_LEN, D)
    y_sample = h.reshape(N_LAT_SEQ, LAT_LEN, D)
    return y_prompt, y_sample, jnp.stack(new_k, axis=1), jnp.stack(new_v, axis=1)
```
